```python
import jax, jax.numpy as jnp
from jax import lax
import numpy as np

D_MODEL = 2048
BATCH = 8
SEQ = 4096
DEPTH = 4

PLE_DIM = 256
POOL_EXPAND = 2
E_POOL = POOL_EXPAND * D_MODEL
POOL_WINDOWS = (2, 4, 8, 16)
N_POOL_GROUPS = len(POOL_WINDOWS)
G_POOL = E_POOL // N_POOL_GROUPS
N_HEADS = D_MODEL // 128
Q_RANK = 512
KV_RANK = 512
NOPE_DIM = 128
ROPE_DIM = 64
V_DIM = 128
QK_DIM = NOPE_DIM + ROPE_DIM
E_MLA = N_HEADS * V_DIM
ROPE_THETA = 10000.0
Q_BLOCK = 128
SM_SCALE = QK_DIM ** -0.5
NEG_INF = -1e30
EPS = 1e-6
N_POOL_LAYERS = (DEPTH + 1) // 2
N_MLA_LAYERS = DEPTH // 2
MLA_IN_DIM = Q_RANK + KV_RANK + ROPE_DIM + E_MLA

kernel_name = "hybrid_pool_mla_sandwich_ple"


def rms_norm(x, g):
    xf = x.astype(jnp.float32)
    y = xf * lax.rsqrt(jnp.mean(xf * xf, axis=-1, keepdims=True) + EPS)
    return (y * g.astype(jnp.float32)).astype(x.dtype)


def rope(x, cos, sin):
    x1, x2 = jnp.split(x.astype(jnp.float32), 2, axis=-1)
    out = jnp.concatenate([x1 * cos - x2 * sin, x2 * cos + x1 * sin], axis=-1)
    return out.astype(x.dtype)


def pool_mixer(xn, w_in, w_group, scale, w_out):
    B, S, _ = xn.shape
    z = xn @ w_in
    u, g = jnp.split(z, 2, axis=-1)
    uf = u.astype(jnp.float32)
    csum = jnp.concatenate([jnp.zeros((B, 1, E_POOL), jnp.float32), jnp.cumsum(uf, axis=1)], axis=1)
    t = jnp.arange(S)
    pooled = []
    for j, w in enumerate(POOL_WINDOWS):
        sl = slice(j * G_POOL, (j + 1) * G_POOL)
        c = csum[:, :, sl]
        lagged = jnp.concatenate([jnp.zeros((B, w - 1, G_POOL), jnp.float32), c[:, :S - w + 1]], axis=1)
        count = jnp.minimum(t + 1, w).astype(jnp.float32)[None, :, None]
        pooled.append((c[:, 1:] - lagged) / count - uf[:, :, sl])
    pooled = jnp.stack(pooled, axis=2).astype(xn.dtype)
    mixed = jnp.einsum('bsgc,gcd->bsgd', pooled, w_group).reshape(B, S, E_POOL)
    return (mixed * scale * jax.nn.silu(g)) @ w_out


def causal_block_attention(q, k, v):
    B, S, H, Dq = q.shape
    nb = S // Q_BLOCK
    qb = q.reshape(B, nb, Q_BLOCK, H, Dq).transpose(1, 0, 2, 3, 4)
    kpos = jnp.arange(S)

    def one_block(args):
        qi, blk = args
        qpos = blk * Q_BLOCK + jnp.arange(Q_BLOCK)
        s = jnp.einsum('bqhd,bkhd->bhqk', qi, k, preferred_element_type=jnp.float32) * SM_SCALE
        s = jnp.where(kpos[None, :] <= qpos[:, None], s, NEG_INF)
        pr = jax.nn.softmax(s, axis=-1).astype(v.dtype)
        return jnp.einsum('bhqk,bkhd->bqhd', pr, v)

    out = lax.map(one_block, (qb, jnp.arange(nb)))
    return out.transpose(1, 0, 2, 3, 4).reshape(B, S, H, V_DIM)


def mla_mixer(xn, cos, sin, w_in, q_norm, w_uq, kv_norm, w_ukv, w_out):
    B, S, _ = xn.shape
    z = xn @ w_in
    q_lat = z[..., :Q_RANK]
    kv_lat = z[..., Q_RANK:Q_RANK + KV_RANK]
    k_pe = z[..., Q_RANK + KV_RANK:Q_RANK + KV_RANK + ROPE_DIM]
    g = z[..., Q_RANK + KV_RANK + ROPE_DIM:]
    q = (rms_norm(q_lat, q_norm) @ w_uq).reshape(B, S, N_HEADS, QK_DIM)
    kv = (rms_norm(kv_lat, kv_norm) @ w_ukv).reshape(B, S, N_HEADS, NOPE_DIM + V_DIM)
    q_nope, q_pe = q[..., :NOPE_DIM], q[..., NOPE_DIM:]
    k_nope, v = kv[..., :NOPE_DIM], kv[..., NOPE_DIM:]
    q_pe = rope(q_pe, cos[:, :, None, :], sin[:, :, None, :])
    k_pe = rope(k_pe, cos, sin)
    q = jnp.concatenate([q_nope, q_pe], axis=-1)
    k = jnp.concatenate([k_nope, jnp.broadcast_to(k_pe[:, :, None, :], (B, S, N_HEADS, ROPE_DIM))], axis=-1)
    o = causal_block_attention(q, k, v).reshape(B, S, E_MLA)
    return (o * jax.nn.silu(g)) @ w_out


def _fwd_setup_inputs(seed: int = 0) -> dict:
    key = jax.random.key(seed)
    ks = jax.random.split(key, 24)
    f32 = jnp.float32

    def nrm(k, shape, fan_in):
        return jax.random.normal(k, shape, f32) * (fan_in ** -0.5)

    def gain(k, shape):
        return 1.0 + 0.02 * jax.random.normal(k, shape, f32)

    offs = jax.random.randint(ks[2], (BATCH, 1), 0, 1024, dtype=jnp.int32)
    positions = offs + jnp.arange(SEQ, dtype=jnp.int32)[None, :]
    return {
        "x": jax.random.normal(ks[0], (BATCH, SEQ, D_MODEL), f32),
        "p": jax.random.normal(ks[1], (DEPTH, BATCH, SEQ, PLE_DIM), f32),
        "positions": positions,
        "pre_norm": gain(ks[3], (DEPTH, D_MODEL)),
        "post_norm": gain(ks[4], (DEPTH, D_MODEL)),
        "pool_w_in": nrm(ks[5], (N_POOL_LAYERS, D_MODEL, 2 * E_POOL), D_MODEL),
        "pool_w_group": nrm(ks[6], (N_POOL_LAYERS, N_POOL_GROUPS, G_POOL, G_POOL), G_POOL),
        "pool_scale": gain(ks[7], (N_POOL_LAYERS, E_POOL)),
        "pool_w_out": nrm(ks[8], (N_POOL_LAYERS, E_POOL, D_MODEL), E_POOL),
        "mla_w_in": nrm(ks[9], (N_MLA_LAYERS, D_MODEL, MLA_IN_DIM), D_MODEL),
        "mla_q_norm": gain(ks[10], (N_MLA_LAYERS, Q_RANK)),
        "mla_w_uq": nrm(ks[11], (N_MLA_LAYERS, Q_RANK, N_HEADS * QK_DIM), Q_RANK),
        "mla_kv_norm": gain(ks[12], (N_MLA_LAYERS, KV_RANK)),
        "mla_w_ukv": nrm(ks[13], (N_MLA_LAYERS, KV_RANK, N_HEADS * (NOPE_DIM + V_DIM)), KV_RANK),
        "mla_w_out": nrm(ks[14], (N_MLA_LAYERS, E_MLA, D_MODEL), E_MLA),
        "ple_norm": gain(ks[15], (DEPTH, D_MODEL)),
        "ple_w_gate": nrm(ks[16], (DEPTH, D_MODEL, D_MODEL), D_MODEL),
        "ple_w_proj": nrm(ks[17], (DEPTH, PLE_DIM, D_MODEL), PLE_DIM),
    }


def _fwd_reference(x, p, positions, pre_norm, post_norm, pool_w_in, pool_w_group, pool_scale, pool_w_out,
              mla_w_in, mla_q_norm, mla_w_uq, mla_kv_norm, mla_w_ukv, mla_w_out,
              ple_norm, ple_w_gate, ple_w_proj):
    inv_freq = ROPE_THETA ** (-jnp.arange(0, ROPE_DIM, 2, dtype=jnp.float32) / ROPE_DIM)
    ang = positions.astype(jnp.float32)[..., None] * inv_freq
    cos, sin = jnp.cos(ang), jnp.sin(ang)
    h = x
    for i in range(DEPTH):
        xn = rms_norm(h, pre_norm[i])
        j = i // 2
        if i % 2 == 0:
            out = pool_mixer(xn, pool_w_in[j], pool_w_group[j], pool_scale[j], pool_w_out[j])
        else:
            out = mla_mixer(xn, cos, sin, mla_w_in[j], mla_q_norm[j], mla_w_uq[j],
                            mla_kv_norm[j], mla_w_ukv[j], mla_w_out[j])
        h = h + rms_norm(out, post_norm[i])
        gate = jax.nn.sigmoid(rms_norm(h, ple_norm[i]) @ ple_w_gate[i])
        h = h + (p[i] @ ple_w_proj[i]) * gate
    return h


import jax as _jax
import jax.numpy as _jnp

TWIN_FORMAT = 'train_step'
FWD_PARAMS = ['x', 'p', 'positions', 'pre_norm', 'post_norm', 'pool_w_in', 'pool_w_group', 'pool_scale', 'pool_w_out', 'mla_w_in', 'mla_q_norm', 'mla_w_uq', 'mla_kv_norm', 'mla_w_ukv', 'mla_w_out', 'ple_norm', 'ple_w_gate', 'ple_w_proj']
TWIN_WEIGHTS = ['pre_norm', 'post_norm', 'pool_w_in', 'pool_w_group', 'pool_scale', 'pool_w_out', 'mla_w_in', 'mla_q_norm', 'mla_w_uq', 'mla_kv_norm', 'mla_w_ukv', 'mla_w_out', 'ple_norm', 'ple_w_gate', 'ple_w_proj']
TWIN_DIFF_INPUT = 'x'
TWIN_INPUTS = ['x', 'p', 'positions', 'pre_norm', 'post_norm', 'pool_w_in', 'pool_w_group', 'pool_scale', 'pool_w_out', 'mla_w_in', 'mla_q_norm', 'mla_w_uq', 'mla_kv_norm', 'mla_w_ukv', 'mla_w_out', 'ple_norm', 'ple_w_gate', 'ple_w_proj', 'loss_target', 'm_pre_norm', 'm_post_norm', 'm_pool_w_in', 'm_pool_w_group', 'm_pool_scale', 'm_pool_w_out', 'm_mla_w_in', 'm_mla_q_norm', 'm_mla_w_uq', 'm_mla_kv_norm', 'm_mla_w_ukv', 'm_mla_w_out', 'm_ple_norm', 'm_ple_w_gate', 'm_ple_w_proj', 'v_pre_norm', 'v_post_norm', 'v_pool_w_in', 'v_pool_w_group', 'v_pool_scale', 'v_pool_w_out', 'v_mla_w_in', 'v_mla_q_norm', 'v_mla_w_uq', 'v_mla_kv_norm', 'v_mla_w_ukv', 'v_mla_w_out', 'v_ple_norm', 'v_ple_w_gate', 'v_ple_w_proj']
TWIN_OUTPUTS = ['loss', 'grad_x', 'grad_pre_norm', 'grad_post_norm', 'grad_pool_w_in', 'grad_pool_w_group', 'grad_pool_scale', 'grad_pool_w_out', 'grad_mla_w_in', 'grad_mla_q_norm', 'grad_mla_w_uq', 'grad_mla_kv_norm', 'grad_mla_w_ukv', 'grad_mla_w_out', 'grad_ple_norm', 'grad_ple_w_gate', 'grad_ple_w_proj', 'delta_pre_norm', 'delta_post_norm', 'delta_pool_w_in', 'delta_pool_w_group', 'delta_pool_scale', 'delta_pool_w_out', 'delta_mla_w_in', 'delta_mla_q_norm', 'delta_mla_w_uq', 'delta_mla_kv_norm', 'delta_mla_w_ukv', 'delta_mla_w_out', 'delta_ple_norm', 'delta_ple_w_gate', 'delta_ple_w_proj', 'new_m_pre_norm', 'new_m_post_norm', 'new_m_pool_w_in', 'new_m_pool_w_group', 'new_m_pool_scale', 'new_m_pool_w_out', 'new_m_mla_w_in', 'new_m_mla_q_norm', 'new_m_mla_w_uq', 'new_m_mla_kv_norm', 'new_m_mla_w_ukv', 'new_m_mla_w_out', 'new_m_ple_norm', 'new_m_ple_w_gate', 'new_m_ple_w_proj', 'new_v_pre_norm', 'new_v_post_norm', 'new_v_pool_w_in', 'new_v_pool_w_group', 'new_v_pool_scale', 'new_v_pool_w_out', 'new_v_mla_w_in', 'new_v_mla_q_norm', 'new_v_mla_w_uq', 'new_v_mla_kv_norm', 'new_v_mla_w_ukv', 'new_v_mla_w_out', 'new_v_ple_norm', 'new_v_ple_w_gate', 'new_v_ple_w_proj']
TWIN_LEAF_KINDS = {'loss': 'loss', 'grad_x': 'grad_x', 'grad_pre_norm': 'grad_w', 'grad_post_norm': 'grad_w', 'grad_pool_w_in': 'grad_w', 'grad_pool_w_group': 'grad_w', 'grad_pool_scale': 'grad_w', 'grad_pool_w_out': 'grad_w', 'grad_mla_w_in': 'grad_w', 'grad_mla_q_norm': 'grad_w', 'grad_mla_w_uq': 'grad_w', 'grad_mla_kv_norm': 'grad_w', 'grad_mla_w_ukv': 'grad_w', 'grad_mla_w_out': 'grad_w', 'grad_ple_norm': 'grad_w', 'grad_ple_w_gate': 'grad_w', 'grad_ple_w_proj': 'grad_w', 'delta_pre_norm': 'delta_w', 'delta_post_norm': 'delta_w', 'delta_pool_w_in': 'delta_w', 'delta_pool_w_group': 'delta_w', 'delta_pool_scale': 'delta_w', 'delta_pool_w_out': 'delta_w', 'delta_mla_w_in': 'delta_w', 'delta_mla_q_norm': 'delta_w', 'delta_mla_w_uq': 'delta_w', 'delta_mla_kv_norm': 'delta_w', 'delta_mla_w_ukv': 'delta_w', 'delta_mla_w_out': 'delta_w', 'delta_ple_norm': 'delta_w', 'delta_ple_w_gate': 'delta_w', 'delta_ple_w_proj': 'delta_w', 'new_m_pre_norm': 'new_m', 'new_m_post_norm': 'new_m', 'new_m_pool_w_in': 'new_m', 'new_m_pool_w_group': 'new_m', 'new_m_pool_scale': 'new_m', 'new_m_pool_w_out': 'new_m', 'new_m_mla_w_in': 'new_m', 'new_m_mla_q_norm': 'new_m', 'new_m_mla_w_uq': 'new_m', 'new_m_mla_kv_norm': 'new_m', 'new_m_mla_w_ukv': 'new_m', 'new_m_mla_w_out': 'new_m', 'new_m_ple_norm': 'new_m', 'new_m_ple_w_gate': 'new_m', 'new_m_ple_w_proj': 'new_m', 'new_v_pre_norm': 'new_v', 'new_v_post_norm': 'new_v', 'new_v_pool_w_in': 'new_v', 'new_v_pool_w_group': 'new_v', 'new_v_pool_scale': 'new_v', 'new_v_pool_w_out': 'new_v', 'new_v_mla_w_in': 'new_v', 'new_v_mla_q_norm': 'new_v', 'new_v_mla_w_uq': 'new_v', 'new_v_mla_kv_norm': 'new_v', 'new_v_mla_w_ukv': 'new_v', 'new_v_mla_w_out': 'new_v', 'new_v_ple_norm': 'new_v', 'new_v_ple_w_gate': 'new_v', 'new_v_ple_w_proj': 'new_v'}


def _forward(args):
    return _fwd_reference(*[args[k] for k in FWD_PARAMS])


def _output_shape():
    def fwd():
        inp = _fwd_setup_inputs(0)
        return _fwd_reference(*[inp[k] for k in FWD_PARAMS])
    out = _jax.eval_shape(fwd)
    return out.shape, out.dtype

N_MICROBATCH = 1
ADAM_LR = 0.001
ADAM_B1 = 0.9
ADAM_B2 = 0.999
ADAM_EPS = 1e-08
ADAM_WD = 0.01
ADAM_STEP = 10
PER_EXAMPLE_BATCH_AXIS = {'x': 0, 'p': 1, 'positions': 0, 'loss_target': 0}
SHARED_INPUTS = []
_WEIGHT_DTYPES = {'pre_norm': _jnp.float32, 'post_norm': _jnp.float32, 'pool_w_in': _jnp.float32, 'pool_w_group': _jnp.float32, 'pool_scale': _jnp.float32, 'pool_w_out': _jnp.float32, 'mla_w_in': _jnp.float32, 'mla_q_norm': _jnp.float32, 'mla_w_uq': _jnp.float32, 'mla_kv_norm': _jnp.float32, 'mla_w_ukv': _jnp.float32, 'mla_w_out': _jnp.float32, 'ple_norm': _jnp.float32, 'ple_w_gate': _jnp.float32, 'ple_w_proj': _jnp.float32}
MOMENT_SCALE = {'pre_norm': 6.439882e-01, 'post_norm': 1.600465e+01, 'pool_w_in': 3.359658e-01, 'pool_w_group': 3.313913e-01, 'pool_scale': 3.301968e-01, 'pool_w_out': 4.724962e-01, 'mla_w_in': 4.785361e-01, 'mla_q_norm': 5.312763e-01, 'mla_w_uq': 2.244401e-01, 'mla_kv_norm': 1.017703e+00, 'mla_w_ukv': 3.277742e-01, 'mla_w_out': 4.017829e-01, 'ple_norm': 4.609644e-01, 'ple_w_gate': 9.022480e-02, 'ple_w_proj': 2.946752e-01}


def _to_microbatches(a, axis):
    t = _jnp.moveaxis(a, axis, 0)
    t = t.reshape((N_MICROBATCH, t.shape[0] // N_MICROBATCH) + t.shape[1:])
    return _jnp.moveaxis(t, 1, axis + 1)


def setup_inputs(seed: int = 0) -> dict:
    inp = _fwd_setup_inputs(seed)
    key = _jax.random.fold_in(_jax.random.key(seed), 7919)
    shape, _ = _output_shape()
    out = dict(inp)
    out["loss_target"] = _jax.random.normal(_jax.random.fold_in(key, 0), shape, _jnp.float32)
    for i, name in enumerate(TWIN_WEIGHTS):
        w = inp[name].astype(_jnp.float32)
        if MOMENT_SCALE is None:
            s = _jnp.sqrt(_jnp.mean(_jnp.square(w)) + 1e-30)
        else:
            s = MOMENT_SCALE[name]
        km, kv = _jax.random.split(_jax.random.fold_in(key, i + 1))
        out[name] = w
        out["m_" + name] = s * _jax.random.normal(km, w.shape, _jnp.float32)
        out["v_" + name] = (s * s) * _jax.random.uniform(kv, w.shape, _jnp.float32, 0.5, 1.5)
    if N_MICROBATCH > 1:
        for name, axis in PER_EXAMPLE_BATCH_AXIS.items():
            out[name] = _to_microbatches(out[name], axis)
    return {'x': out['x'], 'p': out['p'], 'positions': out['positions'], 'pre_norm': out['pre_norm'], 'post_norm': out['post_norm'], 'pool_w_in': out['pool_w_in'], 'pool_w_group': out['pool_w_group'], 'pool_scale': out['pool_scale'], 'pool_w_out': out['pool_w_out'], 'mla_w_in': out['mla_w_in'], 'mla_q_norm': out['mla_q_norm'], 'mla_w_uq': out['mla_w_uq'], 'mla_kv_norm': out['mla_kv_norm'], 'mla_w_ukv': out['mla_w_ukv'], 'mla_w_out': out['mla_w_out'], 'ple_norm': out['ple_norm'], 'ple_w_gate': out['ple_w_gate'], 'ple_w_proj': out['ple_w_proj'], 'loss_target': out['loss_target'], 'm_pre_norm': out['m_pre_norm'], 'm_post_norm': out['m_post_norm'], 'm_pool_w_in': out['m_pool_w_in'], 'm_pool_w_group': out['m_pool_w_group'], 'm_pool_scale': out['m_pool_scale'], 'm_pool_w_out': out['m_pool_w_out'], 'm_mla_w_in': out['m_mla_w_in'], 'm_mla_q_norm': out['m_mla_q_norm'], 'm_mla_w_uq': out['m_mla_w_uq'], 'm_mla_kv_norm': out['m_mla_kv_norm'], 'm_mla_w_ukv': out['m_mla_w_ukv'], 'm_mla_w_out': out['m_mla_w_out'], 'm_ple_norm': out['m_ple_norm'], 'm_ple_w_gate': out['m_ple_w_gate'], 'm_ple_w_proj': out['m_ple_w_proj'], 'v_pre_norm': out['v_pre_norm'], 'v_post_norm': out['v_post_norm'], 'v_pool_w_in': out['v_pool_w_in'], 'v_pool_w_group': out['v_pool_w_group'], 'v_pool_scale': out['v_pool_scale'], 'v_pool_w_out': out['v_pool_w_out'], 'v_mla_w_in': out['v_mla_w_in'], 'v_mla_q_norm': out['v_mla_q_norm'], 'v_mla_w_uq': out['v_mla_w_uq'], 'v_mla_kv_norm': out['v_mla_kv_norm'], 'v_mla_w_ukv': out['v_mla_w_ukv'], 'v_mla_w_out': out['v_mla_w_out'], 'v_ple_norm': out['v_ple_norm'], 'v_ple_w_gate': out['v_ple_w_gate'], 'v_ple_w_proj': out['v_ple_w_proj']}


def _loss(weights, diff, rest, loss_target):
    with _jax.named_scope("forward"):
        args = {**rest, TWIN_DIFF_INPUT: diff, **{k: w.astype(_WEIGHT_DTYPES[k]) for k, w in weights.items()}}
        y = _forward(args)
    with _jax.named_scope("loss_head"):
        err = _jnp.square(y.astype(_jnp.float32) - loss_target)
        return 0.5 * _jnp.sum(_jnp.mean(err, axis=-1)) if err.ndim else 0.5 * err


def _adamw(w, g, m, v):
    m = ADAM_B1 * m + (1.0 - ADAM_B1) * g
    v = ADAM_B2 * v + (1.0 - ADAM_B2) * _jnp.square(g)
    m_hat = m / (1.0 - ADAM_B1 ** ADAM_STEP)
    v_hat = v / (1.0 - ADAM_B2 ** ADAM_STEP)
    delta = -ADAM_LR * (m_hat / (_jnp.sqrt(v_hat) + ADAM_EPS) + ADAM_WD * w)
    return delta, m, v


def reference(x, p, positions, pre_norm, post_norm, pool_w_in, pool_w_group, pool_scale, pool_w_out, mla_w_in, mla_q_norm, mla_w_uq, mla_kv_norm, mla_w_ukv, mla_w_out, ple_norm, ple_w_gate, ple_w_proj, loss_target, m_pre_norm, m_post_norm, m_pool_w_in, m_pool_w_group, m_pool_scale, m_pool_w_out, m_mla_w_in, m_mla_q_norm, m_mla_w_uq, m_mla_kv_norm, m_mla_w_ukv, m_mla_w_out, m_ple_norm, m_ple_w_gate, m_ple_w_proj, v_pre_norm, v_post_norm, v_pool_w_in, v_pool_w_group, v_pool_scale, v_pool_w_out, v_mla_w_in, v_mla_q_norm, v_mla_w_uq, v_mla_kv_norm, v_mla_w_ukv, v_mla_w_out, v_ple_norm, v_ple_w_gate, v_ple_w_proj):
    given = dict(x=x, p=p, positions=positions, pre_norm=pre_norm, post_norm=post_norm, pool_w_in=pool_w_in, pool_w_group=pool_w_group, pool_scale=pool_scale, pool_w_out=pool_w_out, mla_w_in=mla_w_in, mla_q_norm=mla_q_norm, mla_w_uq=mla_w_uq, mla_kv_norm=mla_kv_norm, mla_w_ukv=mla_w_ukv, mla_w_out=mla_w_out, ple_norm=ple_norm, ple_w_gate=ple_w_gate, ple_w_proj=ple_w_proj, loss_target=loss_target, m_pre_norm=m_pre_norm, m_post_norm=m_post_norm, m_pool_w_in=m_pool_w_in, m_pool_w_group=m_pool_w_group, m_pool_scale=m_pool_scale, m_pool_w_out=m_pool_w_out, m_mla_w_in=m_mla_w_in, m_mla_q_norm=m_mla_q_norm, m_mla_w_uq=m_mla_w_uq, m_mla_kv_norm=m_mla_kv_norm, m_mla_w_ukv=m_mla_w_ukv, m_mla_w_out=m_mla_w_out, m_ple_norm=m_ple_norm, m_ple_w_gate=m_ple_w_gate, m_ple_w_proj=m_ple_w_proj, v_pre_norm=v_pre_norm, v_post_norm=v_post_norm, v_pool_w_in=v_pool_w_in, v_pool_w_group=v_pool_w_group, v_pool_scale=v_pool_scale, v_pool_w_out=v_pool_w_out, v_mla_w_in=v_mla_w_in, v_mla_q_norm=v_mla_q_norm, v_mla_w_uq=v_mla_w_uq, v_mla_kv_norm=v_mla_kv_norm, v_mla_w_ukv=v_mla_w_ukv, v_mla_w_out=v_mla_w_out, v_ple_norm=v_ple_norm, v_ple_w_gate=v_ple_w_gate, v_ple_w_proj=v_ple_w_proj)
    weights = {n: given[n] for n in TWIN_WEIGHTS}
    shared = {n: given[n] for n in SHARED_INPUTS}
    per_example = {n: given[n] for n in ['x', 'p', 'positions']}
    grad_fn = _jax.value_and_grad(_loss, argnums=(0, 1))

    def one_microbatch(ex, loss_target):
        ex = dict(ex)
        diff = ex.pop(TWIN_DIFF_INPUT)
        return grad_fn(weights, diff, {**shared, **ex}, loss_target)

    if N_MICROBATCH == 1:
        loss, (grad_w, grad_x) = one_microbatch(per_example, given["loss_target"])
    else:
        def body(carry, xs):
            loss_sum, grad_sum = carry
            l_k, (gw_k, gx_k) = one_microbatch(xs[0], xs[1])
            with _jax.named_scope("update"):
                return (loss_sum + l_k, _jax.tree.map(_jnp.add, grad_sum, gw_k)), gx_k

        init = (_jnp.zeros((), _jnp.float32), _jax.tree.map(_jnp.zeros_like, weights))
        (loss, grad_w), grad_x = _jax.lax.scan(body, init, (per_example, given["loss_target"]))
    with _jax.named_scope("update"):
        delta_w, new_m, new_v = {}, {}, {}
        for n in TWIN_WEIGHTS:
            delta_w[n], new_m[n], new_v[n] = _adamw(weights[n], grad_w[n], given["m_" + n], given["v_" + n])
    return (loss, grad_x, *[grad_w[n] for n in TWIN_WEIGHTS], *[delta_w[n] for n in TWIN_WEIGHTS],
            *[new_m[n] for n in TWIN_WEIGHTS], *[new_v[n] for n in TWIN_WEIGHTS])
```

```python
import functools
import math

import jax
import jax.numpy as jnp
from jax import lax
from jax.experimental import pallas as pl
from jax.experimental.pallas import tpu as pltpu

F32 = jnp.float32
BF16 = jnp.bfloat16

N_DEV = 8
EPS = 1e-6
ROPE_THETA = 10000.0
NOPE_DIM = 128
ROPE_DIM = 64
V_DIM = 128
HEAD_PAD = 256
LANE = 128
POOL_WINDOWS = (2, 4, 8, 16)
POOL_HALO = 16
NEG_INF = -1e30
ADAM_LR = 0.001
ADAM_B1 = 0.9
ADAM_B2 = 0.999
ADAM_EPS = 1e-08
ADAM_WD = 0.01
ADAM_STEP = 10
VMEM_LIMIT_BYTES = 56 * 1024 * 1024
MESH = pl.DeviceIdType.MESH

SHARD_AXIS = dict(pre_norm=None, post_norm=None, pool_w_in=2, pool_w_group=2, pool_scale=None, pool_w_out=1,
                  mla_w_in=2, mla_q_norm=1, mla_w_uq=2, mla_kv_norm=1, mla_w_ukv=2, mla_w_out=1,
                  ple_norm=None, ple_w_gate=1, ple_w_proj=2)
WEIGHTS = tuple(SHARD_AXIS)
BIG = ('pool_w_in', 'pool_w_group', 'pool_w_out', 'mla_w_in', 'mla_w_uq', 'mla_w_ukv', 'mla_w_out',
       'ple_w_gate', 'ple_w_proj')
SMALL_REPL = ('pre_norm', 'post_norm', 'pool_scale', 'ple_norm')
SMALL_SHARD = ('mla_q_norm', 'mla_kv_norm')


def _pcall(body, **kw):
    return pl.pallas_call(body, **kw)


def _cparams(*sem):
    return pltpu.CompilerParams(dimension_semantics=sem, vmem_limit_bytes=VMEM_LIMIT_BYTES)


def _pick(n, cands):
    for c in cands:
        if n % c == 0:
            return c
    return n


def _sigmoid(x):
    return 1.0 / (1.0 + jnp.exp(-x))


def mm(name, a, b, mode, out_dtype=F32):
    squeeze = a.ndim == 2
    if squeeze:
        a, b = a[None], b[None]
    G = a.shape[0]
    if mode == 'nn':
        M, K = a.shape[1:]
        N = b.shape[2]
    elif mode == 'tn':
        K, M = a.shape[1:]
        N = b.shape[2]
    else:
        M, K = a.shape[1:]
        N = b.shape[1]
    tm = _pick(M, (1024, 512, 256, 128))
    tn = _pick(N, (1024, 768, 640, 512, 384, 256, 128))
    tk = _pick(K, (512, 640, 384, 256, 128))
    nk = K // tk
    if mode == 'nn':
        a_spec = pl.BlockSpec((None, tm, tk), lambda g, i, j, k: (g, i, k))
        b_spec = pl.BlockSpec((None, tk, tn), lambda g, i, j, k: (g, k, j))
        dims = (((1,), (0,)), ((), ()))
    elif mode == 'tn':
        a_spec = pl.BlockSpec((None, tk, tm), lambda g, i, j, k: (g, k, i))
        b_spec = pl.BlockSpec((None, tk, tn), lambda g, i, j, k: (g, k, j))
        dims = (((0,), (0,)), ((), ()))
    else:
        a_spec = pl.BlockSpec((None, tm, tk), lambda g, i, j, k: (g, i, k))
        b_spec = pl.BlockSpec((None, tn, tk), lambda g, i, j, k: (g, j, k))
        dims = (((1,), (1,)), ((), ()))

    def body(a_ref, b_ref, o_ref, acc_ref):
        k = pl.program_id(3)

        @pl.when(k == 0)
        def _():
            acc_ref[...] = jnp.zeros_like(acc_ref)

        acc_ref[...] += lax.dot_general(a_ref[...].astype(BF16), b_ref[...].astype(BF16), dims,
                                        preferred_element_type=F32)

        @pl.when(k == nk - 1)
        def _():
            o_ref[...] = acc_ref[...].astype(out_dtype)

    out = _pcall(
        body, name=name,
        out_shape=jax.ShapeDtypeStruct((G, M, N), out_dtype),
        grid=(G, M // tm, N // tn, nk),
        in_specs=[a_spec, b_spec],
        out_specs=pl.BlockSpec((None, tm, tn), lambda g, i, j, k: (g, i, j)),
        scratch_shapes=[pltpu.VMEM((tm, tn), F32)],
        compiler_params=_cparams("parallel", "parallel", "parallel", "arbitrary"),
    )(a, b)
    return out[0] if squeeze else out


def rows(ts, width, colblk=0):
    return pl.BlockSpec((ts, width), lambda i: (i, colblk))


def whole(shape):
    return pl.BlockSpec(shape, lambda i: (0,) * len(shape))


def rowwise(name, fn, S, ts, ins, outs, accs=(), scratch=(), reverse=False):
    n_in, n_out, n_acc = len(ins), len(outs), len(accs)
    nt = S // ts

    def body(*refs):
        step = pl.program_id(0)
        i = nt - 1 - step if reverse else step
        in_refs = refs[:n_in]
        out_refs = refs[n_in:n_in + n_out]
        acc_refs = refs[n_in + n_out:n_in + n_out + n_acc]
        scr = refs[n_in + n_out + n_acc:]

        @pl.when(step == 0)
        def _():
            for r in acc_refs:
                r[...] = jnp.zeros_like(r)

        fn(i, step, in_refs, out_refs, acc_refs, scr)

    def fix(spec):
        if not reverse:
            return spec
        imap = spec.index_map
        return pl.BlockSpec(spec.block_shape, lambda s: imap(nt - 1 - s))

    res = _pcall(
        body, name=name,
        out_shape=[jax.ShapeDtypeStruct(s, d) for s, d, _ in outs] + [jax.ShapeDtypeStruct(s, d) for s, d in accs],
        grid=(nt,),
        in_specs=[fix(sp) for _, sp in ins],
        out_specs=[fix(sp) for _, _, sp in outs] + [whole(s) for s, _ in accs],
        scratch_shapes=list(scratch),
        compiler_params=_cparams("arbitrary"),
    )(*[a for a, _ in ins])
    return res


def _rstd(x):
    return lax.rsqrt(jnp.mean(x * x, axis=-1, keepdims=True) + EPS)


def _rms_bwd(dy, x, g):
    r = _rstd(x)
    xh = x * r
    gdy = dy * g
    dx = r * (gdy - xh * jnp.mean(xh * gdy, axis=-1, keepdims=True))
    return dx, jnp.sum(dy * xh, axis=0, keepdims=True)


def _rope(v, cos_t, sin_a, sin_b, sign):
    return v * cos_t + sign * (pltpu.roll(v, LANE - ROPE_DIM // 2, axis=1) * sin_a
                               + pltpu.roll(v, ROPE_DIM // 2, axis=1) * sin_b)


def rms_fwd(name, h, gain, S, D, ts):
    def fn(i, step, ins, outs, accs, scr):
        x = ins[0][...]
        outs[0][...] = (x * _rstd(x) * ins[1][...]).astype(BF16)
    return rowwise(name, fn, S, ts, [(h, rows(ts, D)), (gain, whole((1, D)))], [((S, D), BF16, rows(ts, D))])[0]


def post_fwd(name, h, out, post_g, ple_g, S, D, ts):
    def fn(i, step, ins, outs, accs, scr):
        o = ins[1][...]
        h1 = ins[0][...] + o * _rstd(o) * ins[2][...]
        outs[0][...] = h1
        outs[1][...] = (h1 * _rstd(h1) * ins[3][...]).astype(BF16)
    return rowwise(name, fn, S, ts,
                   [(h, rows(ts, D)), (out, rows(ts, D)), (post_g, whole((1, D))), (ple_g, whole((1, D)))],
                   [((S, D), F32, rows(ts, D)), ((S, D), BF16, rows(ts, D))])


def ple_fwd(name, h1, pp, gl, S, D, ts):
    def fn(i, step, ins, outs, accs, scr):
        outs[0][...] = ins[0][...] + ins[1][...] * _sigmoid(ins[2][...])
    return rowwise(name, fn, S, ts, [(h1, rows(ts, D)), (pp, rows(ts, D)), (gl, rows(ts, D))],
                   [((S, D), F32, rows(ts, D))])[0]


def loss_fwd_bwd(name, h, tgt, S, D, ts):
    def fn(i, step, ins, outs, accs, scr):
        e = ins[0][...] - ins[1][...]
        outs[0][...] = e * (1.0 / D)
        accs[0][...] += jnp.broadcast_to(jnp.sum(e * e), (1, LANE))
    return rowwise(name, fn, S, ts, [(h, rows(ts, D)), (tgt, rows(ts, D))], [((S, D), F32, rows(ts, D))],
                   accs=[((1, LANE), F32)])


def ple_bwd(name, dh, pp, gl, S, D, ts):
    def fn(i, step, ins, outs, accs, scr):
        d = ins[0][...]
        gate = _sigmoid(ins[2][...])
        outs[0][...] = (d * gate).astype(BF16)
        outs[1][...] = (d * ins[1][...] * gate * (1.0 - gate)).astype(BF16)
    return rowwise(name, fn, S, ts, [(dh, rows(ts, D)), (pp, rows(ts, D)), (gl, rows(ts, D))],
                   [((S, D), BF16, rows(ts, D)), ((S, D), BF16, rows(ts, D))])


def post_bwd(name, da, dh, h1, out, post_g, ple_g, S, D, ts):
    def fn(i, step, ins, outs, accs, scr):
        dx, dple = _rms_bwd(ins[0][...], ins[2][...], ins[5][...])
        dh1 = ins[1][...] + dx
        dout, dpost = _rms_bwd(dh1, ins[3][...], ins[4][...])
        outs[0][...] = dh1
        outs[1][...] = dout.astype(BF16)
        accs[0][...] += dpost
        accs[1][...] += dple
    return rowwise(name, fn, S, ts,
                   [(da, rows(ts, D)), (dh, rows(ts, D)), (h1, rows(ts, D)), (out, rows(ts, D)),
                    (post_g, whole((1, D))), (ple_g, whole((1, D)))],
                   [((S, D), F32, rows(ts, D)), ((S, D), BF16, rows(ts, D))],
                   accs=[((1, D), F32), ((1, D), F32)])


def pre_bwd(name, dxn, dh1, h, pre_g, S, D, ts):
    def fn(i, step, ins, outs, accs, scr):
        dx, dpre = _rms_bwd(ins[0][...], ins[2][...], ins[3][...])
        outs[0][...] = ins[1][...] + dx
        accs[0][...] += dpre
    return rowwise(name, fn, S, ts,
                   [(dxn, rows(ts, D)), (dh1, rows(ts, D)), (h, rows(ts, D)), (pre_g, whole((1, D)))],
                   [((S, D), F32, rows(ts, D))], accs=[((1, D), F32)])


def _window_sums(ext, w, back):
    n = ext.shape[0]
    s, win = ext, 1
    while win < w:
        s = s + pltpu.roll(s, win if back else n - win, axis=0)
        win *= 2
    return s


def pool_fwd(name, z, S, E, NG, ts):
    G = E // NG

    def fn(i, step, ins, outs, accs, scr):
        carry = scr[0]

        @pl.when(step == 0)
        def _():
            carry[...] = jnp.zeros_like(carry)

        t = i * ts + lax.broadcasted_iota(jnp.int32, (ts, 1), 0)
        for j, w in enumerate(POOL_WINDOWS):
            u = ins[0][:, j * G:(j + 1) * G]
            ext = jnp.concatenate([carry[:, j * G:(j + 1) * G], u], axis=0)
            sw = _window_sums(ext, w, True)[POOL_HALO:, :]
            cnt = jnp.minimum(t + 1, w).astype(F32)
            outs[0][j] = (sw / cnt - u).astype(BF16)
        carry[...] = ins[0][ts - POOL_HALO:, :]

    return rowwise(name, fn, S, ts, [(z, rows(ts, E, 0))],
                   [((NG, S, G), BF16, pl.BlockSpec((NG, ts, G), lambda i: (0, i, 0)))],
                   scratch=[pltpu.VMEM((POOL_HALO, E), F32)])[0]


def pool_bwd(name, dpooled, dg, S, E, NG, ts):
    G = E // NG

    def fn(i, step, ins, outs, accs, scr):
        carry = scr[0]

        @pl.when(step == 0)
        def _():
            carry[...] = jnp.zeros_like(carry)

        t = i * ts + lax.broadcasted_iota(jnp.int32, (ts, 1), 0)
        for j, w in enumerate(POOL_WINDOWS):
            d = ins[0][j]
            e = d / jnp.minimum(t + 1, w).astype(F32)
            ext = jnp.concatenate([e, carry[:, j * G:(j + 1) * G]], axis=0)
            sw = _window_sums(ext, w, False)[:ts, :]
            outs[0][:, j * G:(j + 1) * G] = (sw - d).astype(BF16)
            carry[:, j * G:(j + 1) * G] = e[:POOL_HALO, :]
        outs[0][:, E:] = ins[1][...]

    return rowwise(name, fn, S, ts,
                   [(dpooled, pl.BlockSpec((NG, ts, G), lambda i: (0, i, 0))), (dg, rows(ts, E))],
                   [((S, 2 * E), BF16, rows(ts, 2 * E))],
                   scratch=[pltpu.VMEM((POOL_HALO, E), F32)], reverse=True)[0]


def pool_gate_fwd(name, mixed, z, scale, S, E, NG, ts):
    G = E // NG

    def fn(i, step, ins, outs, accs, scr):
        for j in range(NG):
            sl = slice(j * G, (j + 1) * G)
            g = ins[1][:, sl]
            outs[0][:, sl] = (ins[0][j] * ins[2][:, sl] * (g * _sigmoid(g))).astype(BF16)

    return rowwise(name, fn, S, ts,
                   [(mixed, pl.BlockSpec((NG, ts, G), lambda i: (0, i, 0))), (z, rows(ts, E, 1)),
                    (scale, whole((1, E)))],
                   [((S, E), BF16, rows(ts, E))])[0]


def pool_gate_bwd(name, dy, mixed, z, scale, S, E, NG, ts):
    G = E // NG

    def fn(i, step, ins, outs, accs, scr):
        for j in range(NG):
            sl = slice(j * G, (j + 1) * G)
            d = ins[0][:, sl]
            mx = ins[1][j]
            g = ins[2][:, sl]
            sc = ins[3][:, sl]
            sg = _sigmoid(g)
            si = g * sg
            outs[0][j] = (d * sc * si).astype(BF16)
            outs[1][:, sl] = (d * mx * sc * (sg * (1.0 + g * (1.0 - sg)))).astype(BF16)
            accs[0][:, sl] += jnp.sum(d * mx * si, axis=0, keepdims=True)

    return rowwise(name, fn, S, ts,
                   [(dy, rows(ts, E)), (mixed, pl.BlockSpec((NG, ts, G), lambda i: (0, i, 0))),
                    (z, rows(ts, E, 1)), (scale, whole((1, E)))],
                   [((NG, S, G), BF16, pl.BlockSpec((NG, ts, G), lambda i: (0, i, 0))), ((S, E), BF16, rows(ts, E))],
                   accs=[((1, E), F32)])


def mla_prep_fwd(name, z, qg, kvg, tabs, S, E, R, ts):
    qb, kb, pb = E // R, E // R + 1, (E + 2 * R) // LANE

    def fn(i, step, ins, outs, accs, scr):
        zq, zkv = ins[0][...], ins[1][...]
        outs[0][...] = (zq * _rstd(zq) * ins[3][...]).astype(BF16)
        outs[1][...] = (zkv * _rstd(zkv) * ins[4][...]).astype(BF16)
        outs[2][...] = _rope(ins[2][...], ins[5][...], ins[6][...], ins[7][...], 1.0)

    return rowwise(name, fn, S, ts,
                   [(z, rows(ts, R, qb)), (z, rows(ts, R, kb)), (z, rows(ts, LANE, pb)),
                    (qg, whole((1, R))), (kvg, whole((1, R)))] + [(t, rows(ts, LANE)) for t in tabs],
                   [((S, R), BF16, rows(ts, R)), ((S, R), BF16, rows(ts, R)), ((S, LANE), F32, rows(ts, LANE))])


def mla_pack_fwd(name, q_raw, k_raw, kper, tabs, S, H, ts, scale):
    W = H * HEAD_PAD

    def fn(i, step, ins, outs, accs, scr):
        cos_t, sin_a, sin_b = ins[3][...], ins[4][...], ins[5][...]
        kp = ins[2][...].astype(BF16)
        for h in range(H):
            a, b, c = h * HEAD_PAD, h * HEAD_PAD + NOPE_DIM, (h + 1) * HEAD_PAD
            outs[0][:, a:b] = (ins[0][:, a:b] * scale).astype(BF16)
            outs[0][:, b:c] = (_rope(ins[0][:, b:c], cos_t, sin_a, sin_b, 1.0) * scale).astype(BF16)
            outs[1][:, a:b] = ins[1][:, a:b].astype(BF16)
            outs[1][:, b:c] = kp

    return rowwise(name, fn, S, ts,
                   [(q_raw, rows(ts, W)), (k_raw, rows(ts, W)), (kper, rows(ts, LANE))]
                   + [(t, rows(ts, LANE)) for t in tabs],
                   [((S, W), BF16, rows(ts, W)), ((S, W), BF16, rows(ts, W))])


def mla_gate_fwd(name, o, z, S, E, ts):
    def fn(i, step, ins, outs, accs, scr):
        g = ins[1][...]
        outs[0][...] = (ins[0][...] * (g * _sigmoid(g))).astype(BF16)
    return rowwise(name, fn, S, ts, [(o, rows(ts, E)), (z, rows(ts, E, 0))], [((S, E), BF16, rows(ts, E))])[0]


def mla_gate_bwd(name, dy, o, z, S, E, H, ts):
    def fn(i, step, ins, outs, accs, scr):
        for h in range(H):
            sl = slice(h * V_DIM, (h + 1) * V_DIM)
            d, ov, g = ins[0][:, sl], ins[1][:, sl], ins[2][:, sl]
            sg = _sigmoid(g)
            do = d * (g * sg)
            outs[0][:, sl] = do.astype(BF16)
            outs[1][:, sl] = (d * ov * (sg * (1.0 + g * (1.0 - sg)))).astype(BF16)
            outs[2][:, sl] = jnp.broadcast_to(jnp.sum(do * ov, axis=1, keepdims=True), (ts, V_DIM))

    return rowwise(name, fn, S, ts, [(dy, rows(ts, E)), (o, rows(ts, E)), (z, rows(ts, E, 0))],
                   [((S, E), BF16, rows(ts, E)), ((S, E), BF16, rows(ts, E)), ((S, E), F32, rows(ts, E))])


def mla_unpack_bwd(name, dq, dk, tabs, S, H, ts, scale):
    W = H * HEAD_PAD

    def fn(i, step, ins, outs, accs, scr):
        cos_t, sin_a, sin_b = ins[2][...], ins[3][...], ins[4][...]
        dkpe = jnp.zeros((ts, LANE), F32)
        for h in range(H):
            a, b, c = h * HEAD_PAD, h * HEAD_PAD + NOPE_DIM, (h + 1) * HEAD_PAD
            outs[0][:, a:b] = (ins[0][:, a:b] * scale).astype(BF16)
            outs[0][:, b:c] = (_rope(ins[0][:, b:c], cos_t, sin_a, sin_b, -1.0) * scale).astype(BF16)
            outs[1][:, a:b] = ins[1][:, a:b].astype(BF16)
            outs[1][:, b:c] = jnp.zeros((ts, LANE), BF16)
            dkpe = dkpe + ins[1][:, b:c]
        outs[2][...] = dkpe

    return rowwise(name, fn, S, ts,
                   [(dq, rows(ts, W)), (dk, rows(ts, W))] + [(t, rows(ts, LANE)) for t in tabs],
                   [((S, W), BF16, rows(ts, W)), ((S, W), BF16, rows(ts, W)), ((S, LANE), F32, rows(ts, LANE))])


def mla_prep_bwd(name, dqn, dkvn_k, dkvn_v, z, dkpe, dg, qg, kvg, tabs, S, E, R, ts):
    qb, kb = E // R, E // R + 1
    ZW = E + 2 * R + LANE

    def fn(i, step, ins, outs, accs, scr):
        dzq, dqg = _rms_bwd(ins[0][...], ins[3][...], ins[7][...])
        dzkv, dkvg = _rms_bwd(ins[1][...] + ins[2][...], ins[4][...], ins[8][...])
        outs[0][:, :E] = ins[6][...]
        outs[0][:, E:E + R] = dzq.astype(BF16)
        outs[0][:, E + R:E + 2 * R] = dzkv.astype(BF16)
        outs[0][:, E + 2 * R:] = _rope(ins[5][...], ins[9][...], ins[10][...], ins[11][...], -1.0).astype(BF16)
        accs[0][...] += dqg
        accs[1][...] += dkvg

    return rowwise(name, fn, S, ts,
                   [(dqn, rows(ts, R)), (dkvn_k, rows(ts, R)), (dkvn_v, rows(ts, R)), (z, rows(ts, R, qb)),
                    (z, rows(ts, R, kb)), (dkpe, rows(ts, LANE)), (dg, rows(ts, E)),
                    (qg, whole((1, R))), (kvg, whole((1, R)))] + [(t, rows(ts, LANE)) for t in tabs],
                   [((S, ZW), BF16, rows(ts, ZW))], accs=[((1, R), F32), ((1, R), F32)])


_NT = (((1,), (1,)), ((), ()))
_TN = (((0,), (0,)), ((), ()))
_NN = (((1,), (0,)), ((), ()))


def _causal_mask(t):
    return lax.broadcasted_iota(jnp.int32, (t, t), 1) <= lax.broadcasted_iota(jnp.int32, (t, t), 0)


def flash_fwd(name, q, k, v, S, H, t):
    nt = S // t

    def body(q_ref, k_ref, v_ref, o_ref, lse_ref, m_sc, l_sc, acc_sc):
        qi, ki = pl.program_id(1), pl.program_id(2)

        @pl.when(ki == 0)
        def _():
            m_sc[...] = jnp.full_like(m_sc, NEG_INF)
            l_sc[...] = jnp.zeros_like(l_sc)
            acc_sc[...] = jnp.zeros_like(acc_sc)

        def step(diag):
            s = lax.dot_general(q_ref[...], k_ref[...], _NT, preferred_element_type=F32)
            if diag:
                s = jnp.where(_causal_mask(t), s, NEG_INF)
            m_prev = m_sc[...]
            m_new = jnp.maximum(m_prev, jnp.max(s, axis=1, keepdims=True))
            alpha = jnp.exp(m_prev - m_new)
            p = jnp.exp(s - m_new)
            l_sc[...] = alpha * l_sc[...] + jnp.sum(p, axis=1, keepdims=True)
            acc_sc[...] = alpha * acc_sc[...] + lax.dot_general(p.astype(BF16), v_ref[...], _NN,
                                                                 preferred_element_type=F32)
            m_sc[...] = m_new

        @pl.when(ki < qi)
        def _():
            step(False)

        @pl.when(ki == qi)
        def _():
            step(True)
            l = l_sc[...]
            o_ref[...] = acc_sc[...] / l
            lse_ref[...] = jnp.broadcast_to(m_sc[...] + jnp.log(l), (t, V_DIM))

    return _pcall(
        body, name=name,
        out_shape=[jax.ShapeDtypeStruct((S, H * V_DIM), F32), jax.ShapeDtypeStruct((S, H * V_DIM), F32)],
        grid=(H, nt, nt),
        in_specs=[pl.BlockSpec((t, HEAD_PAD), lambda h, i, j: (i, h)),
                  pl.BlockSpec((t, HEAD_PAD), lambda h, i, j: (jnp.minimum(i, j), h)),
                  pl.BlockSpec((t, V_DIM), lambda h, i, j: (jnp.minimum(i, j), h))],
        out_specs=[pl.BlockSpec((t, V_DIM), lambda h, i, j: (i, h)), pl.BlockSpec((t, V_DIM), lambda h, i, j: (i, h))],
        scratch_shapes=[pltpu.VMEM((t, 1), F32), pltpu.VMEM((t, 1), F32), pltpu.VMEM((t, V_DIM), F32)],
        compiler_params=_cparams("parallel", "parallel", "arbitrary"),
    )(q, k, v)


def _p_and_ds(q, k, v, do, lse, delta, t, diag):
    s = lax.dot_general(q, k, _NT, preferred_element_type=F32)
    p = jnp.exp(s - lse[:, :1])
    if diag:
        p = jnp.where(_causal_mask(t), p, 0.0)
    dp = lax.dot_general(do, v, _NT, preferred_element_type=F32)
    ds = p * (dp - delta[:, :1])
    return p, ds


def flash_bwd_dq(name, q, k, v, do, lse, delta, S, H, t):
    nt = S // t

    def body(q_ref, k_ref, v_ref, do_ref, lse_ref, dl_ref, dq_ref, acc_sc):
        qi, ki = pl.program_id(1), pl.program_id(2)

        @pl.when(ki == 0)
        def _():
            acc_sc[...] = jnp.zeros_like(acc_sc)

        def step(diag):
            _, ds = _p_and_ds(q_ref[...], k_ref[...], v_ref[...], do_ref[...], lse_ref[...], dl_ref[...], t, diag)
            acc_sc[...] += lax.dot_general(ds.astype(BF16), k_ref[...], _NN, preferred_element_type=F32)

        @pl.when(ki < qi)
        def _():
            step(False)

        @pl.when(ki == qi)
        def _():
            step(True)
            dq_ref[...] = acc_sc[...]

    qspec = lambda w: pl.BlockSpec((t, w), lambda h, i, j: (i, h))
    kspec = lambda w: pl.BlockSpec((t, w), lambda h, i, j: (jnp.minimum(i, j), h))
    return _pcall(
        body, name=name,
        out_shape=jax.ShapeDtypeStruct((S, H * HEAD_PAD), F32),
        grid=(H, nt, nt),
        in_specs=[qspec(HEAD_PAD), kspec(HEAD_PAD), kspec(V_DIM), qspec(V_DIM), qspec(V_DIM), qspec(V_DIM)],
        out_specs=qspec(HEAD_PAD),
        scratch_shapes=[pltpu.VMEM((t, HEAD_PAD), F32)],
        compiler_params=_cparams("parallel", "parallel", "arbitrary"),
    )(q, k, v, do, lse, delta)


def flash_bwd_dkv(name, q, k, v, do, lse, delta, S, H, t):
    nt = S // t

    def body(q_ref, k_ref, v_ref, do_ref, lse_ref, dl_ref, dk_ref, dv_ref, dk_sc, dv_sc):
        ki, qi = pl.program_id(1), pl.program_id(2)

        @pl.when(qi == 0)
        def _():
            dk_sc[...] = jnp.zeros_like(dk_sc)
            dv_sc[...] = jnp.zeros_like(dv_sc)

        def step(diag):
            p, ds = _p_and_ds(q_ref[...], k_ref[...], v_ref[...], do_ref[...], lse_ref[...], dl_ref[...], t, diag)
            dv_sc[...] += lax.dot_general(p.astype(BF16), do_ref[...], _TN, preferred_element_type=F32)
            dk_sc[...] += lax.dot_general(ds.astype(BF16), q_ref[...], _TN, preferred_element_type=F32)

        @pl.when(qi > ki)
        def _():
            step(False)

        @pl.when(qi == ki)
        def _():
            step(True)

        @pl.when(qi == nt - 1)
        def _():
            dk_ref[...] = dk_sc[...]
            dv_ref[...] = dv_sc[...]

    qspec = lambda w: pl.BlockSpec((t, w), lambda h, j, i: (jnp.maximum(i, j), h))
    kspec = lambda w: pl.BlockSpec((t, w), lambda h, j, i: (j, h))
    return _pcall(
        body, name=name,
        out_shape=[jax.ShapeDtypeStruct((S, H * HEAD_PAD), F32), jax.ShapeDtypeStruct((S, H * V_DIM), F32)],
        grid=(H, nt, nt),
        in_specs=[qspec(HEAD_PAD), kspec(HEAD_PAD), kspec(V_DIM), qspec(V_DIM), qspec(V_DIM), qspec(V_DIM)],
        out_specs=[kspec(HEAD_PAD), kspec(V_DIM)],
        scratch_shapes=[pltpu.VMEM((t, HEAD_PAD), F32), pltpu.VMEM((t, V_DIM), F32)],
        compiler_params=_cparams("parallel", "parallel", "arbitrary"),
    )(q, k, v, do, lse, delta)


def _peers():
    x, y, c = lax.axis_index("x"), lax.axis_index("y"), lax.axis_index("c")
    me = 4 * x + 2 * y + c
    peers = []
    for fx, fy, fc in ((0, 0, 1), (1, 0, 0), (0, 1, 0), (1, 1, 0), (1, 0, 1), (0, 1, 1), (1, 1, 1)):
        px, py, pc = x ^ fx, y ^ fy, c ^ fc
        peers.append(((px, py, pc), 4 * px + 2 * py + pc))
    return me, peers


def _hbm_specs(n):
    return [pl.BlockSpec(memory_space=pl.ANY)] * n


def exchange(name, arrs, gather):
    n = len(arrs)
    out_shapes = [jax.ShapeDtypeStruct((N_DEV,) + a.shape if gather else a.shape, a.dtype) for a in arrs]

    def body(*refs):
        srcs, dsts = refs[:n], refs[n:2 * n]
        send_sems, recv_sems, local_sems = refs[2 * n:]
        me, peers = _peers()

        def remote(a, k):
            peer, pid = peers[k]
            return pltpu.make_async_remote_copy(
                src_ref=srcs[a] if gather else srcs[a].at[pid], dst_ref=dsts[a].at[me],
                send_sem=send_sems.at[a, k], recv_sem=recv_sems.at[a, k], device_id=peer, device_id_type=MESH)

        def arrival(a, k):
            peer, pid = peers[k]
            return pltpu.make_async_remote_copy(
                src_ref=srcs[a] if gather else srcs[a].at[pid], dst_ref=dsts[a].at[pid],
                send_sem=send_sems.at[a, k], recv_sem=recv_sems.at[a, k], device_id=peer, device_id_type=MESH)

        local = [pltpu.make_async_copy(srcs[a] if gather else srcs[a].at[me], dsts[a].at[me], local_sems.at[a])
                 for a in range(n)]
        for a in range(n):
            local[a].start()
            for k in range(N_DEV - 1):
                remote(a, k).start()
        for a in range(n):
            for k in range(N_DEV - 1):
                arrival(a, k).wait_recv()
        for a in range(n):
            for k in range(N_DEV - 1):
                remote(a, k).wait_send()
            local[a].wait()

    return _pcall(
        body, name=name, out_shape=out_shapes, in_specs=_hbm_specs(n), out_specs=_hbm_specs(n),
        scratch_shapes=[pltpu.SemaphoreType.DMA((n, N_DEV - 1)), pltpu.SemaphoreType.DMA((n, N_DEV - 1)),
                        pltpu.SemaphoreType.DMA((n,))],
    )(*arrs)


def adamw(name, gslots, w, m, v):
    K, R, C = gslots.shape
    per_row = C * (K * gslots.dtype.itemsize + 7 * 4) * 2
    tr = R
    for cand in (1024, 512, 256, 128, 64, 32, 16, 8):
        if R % cand == 0:
            tr = cand
            if cand * per_row <= VMEM_LIMIT_BYTES // 2:
                break
    c1 = 1.0 / (1.0 - ADAM_B1 ** ADAM_STEP)
    c2 = 1.0 / (1.0 - ADAM_B2 ** ADAM_STEP)

    def body(g_ref, w_ref, m_ref, v_ref, go_ref, d_ref, mo_ref, vo_ref):
        g = g_ref[0].astype(F32)
        for s in range(1, K):
            g = g + g_ref[s].astype(F32)
        mn = ADAM_B1 * m_ref[...] + (1.0 - ADAM_B1) * g
        vn = ADAM_B2 * v_ref[...] + (1.0 - ADAM_B2) * (g * g)
        go_ref[...] = g
        mo_ref[...] = mn
        vo_ref[...] = vn
        d_ref[...] = -ADAM_LR * ((mn * c1) / (jnp.sqrt(vn * c2) + ADAM_EPS) + ADAM_WD * w_ref[...])

    blk = pl.BlockSpec((tr, C), lambda i: (i, 0))
    return _pcall(
        body, name=name, out_shape=[jax.ShapeDtypeStruct((R, C), F32)] * 4, grid=(R // tr,),
        in_specs=[pl.BlockSpec((K, tr, C), lambda i: (0, i, 0)), blk, blk, blk], out_specs=[blk] * 4,
        compiler_params=_cparams("parallel"),
    )(gslots, w, m, v)


def _from_slots(gathered, ax):
    g = jnp.moveaxis(gathered, 0, ax)
    s = g.shape
    return g.reshape(s[:ax] + (s[ax] * s[ax + 1],) + s[ax + 2:])


def _to_slots(full, ax):
    s = full.shape
    g = full.reshape(s[:ax] + (N_DEV, s[ax] // N_DEV) + s[ax + 1:])
    g = jnp.moveaxis(g, ax, 0)
    return g.reshape(N_DEV, -1, g.shape[-1])


def _rope_tables(pos, S):
    inv_freq = ROPE_THETA ** (-jnp.arange(0, ROPE_DIM, 2, dtype=F32) / ROPE_DIM)
    ang = pos.astype(F32)[:, None] * inv_freq
    cos, sin = jnp.cos(ang), jnp.sin(ang)
    z = jnp.zeros((S, ROPE_DIM // 2), F32)
    cos_t = jnp.concatenate([cos, cos, z, z], axis=1)
    sin_a = jnp.concatenate([-sin, z, z, z], axis=1)
    sin_b = jnp.concatenate([z, sin, z, z], axis=1)
    return cos_t, sin_a, sin_b


def kernel(x, p, positions, pre_norm, post_norm, pool_w_in, pool_w_group, pool_scale, pool_w_out, mla_w_in, mla_q_norm, mla_w_uq, mla_kv_norm, mla_w_ukv, mla_w_out, ple_norm, ple_w_gate, ple_w_proj, loss_target, m_pre_norm, m_post_norm, m_pool_w_in, m_pool_w_group, m_pool_scale, m_pool_w_out, m_mla_w_in, m_mla_q_norm, m_mla_w_uq, m_mla_kv_norm, m_mla_w_ukv, m_mla_w_out, m_ple_norm, m_ple_w_gate, m_ple_w_proj, v_pre_norm, v_post_norm, v_pool_w_in, v_pool_w_group, v_pool_scale, v_pool_w_out, v_mla_w_in, v_mla_q_norm, v_mla_w_uq, v_mla_kv_norm, v_mla_w_ukv, v_mla_w_out, v_ple_norm, v_ple_w_gate, v_ple_w_proj):
    wl = dict(pre_norm=pre_norm, post_norm=post_norm, pool_w_in=pool_w_in, pool_w_group=pool_w_group,
              pool_scale=pool_scale, pool_w_out=pool_w_out, mla_w_in=mla_w_in, mla_q_norm=mla_q_norm,
              mla_w_uq=mla_w_uq, mla_kv_norm=mla_kv_norm, mla_w_ukv=mla_w_ukv, mla_w_out=mla_w_out,
              ple_norm=ple_norm, ple_w_gate=ple_w_gate, ple_w_proj=ple_w_proj)
    ml = dict(pre_norm=m_pre_norm, post_norm=m_post_norm, pool_w_in=m_pool_w_in, pool_w_group=m_pool_w_group,
              pool_scale=m_pool_scale, pool_w_out=m_pool_w_out, mla_w_in=m_mla_w_in, mla_q_norm=m_mla_q_norm,
              mla_w_uq=m_mla_w_uq, mla_kv_norm=m_mla_kv_norm, mla_w_ukv=m_mla_w_ukv, mla_w_out=m_mla_w_out,
              ple_norm=m_ple_norm, ple_w_gate=m_ple_w_gate, ple_w_proj=m_ple_w_proj)
    vl = dict(pre_norm=v_pre_norm, post_norm=v_post_norm, pool_w_in=v_pool_w_in, pool_w_group=v_pool_w_group,
              pool_scale=v_pool_scale, pool_w_out=v_pool_w_out, mla_w_in=v_mla_w_in, mla_q_norm=v_mla_q_norm,
              mla_w_uq=v_mla_w_uq, mla_kv_norm=v_mla_kv_norm, mla_w_ukv=v_mla_w_ukv, mla_w_out=v_mla_w_out,
              ple_norm=v_ple_norm, ple_w_gate=v_ple_w_gate, ple_w_proj=v_ple_w_proj)

    S, D = x.shape[1], x.shape[2]
    L = pre_norm.shape[0]
    E = pool_scale.shape[1]
    NG = pool_w_group.shape[1]
    R = mla_w_uq.shape[1]
    H = D // 128
    EM = H * V_DIM
    PD = p.shape[-1]
    me = 4 * lax.axis_index("x") + 2 * lax.axis_index("y") + lax.axis_index("c")
    ts = min(S, 256)
    tsw = min(S, 128)
    ta = min(S, 512)
    sm_scale = (NOPE_DIM + ROPE_DIM) ** -0.5

    small_sh = jnp.concatenate([wl[n].reshape(1, -1) for n in SMALL_SHARD], axis=1)
    gathered = exchange("gather_weights", [wl[n].astype(BF16) for n in BIG] + [small_sh], True)
    full = {n: _from_slots(g, SHARD_AXIS[n]) for n, g in zip(BIG, gathered[:-1])}
    nq = mla_q_norm.size
    q_norm = _from_slots(gathered[-1][:, 0, :nq].reshape((N_DEV,) + mla_q_norm.shape), 1)
    kv_norm = _from_slots(gathered[-1][:, 0, nq:].reshape((N_DEV,) + mla_kv_norm.shape), 1)

    w_in = full['mla_w_in']
    NML = w_in.shape[0]
    w_in_k = jnp.concatenate([w_in[:, :, 2 * R + ROPE_DIM:], w_in[:, :, :2 * R + ROPE_DIM],
                              jnp.zeros((NML, D, LANE - ROPE_DIM), BF16)], axis=2)
    w_uq_k = jnp.pad(full['mla_w_uq'].reshape(NML, R, H, NOPE_DIM + ROPE_DIM),
                     ((0, 0), (0, 0), (0, 0), (0, HEAD_PAD - NOPE_DIM - ROPE_DIM))).reshape(NML, R, H * HEAD_PAD)
    w_ukv = full['mla_w_ukv'].reshape(NML, R, H, NOPE_DIM + V_DIM)
    w_uk_k = jnp.pad(w_ukv[..., :NOPE_DIM], ((0, 0), (0, 0), (0, 0), (0, HEAD_PAD - NOPE_DIM))
                     ).reshape(NML, R, H * HEAD_PAD)
    w_uv_k = w_ukv[..., NOPE_DIM:].reshape(NML, R, H * V_DIM)
    tabs = _rope_tables(positions[0], S)

    h = x[0]
    saved = []
    for i in range(L):
        j = i // 2
        sv = dict(h=h)
        xn = rms_fwd(f"pre_norm_{i}", h, pre_norm[i:i + 1], S, D, ts)
        sv['xn'] = xn
        if i % 2 == 0:
            z = mm(f"pool_in_{i}", xn, full['pool_w_in'][j], 'nn')
            pooled = pool_fwd(f"pool_window_{i}", z, S, E, NG, tsw)
            mixed = mm(f"pool_group_{i}", pooled, full['pool_w_group'][j], 'nn')
            y = pool_gate_fwd(f"pool_gate_{i}", mixed, z, pool_scale[j:j + 1], S, E, NG, tsw)
            out = mm(f"pool_out_{i}", y, full['pool_w_out'][j], 'nn')
            sv.update(z=z, pooled=pooled, mixed=mixed, y=y)
        else:
            z = mm(f"mla_in_{i}", xn, w_in_k[j], 'nn')
            qn, kvn, kper = mla_prep_fwd(f"mla_prep_{i}", z, q_norm[j:j + 1], kv_norm[j:j + 1], tabs, S, EM, R, ts)
            q_raw = mm(f"mla_uq_{i}", qn, w_uq_k[j], 'nn')
            k_raw = mm(f"mla_uk_{i}", kvn, w_uk_k[j], 'nn')
            vv = mm(f"mla_uv_{i}", kvn, w_uv_k[j], 'nn', BF16)
            qp, kp = mla_pack_fwd(f"mla_pack_{i}", q_raw, k_raw, kper, tabs, S, H, tsw, sm_scale)
            o, lse = flash_fwd(f"attn_{i}", qp, kp, vv, S, H, ta)
            y = mla_gate_fwd(f"mla_gate_{i}", o, z, S, EM, ts)
            out = mm(f"mla_out_{i}", y, full['mla_w_out'][j], 'nn')
            sv.update(z=z, qn=qn, kvn=kvn, qp=qp, kp=kp, vv=vv, o=o, lse=lse, y=y)
        h1, a = post_fwd(f"post_norm_{i}", h, out, post_norm[i:i + 1], ple_norm[i:i + 1], S, D, ts)
        gl = mm(f"ple_gate_{i}", a, full['ple_w_gate'][i], 'nn')
        pp = mm(f"ple_proj_{i}", p[i, 0], full['ple_w_proj'][i], 'nn')
        h = ple_fwd(f"ple_{i}", h1, pp, gl, S, D, ts)
        sv.update(out=out, h1=h1, a=a, gl=gl, pp=pp)
        saved.append(sv)

    dh, loss_acc = loss_fwd_bwd("loss", h, loss_target[0], S, D, ts)
    loss = lax.psum(loss_acc[0, 0] * (0.5 / D), ("x", "y", "c"))

    gw = {n: [None] * wl[n].shape[0] for n in WEIGHTS}
    for i in reversed(range(L)):
        j = i // 2
        sv = saved[i]
        dpp, dgl = ple_bwd(f"ple_bwd_{i}", dh, sv['pp'], sv['gl'], S, D, ts)
        gw['ple_w_proj'][i] = mm(f"ple_proj_dw_{i}", p[i, 0], dpp, 'tn')
        gw['ple_w_gate'][i] = mm(f"ple_gate_dw_{i}", sv['a'], dgl, 'tn')
        da = mm(f"ple_gate_dx_{i}", dgl, full['ple_w_gate'][i], 'nt')
        dh1, dout, dpost, dple = post_bwd(f"post_norm_bwd_{i}", da, dh, sv['h1'], sv['out'],
                                          post_norm[i:i + 1], ple_norm[i:i + 1], S, D, ts)
        gw['post_norm'][i], gw['ple_norm'][i] = dpost[0], dple[0]
        xn = sv['xn']
        if i % 2 == 0:
            gw['pool_w_out'][j] = mm(f"pool_out_dw_{i}", sv['y'], dout, 'tn')
            dy = mm(f"pool_out_dx_{i}", dout, full['pool_w_out'][j], 'nt')
            dmixed, dg, dscale = pool_gate_bwd(f"pool_gate_bwd_{i}", dy, sv['mixed'], sv['z'], pool_scale[j:j + 1],
                                               S, E, NG, tsw)
            gw['pool_scale'][j] = dscale[0]
            gw['pool_w_group'][j] = mm(f"pool_group_dw_{i}", sv['pooled'], dmixed, 'tn')
            dpooled = mm(f"pool_group_dx_{i}", dmixed, full['pool_w_group'][j], 'nt')
            dz = pool_bwd(f"pool_window_bwd_{i}", dpooled, dg, S, E, NG, tsw)
            gw['pool_w_in'][j] = mm(f"pool_in_dw_{i}", xn, dz, 'tn')
            dxn = mm(f"pool_in_dx_{i}", dz, full['pool_w_in'][j], 'nt')
        else:
            gw['mla_w_out'][j] = mm(f"mla_out_dw_{i}", sv['y'], dout, 'tn')
            dy = mm(f"mla_out_dx_{i}", dout, full['mla_w_out'][j], 'nt')
            do, dg, delta = mla_gate_bwd(f"mla_gate_bwd_{i}", dy, sv['o'], sv['z'], S, EM, H, ts)
            dqp = flash_bwd_dq(f"attn_dq_{i}", sv['qp'], sv['kp'], sv['vv'], do, sv['lse'], delta, S, H, ta)
            dkp, dvv = flash_bwd_dkv(f"attn_dkv_{i}", sv['qp'], sv['kp'], sv['vv'], do, sv['lse'], delta, S, H, ta)
            dq_raw, dk_raw, dkpe = mla_unpack_bwd(f"mla_pack_bwd_{i}", dqp, dkp, tabs, S, H, tsw, sm_scale)
            g_uq = mm(f"mla_uq_dw_{i}", sv['qn'], dq_raw, 'tn')
            g_uk = mm(f"mla_uk_dw_{i}", sv['kvn'], dk_raw, 'tn')
            g_uv = mm(f"mla_uv_dw_{i}", sv['kvn'], dvv, 'tn')
            dqn = mm(f"mla_uq_dx_{i}", dq_raw, w_uq_k[j], 'nt')
            dkvn_k = mm(f"mla_uk_dx_{i}", dk_raw, w_uk_k[j], 'nt')
            dkvn_v = mm(f"mla_uv_dx_{i}", dvv, w_uv_k[j], 'nt')
            dz, dqg, dkvg = mla_prep_bwd(f"mla_prep_bwd_{i}", dqn, dkvn_k, dkvn_v, sv['z'], dkpe, dg,
                                         q_norm[j:j + 1], kv_norm[j:j + 1], tabs, S, EM, R, ts)
            g_in = mm(f"mla_in_dw_{i}", xn, dz, 'tn')
            dxn = mm(f"mla_in_dx_{i}", dz, w_in_k[j], 'nt')
            gw['mla_q_norm'][j], gw['mla_kv_norm'][j] = dqg[0], dkvg[0]
            gw['mla_w_in'][j] = jnp.concatenate([g_in[:, EM:EM + 2 * R + ROPE_DIM], g_in[:, :EM]], axis=1)
            gw['mla_w_uq'][j] = g_uq.reshape(R, H, HEAD_PAD)[:, :, :NOPE_DIM + ROPE_DIM].reshape(R, -1)
            gw['mla_w_ukv'][j] = jnp.concatenate(
                [g_uk.reshape(R, H, HEAD_PAD)[:, :, :NOPE_DIM], g_uv.reshape(R, H, V_DIM)], axis=2).reshape(R, -1)
        dh, dpre = pre_bwd(f"pre_norm_bwd_{i}", dxn, dh1, sv['h'], pre_norm[i:i + 1], S, D, ts)
        gw['pre_norm'][i] = dpre[0]
    grad_x = dh[None]
    gw = {n: jnp.stack(gs) for n, gs in gw.items()}

    send = [_to_slots(gw[n], SHARD_AXIS[n]).astype(BF16) for n in BIG]
    small_names = SMALL_REPL + SMALL_SHARD
    small_g = jnp.concatenate([gw[n].reshape(1, -1) for n in small_names], axis=1)
    recv = exchange("scatter_grads", send, False)
    small_all = exchange("gather_small_grads", [small_g], True)[0]

    outs = {}
    for n, r in zip(BIG, recv):
        shp = wl[n].shape
        two = lambda a: a.reshape(-1, shp[-1])
        res = adamw(f"adamw_{n}", r, two(wl[n]), two(ml[n]), two(vl[n]))
        outs[n] = [a.reshape(shp) for a in res]

    pieces, off = [], 0
    for n in small_names:
        sz = gw[n].size
        g = small_all[:, :, off:off + sz]
        off += sz
        if n in SMALL_SHARD:
            rows_, cols_ = gw[n].shape
            g = lax.dynamic_slice_in_dim(g.reshape(N_DEV, rows_, cols_), me * (cols_ // N_DEV), cols_ // N_DEV, axis=2)
            g = g.reshape(N_DEV, 1, -1)
        pieces.append(g)
    gs = jnp.concatenate(pieces, axis=2)
    flat = lambda d: jnp.concatenate([d[n].reshape(1, -1) for n in small_names], axis=1)
    res = adamw("adamw_small", gs, flat(wl), flat(ml), flat(vl))
    off = 0
    for n in small_names:
        sz = wl[n].size
        outs[n] = [a[:, off:off + sz].reshape(wl[n].shape) for a in res]
        off += sz

    return (loss, grad_x, *[outs[n][0] for n in WEIGHTS], *[outs[n][1] for n in WEIGHTS],
            *[outs[n][2] for n in WEIGHTS], *[outs[n][3] for n in WEIGHTS])
```

```python
import functools
import math

import jax
import jax.numpy as jnp
from jax import lax
from jax.experimental import pallas as pl
from jax.experimental.pallas import tpu as pltpu

F32 = jnp.float32
BF16 = jnp.bfloat16

N_DEV = 8
EPS = 1e-6
ROPE_THETA = 10000.0
NOPE_DIM = 128
ROPE_DIM = 64
V_DIM = 128
HEAD_PAD = 256
LANE = 128
POOL_WINDOWS = (2, 4, 8, 16)
POOL_HALO = 16
NEG_INF = -1e30
ADAM_LR = 0.001
ADAM_B1 = 0.9
ADAM_B2 = 0.999
ADAM_EPS = 1e-08
ADAM_WD = 0.01
ADAM_STEP = 10
VMEM_LIMIT_BYTES = 56 * 1024 * 1024
MESH = pl.DeviceIdType.MESH

SHARD_AXIS = dict(pre_norm=None, post_norm=None, pool_w_in=2, pool_w_group=2, pool_scale=None, pool_w_out=1,
                  mla_w_in=2, mla_q_norm=1, mla_w_uq=2, mla_kv_norm=1, mla_w_ukv=2, mla_w_out=1,
                  ple_norm=None, ple_w_gate=1, ple_w_proj=2)
WEIGHTS = tuple(SHARD_AXIS)
BIG = ('pool_w_in', 'pool_w_group', 'pool_w_out', 'mla_w_in', 'mla_w_uq', 'mla_w_ukv', 'mla_w_out',
       'ple_w_gate', 'ple_w_proj')
SMALL_REPL = ('pre_norm', 'post_norm', 'pool_scale', 'ple_norm')
SMALL_SHARD = ('mla_q_norm', 'mla_kv_norm')

AG_PLAN = {
    "pool_in_0": [("pool_w_group", 0), ("pool_w_out", 0)],
    "pool_group_0": [("ple_w_gate", 0), ("ple_w_proj", 0)],
    "pool_out_0": [("mla_w_in", 0)],
    "ple_gate_0": [("mla_w_uq", 0), ("mla_w_ukv", 0)],
    "attn_1": [("mla_w_out", 0), ("ple_w_gate", 1), ("ple_w_proj", 1), ("pool_w_in", 1), ("pool_w_group", 1),
               ("pool_w_out", 1), ("ple_w_gate", 2), ("ple_w_proj", 2)],
    "pool_in_2": [("mla_w_in", 1), ("mla_w_uq", 1), ("mla_w_ukv", 1)],
    "attn_3": [("mla_w_out", 1), ("ple_w_gate", 3), ("ple_w_proj", 3)],
}
RS_PLAN = {
    "attn_dq_3": [("ple_w_gate", 3), ("ple_w_proj", 3), ("mla_w_out", 1)],
    "ple_gate_dx_2": [("mla_w_uq", 1), ("mla_w_ukv", 1)],
    "pool_out_dw_2": [("ple_w_gate", 2), ("ple_w_proj", 2)],
    "pool_out_dx_2": [("mla_w_in", 1)],
    "pool_in_dx_2": [("pool_w_out", 1)],
    "attn_dq_1": [("pool_w_group", 1), ("pool_w_in", 1), ("ple_w_gate", 1), ("ple_w_proj", 1), ("mla_w_out", 0)],
    "ple_gate_dx_0": [("mla_w_uq", 0), ("mla_w_ukv", 0)],
    "pool_out_dw_0": [("ple_w_gate", 0), ("ple_w_proj", 0)],
    "pool_out_dx_0": [("mla_w_in", 0)],
    "pool_in_dx_0": [("pool_w_out", 0), ("pool_w_group", 0)],
}
RS_LAST = [("pool_w_in", 0)]


def _pcall(body, **kw):
    return pl.pallas_call(body, **kw)


def _cparams(*sem):
    return pltpu.CompilerParams(dimension_semantics=sem, vmem_limit_bytes=VMEM_LIMIT_BYTES)


def _pick(n, cands):
    for c in cands:
        if n % c == 0:
            return c
    return n


def _sigmoid(x):
    return 1.0 / (1.0 + jnp.exp(-x))


def mm(name, a, b, mode, out_dtype=F32, rider=None):
    squeeze = a.ndim == 2
    if squeeze:
        a, b = a[None], b[None]
    G = a.shape[0]
    if mode == 'nn':
        M, K = a.shape[1:]
        N = b.shape[2]
    elif mode == 'tn':
        K, M = a.shape[1:]
        N = b.shape[2]
    else:
        M, K = a.shape[1:]
        N = b.shape[1]
    tm = _pick(M, (1024, 512, 256, 128))
    tn = _pick(N, (1024, 768, 640, 512, 384, 256, 128))
    tk = _pick(K, (512, 640, 384, 256, 128))
    nk = K // tk
    if mode == 'nn':
        a_spec = pl.BlockSpec((None, tm, tk), lambda g, i, j, k: (g, i, k))
        b_spec = pl.BlockSpec((None, tk, tn), lambda g, i, j, k: (g, k, j))
        dims = (((1,), (0,)), ((), ()))
    elif mode == 'tn':
        a_spec = pl.BlockSpec((None, tk, tm), lambda g, i, j, k: (g, k, i))
        b_spec = pl.BlockSpec((None, tk, tn), lambda g, i, j, k: (g, k, j))
        dims = (((0,), (0,)), ((), ()))
    else:
        a_spec = pl.BlockSpec((None, tm, tk), lambda g, i, j, k: (g, i, k))
        b_spec = pl.BlockSpec((None, tn, tk), lambda g, i, j, k: (g, j, k))
        dims = (((1,), (1,)), ((), ()))

    def body(a_ref, b_ref, o_ref, acc_ref):
        k = pl.program_id(3)

        @pl.when(k == 0)
        def _():
            acc_ref[...] = jnp.zeros_like(acc_ref)

        acc_ref[...] += lax.dot_general(a_ref[...].astype(BF16), b_ref[...].astype(BF16), dims,
                                        preferred_element_type=F32)

        @pl.when(k == nk - 1)
        def _():
            o_ref[...] = acc_ref[...].astype(out_dtype)

    (out,), carried = _call(
        body, name, [a, b], [a_spec, b_spec], [jax.ShapeDtypeStruct((G, M, N), out_dtype)],
        [pl.BlockSpec((None, tm, tn), lambda g, i, j, k: (g, i, j))], (G, M // tm, N // tn, nk),
        [pltpu.VMEM((tm, tn), F32)], ("parallel", "parallel", "parallel", "arbitrary"), rider)
    out = out[0] if squeeze else out
    return out if rider is None else (out, carried)


def rows(ts, width, colblk=0):
    return pl.BlockSpec((ts, width), lambda i: (i, colblk))


def whole(shape):
    return pl.BlockSpec(shape, lambda i: (0,) * len(shape))


def rowwise(name, fn, S, ts, ins, outs, accs=(), scratch=(), reverse=False):
    n_in, n_out, n_acc = len(ins), len(outs), len(accs)
    nt = S // ts

    def body(*refs):
        step = pl.program_id(0)
        i = nt - 1 - step if reverse else step
        in_refs = refs[:n_in]
        out_refs = refs[n_in:n_in + n_out]
        acc_refs = refs[n_in + n_out:n_in + n_out + n_acc]
        scr = refs[n_in + n_out + n_acc:]

        @pl.when(step == 0)
        def _():
            for r in acc_refs:
                r[...] = jnp.zeros_like(r)

        fn(i, step, in_refs, out_refs, acc_refs, scr)

    def fix(spec):
        if not reverse:
            return spec
        imap = spec.index_map
        return pl.BlockSpec(spec.block_shape, lambda s: imap(nt - 1 - s))

    res = _pcall(
        body, name=name,
        out_shape=[jax.ShapeDtypeStruct(s, d) for s, d, _ in outs] + [jax.ShapeDtypeStruct(s, d) for s, d in accs],
        grid=(nt,),
        in_specs=[fix(sp) for _, sp in ins],
        out_specs=[fix(sp) for _, _, sp in outs] + [whole(s) for s, _ in accs],
        scratch_shapes=list(scratch),
        compiler_params=_cparams("arbitrary"),
    )(*[a for a, _ in ins])
    return res


def _rstd(x):
    return lax.rsqrt(jnp.mean(x * x, axis=-1, keepdims=True) + EPS)


def _rms_bwd(dy, x, g):
    r = _rstd(x)
    xh = x * r
    gdy = dy * g
    dx = r * (gdy - xh * jnp.mean(xh * gdy, axis=-1, keepdims=True))
    return dx, jnp.sum(dy * xh, axis=0, keepdims=True)


def _rope(v, cos_t, sin_a, sin_b, sign):
    return v * cos_t + sign * (pltpu.roll(v, LANE - ROPE_DIM // 2, axis=1) * sin_a
                               + pltpu.roll(v, ROPE_DIM // 2, axis=1) * sin_b)


def rms_fwd(name, h, gain, S, D, ts):
    def fn(i, step, ins, outs, accs, scr):
        x = ins[0][...]
        outs[0][...] = (x * _rstd(x) * ins[1][...]).astype(BF16)
    return rowwise(name, fn, S, ts, [(h, rows(ts, D)), (gain, whole((1, D)))], [((S, D), BF16, rows(ts, D))])[0]


def post_fwd(name, h, out, post_g, ple_g, S, D, ts):
    def fn(i, step, ins, outs, accs, scr):
        o = ins[1][...]
        h1 = ins[0][...] + o * _rstd(o) * ins[2][...]
        outs[0][...] = h1
        outs[1][...] = (h1 * _rstd(h1) * ins[3][...]).astype(BF16)
    return rowwise(name, fn, S, ts,
                   [(h, rows(ts, D)), (out, rows(ts, D)), (post_g, whole((1, D))), (ple_g, whole((1, D)))],
                   [((S, D), F32, rows(ts, D)), ((S, D), BF16, rows(ts, D))])


def ple_fwd(name, h1, pp, gl, S, D, ts):
    def fn(i, step, ins, outs, accs, scr):
        outs[0][...] = ins[0][...] + ins[1][...] * _sigmoid(ins[2][...])
    return rowwise(name, fn, S, ts, [(h1, rows(ts, D)), (pp, rows(ts, D)), (gl, rows(ts, D))],
                   [((S, D), F32, rows(ts, D))])[0]


def loss_fwd_bwd(name, h, tgt, S, D, ts):
    def fn(i, step, ins, outs, accs, scr):
        e = ins[0][...] - ins[1][...]
        outs[0][...] = e * (1.0 / D)
        accs[0][...] += jnp.broadcast_to(jnp.sum(e * e), (1, LANE))
    return rowwise(name, fn, S, ts, [(h, rows(ts, D)), (tgt, rows(ts, D))], [((S, D), F32, rows(ts, D))],
                   accs=[((1, LANE), F32)])


def ple_bwd(name, dh, pp, gl, S, D, ts):
    def fn(i, step, ins, outs, accs, scr):
        d = ins[0][...]
        gate = _sigmoid(ins[2][...])
        outs[0][...] = (d * gate).astype(BF16)
        outs[1][...] = (d * ins[1][...] * gate * (1.0 - gate)).astype(BF16)
    return rowwise(name, fn, S, ts, [(dh, rows(ts, D)), (pp, rows(ts, D)), (gl, rows(ts, D))],
                   [((S, D), BF16, rows(ts, D)), ((S, D), BF16, rows(ts, D))])


def post_bwd(name, da, dh, h1, out, post_g, ple_g, S, D, ts):
    def fn(i, step, ins, outs, accs, scr):
        dx, dple = _rms_bwd(ins[0][...], ins[2][...], ins[5][...])
        dh1 = ins[1][...] + dx
        dout, dpost = _rms_bwd(dh1, ins[3][...], ins[4][...])
        outs[0][...] = dh1
        outs[1][...] = dout.astype(BF16)
        accs[0][...] += dpost
        accs[1][...] += dple
    return rowwise(name, fn, S, ts,
                   [(da, rows(ts, D)), (dh, rows(ts, D)), (h1, rows(ts, D)), (out, rows(ts, D)),
                    (post_g, whole((1, D))), (ple_g, whole((1, D)))],
                   [((S, D), F32, rows(ts, D)), ((S, D), BF16, rows(ts, D))],
                   accs=[((1, D), F32), ((1, D), F32)])


def pre_bwd(name, dxn, dh1, h, pre_g, S, D, ts):
    def fn(i, step, ins, outs, accs, scr):
        dx, dpre = _rms_bwd(ins[0][...], ins[2][...], ins[3][...])
        outs[0][...] = ins[1][...] + dx
        accs[0][...] += dpre
    return rowwise(name, fn, S, ts,
                   [(dxn, rows(ts, D)), (dh1, rows(ts, D)), (h, rows(ts, D)), (pre_g, whole((1, D)))],
                   [((S, D), F32, rows(ts, D))], accs=[((1, D), F32)])


def _window_sums(ext, w, back):
    n = ext.shape[0]
    s, win = ext, 1
    while win < w:
        s = s + pltpu.roll(s, win if back else n - win, axis=0)
        win *= 2
    return s


def pool_fwd(name, z, S, E, NG, ts):
    G = E // NG

    def fn(i, step, ins, outs, accs, scr):
        carry = scr[0]

        @pl.when(step == 0)
        def _():
            carry[...] = jnp.zeros_like(carry)

        t = i * ts + lax.broadcasted_iota(jnp.int32, (ts, 1), 0)
        for j, w in enumerate(POOL_WINDOWS):
            u = ins[0][:, j * G:(j + 1) * G]
            ext = jnp.concatenate([carry[:, j * G:(j + 1) * G], u], axis=0)
            sw = _window_sums(ext, w, True)[POOL_HALO:, :]
            cnt = jnp.minimum(t + 1, w).astype(F32)
            outs[0][j] = (sw / cnt - u).astype(BF16)
        carry[...] = ins[0][ts - POOL_HALO:, :]

    return rowwise(name, fn, S, ts, [(z, rows(ts, E, 0))],
                   [((NG, S, G), BF16, pl.BlockSpec((NG, ts, G), lambda i: (0, i, 0)))],
                   scratch=[pltpu.VMEM((POOL_HALO, E), F32)])[0]


def pool_bwd(name, dpooled, dg, S, E, NG, ts):
    G = E // NG

    def fn(i, step, ins, outs, accs, scr):
        carry = scr[0]

        @pl.when(step == 0)
        def _():
            carry[...] = jnp.zeros_like(carry)

        t = i * ts + lax.broadcasted_iota(jnp.int32, (ts, 1), 0)
        for j, w in enumerate(POOL_WINDOWS):
            d = ins[0][j]
            e = d / jnp.minimum(t + 1, w).astype(F32)
            ext = jnp.concatenate([e, carry[:, j * G:(j + 1) * G]], axis=0)
            sw = _window_sums(ext, w, False)[:ts, :]
            outs[0][:, j * G:(j + 1) * G] = (sw - d).astype(BF16)
            carry[:, j * G:(j + 1) * G] = e[:POOL_HALO, :]
        outs[0][:, E:] = ins[1][...]

    return rowwise(name, fn, S, ts,
                   [(dpooled, pl.BlockSpec((NG, ts, G), lambda i: (0, i, 0))), (dg, rows(ts, E))],
                   [((S, 2 * E), BF16, rows(ts, 2 * E))],
                   scratch=[pltpu.VMEM((POOL_HALO, E), F32)], reverse=True)[0]


def pool_gate_fwd(name, mixed, z, scale, S, E, NG, ts):
    G = E // NG

    def fn(i, step, ins, outs, accs, scr):
        for j in range(NG):
            sl = slice(j * G, (j + 1) * G)
            g = ins[1][:, sl]
            outs[0][:, sl] = (ins[0][j] * ins[2][:, sl] * (g * _sigmoid(g))).astype(BF16)

    return rowwise(name, fn, S, ts,
                   [(mixed, pl.BlockSpec((NG, ts, G), lambda i: (0, i, 0))), (z, rows(ts, E, 1)),
                    (scale, whole((1, E)))],
                   [((S, E), BF16, rows(ts, E))])[0]


def pool_gate_bwd(name, dy, mixed, z, scale, S, E, NG, ts):
    G = E // NG

    def fn(i, step, ins, outs, accs, scr):
        for j in range(NG):
            sl = slice(j * G, (j + 1) * G)
            d = ins[0][:, sl]
            mx = ins[1][j]
            g = ins[2][:, sl]
            sc = ins[3][:, sl]
            sg = _sigmoid(g)
            si = g * sg
            outs[0][j] = (d * sc * si).astype(BF16)
            outs[1][:, sl] = (d * mx * sc * (sg * (1.0 + g * (1.0 - sg)))).astype(BF16)
            accs[0][:, sl] += jnp.sum(d * mx * si, axis=0, keepdims=True)

    return rowwise(name, fn, S, ts,
                   [(dy, rows(ts, E)), (mixed, pl.BlockSpec((NG, ts, G), lambda i: (0, i, 0))),
                    (z, rows(ts, E, 1)), (scale, whole((1, E)))],
                   [((NG, S, G), BF16, pl.BlockSpec((NG, ts, G), lambda i: (0, i, 0))), ((S, E), BF16, rows(ts, E))],
                   accs=[((1, E), F32)])


def mla_prep_fwd(name, z, qg, kvg, tabs, S, E, R, ts):
    qb, kb, pb = E // R, E // R + 1, (E + 2 * R) // LANE

    def fn(i, step, ins, outs, accs, scr):
        zq, zkv = ins[0][...], ins[1][...]
        outs[0][...] = (zq * _rstd(zq) * ins[3][...]).astype(BF16)
        outs[1][...] = (zkv * _rstd(zkv) * ins[4][...]).astype(BF16)
        outs[2][...] = _rope(ins[2][...], ins[5][...], ins[6][...], ins[7][...], 1.0)

    return rowwise(name, fn, S, ts,
                   [(z, rows(ts, R, qb)), (z, rows(ts, R, kb)), (z, rows(ts, LANE, pb)),
                    (qg, whole((1, R))), (kvg, whole((1, R)))] + [(t, rows(ts, LANE)) for t in tabs],
                   [((S, R), BF16, rows(ts, R)), ((S, R), BF16, rows(ts, R)), ((S, LANE), F32, rows(ts, LANE))])


def mla_pack_fwd(name, q_raw, k_raw, kper, tabs, S, H, ts, scale):
    W = H * HEAD_PAD

    def fn(i, step, ins, outs, accs, scr):
        cos_t, sin_a, sin_b = ins[3][...], ins[4][...], ins[5][...]
        kp = ins[2][...].astype(BF16)
        for h in range(H):
            a, b, c = h * HEAD_PAD, h * HEAD_PAD + NOPE_DIM, (h + 1) * HEAD_PAD
            outs[0][:, a:b] = (ins[0][:, a:b] * scale).astype(BF16)
            outs[0][:, b:c] = (_rope(ins[0][:, b:c], cos_t, sin_a, sin_b, 1.0) * scale).astype(BF16)
            outs[1][:, a:b] = ins[1][:, a:b].astype(BF16)
            outs[1][:, b:c] = kp

    return rowwise(name, fn, S, ts,
                   [(q_raw, rows(ts, W)), (k_raw, rows(ts, W)), (kper, rows(ts, LANE))]
                   + [(t, rows(ts, LANE)) for t in tabs],
                   [((S, W), BF16, rows(ts, W)), ((S, W), BF16, rows(ts, W))])


def mla_gate_fwd(name, o, z, S, E, ts):
    def fn(i, step, ins, outs, accs, scr):
        g = ins[1][...]
        outs[0][...] = (ins[0][...] * (g * _sigmoid(g))).astype(BF16)
    return rowwise(name, fn, S, ts, [(o, rows(ts, E)), (z, rows(ts, E, 0))], [((S, E), BF16, rows(ts, E))])[0]


def mla_gate_bwd(name, dy, o, z, S, E, H, ts):
    def fn(i, step, ins, outs, accs, scr):
        for h in range(H):
            sl = slice(h * V_DIM, (h + 1) * V_DIM)
            d, ov, g = ins[0][:, sl], ins[1][:, sl], ins[2][:, sl]
            sg = _sigmoid(g)
            do = d * (g * sg)
            outs[0][:, sl] = do.astype(BF16)
            outs[1][:, sl] = (d * ov * (sg * (1.0 + g * (1.0 - sg)))).astype(BF16)
            outs[2][:, sl] = jnp.broadcast_to(jnp.sum(do * ov, axis=1, keepdims=True), (ts, V_DIM))

    return rowwise(name, fn, S, ts, [(dy, rows(ts, E)), (o, rows(ts, E)), (z, rows(ts, E, 0))],
                   [((S, E), BF16, rows(ts, E)), ((S, E), BF16, rows(ts, E)), ((S, E), F32, rows(ts, E))])


def mla_unpack_bwd(name, dq, dk, tabs, S, H, ts, scale):
    W = H * HEAD_PAD

    def fn(i, step, ins, outs, accs, scr):
        cos_t, sin_a, sin_b = ins[2][...], ins[3][...], ins[4][...]
        dkpe = jnp.zeros((ts, LANE), F32)
        for h in range(H):
            a, b, c = h * HEAD_PAD, h * HEAD_PAD + NOPE_DIM, (h + 1) * HEAD_PAD
            outs[0][:, a:b] = (ins[0][:, a:b] * scale).astype(BF16)
            outs[0][:, b:c] = (_rope(ins[0][:, b:c], cos_t, sin_a, sin_b, -1.0) * scale).astype(BF16)
            outs[1][:, a:b] = ins[1][:, a:b].astype(BF16)
            outs[1][:, b:c] = jnp.zeros((ts, LANE), BF16)
            dkpe = dkpe + ins[1][:, b:c]
        outs[2][...] = dkpe

    return rowwise(name, fn, S, ts,
                   [(dq, rows(ts, W)), (dk, rows(ts, W))] + [(t, rows(ts, LANE)) for t in tabs],
                   [((S, W), BF16, rows(ts, W)), ((S, W), BF16, rows(ts, W)), ((S, LANE), F32, rows(ts, LANE))])


def mla_prep_bwd(name, dqn, dkvn_k, dkvn_v, z, dkpe, dg, qg, kvg, tabs, S, E, R, ts):
    qb, kb = E // R, E // R + 1
    ZW = E + 2 * R + LANE

    def fn(i, step, ins, outs, accs, scr):
        dzq, dqg = _rms_bwd(ins[0][...], ins[3][...], ins[7][...])
        dzkv, dkvg = _rms_bwd(ins[1][...] + ins[2][...], ins[4][...], ins[8][...])
        outs[0][:, :E] = ins[6][...]
        outs[0][:, E:E + R] = dzq.astype(BF16)
        outs[0][:, E + R:E + 2 * R] = dzkv.astype(BF16)
        outs[0][:, E + 2 * R:] = _rope(ins[5][...], ins[9][...], ins[10][...], ins[11][...], -1.0).astype(BF16)
        accs[0][...] += dqg
        accs[1][...] += dkvg

    return rowwise(name, fn, S, ts,
                   [(dqn, rows(ts, R)), (dkvn_k, rows(ts, R)), (dkvn_v, rows(ts, R)), (z, rows(ts, R, qb)),
                    (z, rows(ts, R, kb)), (dkpe, rows(ts, LANE)), (dg, rows(ts, E)),
                    (qg, whole((1, R))), (kvg, whole((1, R)))] + [(t, rows(ts, LANE)) for t in tabs],
                   [((S, ZW), BF16, rows(ts, ZW))], accs=[((1, R), F32), ((1, R), F32)])


_NT = (((1,), (1,)), ((), ()))
_TN = (((0,), (0,)), ((), ()))
_NN = (((1,), (0,)), ((), ()))


def _causal_mask(t):
    return lax.broadcasted_iota(jnp.int32, (t, t), 1) <= lax.broadcasted_iota(jnp.int32, (t, t), 0)


def flash_fwd(name, q, k, v, S, H, t, rider=None):
    nt = S // t

    def body(q_ref, k_ref, v_ref, o_ref, lse_ref, m_sc, l_sc, acc_sc):
        qi, ki = pl.program_id(1), pl.program_id(2)

        @pl.when(ki == 0)
        def _():
            m_sc[...] = jnp.full_like(m_sc, NEG_INF)
            l_sc[...] = jnp.zeros_like(l_sc)
            acc_sc[...] = jnp.zeros_like(acc_sc)

        def step(diag):
            s = lax.dot_general(q_ref[...], k_ref[...], _NT, preferred_element_type=F32)
            if diag:
                s = jnp.where(_causal_mask(t), s, NEG_INF)
            m_prev = m_sc[...]
            m_new = jnp.maximum(m_prev, jnp.max(s, axis=1, keepdims=True))
            alpha = jnp.exp(m_prev - m_new)
            p = jnp.exp(s - m_new)
            l_sc[...] = alpha * l_sc[...] + jnp.sum(p, axis=1, keepdims=True)
            acc_sc[...] = alpha * acc_sc[...] + lax.dot_general(p.astype(BF16), v_ref[...], _NN,
                                                                 preferred_element_type=F32)
            m_sc[...] = m_new

        @pl.when(ki < qi)
        def _():
            step(False)

        @pl.when(ki == qi)
        def _():
            step(True)
            l = l_sc[...]
            o_ref[...] = acc_sc[...] / l
            lse_ref[...] = jnp.broadcast_to(m_sc[...] + jnp.log(l), (t, V_DIM))

    (o, lse), carried = _call(
        body, name, [q, k, v],
        [pl.BlockSpec((t, HEAD_PAD), lambda h, i, j: (i, h)),
         pl.BlockSpec((t, HEAD_PAD), lambda h, i, j: (jnp.minimum(i, j), h)),
         pl.BlockSpec((t, V_DIM), lambda h, i, j: (jnp.minimum(i, j), h))],
        [jax.ShapeDtypeStruct((S, H * V_DIM), F32), jax.ShapeDtypeStruct((S, H * V_DIM), F32)],
        [pl.BlockSpec((t, V_DIM), lambda h, i, j: (i, h)), pl.BlockSpec((t, V_DIM), lambda h, i, j: (i, h))],
        (H, nt, nt), [pltpu.VMEM((t, 1), F32), pltpu.VMEM((t, 1), F32), pltpu.VMEM((t, V_DIM), F32)],
        ("parallel", "parallel", "arbitrary"), rider)
    return o, lse, carried


def _p_and_ds(q, k, v, do, lse, delta, t, diag):
    s = lax.dot_general(q, k, _NT, preferred_element_type=F32)
    p = jnp.exp(s - lse[:, :1])
    if diag:
        p = jnp.where(_causal_mask(t), p, 0.0)
    dp = lax.dot_general(do, v, _NT, preferred_element_type=F32)
    ds = p * (dp - delta[:, :1])
    return p, ds


def flash_bwd_dq(name, q, k, v, do, lse, delta, S, H, t, rider=None):
    nt = S // t

    def body(q_ref, k_ref, v_ref, do_ref, lse_ref, dl_ref, dq_ref, acc_sc):
        qi, ki = pl.program_id(1), pl.program_id(2)

        @pl.when(ki == 0)
        def _():
            acc_sc[...] = jnp.zeros_like(acc_sc)

        def step(diag):
            _, ds = _p_and_ds(q_ref[...], k_ref[...], v_ref[...], do_ref[...], lse_ref[...], dl_ref[...], t, diag)
            acc_sc[...] += lax.dot_general(ds.astype(BF16), k_ref[...], _NN, preferred_element_type=F32)

        @pl.when(ki < qi)
        def _():
            step(False)

        @pl.when(ki == qi)
        def _():
            step(True)
            dq_ref[...] = acc_sc[...]

    qspec = lambda w: pl.BlockSpec((t, w), lambda h, i, j: (i, h))
    kspec = lambda w: pl.BlockSpec((t, w), lambda h, i, j: (jnp.minimum(i, j), h))
    (dq,), carried = _call(
        body, name, [q, k, v, do, lse, delta],
        [qspec(HEAD_PAD), kspec(HEAD_PAD), kspec(V_DIM), qspec(V_DIM), qspec(V_DIM), qspec(V_DIM)],
        [jax.ShapeDtypeStruct((S, H * HEAD_PAD), F32)], [qspec(HEAD_PAD)], (H, nt, nt),
        [pltpu.VMEM((t, HEAD_PAD), F32)], ("parallel", "parallel", "arbitrary"), rider)
    return dq, carried


def flash_bwd_dkv(name, q, k, v, do, lse, delta, S, H, t):
    nt = S // t

    def body(q_ref, k_ref, v_ref, do_ref, lse_ref, dl_ref, dk_ref, dv_ref, dk_sc, dv_sc):
        ki, qi = pl.program_id(1), pl.program_id(2)

        @pl.when(qi == 0)
        def _():
            dk_sc[...] = jnp.zeros_like(dk_sc)
            dv_sc[...] = jnp.zeros_like(dv_sc)

        def step(diag):
            p, ds = _p_and_ds(q_ref[...], k_ref[...], v_ref[...], do_ref[...], lse_ref[...], dl_ref[...], t, diag)
            dv_sc[...] += lax.dot_general(p.astype(BF16), do_ref[...], _TN, preferred_element_type=F32)
            dk_sc[...] += lax.dot_general(ds.astype(BF16), q_ref[...], _TN, preferred_element_type=F32)

        @pl.when(qi > ki)
        def _():
            step(False)

        @pl.when(qi == ki)
        def _():
            step(True)

        @pl.when(qi == nt - 1)
        def _():
            dk_ref[...] = dk_sc[...]
            dv_ref[...] = dv_sc[...]

    qspec = lambda w: pl.BlockSpec((t, w), lambda h, j, i: (jnp.maximum(i, j), h))
    kspec = lambda w: pl.BlockSpec((t, w), lambda h, j, i: (j, h))
    return _pcall(
        body, name=name,
        out_shape=[jax.ShapeDtypeStruct((S, H * HEAD_PAD), F32), jax.ShapeDtypeStruct((S, H * V_DIM), F32)],
        grid=(H, nt, nt),
        in_specs=[qspec(HEAD_PAD), kspec(HEAD_PAD), kspec(V_DIM), qspec(V_DIM), qspec(V_DIM), qspec(V_DIM)],
        out_specs=[kspec(HEAD_PAD), kspec(V_DIM)],
        scratch_shapes=[pltpu.VMEM((t, HEAD_PAD), F32), pltpu.VMEM((t, V_DIM), F32)],
        compiler_params=_cparams("parallel", "parallel", "arbitrary"),
    )(q, k, v, do, lse, delta)


def _peers():
    x, y, c = lax.axis_index("x"), lax.axis_index("y"), lax.axis_index("c")
    me = 4 * x + 2 * y + c
    peers = []
    for fx, fy, fc in ((0, 0, 1), (1, 0, 0), (0, 1, 0), (1, 1, 0), (1, 0, 1), (0, 1, 1), (1, 1, 1)):
        px, py, pc = x ^ fx, y ^ fy, c ^ fc
        peers.append(((px, py, pc), 4 * px + 2 * py + pc))
    return me, peers


def _hbm_specs(n):
    return [pl.BlockSpec(memory_space=pl.ANY)] * n


class Rider:
    def __init__(self, arrs, gather):
        self.arrs, self.gather, self.n = list(arrs), gather, len(arrs)
        self.out_shapes = [jax.ShapeDtypeStruct((N_DEV,) + a.shape if gather else a.shape, a.dtype) for a in arrs]
        self.sems = [pltpu.SemaphoreType.DMA((self.n, N_DEV - 1)), pltpu.SemaphoreType.DMA((self.n, N_DEV - 1)),
                     pltpu.SemaphoreType.DMA((self.n,))]

    def _copies(self, srcs, dsts, sems):
        send_sems, recv_sems, local_sems = sems
        me, peers = _peers()
        src = (lambda a, pid: srcs[a]) if self.gather else (lambda a, pid: srcs[a].at[pid])
        local = [pltpu.make_async_copy(src(a, me), dsts[a].at[me], local_sems.at[a]) for a in range(self.n)]

        def remote(a, k, arrival):
            peer, pid = peers[k]
            return pltpu.make_async_remote_copy(
                src_ref=src(a, pid), dst_ref=dsts[a].at[pid if arrival else me],
                send_sem=send_sems.at[a, k], recv_sem=recv_sems.at[a, k], device_id=peer, device_id_type=MESH)

        return local, remote

    def start(self, srcs, dsts, sems):
        local, remote = self._copies(srcs, dsts, sems)
        for a in range(self.n):
            local[a].start()
            for k in range(N_DEV - 1):
                remote(a, k, False).start()

    def wait(self, srcs, dsts, sems):
        local, remote = self._copies(srcs, dsts, sems)
        for a in range(self.n):
            for k in range(N_DEV - 1):
                remote(a, k, True).wait_recv()
        for a in range(self.n):
            for k in range(N_DEV - 1):
                remote(a, k, False).wait_send()
            local[a].wait()


def _carry(body, n_in, n_out, rider, grid):
    n = rider.n

    def wrapped(*refs):
        ins, r_in = refs[:n_in], refs[n_in:n_in + n]
        outs = refs[n_in + n:n_in + n + n_out]
        r_out = refs[n_in + n + n_out:n_in + 2 * n + n_out]
        scratch, sems = refs[n_in + 2 * n + n_out:-3], refs[-3:]
        ids = [pl.program_id(d) for d in range(len(grid))]
        first = functools.reduce(jnp.logical_and, [i == 0 for i in ids])
        last = functools.reduce(jnp.logical_and, [i == g - 1 for i, g in zip(ids, grid)])

        @pl.when(first)
        def _():
            rider.start(r_in, r_out, sems)

        body(*ins, *outs, *scratch)

        @pl.when(last)
        def _():
            rider.wait(r_in, r_out, sems)

    return wrapped


def _call(body, name, ins, in_specs, out_shape, out_specs, grid, scratch, sem, rider=None):
    if rider is None:
        return _pcall(body, name=name, out_shape=list(out_shape), grid=grid, in_specs=list(in_specs),
                      out_specs=list(out_specs), scratch_shapes=list(scratch), compiler_params=_cparams(*sem))(*ins), None
    res = _pcall(
        _carry(body, len(ins), len(out_shape), rider, grid), name=name,
        out_shape=list(out_shape) + rider.out_shapes, grid=grid,
        in_specs=list(in_specs) + _hbm_specs(rider.n), out_specs=list(out_specs) + _hbm_specs(rider.n),
        scratch_shapes=list(scratch) + rider.sems, compiler_params=_cparams(*(("arbitrary",) * len(grid))),
    )(*ins, *rider.arrs)
    return res[:len(out_shape)], res[len(out_shape):]


def exchange(name, arrs, gather):
    rider = Rider(arrs, gather)

    def body(*refs):
        srcs, dsts, sems = refs[:rider.n], refs[rider.n:2 * rider.n], refs[2 * rider.n:]
        rider.start(srcs, dsts, sems)
        rider.wait(srcs, dsts, sems)

    return _pcall(body, name=name, out_shape=rider.out_shapes, in_specs=_hbm_specs(rider.n),
                  out_specs=_hbm_specs(rider.n), scratch_shapes=rider.sems)(*arrs)


def adamw(name, gslots, w, m, v, layer=0, prev=None):
    K, R, C = gslots.shape
    per_row = C * (K * gslots.dtype.itemsize + 7 * 4) * 2
    tr = R
    for cand in (1024, 512, 256, 128, 64, 32, 16, 8):
        if R % cand == 0:
            tr = cand
            if cand * per_row <= VMEM_LIMIT_BYTES // 2:
                break
    c1 = 1.0 / (1.0 - ADAM_B1 ** ADAM_STEP)
    c2 = 1.0 / (1.0 - ADAM_B2 ** ADAM_STEP)

    def body(g_ref, w_ref, m_ref, v_ref, *rest):
        go_ref, d_ref, mo_ref, vo_ref = rest[-4:]
        g = g_ref[0].astype(F32)
        for s in range(1, K):
            g = g + g_ref[s].astype(F32)
        mn = ADAM_B1 * m_ref[...] + (1.0 - ADAM_B1) * g
        vn = ADAM_B2 * v_ref[...] + (1.0 - ADAM_B2) * (g * g)
        go_ref[...] = g
        mo_ref[...] = mn
        vo_ref[...] = vn
        d_ref[...] = -ADAM_LR * ((mn * c1) / (jnp.sqrt(vn * c2) + ADAM_EPS) + ADAM_WD * w_ref[...])

    blk = pl.BlockSpec((None, tr, C), lambda i: (layer, i, 0))
    prev = [] if prev is None else list(prev)
    return _pcall(
        body, name=name, out_shape=[jax.ShapeDtypeStruct(w.shape, F32)] * 4, grid=(R // tr,),
        in_specs=[pl.BlockSpec((K, tr, C), lambda i: (0, i, 0)), blk, blk, blk] + _hbm_specs(len(prev)),
        out_specs=[blk] * 4, input_output_aliases={4 + q: q for q in range(len(prev))},
        compiler_params=_cparams("parallel"),
    )(gslots, w, m, v, *prev)


def _from_slots(gathered, ax):
    g = jnp.moveaxis(gathered, 0, ax)
    s = g.shape
    return g.reshape(s[:ax] + (s[ax] * s[ax + 1],) + s[ax + 2:])


def _to_slots(full, ax):
    s = full.shape
    g = full.reshape(s[:ax] + (N_DEV, s[ax] // N_DEV) + s[ax + 1:])
    g = jnp.moveaxis(g, ax, 0)
    return g.reshape(N_DEV, -1, g.shape[-1])


def _rope_tables(pos, S):
    inv_freq = ROPE_THETA ** (-jnp.arange(0, ROPE_DIM, 2, dtype=F32) / ROPE_DIM)
    ang = pos.astype(F32)[:, None] * inv_freq
    cos, sin = jnp.cos(ang), jnp.sin(ang)
    z = jnp.zeros((S, ROPE_DIM // 2), F32)
    cos_t = jnp.concatenate([cos, cos, z, z], axis=1)
    sin_a = jnp.concatenate([-sin, z, z, z], axis=1)
    sin_b = jnp.concatenate([z, sin, z, z], axis=1)
    return cos_t, sin_a, sin_b


def kernel(x, p, positions, pre_norm, post_norm, pool_w_in, pool_w_group, pool_scale, pool_w_out, mla_w_in, mla_q_norm, mla_w_uq, mla_kv_norm, mla_w_ukv, mla_w_out, ple_norm, ple_w_gate, ple_w_proj, loss_target, m_pre_norm, m_post_norm, m_pool_w_in, m_pool_w_group, m_pool_scale, m_pool_w_out, m_mla_w_in, m_mla_q_norm, m_mla_w_uq, m_mla_kv_norm, m_mla_w_ukv, m_mla_w_out, m_ple_norm, m_ple_w_gate, m_ple_w_proj, v_pre_norm, v_post_norm, v_pool_w_in, v_pool_w_group, v_pool_scale, v_pool_w_out, v_mla_w_in, v_mla_q_norm, v_mla_w_uq, v_mla_kv_norm, v_mla_w_ukv, v_mla_w_out, v_ple_norm, v_ple_w_gate, v_ple_w_proj):
    wl = dict(pre_norm=pre_norm, post_norm=post_norm, pool_w_in=pool_w_in, pool_w_group=pool_w_group,
              pool_scale=pool_scale, pool_w_out=pool_w_out, mla_w_in=mla_w_in, mla_q_norm=mla_q_norm,
              mla_w_uq=mla_w_uq, mla_kv_norm=mla_kv_norm, mla_w_ukv=mla_w_ukv, mla_w_out=mla_w_out,
              ple_norm=ple_norm, ple_w_gate=ple_w_gate, ple_w_proj=ple_w_proj)
    ml = dict(pre_norm=m_pre_norm, post_norm=m_post_norm, pool_w_in=m_pool_w_in, pool_w_group=m_pool_w_group,
              pool_scale=m_pool_scale, pool_w_out=m_pool_w_out, mla_w_in=m_mla_w_in, mla_q_norm=m_mla_q_norm,
              mla_w_uq=m_mla_w_uq, mla_kv_norm=m_mla_kv_norm, mla_w_ukv=m_mla_w_ukv, mla_w_out=m_mla_w_out,
              ple_norm=m_ple_norm, ple_w_gate=m_ple_w_gate, ple_w_proj=m_ple_w_proj)
    vl = dict(pre_norm=v_pre_norm, post_norm=v_post_norm, pool_w_in=v_pool_w_in, pool_w_group=v_pool_w_group,
              pool_scale=v_pool_scale, pool_w_out=v_pool_w_out, mla_w_in=v_mla_w_in, mla_q_norm=v_mla_q_norm,
              mla_w_uq=v_mla_w_uq, mla_kv_norm=v_mla_kv_norm, mla_w_ukv=v_mla_w_ukv, mla_w_out=v_mla_w_out,
              ple_norm=v_ple_norm, ple_w_gate=v_ple_w_gate, ple_w_proj=v_ple_w_proj)

    S, D = x.shape[1], x.shape[2]
    L = pre_norm.shape[0]
    E = pool_scale.shape[1]
    NG = pool_w_group.shape[1]
    R = mla_w_uq.shape[1]
    H = D // 128
    EM = H * V_DIM
    PD = p.shape[-1]
    me = 4 * lax.axis_index("x") + 2 * lax.axis_index("y") + lax.axis_index("c")
    ts = min(S, 256)
    tsw = min(S, 128)
    ta = min(S, 512)
    sm_scale = (NOPE_DIM + ROPE_DIM) ** -0.5

    wb = {n: wl[n].astype(BF16) for n in BIG}
    full = {}

    def ag_rider(host):
        return Rider([wb[n][l] for n, l in AG_PLAN[host]], True) if host in AG_PLAN else None

    def ag_done(host, results):
        for (n, l), g in zip(AG_PLAN[host], results):
            full[n, l] = _from_slots(g, SHARD_AXIS[n] - 1)

    small_sh = jnp.concatenate([wl[n].reshape(1, -1) for n in SMALL_SHARD], axis=1)
    g_in0, g_small = exchange("gather_first", [wb['pool_w_in'][0], small_sh], True)
    full['pool_w_in', 0] = _from_slots(g_in0, SHARD_AXIS['pool_w_in'] - 1)
    nq = mla_q_norm.size
    q_norm = _from_slots(g_small[:, 0, :nq].reshape((N_DEV,) + mla_q_norm.shape), 1)
    kv_norm = _from_slots(g_small[:, 0, nq:].reshape((N_DEV,) + mla_kv_norm.shape), 1)

    def mla_kernel_weights(j):
        w_in = full['mla_w_in', j]
        w_in_k = jnp.concatenate([w_in[:, 2 * R + ROPE_DIM:], w_in[:, :2 * R + ROPE_DIM],
                                  jnp.zeros((D, LANE - ROPE_DIM), BF16)], axis=1)
        w_uq_k = jnp.pad(full['mla_w_uq', j].reshape(R, H, NOPE_DIM + ROPE_DIM),
                         ((0, 0), (0, 0), (0, HEAD_PAD - NOPE_DIM - ROPE_DIM))).reshape(R, H * HEAD_PAD)
        w_ukv = full['mla_w_ukv', j].reshape(R, H, NOPE_DIM + V_DIM)
        w_uk_k = jnp.pad(w_ukv[..., :NOPE_DIM], ((0, 0), (0, 0), (0, HEAD_PAD - NOPE_DIM))).reshape(R, H * HEAD_PAD)
        w_uv_k = w_ukv[..., NOPE_DIM:].reshape(R, H * V_DIM)
        return w_in_k, w_uq_k, w_uk_k, w_uv_k

    def fmm(name, a, b, mode, out_dtype=F32):
        if name not in AG_PLAN:
            return mm(name, a, b, mode, out_dtype)
        out, carried = mm(name, a, b, mode, out_dtype, ag_rider(name))
        ag_done(name, carried)
        return out

    mla_w = {}
    tabs = _rope_tables(positions[0], S)

    h = x[0]
    saved = []
    for i in range(L):
        j = i // 2
        sv = dict(h=h)
        xn = rms_fwd(f"pre_norm_{i}", h, pre_norm[i:i + 1], S, D, ts)
        sv['xn'] = xn
        if i % 2 == 0:
            z = fmm(f"pool_in_{i}", xn, full['pool_w_in', j], 'nn')
            pooled = pool_fwd(f"pool_window_{i}", z, S, E, NG, tsw)
            mixed = fmm(f"pool_group_{i}", pooled, full['pool_w_group', j], 'nn')
            y = pool_gate_fwd(f"pool_gate_{i}", mixed, z, pool_scale[j:j + 1], S, E, NG, tsw)
            out = fmm(f"pool_out_{i}", y, full['pool_w_out', j], 'nn')
            sv.update(z=z, pooled=pooled, mixed=mixed, y=y)
        else:
            w_in_k, w_uq_k, w_uk_k, w_uv_k = mla_w[j] = mla_kernel_weights(j)
            z = fmm(f"mla_in_{i}", xn, w_in_k, 'nn')
            qn, kvn, kper = mla_prep_fwd(f"mla_prep_{i}", z, q_norm[j:j + 1], kv_norm[j:j + 1], tabs, S, EM, R, ts)
            q_raw = fmm(f"mla_uq_{i}", qn, w_uq_k, 'nn')
            k_raw = fmm(f"mla_uk_{i}", kvn, w_uk_k, 'nn')
            vv = fmm(f"mla_uv_{i}", kvn, w_uv_k, 'nn', BF16)
            qp, kp = mla_pack_fwd(f"mla_pack_{i}", q_raw, k_raw, kper, tabs, S, H, tsw, sm_scale)
            o, lse, carried = flash_fwd(f"attn_{i}", qp, kp, vv, S, H, ta, ag_rider(f"attn_{i}"))
            if carried is not None:
                ag_done(f"attn_{i}", carried)
            y = mla_gate_fwd(f"mla_gate_{i}", o, z, S, EM, ts)
            out = fmm(f"mla_out_{i}", y, full['mla_w_out', j], 'nn')
            sv.update(z=z, qn=qn, kvn=kvn, qp=qp, kp=kp, vv=vv, o=o, lse=lse, y=y)
        h1, a = post_fwd(f"post_norm_{i}", h, out, post_norm[i:i + 1], ple_norm[i:i + 1], S, D, ts)
        gl = fmm(f"ple_gate_{i}", a, full['ple_w_gate', i], 'nn')
        pp = fmm(f"ple_proj_{i}", p[i, 0], full['ple_w_proj', i], 'nn')
        h = ple_fwd(f"ple_{i}", h1, pp, gl, S, D, ts)
        sv.update(out=out, h1=h1, a=a, gl=gl, pp=pp)
        saved.append(sv)

    dh, loss_acc = loss_fwd_bwd("loss", h, loss_target[0], S, D, ts)
    loss = lax.psum(loss_acc[0, 0] * (0.5 / D), ("x", "y", "c"))

    gw = {n: [None] * wl[n].shape[0] for n in WEIGHTS}
    recv = {}

    def rs_rider(host):
        if host not in RS_PLAN:
            return None
        return Rider([_to_slots(gw[n][l], SHARD_AXIS[n] - 1).astype(BF16) for n, l in RS_PLAN[host]], False)

    def rs_done(host, results):
        for key, r in zip(RS_PLAN[host], results):
            recv[key] = r

    def bmm(name, a, b, mode):
        if name not in RS_PLAN:
            return mm(name, a, b, mode)
        out, carried = mm(name, a, b, mode, F32, rs_rider(name))
        rs_done(name, carried)
        return out

    for i in reversed(range(L)):
        j = i // 2
        sv = saved[i]
        dpp, dgl = ple_bwd(f"ple_bwd_{i}", dh, sv['pp'], sv['gl'], S, D, ts)
        gw['ple_w_proj'][i] = bmm(f"ple_proj_dw_{i}", p[i, 0], dpp, 'tn')
        gw['ple_w_gate'][i] = bmm(f"ple_gate_dw_{i}", sv['a'], dgl, 'tn')
        da = bmm(f"ple_gate_dx_{i}", dgl, full['ple_w_gate', i], 'nt')
        dh1, dout, dpost, dple = post_bwd(f"post_norm_bwd_{i}", da, dh, sv['h1'], sv['out'],
                                          post_norm[i:i + 1], ple_norm[i:i + 1], S, D, ts)
        gw['post_norm'][i], gw['ple_norm'][i] = dpost[0], dple[0]
        xn = sv['xn']
        if i % 2 == 0:
            gw['pool_w_out'][j] = bmm(f"pool_out_dw_{i}", sv['y'], dout, 'tn')
            dy = bmm(f"pool_out_dx_{i}", dout, full['pool_w_out', j], 'nt')
            dmixed, dg, dscale = pool_gate_bwd(f"pool_gate_bwd_{i}", dy, sv['mixed'], sv['z'], pool_scale[j:j + 1],
                                               S, E, NG, tsw)
            gw['pool_scale'][j] = dscale[0]
            gw['pool_w_group'][j] = bmm(f"pool_group_dw_{i}", sv['pooled'], dmixed, 'tn')
            dpooled = bmm(f"pool_group_dx_{i}", dmixed, full['pool_w_group', j], 'nt')
            dz = pool_bwd(f"pool_window_bwd_{i}", dpooled, dg, S, E, NG, tsw)
            gw['pool_w_in'][j] = bmm(f"pool_in_dw_{i}", xn, dz, 'tn')
            dxn = bmm(f"pool_in_dx_{i}", dz, full['pool_w_in', j], 'nt')
        else:
            w_in_k, w_uq_k, w_uk_k, w_uv_k = mla_w[j]
            gw['mla_w_out'][j] = bmm(f"mla_out_dw_{i}", sv['y'], dout, 'tn')
            dy = bmm(f"mla_out_dx_{i}", dout, full['mla_w_out', j], 'nt')
            do, dg, delta = mla_gate_bwd(f"mla_gate_bwd_{i}", dy, sv['o'], sv['z'], S, EM, H, ts)
            dqp, carried = flash_bwd_dq(f"attn_dq_{i}", sv['qp'], sv['kp'], sv['vv'], do, sv['lse'], delta, S, H, ta,
                                        rs_rider(f"attn_dq_{i}"))
            if carried is not None:
                rs_done(f"attn_dq_{i}", carried)
            dkp, dvv = flash_bwd_dkv(f"attn_dkv_{i}", sv['qp'], sv['kp'], sv['vv'], do, sv['lse'], delta, S, H, ta)
            dq_raw, dk_raw, dkpe = mla_unpack_bwd(f"mla_pack_bwd_{i}", dqp, dkp, tabs, S, H, tsw, sm_scale)
            g_uq = bmm(f"mla_uq_dw_{i}", sv['qn'], dq_raw, 'tn')
            g_uk = bmm(f"mla_uk_dw_{i}", sv['kvn'], dk_raw, 'tn')
            g_uv = bmm(f"mla_uv_dw_{i}", sv['kvn'], dvv, 'tn')
            dqn = bmm(f"mla_uq_dx_{i}", dq_raw, w_uq_k, 'nt')
            dkvn_k = bmm(f"mla_uk_dx_{i}", dk_raw, w_uk_k, 'nt')
            dkvn_v = bmm(f"mla_uv_dx_{i}", dvv, w_uv_k, 'nt')
            dz, dqg, dkvg = mla_prep_bwd(f"mla_prep_bwd_{i}", dqn, dkvn_k, dkvn_v, sv['z'], dkpe, dg,
                                         q_norm[j:j + 1], kv_norm[j:j + 1], tabs, S, EM, R, ts)
            g_in = bmm(f"mla_in_dw_{i}", xn, dz, 'tn')
            dxn = bmm(f"mla_in_dx_{i}", dz, w_in_k, 'nt')
            gw['mla_q_norm'][j], gw['mla_kv_norm'][j] = dqg[0], dkvg[0]
            gw['mla_w_in'][j] = jnp.concatenate([g_in[:, EM:EM + 2 * R + ROPE_DIM], g_in[:, :EM]], axis=1)
            gw['mla_w_uq'][j] = g_uq.reshape(R, H, HEAD_PAD)[:, :, :NOPE_DIM + ROPE_DIM].reshape(R, -1)
            gw['mla_w_ukv'][j] = jnp.concatenate(
                [g_uk.reshape(R, H, HEAD_PAD)[:, :, :NOPE_DIM], g_uv.reshape(R, H, V_DIM)], axis=2).reshape(R, -1)
        dh, dpre = pre_bwd(f"pre_norm_bwd_{i}", dxn, dh1, sv['h'], pre_norm[i:i + 1], S, D, ts)
        gw['pre_norm'][i] = dpre[0]
    grad_x = dh[None]

    last = exchange("scatter_last", [_to_slots(gw[n][l], SHARD_AXIS[n] - 1).astype(BF16) for n, l in RS_LAST], False)
    for key, r in zip(RS_LAST, last):
        recv[key] = r
    small_names = SMALL_REPL + SMALL_SHARD
    gw = {n: jnp.stack(gw[n]) for n in small_names}
    small_g = jnp.concatenate([gw[n].reshape(1, -1) for n in small_names], axis=1)
    small_all = exchange("gather_small_grads", [small_g], True)[0]

    outs = {}
    for n in BIG:
        shp = wl[n].shape
        three = lambda a: a.reshape(shp[0], -1, shp[-1])
        res = None
        for l in range(shp[0]):
            res = adamw(f"adamw_{n}_{l}", recv[n, l], three(wl[n]), three(ml[n]), three(vl[n]), l, res)
        outs[n] = [a.reshape(shp) for a in res]

    pieces, off = [], 0
    for n in small_names:
        sz = gw[n].size
        g = small_all[:, :, off:off + sz]
        off += sz
        if n in SMALL_SHARD:
            rows_, cols_ = gw[n].shape
            g = lax.dynamic_slice_in_dim(g.reshape(N_DEV, rows_, cols_), me * (cols_ // N_DEV), cols_ // N_DEV, axis=2)
            g = g.reshape(N_DEV, 1, -1)
        pieces.append(g)
    gs = jnp.concatenate(pieces, axis=2)
    flat = lambda d: jnp.concatenate([d[n].reshape(1, -1) for n in small_names], axis=1)
    res = adamw("adamw_small", gs, flat(wl)[None], flat(ml)[None], flat(vl)[None])
    off = 0
    for n in small_names:
        sz = wl[n].size
        outs[n] = [a[0, :, off:off + sz].reshape(wl[n].shape) for a in res]
        off += sz

    return (loss, grad_x, *[outs[n][0] for n in WEIGHTS], *[outs[n][1] for n in WEIGHTS],
            *[outs[n][2] for n in WEIGHTS], *[outs[n][3] for n in WEIGHTS])
```

```python
import functools
import math

import jax
import jax.numpy as jnp
from jax import lax
from jax.experimental import pallas as pl
from jax.experimental.pallas import tpu as pltpu

F32 = jnp.float32
BF16 = jnp.bfloat16

N_DEV = 8
EPS = 1e-6
ROPE_THETA = 10000.0
NOPE_DIM = 128
ROPE_DIM = 64
V_DIM = 128
HEAD_PAD = 256
LANE = 128
POOL_WINDOWS = (2, 4, 8, 16)
POOL_HALO = 16
NEG_INF = -1e30
ADAM_LR = 0.001
ADAM_B1 = 0.9
ADAM_B2 = 0.999
ADAM_EPS = 1e-08
ADAM_WD = 0.01
ADAM_STEP = 10
VMEM_LIMIT_BYTES = 56 * 1024 * 1024
MESH = pl.DeviceIdType.MESH

SHARD_AXIS = dict(pre_norm=None, post_norm=None, pool_w_in=2, pool_w_group=2, pool_scale=None, pool_w_out=1,
                  mla_w_in=2, mla_q_norm=1, mla_w_uq=2, mla_kv_norm=1, mla_w_ukv=2, mla_w_out=1,
                  ple_norm=None, ple_w_gate=1, ple_w_proj=2)
WEIGHTS = tuple(SHARD_AXIS)
BIG = ('pool_w_in', 'pool_w_group', 'pool_w_out', 'mla_w_in', 'mla_w_uq', 'mla_w_ukv', 'mla_w_out',
       'ple_w_gate', 'ple_w_proj')
SMALL_REPL = ('pre_norm', 'post_norm', 'pool_scale', 'ple_norm')
SMALL_SHARD = ('mla_q_norm', 'mla_kv_norm')

AG_PLAN = {
    "pool_in_0": [("pool_w_group", 0), ("pool_w_out", 0)],
    "pool_group_0": [("ple_w_gate", 0), ("ple_w_proj", 0)],
    "pool_out_0": [("mla_w_in", 0)],
    "ple_gate_0": [("mla_w_uq", 0), ("mla_w_ukv", 0)],
    "attn_1": [("mla_w_out", 0), ("ple_w_gate", 1), ("ple_w_proj", 1), ("pool_w_in", 1), ("pool_w_group", 1),
               ("pool_w_out", 1), ("ple_w_gate", 2), ("ple_w_proj", 2)],
    "pool_in_2": [("mla_w_in", 1), ("mla_w_uq", 1), ("mla_w_ukv", 1)],
    "attn_3": [("mla_w_out", 1), ("ple_w_gate", 3), ("ple_w_proj", 3)],
}
RS_PLAN = {
    "attn_bwd_3": [("ple_w_gate", 3), ("ple_w_proj", 3), ("mla_w_out", 1)],
    "ple_gate_dx_2": [("mla_w_uq", 1), ("mla_w_ukv", 1)],
    "pool_out_dw_2": [("ple_w_gate", 2), ("ple_w_proj", 2)],
    "pool_out_dx_2": [("mla_w_in", 1)],
    "pool_in_dx_2": [("pool_w_out", 1)],
    "attn_bwd_1": [("pool_w_group", 1), ("pool_w_in", 1), ("ple_w_gate", 1), ("ple_w_proj", 1), ("mla_w_out", 0)],
    "ple_gate_dx_0": [("mla_w_uq", 0), ("mla_w_ukv", 0)],
    "pool_out_dw_0": [("ple_w_gate", 0), ("ple_w_proj", 0)],
    "pool_out_dx_0": [("mla_w_in", 0)],
    "pool_in_dx_0": [("pool_w_out", 0), ("pool_w_group", 0)],
}
RS_LAST = [("pool_w_in", 0)]


def _pcall(body, **kw):
    return pl.pallas_call(body, **kw)


def _cparams(*sem):
    return pltpu.CompilerParams(dimension_semantics=sem, vmem_limit_bytes=VMEM_LIMIT_BYTES)


def _pick(n, cands):
    for c in cands:
        if n % c == 0:
            return c
    return n


def _sigmoid(x):
    return 1.0 / (1.0 + jnp.exp(-x))


def mm(name, a, b, mode, out_dtype=F32, rider=None):
    squeeze = a.ndim == 2
    if squeeze:
        a, b = a[None], b[None]
    G = a.shape[0]
    if mode == 'nn':
        M, K = a.shape[1:]
        N = b.shape[2]
    elif mode == 'tn':
        K, M = a.shape[1:]
        N = b.shape[2]
    else:
        M, K = a.shape[1:]
        N = b.shape[1]
    tm = _pick(M, (1024, 512, 256, 128))
    tn = _pick(N, (1024, 768, 640, 512, 384, 256, 128))
    tk = _pick(K, (512, 640, 384, 256, 128))
    nk = K // tk
    if mode == 'nn':
        a_spec = pl.BlockSpec((None, tm, tk), lambda g, i, j, k: (g, i, k))
        b_spec = pl.BlockSpec((None, tk, tn), lambda g, i, j, k: (g, k, j))
        dims = (((1,), (0,)), ((), ()))
    elif mode == 'tn':
        a_spec = pl.BlockSpec((None, tk, tm), lambda g, i, j, k: (g, k, i))
        b_spec = pl.BlockSpec((None, tk, tn), lambda g, i, j, k: (g, k, j))
        dims = (((0,), (0,)), ((), ()))
    else:
        a_spec = pl.BlockSpec((None, tm, tk), lambda g, i, j, k: (g, i, k))
        b_spec = pl.BlockSpec((None, tn, tk), lambda g, i, j, k: (g, j, k))
        dims = (((1,), (1,)), ((), ()))

    def body(a_ref, b_ref, o_ref, acc_ref):
        k = pl.program_id(3)

        @pl.when(k == 0)
        def _():
            acc_ref[...] = jnp.zeros_like(acc_ref)

        acc_ref[...] += lax.dot_general(a_ref[...].astype(BF16), b_ref[...].astype(BF16), dims,
                                        preferred_element_type=F32)

        @pl.when(k == nk - 1)
        def _():
            o_ref[...] = acc_ref[...].astype(out_dtype)

    (out,), carried = _call(
        body, name, [a, b], [a_spec, b_spec], [jax.ShapeDtypeStruct((G, M, N), out_dtype)],
        [pl.BlockSpec((None, tm, tn), lambda g, i, j, k: (g, i, j))], (G, M // tm, N // tn, nk),
        [pltpu.VMEM((tm, tn), F32)], ("parallel", "parallel", "parallel", "arbitrary"), rider)
    out = out[0] if squeeze else out
    return out if rider is None else (out, carried)


def rows(ts, width, colblk=0):
    return pl.BlockSpec((ts, width), lambda i: (i, colblk))


def whole(shape):
    return pl.BlockSpec(shape, lambda i: (0,) * len(shape))


def rowwise(name, fn, S, ts, ins, outs, accs=(), scratch=(), reverse=False):
    n_in, n_out, n_acc = len(ins), len(outs), len(accs)
    nt = S // ts

    def body(*refs):
        step = pl.program_id(0)
        i = nt - 1 - step if reverse else step
        in_refs = refs[:n_in]
        out_refs = refs[n_in:n_in + n_out]
        acc_refs = refs[n_in + n_out:n_in + n_out + n_acc]
        scr = refs[n_in + n_out + n_acc:]

        @pl.when(step == 0)
        def _():
            for r in acc_refs:
                r[...] = jnp.zeros_like(r)

        fn(i, step, in_refs, out_refs, acc_refs, scr)

    def fix(spec):
        if not reverse:
            return spec
        imap = spec.index_map
        return pl.BlockSpec(spec.block_shape, lambda s: imap(nt - 1 - s))

    res = _pcall(
        body, name=name,
        out_shape=[jax.ShapeDtypeStruct(s, d) for s, d, _ in outs] + [jax.ShapeDtypeStruct(s, d) for s, d in accs],
        grid=(nt,),
        in_specs=[fix(sp) for _, sp in ins],
        out_specs=[fix(sp) for _, _, sp in outs] + [whole(s) for s, _ in accs],
        scratch_shapes=list(scratch),
        compiler_params=_cparams("arbitrary"),
    )(*[a for a, _ in ins])
    return res


def _rstd(x):
    return lax.rsqrt(jnp.mean(x * x, axis=-1, keepdims=True) + EPS)


def _rms_bwd(dy, x, g):
    r = _rstd(x)
    xh = x * r
    gdy = dy * g
    dx = r * (gdy - xh * jnp.mean(xh * gdy, axis=-1, keepdims=True))
    return dx, jnp.sum(dy * xh, axis=0, keepdims=True)


def _rope(v, cos_t, sin_a, sin_b, sign):
    return v * cos_t + sign * (pltpu.roll(v, LANE - ROPE_DIM // 2, axis=1) * sin_a
                               + pltpu.roll(v, ROPE_DIM // 2, axis=1) * sin_b)


def rms_fwd(name, h, gain, S, D, ts):
    def fn(i, step, ins, outs, accs, scr):
        x = ins[0][...]
        outs[0][...] = (x * _rstd(x) * ins[1][...]).astype(BF16)
    return rowwise(name, fn, S, ts, [(h, rows(ts, D)), (gain, whole((1, D)))], [((S, D), BF16, rows(ts, D))])[0]


def post_fwd(name, h, out, post_g, ple_g, S, D, ts):
    def fn(i, step, ins, outs, accs, scr):
        o = ins[1][...]
        h1 = ins[0][...] + o * _rstd(o) * ins[2][...]
        outs[0][...] = h1
        outs[1][...] = (h1 * _rstd(h1) * ins[3][...]).astype(BF16)
    return rowwise(name, fn, S, ts,
                   [(h, rows(ts, D)), (out, rows(ts, D)), (post_g, whole((1, D))), (ple_g, whole((1, D)))],
                   [((S, D), F32, rows(ts, D)), ((S, D), BF16, rows(ts, D))])


def ple_fwd(name, h1, pp, gl, S, D, ts):
    def fn(i, step, ins, outs, accs, scr):
        outs[0][...] = ins[0][...] + ins[1][...] * _sigmoid(ins[2][...])
    return rowwise(name, fn, S, ts, [(h1, rows(ts, D)), (pp, rows(ts, D)), (gl, rows(ts, D))],
                   [((S, D), F32, rows(ts, D))])[0]


def loss_fwd_bwd(name, h, tgt, S, D, ts):
    def fn(i, step, ins, outs, accs, scr):
        e = ins[0][...] - ins[1][...]
        outs[0][...] = e * (1.0 / D)
        accs[0][...] += jnp.broadcast_to(jnp.sum(e * e), (1, LANE))
    return rowwise(name, fn, S, ts, [(h, rows(ts, D)), (tgt, rows(ts, D))], [((S, D), F32, rows(ts, D))],
                   accs=[((1, LANE), F32)])


def ple_bwd(name, dh, pp, gl, S, D, ts):
    def fn(i, step, ins, outs, accs, scr):
        d = ins[0][...]
        gate = _sigmoid(ins[2][...])
        outs[0][...] = (d * gate).astype(BF16)
        outs[1][...] = (d * ins[1][...] * gate * (1.0 - gate)).astype(BF16)
    return rowwise(name, fn, S, ts, [(dh, rows(ts, D)), (pp, rows(ts, D)), (gl, rows(ts, D))],
                   [((S, D), BF16, rows(ts, D)), ((S, D), BF16, rows(ts, D))])


def post_bwd(name, da, dh, h1, out, post_g, ple_g, S, D, ts):
    def fn(i, step, ins, outs, accs, scr):
        dx, dple = _rms_bwd(ins[0][...], ins[2][...], ins[5][...])
        dh1 = ins[1][...] + dx
        dout, dpost = _rms_bwd(dh1, ins[3][...], ins[4][...])
        outs[0][...] = dh1
        outs[1][...] = dout.astype(BF16)
        accs[0][...] += dpost
        accs[1][...] += dple
    return rowwise(name, fn, S, ts,
                   [(da, rows(ts, D)), (dh, rows(ts, D)), (h1, rows(ts, D)), (out, rows(ts, D)),
                    (post_g, whole((1, D))), (ple_g, whole((1, D)))],
                   [((S, D), F32, rows(ts, D)), ((S, D), BF16, rows(ts, D))],
                   accs=[((1, D), F32), ((1, D), F32)])


def pre_bwd(name, dxn, dh1, h, pre_g, S, D, ts):
    def fn(i, step, ins, outs, accs, scr):
        dx, dpre = _rms_bwd(ins[0][...], ins[2][...], ins[3][...])
        outs[0][...] = ins[1][...] + dx
        accs[0][...] += dpre
    return rowwise(name, fn, S, ts,
                   [(dxn, rows(ts, D)), (dh1, rows(ts, D)), (h, rows(ts, D)), (pre_g, whole((1, D)))],
                   [((S, D), F32, rows(ts, D))], accs=[((1, D), F32)])


def _window_sums(ext, w, back):
    n = ext.shape[0]
    s, win = ext, 1
    while win < w:
        s = s + pltpu.roll(s, win if back else n - win, axis=0)
        win *= 2
    return s


def pool_fwd(name, z, S, E, NG, ts):
    G = E // NG

    def fn(i, step, ins, outs, accs, scr):
        carry = scr[0]

        @pl.when(step == 0)
        def _():
            carry[...] = jnp.zeros_like(carry)

        t = i * ts + lax.broadcasted_iota(jnp.int32, (ts, 1), 0)
        for j, w in enumerate(POOL_WINDOWS):
            u = ins[0][:, j * G:(j + 1) * G]
            ext = jnp.concatenate([carry[:, j * G:(j + 1) * G], u], axis=0)
            sw = _window_sums(ext, w, True)[POOL_HALO:, :]
            cnt = jnp.minimum(t + 1, w).astype(F32)
            outs[0][j] = (sw / cnt - u).astype(BF16)
        carry[...] = ins[0][ts - POOL_HALO:, :]

    return rowwise(name, fn, S, ts, [(z, rows(ts, E, 0))],
                   [((NG, S, G), BF16, pl.BlockSpec((NG, ts, G), lambda i: (0, i, 0)))],
                   scratch=[pltpu.VMEM((POOL_HALO, E), F32)])[0]


def pool_bwd(name, dpooled, dg, S, E, NG, ts):
    G = E // NG

    def fn(i, step, ins, outs, accs, scr):
        carry = scr[0]

        @pl.when(step == 0)
        def _():
            carry[...] = jnp.zeros_like(carry)

        t = i * ts + lax.broadcasted_iota(jnp.int32, (ts, 1), 0)
        for j, w in enumerate(POOL_WINDOWS):
            d = ins[0][j]
            e = d / jnp.minimum(t + 1, w).astype(F32)
            ext = jnp.concatenate([e, carry[:, j * G:(j + 1) * G]], axis=0)
            sw = _window_sums(ext, w, False)[:ts, :]
            outs[0][:, j * G:(j + 1) * G] = (sw - d).astype(BF16)
            carry[:, j * G:(j + 1) * G] = e[:POOL_HALO, :]
        outs[0][:, E:] = ins[1][...]

    return rowwise(name, fn, S, ts,
                   [(dpooled, pl.BlockSpec((NG, ts, G), lambda i: (0, i, 0))), (dg, rows(ts, E))],
                   [((S, 2 * E), BF16, rows(ts, 2 * E))],
                   scratch=[pltpu.VMEM((POOL_HALO, E), F32)], reverse=True)[0]


def pool_gate_fwd(name, mixed, z, scale, S, E, NG, ts):
    G = E // NG

    def fn(i, step, ins, outs, accs, scr):
        for j in range(NG):
            sl = slice(j * G, (j + 1) * G)
            g = ins[1][:, sl]
            outs[0][:, sl] = (ins[0][j] * ins[2][:, sl] * (g * _sigmoid(g))).astype(BF16)

    return rowwise(name, fn, S, ts,
                   [(mixed, pl.BlockSpec((NG, ts, G), lambda i: (0, i, 0))), (z, rows(ts, E, 1)),
                    (scale, whole((1, E)))],
                   [((S, E), BF16, rows(ts, E))])[0]


def pool_gate_bwd(name, dy, mixed, z, scale, S, E, NG, ts):
    G = E // NG

    def fn(i, step, ins, outs, accs, scr):
        for j in range(NG):
            sl = slice(j * G, (j + 1) * G)
            d = ins[0][:, sl]
            mx = ins[1][j]
            g = ins[2][:, sl]
            sc = ins[3][:, sl]
            sg = _sigmoid(g)
            si = g * sg
            outs[0][j] = (d * sc * si).astype(BF16)
            outs[1][:, sl] = (d * mx * sc * (sg * (1.0 + g * (1.0 - sg)))).astype(BF16)
            accs[0][:, sl] += jnp.sum(d * mx * si, axis=0, keepdims=True)

    return rowwise(name, fn, S, ts,
                   [(dy, rows(ts, E)), (mixed, pl.BlockSpec((NG, ts, G), lambda i: (0, i, 0))),
                    (z, rows(ts, E, 1)), (scale, whole((1, E)))],
                   [((NG, S, G), BF16, pl.BlockSpec((NG, ts, G), lambda i: (0, i, 0))), ((S, E), BF16, rows(ts, E))],
                   accs=[((1, E), F32)])


def mla_prep_fwd(name, z, qg, kvg, tabs, S, E, R, ts):
    qb, kb, pb = E // R, E // R + 1, (E + 2 * R) // LANE

    def fn(i, step, ins, outs, accs, scr):
        zq, zkv = ins[0][...], ins[1][...]
        outs[0][...] = (zq * _rstd(zq) * ins[3][...]).astype(BF16)
        outs[1][...] = (zkv * _rstd(zkv) * ins[4][...]).astype(BF16)
        outs[2][...] = _rope(ins[2][...], ins[5][...], ins[6][...], ins[7][...], 1.0)

    return rowwise(name, fn, S, ts,
                   [(z, rows(ts, R, qb)), (z, rows(ts, R, kb)), (z, rows(ts, LANE, pb)),
                    (qg, whole((1, R))), (kvg, whole((1, R)))] + [(t, rows(ts, LANE)) for t in tabs],
                   [((S, R), BF16, rows(ts, R)), ((S, R), BF16, rows(ts, R)), ((S, LANE), F32, rows(ts, LANE))])


def mla_pack_fwd(name, q_raw, k_raw, kper, tabs, S, H, ts, scale):
    W = H * HEAD_PAD

    def fn(i, step, ins, outs, accs, scr):
        cos_t, sin_a, sin_b = ins[3][...], ins[4][...], ins[5][...]
        kp = ins[2][...].astype(BF16)
        for h in range(H):
            a, b, c = h * HEAD_PAD, h * HEAD_PAD + NOPE_DIM, (h + 1) * HEAD_PAD
            outs[0][:, a:b] = (ins[0][:, a:b] * scale).astype(BF16)
            outs[0][:, b:c] = (_rope(ins[0][:, b:c], cos_t, sin_a, sin_b, 1.0) * scale).astype(BF16)
            outs[1][:, a:b] = ins[1][:, a:b].astype(BF16)
            outs[1][:, b:c] = kp

    return rowwise(name, fn, S, ts,
                   [(q_raw, rows(ts, W)), (k_raw, rows(ts, W)), (kper, rows(ts, LANE))]
                   + [(t, rows(ts, LANE)) for t in tabs],
                   [((S, W), BF16, rows(ts, W)), ((S, W), BF16, rows(ts, W))])


def mla_gate_fwd(name, o, z, S, E, ts):
    def fn(i, step, ins, outs, accs, scr):
        g = ins[1][...]
        outs[0][...] = (ins[0][...] * (g * _sigmoid(g))).astype(BF16)
    return rowwise(name, fn, S, ts, [(o, rows(ts, E)), (z, rows(ts, E, 0))], [((S, E), BF16, rows(ts, E))])[0]


def mla_gate_bwd(name, dy, o, z, S, E, H, ts):
    def fn(i, step, ins, outs, accs, scr):
        d, ov, g = ins[0][...], ins[1][...], ins[2][...]
        sg = _sigmoid(g)
        outs[0][...] = (d * (g * sg)).astype(BF16)
        outs[1][...] = (d * ov * (sg * (1.0 + g * (1.0 - sg)))).astype(BF16)

    return rowwise(name, fn, S, ts, [(dy, rows(ts, E)), (o, rows(ts, E)), (z, rows(ts, E, 0))],
                   [((S, E), BF16, rows(ts, E)), ((S, E), BF16, rows(ts, E))])


def mla_unpack_q_bwd(name, dqt, tabs, S, H, t, scale):
    W = H * HEAD_PAD

    def fn(i, step, ins, outs, accs, scr):
        cos_t, sin_a, sin_b = ins[1][...], ins[2][...], ins[3][...]
        for h in range(H):
            a, b, c = h * HEAD_PAD, h * HEAD_PAD + NOPE_DIM, (h + 1) * HEAD_PAD
            dq = ins[0][h].T
            outs[0][:, a:b] = (dq[:, :NOPE_DIM] * scale).astype(BF16)
            outs[0][:, b:c] = (_rope(dq[:, NOPE_DIM:], cos_t, sin_a, sin_b, -1.0) * scale).astype(BF16)

    return rowwise(name, fn, S, t,
                   [(dqt, pl.BlockSpec((H, None, HEAD_PAD, t), lambda i: (0, i, 0, 0)))]
                   + [(tb, rows(t, LANE)) for tb in tabs],
                   [((S, W), BF16, rows(t, W))])[0]


def mla_unpack_k_bwd(name, dk, S, H, ts):
    W = H * HEAD_PAD

    def fn(i, step, ins, outs, accs, scr):
        dkpe = jnp.zeros((ts, LANE), F32)
        for h in range(H):
            a, b, c = h * HEAD_PAD, h * HEAD_PAD + NOPE_DIM, (h + 1) * HEAD_PAD
            outs[0][:, a:b] = ins[0][:, a:b].astype(BF16)
            outs[0][:, b:c] = jnp.zeros((ts, LANE), BF16)
            dkpe = dkpe + ins[0][:, b:c]
        outs[1][...] = dkpe

    return rowwise(name, fn, S, ts, [(dk, rows(ts, W))],
                   [((S, W), BF16, rows(ts, W)), ((S, LANE), F32, rows(ts, LANE))])


def mla_prep_bwd(name, dqn, dkvn_k, dkvn_v, z, dkpe, dg, qg, kvg, tabs, S, E, R, ts):
    qb, kb = E // R, E // R + 1
    ZW = E + 2 * R + LANE

    def fn(i, step, ins, outs, accs, scr):
        dzq, dqg = _rms_bwd(ins[0][...], ins[3][...], ins[7][...])
        dzkv, dkvg = _rms_bwd(ins[1][...] + ins[2][...], ins[4][...], ins[8][...])
        outs[0][:, :E] = ins[6][...]
        outs[0][:, E:E + R] = dzq.astype(BF16)
        outs[0][:, E + R:E + 2 * R] = dzkv.astype(BF16)
        outs[0][:, E + 2 * R:] = _rope(ins[5][...], ins[9][...], ins[10][...], ins[11][...], -1.0).astype(BF16)
        accs[0][...] += dqg
        accs[1][...] += dkvg

    return rowwise(name, fn, S, ts,
                   [(dqn, rows(ts, R)), (dkvn_k, rows(ts, R)), (dkvn_v, rows(ts, R)), (z, rows(ts, R, qb)),
                    (z, rows(ts, R, kb)), (dkpe, rows(ts, LANE)), (dg, rows(ts, E)),
                    (qg, whole((1, R))), (kvg, whole((1, R)))] + [(t, rows(ts, LANE)) for t in tabs],
                   [((S, ZW), BF16, rows(ts, ZW))], accs=[((1, R), F32), ((1, R), F32)])


_NT = (((1,), (1,)), ((), ()))
_NN = (((1,), (0,)), ((), ()))


def _causal_mask_t(t):
    return lax.broadcasted_iota(jnp.int32, (t, t), 0) <= lax.broadcasted_iota(jnp.int32, (t, t), 1)


def _tile(i, t):
    return pl.ds(pl.multiple_of(i * t, t), t)


def flash_fwd(name, q, k, vt, S, H, t, rider=None):
    nt = S // t

    def body(q_ref, k_ref, vt_ref, o_ref, lse_ref, m_sc, l_sc, acc_sc):
        i = pl.program_id(1)
        m_sc[...] = jnp.full_like(m_sc, NEG_INF)
        l_sc[...] = jnp.zeros_like(l_sc)
        acc_sc[...] = jnp.zeros_like(acc_sc)
        q = q_ref[...]

        def tile(j, diag):
            s = lax.dot_general(k_ref[_tile(j, t), :], q, _NT, preferred_element_type=F32)
            if diag:
                s = jnp.where(_causal_mask_t(t), s, NEG_INF)
            m_prev = m_sc[...]
            m_new = jnp.maximum(m_prev, jnp.max(s, axis=0, keepdims=True))
            alpha = jnp.exp(m_prev - m_new)
            p = jnp.exp(s - m_new)
            l_sc[...] = alpha * l_sc[...] + jnp.sum(p, axis=0, keepdims=True)
            acc_sc[...] = alpha * acc_sc[...] + lax.dot_general(vt_ref[j], p.astype(BF16), _NN,
                                                                 preferred_element_type=F32)
            m_sc[...] = m_new

        def off_diagonal(j, carry):
            tile(j, False)
            return carry

        lax.fori_loop(0, i, off_diagonal, 0)
        tile(i, True)
        l = l_sc[...]
        o_ref[...] = (acc_sc[...] / l).T
        lse_ref[...] = m_sc[...] + jnp.log(l)

    (o, lse), carried = _call(
        body, name, [q, k, vt],
        [pl.BlockSpec((t, HEAD_PAD), lambda h, i: (i, h)), pl.BlockSpec((S, HEAD_PAD), lambda h, i: (0, h)),
         pl.BlockSpec((None, nt, V_DIM, t), lambda h, i: (h, 0, 0, 0))],
        [jax.ShapeDtypeStruct((S, H * V_DIM), F32), jax.ShapeDtypeStruct((H, nt, 1, t), F32)],
        [pl.BlockSpec((t, V_DIM), lambda h, i: (i, h)), pl.BlockSpec((None, None, 1, t), lambda h, i: (h, i, 0, 0))],
        (H, nt), [pltpu.VMEM((1, t), F32), pltpu.VMEM((1, t), F32), pltpu.VMEM((V_DIM, t), F32)],
        ("parallel", "parallel"), rider)
    return o, lse, carried


def flash_bwd(name, q, k, v, do, o, lse, S, H, t, rider=None):
    nt = S // t

    def body(q_ref, k_ref, v_ref, do_ref, o_ref, lse_ref, dq_ref, dk_ref, dv_ref, kt_sc, dl_sc):
        j = pl.program_id(1)

        @pl.when(j == 0)
        def _():
            dq_ref[...] = jnp.zeros_like(dq_ref)
            ones = jnp.ones((8, V_DIM), BF16)
            for i in range(nt):
                x = do_ref[i * t:(i + 1) * t, :].astype(F32) * o_ref[i * t:(i + 1) * t, :]
                hi = x.astype(BF16)
                lo = (x - hi.astype(F32)).astype(BF16)
                dl_sc[i] = (lax.dot_general(ones, hi, _NT, preferred_element_type=F32)
                            + lax.dot_general(ones, lo, _NT, preferred_element_type=F32))

        kj, vj = k_ref[...], v_ref[...]
        kt_sc[...] = kj.astype(F32).T.astype(BF16)
        dk_ref[...] = jnp.zeros_like(dk_ref)
        dv_ref[...] = jnp.zeros_like(dv_ref)

        def tile(i, diag):
            qi, doi = q_ref[_tile(i, t), :], do_ref[_tile(i, t), :]
            s = lax.dot_general(kj, qi, _NT, preferred_element_type=F32)
            p = jnp.exp(s - lse_ref[i])
            if diag:
                p = jnp.where(_causal_mask_t(t), p, 0.0)
            dv_ref[...] += lax.dot_general(p.astype(BF16), doi, _NN, preferred_element_type=F32)
            dp = lax.dot_general(vj, doi, _NT, preferred_element_type=F32)
            ds = (p * (dp - dl_sc[i, 0:1, :])).astype(BF16)
            dk_ref[...] += lax.dot_general(ds, qi, _NN, preferred_element_type=F32)
            dq_ref[i] += lax.dot_general(kt_sc[...], ds, _NN, preferred_element_type=F32)

        def off_diagonal(i, carry):
            tile(i, False)
            return carry

        tile(j, True)
        lax.fori_loop(j + 1, nt, off_diagonal, 0)

    head = lambda w: pl.BlockSpec((S, w), lambda h, j: (0, h))
    ktile = lambda w: pl.BlockSpec((t, w), lambda h, j: (j, h))
    (dq, dk, dv), carried = _call(
        body, name, [q, k, v, do, o, lse],
        [head(HEAD_PAD), ktile(HEAD_PAD), ktile(V_DIM), head(V_DIM), head(V_DIM),
         pl.BlockSpec((None, nt, 1, t), lambda h, j: (h, 0, 0, 0))],
        [jax.ShapeDtypeStruct((H, nt, HEAD_PAD, t), F32), jax.ShapeDtypeStruct((S, H * HEAD_PAD), F32),
         jax.ShapeDtypeStruct((S, H * V_DIM), F32)],
        [pl.BlockSpec((None, nt, HEAD_PAD, t), lambda h, j: (h, 0, 0, 0)), ktile(HEAD_PAD), ktile(V_DIM)],
        (H, nt), [pltpu.VMEM((HEAD_PAD, t), BF16), pltpu.VMEM((nt, 8, t), F32)],
        ("parallel", "arbitrary"), rider)
    return dq, dk, dv, carried


def _peers():
    x, y, c = lax.axis_index("x"), lax.axis_index("y"), lax.axis_index("c")
    me = 4 * x + 2 * y + c
    peers = []
    for fx, fy, fc in ((0, 0, 1), (1, 0, 0), (0, 1, 0), (1, 1, 0), (1, 0, 1), (0, 1, 1), (1, 1, 1)):
        px, py, pc = x ^ fx, y ^ fy, c ^ fc
        peers.append(((px, py, pc), 4 * px + 2 * py + pc))
    return me, peers


def _hbm_specs(n):
    return [pl.BlockSpec(memory_space=pl.ANY)] * n


class Rider:
    def __init__(self, arrs, gather):
        self.arrs, self.gather, self.n = list(arrs), gather, len(arrs)
        self.out_shapes = [jax.ShapeDtypeStruct((N_DEV,) + a.shape if gather else a.shape, a.dtype) for a in arrs]
        self.sems = [pltpu.SemaphoreType.DMA((self.n, N_DEV - 1)), pltpu.SemaphoreType.DMA((self.n, N_DEV - 1)),
                     pltpu.SemaphoreType.DMA((self.n,))]

    def _copies(self, srcs, dsts, sems):
        send_sems, recv_sems, local_sems = sems
        me, peers = _peers()
        src = (lambda a, pid: srcs[a]) if self.gather else (lambda a, pid: srcs[a].at[pid])
        local = [pltpu.make_async_copy(src(a, me), dsts[a].at[me], local_sems.at[a]) for a in range(self.n)]

        def remote(a, k, arrival):
            peer, pid = peers[k]
            return pltpu.make_async_remote_copy(
                src_ref=src(a, pid), dst_ref=dsts[a].at[pid if arrival else me],
                send_sem=send_sems.at[a, k], recv_sem=recv_sems.at[a, k], device_id=peer, device_id_type=MESH)

        return local, remote

    def start(self, srcs, dsts, sems):
        local, remote = self._copies(srcs, dsts, sems)
        for a in range(self.n):
            local[a].start()
            for k in range(N_DEV - 1):
                remote(a, k, False).start()

    def wait(self, srcs, dsts, sems):
        local, remote = self._copies(srcs, dsts, sems)
        for a in range(self.n):
            for k in range(N_DEV - 1):
                remote(a, k, True).wait_recv()
        for a in range(self.n):
            for k in range(N_DEV - 1):
                remote(a, k, False).wait_send()
            local[a].wait()


def _carry(body, n_in, n_out, rider, grid):
    n = rider.n

    def wrapped(*refs):
        ins, r_in = refs[:n_in], refs[n_in:n_in + n]
        outs = refs[n_in + n:n_in + n + n_out]
        r_out = refs[n_in + n + n_out:n_in + 2 * n + n_out]
        scratch, sems = refs[n_in + 2 * n + n_out:-3], refs[-3:]
        ids = [pl.program_id(d) for d in range(len(grid))]
        first = functools.reduce(jnp.logical_and, [i == 0 for i in ids])
        last = functools.reduce(jnp.logical_and, [i == g - 1 for i, g in zip(ids, grid)])

        @pl.when(first)
        def _():
            rider.start(r_in, r_out, sems)

        body(*ins, *outs, *scratch)

        @pl.when(last)
        def _():
            rider.wait(r_in, r_out, sems)

    return wrapped


def _call(body, name, ins, in_specs, out_shape, out_specs, grid, scratch, sem, rider=None):
    if rider is None:
        return _pcall(body, name=name, out_shape=list(out_shape), grid=grid, in_specs=list(in_specs),
                      out_specs=list(out_specs), scratch_shapes=list(scratch), compiler_params=_cparams(*sem))(*ins), None
    res = _pcall(
        _carry(body, len(ins), len(out_shape), rider, grid), name=name,
        out_shape=list(out_shape) + rider.out_shapes, grid=grid,
        in_specs=list(in_specs) + _hbm_specs(rider.n), out_specs=list(out_specs) + _hbm_specs(rider.n),
        scratch_shapes=list(scratch) + rider.sems, compiler_params=_cparams(*(("arbitrary",) * len(grid))),
    )(*ins, *rider.arrs)
    return res[:len(out_shape)], res[len(out_shape):]


def exchange(name, arrs, gather):
    rider = Rider(arrs, gather)

    def body(*refs):
        srcs, dsts, sems = refs[:rider.n], refs[rider.n:2 * rider.n], refs[2 * rider.n:]
        rider.start(srcs, dsts, sems)
        rider.wait(srcs, dsts, sems)

    return _pcall(body, name=name, out_shape=rider.out_shapes, in_specs=_hbm_specs(rider.n),
                  out_specs=_hbm_specs(rider.n), scratch_shapes=rider.sems)(*arrs)


def adamw(name, gslots, w, m, v, layer=0, prev=None):
    K, R, C = gslots.shape
    per_row = C * (K * gslots.dtype.itemsize + 7 * 4) * 2
    tr = R
    for cand in (1024, 512, 256, 128, 64, 32, 16, 8):
        if R % cand == 0:
            tr = cand
            if cand * per_row <= VMEM_LIMIT_BYTES // 2:
                break
    c1 = 1.0 / (1.0 - ADAM_B1 ** ADAM_STEP)
    c2 = 1.0 / (1.0 - ADAM_B2 ** ADAM_STEP)

    def body(g_ref, w_ref, m_ref, v_ref, *rest):
        go_ref, d_ref, mo_ref, vo_ref = rest[-4:]
        g = g_ref[0].astype(F32)
        for s in range(1, K):
            g = g + g_ref[s].astype(F32)
        mn = ADAM_B1 * m_ref[...] + (1.0 - ADAM_B1) * g
        vn = ADAM_B2 * v_ref[...] + (1.0 - ADAM_B2) * (g * g)
        go_ref[...] = g
        mo_ref[...] = mn
        vo_ref[...] = vn
        d_ref[...] = -ADAM_LR * ((mn * c1) / (jnp.sqrt(vn * c2) + ADAM_EPS) + ADAM_WD * w_ref[...])

    blk = pl.BlockSpec((None, tr, C), lambda i: (layer, i, 0))
    prev = [] if prev is None else list(prev)
    return _pcall(
        body, name=name, out_shape=[jax.ShapeDtypeStruct(w.shape, F32)] * 4, grid=(R // tr,),
        in_specs=[pl.BlockSpec((K, tr, C), lambda i: (0, i, 0)), blk, blk, blk] + _hbm_specs(len(prev)),
        out_specs=[blk] * 4, input_output_aliases={4 + q: q for q in range(len(prev))},
        compiler_params=_cparams("parallel"),
    )(gslots, w, m, v, *prev)


def _from_slots(gathered, ax):
    g = jnp.moveaxis(gathered, 0, ax)
    s = g.shape
    return g.reshape(s[:ax] + (s[ax] * s[ax + 1],) + s[ax + 2:])


def _to_slots(full, ax):
    s = full.shape
    g = full.reshape(s[:ax] + (N_DEV, s[ax] // N_DEV) + s[ax + 1:])
    g = jnp.moveaxis(g, ax, 0)
    return g.reshape(N_DEV, -1, g.shape[-1])


def _rope_tables(pos, S):
    inv_freq = ROPE_THETA ** (-jnp.arange(0, ROPE_DIM, 2, dtype=F32) / ROPE_DIM)
    ang = pos.astype(F32)[:, None] * inv_freq
    cos, sin = jnp.cos(ang), jnp.sin(ang)
    z = jnp.zeros((S, ROPE_DIM // 2), F32)
    cos_t = jnp.concatenate([cos, cos, z, z], axis=1)
    sin_a = jnp.concatenate([-sin, z, z, z], axis=1)
    sin_b = jnp.concatenate([z, sin, z, z], axis=1)
    return cos_t, sin_a, sin_b


def kernel(x, p, positions, pre_norm, post_norm, pool_w_in, pool_w_group, pool_scale, pool_w_out, mla_w_in, mla_q_norm, mla_w_uq, mla_kv_norm, mla_w_ukv, mla_w_out, ple_norm, ple_w_gate, ple_w_proj, loss_target, m_pre_norm, m_post_norm, m_pool_w_in, m_pool_w_group, m_pool_scale, m_pool_w_out, m_mla_w_in, m_mla_q_norm, m_mla_w_uq, m_mla_kv_norm, m_mla_w_ukv, m_mla_w_out, m_ple_norm, m_ple_w_gate, m_ple_w_proj, v_pre_norm, v_post_norm, v_pool_w_in, v_pool_w_group, v_pool_scale, v_pool_w_out, v_mla_w_in, v_mla_q_norm, v_mla_w_uq, v_mla_kv_norm, v_mla_w_ukv, v_mla_w_out, v_ple_norm, v_ple_w_gate, v_ple_w_proj):
    wl = dict(pre_norm=pre_norm, post_norm=post_norm, pool_w_in=pool_w_in, pool_w_group=pool_w_group,
              pool_scale=pool_scale, pool_w_out=pool_w_out, mla_w_in=mla_w_in, mla_q_norm=mla_q_norm,
              mla_w_uq=mla_w_uq, mla_kv_norm=mla_kv_norm, mla_w_ukv=mla_w_ukv, mla_w_out=mla_w_out,
              ple_norm=ple_norm, ple_w_gate=ple_w_gate, ple_w_proj=ple_w_proj)
    ml = dict(pre_norm=m_pre_norm, post_norm=m_post_norm, pool_w_in=m_pool_w_in, pool_w_group=m_pool_w_group,
              pool_scale=m_pool_scale, pool_w_out=m_pool_w_out, mla_w_in=m_mla_w_in, mla_q_norm=m_mla_q_norm,
              mla_w_uq=m_mla_w_uq, mla_kv_norm=m_mla_kv_norm, mla_w_ukv=m_mla_w_ukv, mla_w_out=m_mla_w_out,
              ple_norm=m_ple_norm, ple_w_gate=m_ple_w_gate, ple_w_proj=m_ple_w_proj)
    vl = dict(pre_norm=v_pre_norm, post_norm=v_post_norm, pool_w_in=v_pool_w_in, pool_w_group=v_pool_w_group,
              pool_scale=v_pool_scale, pool_w_out=v_pool_w_out, mla_w_in=v_mla_w_in, mla_q_norm=v_mla_q_norm,
              mla_w_uq=v_mla_w_uq, mla_kv_norm=v_mla_kv_norm, mla_w_ukv=v_mla_w_ukv, mla_w_out=v_mla_w_out,
              ple_norm=v_ple_norm, ple_w_gate=v_ple_w_gate, ple_w_proj=v_ple_w_proj)

    S, D = x.shape[1], x.shape[2]
    L = pre_norm.shape[0]
    E = pool_scale.shape[1]
    NG = pool_w_group.shape[1]
    R = mla_w_uq.shape[1]
    H = D // 128
    EM = H * V_DIM
    PD = p.shape[-1]
    me = 4 * lax.axis_index("x") + 2 * lax.axis_index("y") + lax.axis_index("c")
    ts = min(S, 256)
    tsw = min(S, 128)
    ta = min(S, 512)
    sm_scale = (NOPE_DIM + ROPE_DIM) ** -0.5

    wb = {n: wl[n].astype(BF16) for n in BIG}
    full = {}

    def ag_rider(host):
        return Rider([wb[n][l] for n, l in AG_PLAN[host]], True) if host in AG_PLAN else None

    def ag_done(host, results):
        for (n, l), g in zip(AG_PLAN[host], results):
            full[n, l] = _from_slots(g, SHARD_AXIS[n] - 1)

    small_sh = jnp.concatenate([wl[n].reshape(1, -1) for n in SMALL_SHARD], axis=1)
    g_in0, g_small = exchange("gather_first", [wb['pool_w_in'][0], small_sh], True)
    full['pool_w_in', 0] = _from_slots(g_in0, SHARD_AXIS['pool_w_in'] - 1)
    nq = mla_q_norm.size
    q_norm = _from_slots(g_small[:, 0, :nq].reshape((N_DEV,) + mla_q_norm.shape), 1)
    kv_norm = _from_slots(g_small[:, 0, nq:].reshape((N_DEV,) + mla_kv_norm.shape), 1)

    def mla_kernel_weights(j):
        w_in = full['mla_w_in', j]
        w_in_k = jnp.concatenate([w_in[:, 2 * R + ROPE_DIM:], w_in[:, :2 * R + ROPE_DIM],
                                  jnp.zeros((D, LANE - ROPE_DIM), BF16)], axis=1)
        w_uq_k = jnp.pad(full['mla_w_uq', j].reshape(R, H, NOPE_DIM + ROPE_DIM),
                         ((0, 0), (0, 0), (0, HEAD_PAD - NOPE_DIM - ROPE_DIM))).reshape(R, H * HEAD_PAD)
        w_ukv = full['mla_w_ukv', j].reshape(R, H, NOPE_DIM + V_DIM)
        w_uk_k = jnp.pad(w_ukv[..., :NOPE_DIM], ((0, 0), (0, 0), (0, HEAD_PAD - NOPE_DIM))).reshape(R, H * HEAD_PAD)
        w_uv_k = w_ukv[..., NOPE_DIM:].reshape(R, H * V_DIM)
        return w_in_k, w_uq_k, w_uk_k, w_uv_k

    def fmm(name, a, b, mode, out_dtype=F32):
        if name not in AG_PLAN:
            return mm(name, a, b, mode, out_dtype)
        out, carried = mm(name, a, b, mode, out_dtype, ag_rider(name))
        ag_done(name, carried)
        return out

    mla_w = {}
    tabs = _rope_tables(positions[0], S)

    h = x[0]
    saved = []
    for i in range(L):
        j = i // 2
        sv = dict(h=h)
        xn = rms_fwd(f"pre_norm_{i}", h, pre_norm[i:i + 1], S, D, ts)
        sv['xn'] = xn
        if i % 2 == 0:
            z = fmm(f"pool_in_{i}", xn, full['pool_w_in', j], 'nn')
            pooled = pool_fwd(f"pool_window_{i}", z, S, E, NG, tsw)
            mixed = fmm(f"pool_group_{i}", pooled, full['pool_w_group', j], 'nn')
            y = pool_gate_fwd(f"pool_gate_{i}", mixed, z, pool_scale[j:j + 1], S, E, NG, tsw)
            out = fmm(f"pool_out_{i}", y, full['pool_w_out', j], 'nn')
            sv.update(z=z, pooled=pooled, mixed=mixed, y=y)
        else:
            w_in_k, w_uq_k, w_uk_k, w_uv_k = mla_w[j] = mla_kernel_weights(j)
            z = fmm(f"mla_in_{i}", xn, w_in_k, 'nn')
            qn, kvn, kper = mla_prep_fwd(f"mla_prep_{i}", z, q_norm[j:j + 1], kv_norm[j:j + 1], tabs, S, EM, R, ts)
            q_raw = fmm(f"mla_uq_{i}", qn, w_uq_k, 'nn')
            k_raw = fmm(f"mla_uk_{i}", kvn, w_uk_k, 'nn')
            vv = fmm(f"mla_uv_{i}", kvn, w_uv_k, 'nn', BF16)
            vt = fmm(f"mla_uvt_{i}", w_uv_k.T, kvn, 'nt', BF16)
            vt = vt.reshape(H, V_DIM, S // ta, ta).transpose(0, 2, 1, 3)
            qp, kp = mla_pack_fwd(f"mla_pack_{i}", q_raw, k_raw, kper, tabs, S, H, tsw, sm_scale)
            o, lse, carried = flash_fwd(f"attn_{i}", qp, kp, vt, S, H, ta, ag_rider(f"attn_{i}"))
            if carried is not None:
                ag_done(f"attn_{i}", carried)
            y = mla_gate_fwd(f"mla_gate_{i}", o, z, S, EM, ts)
            out = fmm(f"mla_out_{i}", y, full['mla_w_out', j], 'nn')
            sv.update(z=z, qn=qn, kvn=kvn, qp=qp, kp=kp, vv=vv, o=o, lse=lse, y=y)
        h1, a = post_fwd(f"post_norm_{i}", h, out, post_norm[i:i + 1], ple_norm[i:i + 1], S, D, ts)
        gl = fmm(f"ple_gate_{i}", a, full['ple_w_gate', i], 'nn')
        pp = fmm(f"ple_proj_{i}", p[i, 0], full['ple_w_proj', i], 'nn')
        h = ple_fwd(f"ple_{i}", h1, pp, gl, S, D, ts)
        sv.update(out=out, h1=h1, a=a, gl=gl, pp=pp)
        saved.append(sv)

    dh, loss_acc = loss_fwd_bwd("loss", h, loss_target[0], S, D, ts)
    loss = lax.psum(loss_acc[0, 0] * (0.5 / D), ("x", "y", "c"))

    gw = {n: [None] * wl[n].shape[0] for n in WEIGHTS}
    recv = {}

    def rs_rider(host):
        if host not in RS_PLAN:
            return None
        return Rider([_to_slots(gw[n][l], SHARD_AXIS[n] - 1).astype(BF16) for n, l in RS_PLAN[host]], False)

    def rs_done(host, results):
        for key, r in zip(RS_PLAN[host], results):
            recv[key] = r

    def bmm(name, a, b, mode):
        if name not in RS_PLAN:
            return mm(name, a, b, mode)
        out, carried = mm(name, a, b, mode, F32, rs_rider(name))
        rs_done(name, carried)
        return out

    for i in reversed(range(L)):
        j = i // 2
        sv = saved[i]
        dpp, dgl = ple_bwd(f"ple_bwd_{i}", dh, sv['pp'], sv['gl'], S, D, ts)
        gw['ple_w_proj'][i] = bmm(f"ple_proj_dw_{i}", p[i, 0], dpp, 'tn')
        gw['ple_w_gate'][i] = bmm(f"ple_gate_dw_{i}", sv['a'], dgl, 'tn')
        da = bmm(f"ple_gate_dx_{i}", dgl, full['ple_w_gate', i], 'nt')
        dh1, dout, dpost, dple = post_bwd(f"post_norm_bwd_{i}", da, dh, sv['h1'], sv['out'],
                                          post_norm[i:i + 1], ple_norm[i:i + 1], S, D, ts)
        gw['post_norm'][i], gw['ple_norm'][i] = dpost[0], dple[0]
        xn = sv['xn']
        if i % 2 == 0:
            gw['pool_w_out'][j] = bmm(f"pool_out_dw_{i}", sv['y'], dout, 'tn')
            dy = bmm(f"pool_out_dx_{i}", dout, full['pool_w_out', j], 'nt')
            dmixed, dg, dscale = pool_gate_bwd(f"pool_gate_bwd_{i}", dy, sv['mixed'], sv['z'], pool_scale[j:j + 1],
                                               S, E, NG, tsw)
            gw['pool_scale'][j] = dscale[0]
            gw['pool_w_group'][j] = bmm(f"pool_group_dw_{i}", sv['pooled'], dmixed, 'tn')
            dpooled = bmm(f"pool_group_dx_{i}", dmixed, full['pool_w_group', j], 'nt')
            dz = pool_bwd(f"pool_window_bwd_{i}", dpooled, dg, S, E, NG, tsw)
            gw['pool_w_in'][j] = bmm(f"pool_in_dw_{i}", xn, dz, 'tn')
            dxn = bmm(f"pool_in_dx_{i}", dz, full['pool_w_in', j], 'nt')
        else:
            w_in_k, w_uq_k, w_uk_k, w_uv_k = mla_w[j]
            gw['mla_w_out'][j] = bmm(f"mla_out_dw_{i}", sv['y'], dout, 'tn')
            dy = bmm(f"mla_out_dx_{i}", dout, full['mla_w_out', j], 'nt')
            do, dg = mla_gate_bwd(f"mla_gate_bwd_{i}", dy, sv['o'], sv['z'], S, EM, H, ts)
            dqt, dkp, dvv, carried = flash_bwd(f"attn_bwd_{i}", sv['qp'], sv['kp'], sv['vv'], do, sv['o'], sv['lse'],
                                               S, H, ta, rs_rider(f"attn_bwd_{i}"))
            if carried is not None:
                rs_done(f"attn_bwd_{i}", carried)
            dq_raw = mla_unpack_q_bwd(f"mla_pack_q_bwd_{i}", dqt, tabs, S, H, ta, sm_scale)
            dk_raw, dkpe = mla_unpack_k_bwd(f"mla_pack_k_bwd_{i}", dkp, S, H, tsw)
            g_uq = bmm(f"mla_uq_dw_{i}", sv['qn'], dq_raw, 'tn')
            g_uk = bmm(f"mla_uk_dw_{i}", sv['kvn'], dk_raw, 'tn')
            g_uv = bmm(f"mla_uv_dw_{i}", sv['kvn'], dvv, 'tn')
            dqn = bmm(f"mla_uq_dx_{i}", dq_raw, w_uq_k, 'nt')
            dkvn_k = bmm(f"mla_uk_dx_{i}", dk_raw, w_uk_k, 'nt')
            dkvn_v = bmm(f"mla_uv_dx_{i}", dvv, w_uv_k, 'nt')
            dz, dqg, dkvg = mla_prep_bwd(f"mla_prep_bwd_{i}", dqn, dkvn_k, dkvn_v, sv['z'], dkpe, dg,
                                         q_norm[j:j + 1], kv_norm[j:j + 1], tabs, S, EM, R, ts)
            g_in = bmm(f"mla_in_dw_{i}", xn, dz, 'tn')
            dxn = bmm(f"mla_in_dx_{i}", dz, w_in_k, 'nt')
            gw['mla_q_norm'][j], gw['mla_kv_norm'][j] = dqg[0], dkvg[0]
            gw['mla_w_in'][j] = jnp.concatenate([g_in[:, EM:EM + 2 * R + ROPE_DIM], g_in[:, :EM]], axis=1)
            gw['mla_w_uq'][j] = g_uq.reshape(R, H, HEAD_PAD)[:, :, :NOPE_DIM + ROPE_DIM].reshape(R, -1)
            gw['mla_w_ukv'][j] = jnp.concatenate(
                [g_uk.reshape(R, H, HEAD_PAD)[:, :, :NOPE_DIM], g_uv.reshape(R, H, V_DIM)], axis=2).reshape(R, -1)
        dh, dpre = pre_bwd(f"pre_norm_bwd_{i}", dxn, dh1, sv['h'], pre_norm[i:i + 1], S, D, ts)
        gw['pre_norm'][i] = dpre[0]
    grad_x = dh[None]

    last = exchange("scatter_last", [_to_slots(gw[n][l], SHARD_AXIS[n] - 1).astype(BF16) for n, l in RS_LAST], False)
    for key, r in zip(RS_LAST, last):
        recv[key] = r
    small_names = SMALL_REPL + SMALL_SHARD
    gw = {n: jnp.stack(gw[n]) for n in small_names}
    small_g = jnp.concatenate([gw[n].reshape(1, -1) for n in small_names], axis=1)
    small_all = exchange("gather_small_grads", [small_g], True)[0]

    outs = {}
    for n in BIG:
        shp = wl[n].shape
        three = lambda a: a.reshape(shp[0], -1, shp[-1])
        res = None
        for l in range(shp[0]):
            res = adamw(f"adamw_{n}_{l}", recv[n, l], three(wl[n]), three(ml[n]), three(vl[n]), l, res)
        outs[n] = [a.reshape(shp) for a in res]

    pieces, off = [], 0
    for n in small_names:
        sz = gw[n].size
        g = small_all[:, :, off:off + sz]
        off += sz
        if n in SMALL_SHARD:
            rows_, cols_ = gw[n].shape
            g = lax.dynamic_slice_in_dim(g.reshape(N_DEV, rows_, cols_), me * (cols_ // N_DEV), cols_ // N_DEV, axis=2)
            g = g.reshape(N_DEV, 1, -1)
        pieces.append(g)
    gs = jnp.concatenate(pieces, axis=2)
    flat = lambda d: jnp.concatenate([d[n].reshape(1, -1) for n in small_names], axis=1)
    res = adamw("adamw_small", gs, flat(wl)[None], flat(ml)[None], flat(vl)[None])
    off = 0
    for n in small_names:
        sz = wl[n].size
        outs[n] = [a[0, :, off:off + sz].reshape(wl[n].shape) for a in res]
        off += sz

    return (loss, grad_x, *[outs[n][0] for n in WEIGHTS], *[outs[n][1] for n in WEIGHTS],
            *[outs[n][2] for n in WEIGHTS], *[outs[n][3] for n in WEIGHTS])
```

```python
import functools
import math

import jax
import jax.numpy as jnp
from jax import lax
from jax.experimental import pallas as pl
from jax.experimental.pallas import tpu as pltpu

F32 = jnp.float32
BF16 = jnp.bfloat16

N_DEV = 8
EPS = 1e-6
ROPE_THETA = 10000.0
NOPE_DIM = 128
ROPE_DIM = 64
V_DIM = 128
HEAD_PAD = 256
LANE = 128
POOL_WINDOWS = (2, 4, 8, 16)
POOL_HALO = 16
NEG_INF = -1e30
ADAM_LR = 0.001
ADAM_B1 = 0.9
ADAM_B2 = 0.999
ADAM_EPS = 1e-08
ADAM_WD = 0.01
ADAM_STEP = 10
VMEM_LIMIT_BYTES = 56 * 1024 * 1024
MM_MAX_TK = 3200
MESH = pl.DeviceIdType.MESH

SHARD_AXIS = dict(pre_norm=None, post_norm=None, pool_w_in=2, pool_w_group=2, pool_scale=None, pool_w_out=1,
                  mla_w_in=2, mla_q_norm=1, mla_w_uq=2, mla_kv_norm=1, mla_w_ukv=2, mla_w_out=1,
                  ple_norm=None, ple_w_gate=1, ple_w_proj=2)
WEIGHTS = tuple(SHARD_AXIS)
BIG = ('pool_w_in', 'pool_w_group', 'pool_w_out', 'mla_w_in', 'mla_w_uq', 'mla_w_ukv', 'mla_w_out',
       'ple_w_gate', 'ple_w_proj')
SMALL_REPL = ('pre_norm', 'post_norm', 'pool_scale', 'ple_norm')
SMALL_SHARD = ('mla_q_norm', 'mla_kv_norm')

AG_PLAN = {
    "pool_in_0": [("pool_w_group", 0), ("pool_w_out", 0)],
    "pool_group_0": [("ple_w_gate", 0), ("ple_w_proj", 0)],
    "pool_out_0": [("mla_w_in", 0)],
    "ple_gate_0": [("mla_w_uq", 0), ("mla_w_ukv", 0)],
    "attn_1": [("mla_w_out", 0), ("ple_w_gate", 1), ("ple_w_proj", 1), ("pool_w_in", 1), ("pool_w_group", 1),
               ("pool_w_out", 1), ("ple_w_gate", 2), ("ple_w_proj", 2)],
    "pool_in_2": [("mla_w_in", 1), ("mla_w_uq", 1), ("mla_w_ukv", 1)],
    "attn_3": [("mla_w_out", 1), ("ple_w_gate", 3), ("ple_w_proj", 3)],
}
RS_PLAN = {
    "attn_bwd_3": [("ple_w_gate", 3), ("ple_w_proj", 3), ("mla_w_out", 1)],
    "ple_gate_dx_2": [("mla_w_uq", 1), ("mla_w_ukv", 1)],
    "pool_out_dw_2": [("ple_w_gate", 2), ("ple_w_proj", 2)],
    "pool_out_dx_2": [("mla_w_in", 1)],
    "pool_in_dx_2": [("pool_w_out", 1)],
    "attn_bwd_1": [("pool_w_group", 1), ("pool_w_in", 1), ("ple_w_gate", 1), ("ple_w_proj", 1), ("mla_w_out", 0)],
    "ple_gate_dx_0": [("mla_w_uq", 0), ("mla_w_ukv", 0)],
    "pool_out_dw_0": [("ple_w_gate", 0), ("ple_w_proj", 0)],
    "pool_out_dx_0": [("mla_w_in", 0)],
    "pool_in_dx_0": [("pool_w_out", 0), ("pool_w_group", 0)],
}
RS_LAST = [("pool_w_in", 0)]


def _pcall(body, **kw):
    return pl.pallas_call(body, **kw)


def _cparams(*sem):
    return pltpu.CompilerParams(dimension_semantics=sem, vmem_limit_bytes=VMEM_LIMIT_BYTES)


def _pick(n, cands):
    for c in cands:
        if n % c == 0:
            return c
    return n


def _sigmoid(x):
    return 1.0 / (1.0 + jnp.exp(-x))


def mm(name, a, b, mode, out_dtype=F32, rider=None):
    squeeze = a.ndim == 2
    if squeeze:
        a, b = a[None], b[None]
    G = a.shape[0]
    if mode == 'nn':
        M, K = a.shape[1:]
        N = b.shape[2]
    elif mode == 'tn':
        K, M = a.shape[1:]
        N = b.shape[2]
    else:
        M, K = a.shape[1:]
        N = b.shape[1]
    tm = _pick(M, (1024, 512, 256, 128))
    tn = _pick(N, (1024, 768, 640, 512, 384, 256, 128))
    tk = K if K <= MM_MAX_TK else _pick(K, (2048, 1024, 640, 512, 384, 256, 128))
    nk = K // tk
    if mode == 'nn':
        a_spec = pl.BlockSpec((None, tm, tk), lambda g, i, j, k: (g, i, k))
        b_spec = pl.BlockSpec((None, tk, tn), lambda g, i, j, k: (g, k, j))
        dims = (((1,), (0,)), ((), ()))
    elif mode == 'tn':
        a_spec = pl.BlockSpec((None, tk, tm), lambda g, i, j, k: (g, k, i))
        b_spec = pl.BlockSpec((None, tk, tn), lambda g, i, j, k: (g, k, j))
        dims = (((0,), (0,)), ((), ()))
    else:
        a_spec = pl.BlockSpec((None, tm, tk), lambda g, i, j, k: (g, i, k))
        b_spec = pl.BlockSpec((None, tn, tk), lambda g, i, j, k: (g, j, k))
        dims = (((1,), (1,)), ((), ()))

    def product(a_ref, b_ref):
        return lax.dot_general(a_ref[...].astype(BF16), b_ref[...].astype(BF16), dims, preferred_element_type=F32)

    def body_one(a_ref, b_ref, o_ref):
        o_ref[...] = product(a_ref, b_ref).astype(out_dtype)

    def body_acc(a_ref, b_ref, o_ref, acc_ref):
        k = pl.program_id(3)

        @pl.when(k == 0)
        def _():
            acc_ref[...] = product(a_ref, b_ref)

        @pl.when(jnp.logical_and(k > 0, k < nk - 1))
        def _():
            acc_ref[...] += product(a_ref, b_ref)

        @pl.when(k == nk - 1)
        def _():
            o_ref[...] = (acc_ref[...] + product(a_ref, b_ref)).astype(out_dtype)

    (out,), carried = _call(
        body_one if nk == 1 else body_acc, name, [a, b], [a_spec, b_spec],
        [jax.ShapeDtypeStruct((G, M, N), out_dtype)],
        [pl.BlockSpec((None, tm, tn), lambda g, i, j, k: (g, i, j))], (G, M // tm, N // tn, nk),
        [] if nk == 1 else [pltpu.VMEM((tm, tn), F32)], ("parallel", "parallel", "parallel", "arbitrary"), rider)
    out = out[0] if squeeze else out
    return out if rider is None else (out, carried)


def rows(ts, width, colblk=0):
    return pl.BlockSpec((ts, width), lambda i: (i, colblk))


def whole(shape):
    return pl.BlockSpec(shape, lambda i: (0,) * len(shape))


def rowwise(name, fn, S, ts, ins, outs, accs=(), scratch=(), reverse=False):
    n_in, n_out, n_acc = len(ins), len(outs), len(accs)
    nt = S // ts

    def body(*refs):
        step = pl.program_id(0)
        i = nt - 1 - step if reverse else step
        in_refs = refs[:n_in]
        out_refs = refs[n_in:n_in + n_out]
        acc_refs = refs[n_in + n_out:n_in + n_out + n_acc]
        scr = refs[n_in + n_out + n_acc:]

        @pl.when(step == 0)
        def _():
            for r in acc_refs:
                r[...] = jnp.zeros_like(r)

        fn(i, step, in_refs, out_refs, acc_refs, scr)

    def fix(spec):
        if not reverse:
            return spec
        imap = spec.index_map
        return pl.BlockSpec(spec.block_shape, lambda s: imap(nt - 1 - s))

    res = _pcall(
        body, name=name,
        out_shape=[jax.ShapeDtypeStruct(s, d) for s, d, _ in outs] + [jax.ShapeDtypeStruct(s, d) for s, d in accs],
        grid=(nt,),
        in_specs=[fix(sp) for _, sp in ins],
        out_specs=[fix(sp) for _, _, sp in outs] + [whole(s) for s, _ in accs],
        scratch_shapes=list(scratch),
        compiler_params=_cparams("arbitrary"),
    )(*[a for a, _ in ins])
    return res


def _rstd(x):
    return lax.rsqrt(jnp.mean(x * x, axis=-1, keepdims=True) + EPS)


def _rms_bwd(dy, x, g):
    r = _rstd(x)
    xh = x * r
    gdy = dy * g
    dx = r * (gdy - xh * jnp.mean(xh * gdy, axis=-1, keepdims=True))
    return dx, jnp.sum(dy * xh, axis=0, keepdims=True)


def _rope(v, cos_t, sin_a, sin_b, sign):
    return v * cos_t + sign * (pltpu.roll(v, LANE - ROPE_DIM // 2, axis=1) * sin_a
                               + pltpu.roll(v, ROPE_DIM // 2, axis=1) * sin_b)


def rms_fwd(name, h, gain, S, D, ts):
    def fn(i, step, ins, outs, accs, scr):
        x = ins[0][...]
        outs[0][...] = (x * _rstd(x) * ins[1][...]).astype(BF16)
    return rowwise(name, fn, S, ts, [(h, rows(ts, D)), (gain, whole((1, D)))], [((S, D), BF16, rows(ts, D))])[0]


def post_fwd(name, h, out, post_g, ple_g, S, D, ts):
    def fn(i, step, ins, outs, accs, scr):
        o = ins[1][...]
        h1 = ins[0][...] + o * _rstd(o) * ins[2][...]
        outs[0][...] = h1
        outs[1][...] = (h1 * _rstd(h1) * ins[3][...]).astype(BF16)
    return rowwise(name, fn, S, ts,
                   [(h, rows(ts, D)), (out, rows(ts, D)), (post_g, whole((1, D))), (ple_g, whole((1, D)))],
                   [((S, D), F32, rows(ts, D)), ((S, D), BF16, rows(ts, D))])


def ple_fwd(name, h1, pp, gl, S, D, ts):
    def fn(i, step, ins, outs, accs, scr):
        outs[0][...] = ins[0][...] + ins[1][...] * _sigmoid(ins[2][...])
    return rowwise(name, fn, S, ts, [(h1, rows(ts, D)), (pp, rows(ts, D)), (gl, rows(ts, D))],
                   [((S, D), F32, rows(ts, D))])[0]


def loss_fwd_bwd(name, h, tgt, S, D, ts):
    def fn(i, step, ins, outs, accs, scr):
        e = ins[0][...] - ins[1][...]
        outs[0][...] = e * (1.0 / D)
        accs[0][...] += jnp.broadcast_to(jnp.sum(e * e), (1, LANE))
    return rowwise(name, fn, S, ts, [(h, rows(ts, D)), (tgt, rows(ts, D))], [((S, D), F32, rows(ts, D))],
                   accs=[((1, LANE), F32)])


def ple_bwd(name, dh, pp, gl, S, D, ts):
    def fn(i, step, ins, outs, accs, scr):
        d = ins[0][...]
        gate = _sigmoid(ins[2][...])
        outs[0][...] = (d * gate).astype(BF16)
        outs[1][...] = (d * ins[1][...] * gate * (1.0 - gate)).astype(BF16)
    return rowwise(name, fn, S, ts, [(dh, rows(ts, D)), (pp, rows(ts, D)), (gl, rows(ts, D))],
                   [((S, D), BF16, rows(ts, D)), ((S, D), BF16, rows(ts, D))])


def post_bwd(name, da, dh, h1, out, post_g, ple_g, S, D, ts):
    def fn(i, step, ins, outs, accs, scr):
        dx, dple = _rms_bwd(ins[0][...], ins[2][...], ins[5][...])
        dh1 = ins[1][...] + dx
        dout, dpost = _rms_bwd(dh1, ins[3][...], ins[4][...])
        outs[0][...] = dh1
        outs[1][...] = dout.astype(BF16)
        accs[0][...] += dpost
        accs[1][...] += dple
    return rowwise(name, fn, S, ts,
                   [(da, rows(ts, D)), (dh, rows(ts, D)), (h1, rows(ts, D)), (out, rows(ts, D)),
                    (post_g, whole((1, D))), (ple_g, whole((1, D)))],
                   [((S, D), F32, rows(ts, D)), ((S, D), BF16, rows(ts, D))],
                   accs=[((1, D), F32), ((1, D), F32)])


def pre_bwd(name, dxn, dh1, h, pre_g, S, D, ts):
    def fn(i, step, ins, outs, accs, scr):
        dx, dpre = _rms_bwd(ins[0][...], ins[2][...], ins[3][...])
        outs[0][...] = ins[1][...] + dx
        accs[0][...] += dpre
    return rowwise(name, fn, S, ts,
                   [(dxn, rows(ts, D)), (dh1, rows(ts, D)), (h, rows(ts, D)), (pre_g, whole((1, D)))],
                   [((S, D), F32, rows(ts, D))], accs=[((1, D), F32)])


def _window_sums(ext, w, back):
    n = ext.shape[0]
    s, win = ext, 1
    while win < w:
        s = s + pltpu.roll(s, win if back else n - win, axis=0)
        win *= 2
    return s


def pool_fwd(name, z, S, E, NG, ts):
    G = E // NG

    def fn(i, step, ins, outs, accs, scr):
        carry = scr[0]

        @pl.when(step == 0)
        def _():
            carry[...] = jnp.zeros_like(carry)

        t = i * ts + lax.broadcasted_iota(jnp.int32, (ts, 1), 0)
        for j, w in enumerate(POOL_WINDOWS):
            u = ins[0][:, j * G:(j + 1) * G]
            ext = jnp.concatenate([carry[:, j * G:(j + 1) * G], u], axis=0)
            sw = _window_sums(ext, w, True)[POOL_HALO:, :]
            cnt = jnp.minimum(t + 1, w).astype(F32)
            outs[0][j] = (sw / cnt - u).astype(BF16)
        carry[...] = ins[0][ts - POOL_HALO:, :]

    return rowwise(name, fn, S, ts, [(z, rows(ts, E, 0))],
                   [((NG, S, G), BF16, pl.BlockSpec((NG, ts, G), lambda i: (0, i, 0)))],
                   scratch=[pltpu.VMEM((POOL_HALO, E), F32)])[0]


def pool_bwd(name, dpooled, dg, S, E, NG, ts):
    G = E // NG

    def fn(i, step, ins, outs, accs, scr):
        carry = scr[0]

        @pl.when(step == 0)
        def _():
            carry[...] = jnp.zeros_like(carry)

        t = i * ts + lax.broadcasted_iota(jnp.int32, (ts, 1), 0)
        for j, w in enumerate(POOL_WINDOWS):
            d = ins[0][j]
            e = d / jnp.minimum(t + 1, w).astype(F32)
            ext = jnp.concatenate([e, carry[:, j * G:(j + 1) * G]], axis=0)
            sw = _window_sums(ext, w, False)[:ts, :]
            outs[0][:, j * G:(j + 1) * G] = (sw - d).astype(BF16)
            carry[:, j * G:(j + 1) * G] = e[:POOL_HALO, :]
        outs[0][:, E:] = ins[1][...]

    return rowwise(name, fn, S, ts,
                   [(dpooled, pl.BlockSpec((NG, ts, G), lambda i: (0, i, 0))), (dg, rows(ts, E))],
                   [((S, 2 * E), BF16, rows(ts, 2 * E))],
                   scratch=[pltpu.VMEM((POOL_HALO, E), F32)], reverse=True)[0]


def pool_gate_fwd(name, mixed, z, scale, S, E, NG, ts):
    G = E // NG

    def fn(i, step, ins, outs, accs, scr):
        for j in range(NG):
            sl = slice(j * G, (j + 1) * G)
            g = ins[1][:, sl]
            outs[0][:, sl] = (ins[0][j] * ins[2][:, sl] * (g * _sigmoid(g))).astype(BF16)

    return rowwise(name, fn, S, ts,
                   [(mixed, pl.BlockSpec((NG, ts, G), lambda i: (0, i, 0))), (z, rows(ts, E, 1)),
                    (scale, whole((1, E)))],
                   [((S, E), BF16, rows(ts, E))])[0]


def pool_gate_bwd(name, dy, mixed, z, scale, S, E, NG, ts):
    G = E // NG

    def fn(i, step, ins, outs, accs, scr):
        for j in range(NG):
            sl = slice(j * G, (j + 1) * G)
            d = ins[0][:, sl]
            mx = ins[1][j]
            g = ins[2][:, sl]
            sc = ins[3][:, sl]
            sg = _sigmoid(g)
            si = g * sg
            outs[0][j] = (d * sc * si).astype(BF16)
            outs[1][:, sl] = (d * mx * sc * (sg * (1.0 + g * (1.0 - sg)))).astype(BF16)
            accs[0][:, sl] += jnp.sum(d * mx * si, axis=0, keepdims=True)

    return rowwise(name, fn, S, ts,
                   [(dy, rows(ts, E)), (mixed, pl.BlockSpec((NG, ts, G), lambda i: (0, i, 0))),
                    (z, rows(ts, E, 1)), (scale, whole((1, E)))],
                   [((NG, S, G), BF16, pl.BlockSpec((NG, ts, G), lambda i: (0, i, 0))), ((S, E), BF16, rows(ts, E))],
                   accs=[((1, E), F32)])


def mla_prep_fwd(name, z, qg, kvg, tabs, S, E, R, ts):
    qb, kb, pb = E // R, E // R + 1, (E + 2 * R) // LANE

    def fn(i, step, ins, outs, accs, scr):
        zq, zkv = ins[0][...], ins[1][...]
        outs[0][...] = (zq * _rstd(zq) * ins[3][...]).astype(BF16)
        outs[1][...] = (zkv * _rstd(zkv) * ins[4][...]).astype(BF16)
        outs[2][...] = _rope(ins[2][...], ins[5][...], ins[6][...], ins[7][...], 1.0)

    return rowwise(name, fn, S, ts,
                   [(z, rows(ts, R, qb)), (z, rows(ts, R, kb)), (z, rows(ts, LANE, pb)),
                    (qg, whole((1, R))), (kvg, whole((1, R)))] + [(t, rows(ts, LANE)) for t in tabs],
                   [((S, R), BF16, rows(ts, R)), ((S, R), BF16, rows(ts, R)), ((S, LANE), F32, rows(ts, LANE))])


def mla_pack_fwd(name, q_raw, k_raw, kper, tabs, S, H, ts, scale):
    W = H * HEAD_PAD

    def fn(i, step, ins, outs, accs, scr):
        cos_t, sin_a, sin_b = ins[3][...], ins[4][...], ins[5][...]
        kp = ins[2][...].astype(BF16)
        for h in range(H):
            a, b, c = h * HEAD_PAD, h * HEAD_PAD + NOPE_DIM, (h + 1) * HEAD_PAD
            outs[0][:, a:b] = (ins[0][:, a:b] * scale).astype(BF16)
            outs[0][:, b:c] = (_rope(ins[0][:, b:c], cos_t, sin_a, sin_b, 1.0) * scale).astype(BF16)
            outs[1][:, a:b] = ins[1][:, a:b].astype(BF16)
            outs[1][:, b:c] = kp

    return rowwise(name, fn, S, ts,
                   [(q_raw, rows(ts, W)), (k_raw, rows(ts, W)), (kper, rows(ts, LANE))]
                   + [(t, rows(ts, LANE)) for t in tabs],
                   [((S, W), BF16, rows(ts, W)), ((S, W), BF16, rows(ts, W))])


def mla_gate_fwd(name, o, z, S, E, ts):
    def fn(i, step, ins, outs, accs, scr):
        g = ins[1][...]
        outs[0][...] = (ins[0][...] * (g * _sigmoid(g))).astype(BF16)
    return rowwise(name, fn, S, ts, [(o, rows(ts, E)), (z, rows(ts, E, 0))], [((S, E), BF16, rows(ts, E))])[0]


def mla_gate_bwd(name, dy, o, z, S, E, H, ts):
    def fn(i, step, ins, outs, accs, scr):
        d, ov, g = ins[0][...], ins[1][...], ins[2][...]
        sg = _sigmoid(g)
        outs[0][...] = (d * (g * sg)).astype(BF16)
        outs[1][...] = (d * ov * (sg * (1.0 + g * (1.0 - sg)))).astype(BF16)

    return rowwise(name, fn, S, ts, [(dy, rows(ts, E)), (o, rows(ts, E)), (z, rows(ts, E, 0))],
                   [((S, E), BF16, rows(ts, E)), ((S, E), BF16, rows(ts, E))])


def mla_unpack_q_bwd(name, dqt, tabs, S, H, t, scale):
    W = H * HEAD_PAD

    def fn(i, step, ins, outs, accs, scr):
        cos_t, sin_a, sin_b = ins[1][...], ins[2][...], ins[3][...]
        for h in range(H):
            a, b, c = h * HEAD_PAD, h * HEAD_PAD + NOPE_DIM, (h + 1) * HEAD_PAD
            dq = ins[0][h].T
            outs[0][:, a:b] = (dq[:, :NOPE_DIM] * scale).astype(BF16)
            outs[0][:, b:c] = (_rope(dq[:, NOPE_DIM:], cos_t, sin_a, sin_b, -1.0) * scale).astype(BF16)

    return rowwise(name, fn, S, t,
                   [(dqt, pl.BlockSpec((H, None, HEAD_PAD, t), lambda i: (0, i, 0, 0)))]
                   + [(tb, rows(t, LANE)) for tb in tabs],
                   [((S, W), BF16, rows(t, W))])[0]


def mla_unpack_k_bwd(name, dk, S, H, ts):
    W = H * HEAD_PAD

    def fn(i, step, ins, outs, accs, scr):
        dkpe = jnp.zeros((ts, LANE), F32)
        for h in range(H):
            a, b, c = h * HEAD_PAD, h * HEAD_PAD + NOPE_DIM, (h + 1) * HEAD_PAD
            outs[0][:, a:b] = ins[0][:, a:b].astype(BF16)
            outs[0][:, b:c] = jnp.zeros((ts, LANE), BF16)
            dkpe = dkpe + ins[0][:, b:c]
        outs[1][...] = dkpe

    return rowwise(name, fn, S, ts, [(dk, rows(ts, W))],
                   [((S, W), BF16, rows(ts, W)), ((S, LANE), F32, rows(ts, LANE))])


def mla_prep_bwd(name, dqn, dkvn_k, dkvn_v, z, dkpe, dg, qg, kvg, tabs, S, E, R, ts):
    qb, kb = E // R, E // R + 1
    ZW = E + 2 * R + LANE

    def fn(i, step, ins, outs, accs, scr):
        dzq, dqg = _rms_bwd(ins[0][...], ins[3][...], ins[7][...])
        dzkv, dkvg = _rms_bwd(ins[1][...] + ins[2][...], ins[4][...], ins[8][...])
        outs[0][:, :E] = ins[6][...]
        outs[0][:, E:E + R] = dzq.astype(BF16)
        outs[0][:, E + R:E + 2 * R] = dzkv.astype(BF16)
        outs[0][:, E + 2 * R:] = _rope(ins[5][...], ins[9][...], ins[10][...], ins[11][...], -1.0).astype(BF16)
        accs[0][...] += dqg
        accs[1][...] += dkvg

    return rowwise(name, fn, S, ts,
                   [(dqn, rows(ts, R)), (dkvn_k, rows(ts, R)), (dkvn_v, rows(ts, R)), (z, rows(ts, R, qb)),
                    (z, rows(ts, R, kb)), (dkpe, rows(ts, LANE)), (dg, rows(ts, E)),
                    (qg, whole((1, R))), (kvg, whole((1, R)))] + [(t, rows(ts, LANE)) for t in tabs],
                   [((S, ZW), BF16, rows(ts, ZW))], accs=[((1, R), F32), ((1, R), F32)])


_NT = (((1,), (1,)), ((), ()))
_NN = (((1,), (0,)), ((), ()))
_TN = (((0,), (0,)), ((), ()))


def _causal_mask_t(t):
    return lax.broadcasted_iota(jnp.int32, (t, t), 0) <= lax.broadcasted_iota(jnp.int32, (t, t), 1)


def _tile(i, t):
    return pl.ds(pl.multiple_of(i * t, t), t)


def flash_fwd(name, q, k, v, S, H, t, rider=None):
    nt = S // t

    def body(q_ref, k_ref, v_ref, o_ref, lse_ref, m_sc, l_sc, acc_sc):
        i = pl.program_id(1)
        m_sc[...] = jnp.full_like(m_sc, NEG_INF)
        l_sc[...] = jnp.zeros_like(l_sc)
        acc_sc[...] = jnp.zeros_like(acc_sc)
        q = q_ref[...]

        def tile(j, diag):
            s = lax.dot_general(k_ref[_tile(j, t), :], q, _NT, preferred_element_type=F32)
            if diag:
                s = jnp.where(_causal_mask_t(t), s, NEG_INF)
            m_prev = m_sc[...]
            m_new = jnp.maximum(m_prev, jnp.max(s, axis=0, keepdims=True))
            alpha = jnp.exp(m_prev - m_new)
            p = jnp.exp(s - m_new)
            l_sc[...] = alpha * l_sc[...] + jnp.sum(p, axis=0, keepdims=True)
            acc_sc[...] = alpha * acc_sc[...] + lax.dot_general(v_ref[_tile(j, t), :], p.astype(BF16), _TN,
                                                                 preferred_element_type=F32)
            m_sc[...] = m_new

        def off_diagonal(j, carry):
            tile(j, False)
            return carry

        lax.fori_loop(0, i, off_diagonal, 0)
        tile(i, True)
        l = l_sc[...]
        o_ref[...] = (acc_sc[...] / l).T
        lse_ref[...] = m_sc[...] + jnp.log(l)

    (o, lse), carried = _call(
        body, name, [q, k, v],
        [pl.BlockSpec((t, HEAD_PAD), lambda h, i: (i, h)), pl.BlockSpec((S, HEAD_PAD), lambda h, i: (0, h)),
         pl.BlockSpec((S, V_DIM), lambda h, i: (0, h))],
        [jax.ShapeDtypeStruct((S, H * V_DIM), F32), jax.ShapeDtypeStruct((H, nt, 1, t), F32)],
        [pl.BlockSpec((t, V_DIM), lambda h, i: (i, h)), pl.BlockSpec((None, None, 1, t), lambda h, i: (h, i, 0, 0))],
        (H, nt), [pltpu.VMEM((1, t), F32), pltpu.VMEM((1, t), F32), pltpu.VMEM((V_DIM, t), F32)],
        ("parallel", "parallel"), rider)
    return o, lse, carried


def flash_bwd(name, q, k, v, do, o, lse, S, H, t, rider=None):
    nt = S // t

    def body(q_ref, k_ref, v_ref, do_ref, o_ref, lse_ref, dq_ref, dk_ref, dv_ref, kt_sc, dl_sc, dv_sc):
        j = pl.program_id(1)

        @pl.when(j == 0)
        def _():
            dq_ref[...] = jnp.zeros_like(dq_ref)
            ones = jnp.ones((8, V_DIM), BF16)
            for i in range(nt):
                x = do_ref[i * t:(i + 1) * t, :].astype(F32) * o_ref[i * t:(i + 1) * t, :]
                hi = x.astype(BF16)
                lo = (x - hi.astype(F32)).astype(BF16)
                dl_sc[i] = (lax.dot_general(ones, hi, _NT, preferred_element_type=F32)
                            + lax.dot_general(ones, lo, _NT, preferred_element_type=F32))

        kj, vj = k_ref[...], v_ref[...]
        kt_sc[...] = kj.astype(F32).T.astype(BF16)
        dk_ref[...] = jnp.zeros_like(dk_ref)
        dv_sc[...] = jnp.zeros_like(dv_sc)

        def tile(i, diag):
            qi, doi = q_ref[_tile(i, t), :], do_ref[_tile(i, t), :]
            s = lax.dot_general(kj, qi, _NT, preferred_element_type=F32)
            p = jnp.exp(s - lse_ref[i])
            if diag:
                p = jnp.where(_causal_mask_t(t), p, 0.0)
            dv_sc[...] += lax.dot_general(p.astype(BF16), doi, _NN, preferred_element_type=F32)
            dp = lax.dot_general(vj, doi, _NT, preferred_element_type=F32)
            ds = (p * (dp - dl_sc[i, 0:1, :])).astype(BF16)
            dk_ref[...] += lax.dot_general(ds, qi, _NN, preferred_element_type=F32)
            dq_ref[i] += lax.dot_general(kt_sc[...], ds, _NN, preferred_element_type=F32)

        def off_diagonal(i, carry):
            tile(i, False)
            return carry

        tile(j, True)
        lax.fori_loop(j + 1, nt, off_diagonal, 0)
        dv_ref[...] = dv_sc[...].astype(BF16)

    head = lambda w: pl.BlockSpec((S, w), lambda h, j: (0, h))
    ktile = lambda w: pl.BlockSpec((t, w), lambda h, j: (j, h))
    (dq, dk, dv), carried = _call(
        body, name, [q, k, v, do, o, lse],
        [head(HEAD_PAD), ktile(HEAD_PAD), ktile(V_DIM), head(V_DIM), head(V_DIM),
         pl.BlockSpec((None, nt, 1, t), lambda h, j: (h, 0, 0, 0))],
        [jax.ShapeDtypeStruct((H, nt, HEAD_PAD, t), F32), jax.ShapeDtypeStruct((S, H * HEAD_PAD), F32),
         jax.ShapeDtypeStruct((S, H * V_DIM), BF16)],
        [pl.BlockSpec((None, nt, HEAD_PAD, t), lambda h, j: (h, 0, 0, 0)), ktile(HEAD_PAD), ktile(V_DIM)],
        (H, nt), [pltpu.VMEM((HEAD_PAD, t), BF16), pltpu.VMEM((nt, 8, t), F32), pltpu.VMEM((t, V_DIM), F32)],
        ("parallel", "arbitrary"), rider)
    return dq, dk, dv, carried


def _peers():
    x, y, c = lax.axis_index("x"), lax.axis_index("y"), lax.axis_index("c")
    me = 4 * x + 2 * y + c
    peers = []
    for fx, fy, fc in ((0, 0, 1), (1, 0, 0), (0, 1, 0), (1, 1, 0), (1, 0, 1), (0, 1, 1), (1, 1, 1)):
        px, py, pc = x ^ fx, y ^ fy, c ^ fc
        peers.append(((px, py, pc), 4 * px + 2 * py + pc))
    return me, peers


def _hbm_specs(n):
    return [pl.BlockSpec(memory_space=pl.ANY)] * n


class Rider:
    def __init__(self, arrs, gather):
        self.arrs, self.gather, self.n = list(arrs), gather, len(arrs)
        self.out_shapes = [jax.ShapeDtypeStruct((N_DEV,) + a.shape if gather else a.shape, a.dtype) for a in arrs]
        self.sems = [pltpu.SemaphoreType.DMA((self.n, N_DEV - 1)), pltpu.SemaphoreType.DMA((self.n, N_DEV - 1)),
                     pltpu.SemaphoreType.DMA((self.n,))]

    def _copies(self, srcs, dsts, sems):
        send_sems, recv_sems, local_sems = sems
        x, y, c = lax.axis_index("x"), lax.axis_index("y"), lax.axis_index("c")
        ident = lambda d: 4 * d[0] + 2 * d[1] + d[2]
        me, sibling = (x, y, c), (x, y, 1 - c)
        chips = [(1 - x, y), (x, 1 - y), (1 - x, 1 - y)]
        _, peers = _peers()

        def remote(a, k, incoming):
            if not self.gather:
                target, pid = peers[k]
                src, block = srcs[a].at[pid], (pid if incoming else ident(me))
            elif k == 0:
                target, src, block = sibling, srcs[a], ident(sibling if incoming else me)
            elif k <= 3:
                target = (*chips[k - 1], c)
                src, block = srcs[a], ident(target if incoming else me)
            else:
                landed = ident((*chips[k - 4], c))
                target, src = sibling, dsts[a].at[landed]
                block = ident((*chips[k - 4], 1 - c)) if incoming else landed
            return pltpu.make_async_remote_copy(
                src_ref=src, dst_ref=dsts[a].at[block], send_sem=send_sems.at[a, k], recv_sem=recv_sems.at[a, k],
                device_id=target, device_id_type=MESH)

        def local(a):
            return pltpu.make_async_copy(srcs[a] if self.gather else srcs[a].at[ident(me)], dsts[a].at[ident(me)],
                                         local_sems.at[a])

        return local, remote

    def start(self, srcs, dsts, sems):
        local, remote = self._copies(srcs, dsts, sems)
        for a in range(self.n):
            local(a).start()
            for k in range(4 if self.gather else N_DEV - 1):
                remote(a, k, False).start()

    def relay(self, srcs, dsts, sems):
        if not self.gather:
            return
        local, remote = self._copies(srcs, dsts, sems)
        for a in range(self.n):
            for k in range(1, 4):
                remote(a, k, True).wait_recv()
                remote(a, k + 3, False).start()

    def wait(self, srcs, dsts, sems):
        local, remote = self._copies(srcs, dsts, sems)
        for a in range(self.n):
            for k in range(N_DEV - 1):
                if not (self.gather and 1 <= k <= 3):
                    remote(a, k, True).wait_recv()
        for a in range(self.n):
            for k in range(N_DEV - 1):
                remote(a, k, False).wait_send()
            local(a).wait()


def _carry(body, n_in, n_out, rider, grid):
    n = rider.n
    steps = math.prod(grid)

    def wrapped(*refs):
        ins, r_in = refs[:n_in], refs[n_in:n_in + n]
        outs = refs[n_in + n:n_in + n + n_out]
        r_out = refs[n_in + n + n_out:n_in + 2 * n + n_out]
        scratch, sems = refs[n_in + 2 * n + n_out:-3], refs[-3:]
        step = 0
        for d, g in enumerate(grid):
            step = step * g + pl.program_id(d)

        @pl.when(step == 0)
        def _():
            rider.start(r_in, r_out, sems)

        if rider.gather:
            @pl.when(step == (3 * steps) // 4)
            def _():
                rider.relay(r_in, r_out, sems)

        body(*ins, *outs, *scratch)

        @pl.when(step == steps - 1)
        def _():
            rider.wait(r_in, r_out, sems)

    return wrapped


def _call(body, name, ins, in_specs, out_shape, out_specs, grid, scratch, sem, rider=None):
    if rider is None:
        return _pcall(body, name=name, out_shape=list(out_shape), grid=grid, in_specs=list(in_specs),
                      out_specs=list(out_specs), scratch_shapes=list(scratch), compiler_params=_cparams(*sem))(*ins), None
    res = _pcall(
        _carry(body, len(ins), len(out_shape), rider, grid), name=name,
        out_shape=list(out_shape) + rider.out_shapes, grid=grid,
        in_specs=list(in_specs) + _hbm_specs(rider.n), out_specs=list(out_specs) + _hbm_specs(rider.n),
        scratch_shapes=list(scratch) + rider.sems, compiler_params=_cparams(*(("arbitrary",) * len(grid))),
    )(*ins, *rider.arrs)
    return res[:len(out_shape)], res[len(out_shape):]


def exchange(name, arrs, gather):
    rider = Rider(arrs, gather)

    def body(*refs):
        srcs, dsts, sems = refs[:rider.n], refs[rider.n:2 * rider.n], refs[2 * rider.n:]
        rider.start(srcs, dsts, sems)
        rider.relay(srcs, dsts, sems)
        rider.wait(srcs, dsts, sems)

    return _pcall(body, name=name, out_shape=rider.out_shapes, in_specs=_hbm_specs(rider.n),
                  out_specs=_hbm_specs(rider.n), scratch_shapes=rider.sems)(*arrs)


def adamw(name, gslots, w, m, v, layer=0, prev=None):
    K, R, C = gslots.shape
    per_row = C * (K * gslots.dtype.itemsize + 7 * 4) * 2
    tr = R
    for cand in (1024, 512, 256, 128, 64, 32, 16, 8):
        if R % cand == 0:
            tr = cand
            if cand * per_row <= VMEM_LIMIT_BYTES // 2:
                break
    c1 = 1.0 / (1.0 - ADAM_B1 ** ADAM_STEP)
    c2 = 1.0 / (1.0 - ADAM_B2 ** ADAM_STEP)

    def body(g_ref, w_ref, m_ref, v_ref, *rest):
        go_ref, d_ref, mo_ref, vo_ref = rest[-4:]
        g = g_ref[0].astype(F32)
        for s in range(1, K):
            g = g + g_ref[s].astype(F32)
        mn = ADAM_B1 * m_ref[...] + (1.0 - ADAM_B1) * g
        vn = ADAM_B2 * v_ref[...] + (1.0 - ADAM_B2) * (g * g)
        go_ref[...] = g
        mo_ref[...] = mn
        vo_ref[...] = vn
        d_ref[...] = -ADAM_LR * ((mn * c1) / (jnp.sqrt(vn * c2) + ADAM_EPS) + ADAM_WD * w_ref[...])

    blk = pl.BlockSpec((None, tr, C), lambda i: (layer, i, 0))
    prev = [] if prev is None else list(prev)
    return _pcall(
        body, name=name, out_shape=[jax.ShapeDtypeStruct(w.shape, F32)] * 4, grid=(R // tr,),
        in_specs=[pl.BlockSpec((K, tr, C), lambda i: (0, i, 0)), blk, blk, blk] + _hbm_specs(len(prev)),
        out_specs=[blk] * 4, input_output_aliases={4 + q: q for q in range(len(prev))},
        compiler_params=_cparams("parallel"),
    )(gslots, w, m, v, *prev)


def _from_slots(gathered, ax):
    g = jnp.moveaxis(gathered, 0, ax)
    s = g.shape
    return g.reshape(s[:ax] + (s[ax] * s[ax + 1],) + s[ax + 2:])


def _to_slots(full, ax):
    s = full.shape
    g = full.reshape(s[:ax] + (N_DEV, s[ax] // N_DEV) + s[ax + 1:])
    g = jnp.moveaxis(g, ax, 0)
    return g.reshape(N_DEV, -1, g.shape[-1])


def _rope_tables(pos, S):
    inv_freq = ROPE_THETA ** (-jnp.arange(0, ROPE_DIM, 2, dtype=F32) / ROPE_DIM)
    ang = pos.astype(F32)[:, None] * inv_freq
    cos, sin = jnp.cos(ang), jnp.sin(ang)
    z = jnp.zeros((S, ROPE_DIM // 2), F32)
    cos_t = jnp.concatenate([cos, cos, z, z], axis=1)
    sin_a = jnp.concatenate([-sin, z, z, z], axis=1)
    sin_b = jnp.concatenate([z, sin, z, z], axis=1)
    return cos_t, sin_a, sin_b


def kernel(x, p, positions, pre_norm, post_norm, pool_w_in, pool_w_group, pool_scale, pool_w_out, mla_w_in, mla_q_norm, mla_w_uq, mla_kv_norm, mla_w_ukv, mla_w_out, ple_norm, ple_w_gate, ple_w_proj, loss_target, m_pre_norm, m_post_norm, m_pool_w_in, m_pool_w_group, m_pool_scale, m_pool_w_out, m_mla_w_in, m_mla_q_norm, m_mla_w_uq, m_mla_kv_norm, m_mla_w_ukv, m_mla_w_out, m_ple_norm, m_ple_w_gate, m_ple_w_proj, v_pre_norm, v_post_norm, v_pool_w_in, v_pool_w_group, v_pool_scale, v_pool_w_out, v_mla_w_in, v_mla_q_norm, v_mla_w_uq, v_mla_kv_norm, v_mla_w_ukv, v_mla_w_out, v_ple_norm, v_ple_w_gate, v_ple_w_proj):
    wl = dict(pre_norm=pre_norm, post_norm=post_norm, pool_w_in=pool_w_in, pool_w_group=pool_w_group,
              pool_scale=pool_scale, pool_w_out=pool_w_out, mla_w_in=mla_w_in, mla_q_norm=mla_q_norm,
              mla_w_uq=mla_w_uq, mla_kv_norm=mla_kv_norm, mla_w_ukv=mla_w_ukv, mla_w_out=mla_w_out,
              ple_norm=ple_norm, ple_w_gate=ple_w_gate, ple_w_proj=ple_w_proj)
    ml = dict(pre_norm=m_pre_norm, post_norm=m_post_norm, pool_w_in=m_pool_w_in, pool_w_group=m_pool_w_group,
              pool_scale=m_pool_scale, pool_w_out=m_pool_w_out, mla_w_in=m_mla_w_in, mla_q_norm=m_mla_q_norm,
              mla_w_uq=m_mla_w_uq, mla_kv_norm=m_mla_kv_norm, mla_w_ukv=m_mla_w_ukv, mla_w_out=m_mla_w_out,
              ple_norm=m_ple_norm, ple_w_gate=m_ple_w_gate, ple_w_proj=m_ple_w_proj)
    vl = dict(pre_norm=v_pre_norm, post_norm=v_post_norm, pool_w_in=v_pool_w_in, pool_w_group=v_pool_w_group,
              pool_scale=v_pool_scale, pool_w_out=v_pool_w_out, mla_w_in=v_mla_w_in, mla_q_norm=v_mla_q_norm,
              mla_w_uq=v_mla_w_uq, mla_kv_norm=v_mla_kv_norm, mla_w_ukv=v_mla_w_ukv, mla_w_out=v_mla_w_out,
              ple_norm=v_ple_norm, ple_w_gate=v_ple_w_gate, ple_w_proj=v_ple_w_proj)

    S, D = x.shape[1], x.shape[2]
    L = pre_norm.shape[0]
    E = pool_scale.shape[1]
    NG = pool_w_group.shape[1]
    R = mla_w_uq.shape[1]
    H = D // 128
    EM = H * V_DIM
    PD = p.shape[-1]
    me = 4 * lax.axis_index("x") + 2 * lax.axis_index("y") + lax.axis_index("c")
    ts = min(S, 256)
    tsw = min(S, 128)
    ta = min(S, 512)
    sm_scale = (NOPE_DIM + ROPE_DIM) ** -0.5

    wb = {n: wl[n].astype(BF16) for n in BIG}
    full = {}

    def ag_rider(host):
        return Rider([wb[n][l] for n, l in AG_PLAN[host]], True) if host in AG_PLAN else None

    def ag_done(host, results):
        for (n, l), g in zip(AG_PLAN[host], results):
            full[n, l] = _from_slots(g, SHARD_AXIS[n] - 1)

    small_sh = jnp.concatenate([wl[n].reshape(1, -1) for n in SMALL_SHARD], axis=1)
    g_in0, g_small = exchange("gather_first", [wb['pool_w_in'][0], small_sh], True)
    full['pool_w_in', 0] = _from_slots(g_in0, SHARD_AXIS['pool_w_in'] - 1)
    nq = mla_q_norm.size
    q_norm = _from_slots(g_small[:, 0, :nq].reshape((N_DEV,) + mla_q_norm.shape), 1)
    kv_norm = _from_slots(g_small[:, 0, nq:].reshape((N_DEV,) + mla_kv_norm.shape), 1)

    def mla_kernel_weights(j):
        w_in = full['mla_w_in', j]
        w_in_k = jnp.concatenate([w_in[:, 2 * R + ROPE_DIM:], w_in[:, :2 * R + ROPE_DIM],
                                  jnp.zeros((D, LANE - ROPE_DIM), BF16)], axis=1)
        w_uq_k = jnp.pad(full['mla_w_uq', j].reshape(R, H, NOPE_DIM + ROPE_DIM),
                         ((0, 0), (0, 0), (0, HEAD_PAD - NOPE_DIM - ROPE_DIM))).reshape(R, H * HEAD_PAD)
        w_ukv = full['mla_w_ukv', j].reshape(R, H, NOPE_DIM + V_DIM)
        w_uk_k = jnp.pad(w_ukv[..., :NOPE_DIM], ((0, 0), (0, 0), (0, HEAD_PAD - NOPE_DIM))).reshape(R, H * HEAD_PAD)
        w_uv_k = w_ukv[..., NOPE_DIM:].reshape(R, H * V_DIM)
        return w_in_k, w_uq_k, w_uk_k, w_uv_k

    def fmm(name, a, b, mode, out_dtype=F32):
        if name not in AG_PLAN:
            return mm(name, a, b, mode, out_dtype)
        out, carried = mm(name, a, b, mode, out_dtype, ag_rider(name))
        ag_done(name, carried)
        return out

    mla_w = {}
    tabs = _rope_tables(positions[0], S)

    h = x[0]
    saved = []
    for i in range(L):
        j = i // 2
        sv = dict(h=h)
        xn = rms_fwd(f"pre_norm_{i}", h, pre_norm[i:i + 1], S, D, ts)
        sv['xn'] = xn
        if i % 2 == 0:
            z = fmm(f"pool_in_{i}", xn, full['pool_w_in', j], 'nn')
            pooled = pool_fwd(f"pool_window_{i}", z, S, E, NG, tsw)
            mixed = fmm(f"pool_group_{i}", pooled, full['pool_w_group', j], 'nn')
            y = pool_gate_fwd(f"pool_gate_{i}", mixed, z, pool_scale[j:j + 1], S, E, NG, tsw)
            out = fmm(f"pool_out_{i}", y, full['pool_w_out', j], 'nn')
            sv.update(z=z, pooled=pooled, mixed=mixed, y=y)
        else:
            w_in_k, w_uq_k, w_uk_k, w_uv_k = mla_w[j] = mla_kernel_weights(j)
            z = fmm(f"mla_in_{i}", xn, w_in_k, 'nn')
            qn, kvn, kper = mla_prep_fwd(f"mla_prep_{i}", z, q_norm[j:j + 1], kv_norm[j:j + 1], tabs, S, EM, R, ts)
            q_raw = fmm(f"mla_uq_{i}", qn, w_uq_k, 'nn')
            k_raw = fmm(f"mla_uk_{i}", kvn, w_uk_k, 'nn')
            vv = fmm(f"mla_uv_{i}", kvn, w_uv_k, 'nn', BF16)
            qp, kp = mla_pack_fwd(f"mla_pack_{i}", q_raw, k_raw, kper, tabs, S, H, tsw, sm_scale)
            o, lse, carried = flash_fwd(f"attn_{i}", qp, kp, vv, S, H, ta, ag_rider(f"attn_{i}"))
            if carried is not None:
                ag_done(f"attn_{i}", carried)
            y = mla_gate_fwd(f"mla_gate_{i}", o, z, S, EM, ts)
            out = fmm(f"mla_out_{i}", y, full['mla_w_out', j], 'nn')
            sv.update(z=z, qn=qn, kvn=kvn, qp=qp, kp=kp, vv=vv, o=o, lse=lse, y=y)
        h1, a = post_fwd(f"post_norm_{i}", h, out, post_norm[i:i + 1], ple_norm[i:i + 1], S, D, ts)
        gl = fmm(f"ple_gate_{i}", a, full['ple_w_gate', i], 'nn')
        pp = fmm(f"ple_proj_{i}", p[i, 0], full['ple_w_proj', i], 'nn')
        h = ple_fwd(f"ple_{i}", h1, pp, gl, S, D, ts)
        sv.update(out=out, h1=h1, a=a, gl=gl, pp=pp)
        saved.append(sv)

    dh, loss_acc = loss_fwd_bwd("loss", h, loss_target[0], S, D, ts)
    loss = lax.psum(loss_acc[0, 0] * (0.5 / D), ("x", "y", "c"))

    gw = {n: [None] * wl[n].shape[0] for n in WEIGHTS}
    recv = {}

    def rs_rider(host):
        if host not in RS_PLAN:
            return None
        return Rider([_to_slots(gw[n][l], SHARD_AXIS[n] - 1).astype(BF16) for n, l in RS_PLAN[host]], False)

    def rs_done(host, results):
        for key, r in zip(RS_PLAN[host], results):
            recv[key] = r

    def bmm(name, a, b, mode):
        out_dtype = BF16 if mode == 'tn' else F32
        if name not in RS_PLAN:
            return mm(name, a, b, mode, out_dtype)
        out, carried = mm(name, a, b, mode, out_dtype, rs_rider(name))
        rs_done(name, carried)
        return out

    for i in reversed(range(L)):
        j = i // 2
        sv = saved[i]
        dpp, dgl = ple_bwd(f"ple_bwd_{i}", dh, sv['pp'], sv['gl'], S, D, ts)
        gw['ple_w_proj'][i] = bmm(f"ple_proj_dw_{i}", p[i, 0], dpp, 'tn')
        gw['ple_w_gate'][i] = bmm(f"ple_gate_dw_{i}", sv['a'], dgl, 'tn')
        da = bmm(f"ple_gate_dx_{i}", dgl, full['ple_w_gate', i], 'nt')
        dh1, dout, dpost, dple = post_bwd(f"post_norm_bwd_{i}", da, dh, sv['h1'], sv['out'],
                                          post_norm[i:i + 1], ple_norm[i:i + 1], S, D, ts)
        gw['post_norm'][i], gw['ple_norm'][i] = dpost[0], dple[0]
        xn = sv['xn']
        if i % 2 == 0:
            gw['pool_w_out'][j] = bmm(f"pool_out_dw_{i}", sv['y'], dout, 'tn')
            dy = bmm(f"pool_out_dx_{i}", dout, full['pool_w_out', j], 'nt')
            dmixed, dg, dscale = pool_gate_bwd(f"pool_gate_bwd_{i}", dy, sv['mixed'], sv['z'], pool_scale[j:j + 1],
                                               S, E, NG, tsw)
            gw['pool_scale'][j] = dscale[0]
            gw['pool_w_group'][j] = bmm(f"pool_group_dw_{i}", sv['pooled'], dmixed, 'tn')
            dpooled = bmm(f"pool_group_dx_{i}", dmixed, full['pool_w_group', j], 'nt')
            dz = pool_bwd(f"pool_window_bwd_{i}", dpooled, dg, S, E, NG, tsw)
            gw['pool_w_in'][j] = bmm(f"pool_in_dw_{i}", xn, dz, 'tn')
            dxn = bmm(f"pool_in_dx_{i}", dz, full['pool_w_in', j], 'nt')
        else:
            w_in_k, w_uq_k, w_uk_k, w_uv_k = mla_w[j]
            gw['mla_w_out'][j] = bmm(f"mla_out_dw_{i}", sv['y'], dout, 'tn')
            dy = bmm(f"mla_out_dx_{i}", dout, full['mla_w_out', j], 'nt')
            do, dg = mla_gate_bwd(f"mla_gate_bwd_{i}", dy, sv['o'], sv['z'], S, EM, H, ts)
            dqt, dkp, dvv, carried = flash_bwd(f"attn_bwd_{i}", sv['qp'], sv['kp'], sv['vv'], do, sv['o'], sv['lse'],
                                               S, H, ta, rs_rider(f"attn_bwd_{i}"))
            if carried is not None:
                rs_done(f"attn_bwd_{i}", carried)
            dq_raw = mla_unpack_q_bwd(f"mla_pack_q_bwd_{i}", dqt, tabs, S, H, ta, sm_scale)
            dk_raw, dkpe = mla_unpack_k_bwd(f"mla_pack_k_bwd_{i}", dkp, S, H, tsw)
            g_uq = bmm(f"mla_uq_dw_{i}", sv['qn'], dq_raw, 'tn')
            g_uk = bmm(f"mla_uk_dw_{i}", sv['kvn'], dk_raw, 'tn')
            g_uv = bmm(f"mla_uv_dw_{i}", sv['kvn'], dvv, 'tn')
            dqn = bmm(f"mla_uq_dx_{i}", dq_raw, w_uq_k, 'nt')
            dkvn_k = bmm(f"mla_uk_dx_{i}", dk_raw, w_uk_k, 'nt')
            dkvn_v = bmm(f"mla_uv_dx_{i}", dvv, w_uv_k, 'nt')
            dz, dqg, dkvg = mla_prep_bwd(f"mla_prep_bwd_{i}", dqn, dkvn_k, dkvn_v, sv['z'], dkpe, dg,
                                         q_norm[j:j + 1], kv_norm[j:j + 1], tabs, S, EM, R, ts)
            g_in = bmm(f"mla_in_dw_{i}", xn, dz, 'tn')
            dxn = bmm(f"mla_in_dx_{i}", dz, w_in_k, 'nt')
            gw['mla_q_norm'][j], gw['mla_kv_norm'][j] = dqg[0], dkvg[0]
            gw['mla_w_in'][j] = jnp.concatenate([g_in[:, EM:EM + 2 * R + ROPE_DIM], g_in[:, :EM]], axis=1)
            gw['mla_w_uq'][j] = g_uq.reshape(R, H, HEAD_PAD)[:, :, :NOPE_DIM + ROPE_DIM].reshape(R, -1)
            gw['mla_w_ukv'][j] = jnp.concatenate(
                [g_uk.reshape(R, H, HEAD_PAD)[:, :, :NOPE_DIM], g_uv.reshape(R, H, V_DIM)], axis=2).reshape(R, -1)
        dh, dpre = pre_bwd(f"pre_norm_bwd_{i}", dxn, dh1, sv['h'], pre_norm[i:i + 1], S, D, ts)
        gw['pre_norm'][i] = dpre[0]
    grad_x = dh[None]

    last = exchange("scatter_last", [_to_slots(gw[n][l], SHARD_AXIS[n] - 1).astype(BF16) for n, l in RS_LAST], False)
    for key, r in zip(RS_LAST, last):
        recv[key] = r
    small_names = SMALL_REPL + SMALL_SHARD
    gw = {n: jnp.stack(gw[n]) for n in small_names}
    small_g = jnp.concatenate([gw[n].reshape(1, -1) for n in small_names], axis=1)
    small_all = exchange("gather_small_grads", [small_g], True)[0]

    outs = {}
    for n in BIG:
        shp = wl[n].shape
        three = lambda a: a.reshape(shp[0], -1, shp[-1])
        res = None
        for l in range(shp[0]):
            res = adamw(f"adamw_{n}_{l}", recv[n, l], three(wl[n]), three(ml[n]), three(vl[n]), l, res)
        outs[n] = [a.reshape(shp) for a in res]

    pieces, off = [], 0
    for n in small_names:
        sz = gw[n].size
        g = small_all[:, :, off:off + sz]
        off += sz
        if n in SMALL_SHARD:
            rows_, cols_ = gw[n].shape
            g = lax.dynamic_slice_in_dim(g.reshape(N_DEV, rows_, cols_), me * (cols_ // N_DEV), cols_ // N_DEV, axis=2)
            g = g.reshape(N_DEV, 1, -1)
        pieces.append(g)
    gs = jnp.concatenate(pieces, axis=2)
    flat = lambda d: jnp.concatenate([d[n].reshape(1, -1) for n in small_names], axis=1)
    res = adamw("adamw_small", gs, flat(wl)[None], flat(ml)[None], flat(vl)[None])
    off = 0
    for n in small_names:
        sz = wl[n].size
        outs[n] = [a[0, :, off:off + sz].reshape(wl[n].shape) for a in res]
        off += sz

    return (loss, grad_x, *[outs[n][0] for n in WEIGHTS], *[outs[n][1] for n in WEIGHTS],
            *[outs[n][2] for n in WEIGHTS], *[outs[n][3] for n in WEIGHTS])
```

```python
import functools
import math

import jax
import jax.numpy as jnp
from jax import lax
from jax.experimental import pallas as pl
from jax.experimental.pallas import tpu as pltpu

F32 = jnp.float32
BF16 = jnp.bfloat16

N_DEV = 8
EPS = 1e-6
ROPE_THETA = 10000.0
NOPE_DIM = 128
ROPE_DIM = 64
V_DIM = 128
HEAD_PAD = 256
LANE = 128
POOL_WINDOWS = (2, 4, 8, 16)
POOL_HALO = 16
NEG_INF = -1e30
ADAM_LR = 0.001
ADAM_B1 = 0.9
ADAM_B2 = 0.999
ADAM_EPS = 1e-08
ADAM_WD = 0.01
ADAM_STEP = 10
VMEM_LIMIT_BYTES = 56 * 1024 * 1024
MM_MAX_TK = 3200
MESH = pl.DeviceIdType.MESH

SHARD_AXIS = dict(pre_norm=None, post_norm=None, pool_w_in=2, pool_w_group=2, pool_scale=None, pool_w_out=1,
                  mla_w_in=2, mla_q_norm=1, mla_w_uq=2, mla_kv_norm=1, mla_w_ukv=2, mla_w_out=1,
                  ple_norm=None, ple_w_gate=1, ple_w_proj=2)
WEIGHTS = tuple(SHARD_AXIS)
BIG = ('pool_w_in', 'pool_w_group', 'pool_w_out', 'mla_w_in', 'mla_w_uq', 'mla_w_ukv', 'mla_w_out',
       'ple_w_gate', 'ple_w_proj')
SMALL_REPL = ('pre_norm', 'post_norm', 'pool_scale', 'ple_norm')
SMALL_SHARD = ('mla_q_norm', 'mla_kv_norm')

AG_PLAN = {
    "pool_in_0": [("pool_w_group", 0), ("pool_w_out", 0), ("ple_w_gate", 0), ("ple_w_proj", 0)],
    "pool_out_0": [("mla_w_in", 0), ("mla_w_uq", 0), ("mla_w_ukv", 0)],
    "attn_1": [("mla_w_out", 0), ("ple_w_gate", 1), ("ple_w_proj", 1), ("pool_w_in", 1), ("pool_w_group", 1),
               ("pool_w_out", 1), ("ple_w_gate", 2), ("ple_w_proj", 2)],
    "pool_in_2": [("mla_w_in", 1), ("mla_w_uq", 1), ("mla_w_ukv", 1)],
    "attn_3": [("mla_w_out", 1), ("ple_w_gate", 3), ("ple_w_proj", 3)],
}
RS_PLAN = {
    "attn_bwd_3": [("ple_w_gate", 3), ("ple_w_proj", 3), ("mla_w_out", 1)],
    "mla_in_dw_3": [("mla_w_uq", 1), ("mla_w_ukv", 1)],
    "pool_out_dx_2": [("mla_w_in", 1)],
    "pool_in_dw_2": [("ple_w_gate", 2), ("ple_w_proj", 2)],
    "pool_in_dx_2": [("pool_w_out", 1)],
    "attn_bwd_1": [("pool_w_group", 1), ("pool_w_in", 1), ("ple_w_gate", 1), ("ple_w_proj", 1), ("mla_w_out", 0)],
    "mla_in_dw_1": [("mla_w_uq", 0), ("mla_w_ukv", 0)],
    "pool_out_dw_0": [("mla_w_in", 0)],
    "pool_out_dx_0": [("ple_w_gate", 0), ("ple_w_proj", 0)],
    "pool_in_dw_0": [("pool_w_out", 0)],
    "pool_in_dx_0": [("pool_w_group", 0)],
}
RS_LAST = [("pool_w_in", 0)]


def _pcall(body, **kw):
    return pl.pallas_call(body, **kw)


def _cparams(*sem):
    return pltpu.CompilerParams(dimension_semantics=sem, vmem_limit_bytes=VMEM_LIMIT_BYTES)


def _pick(n, cands):
    for c in cands:
        if n % c == 0:
            return c
    return n


def _sigmoid(x):
    return 1.0 / (1.0 + jnp.exp(-x))


def mm(name, a, b, mode, out_dtype=F32, rider=None):
    squeeze = a.ndim == 2
    if squeeze:
        a, b = a[None], b[None]
    G = a.shape[0]
    if mode == 'nn':
        M, K = a.shape[1:]
        N = b.shape[2]
    elif mode == 'tn':
        K, M = a.shape[1:]
        N = b.shape[2]
    else:
        M, K = a.shape[1:]
        N = b.shape[1]
    tm = _pick(M, (1024, 512, 256, 128))
    tn = _pick(N, (1024, 768, 640, 512, 384, 256, 128))
    tk = K if K <= MM_MAX_TK else _pick(K, (2048, 1024, 640, 512, 384, 256, 128))
    nk = K // tk
    if mode == 'nn':
        a_spec = pl.BlockSpec((None, tm, tk), lambda g, i, j, k: (g, i, k))
        b_spec = pl.BlockSpec((None, tk, tn), lambda g, i, j, k: (g, k, j))
        dims = (((1,), (0,)), ((), ()))
    elif mode == 'tn':
        a_spec = pl.BlockSpec((None, tk, tm), lambda g, i, j, k: (g, k, i))
        b_spec = pl.BlockSpec((None, tk, tn), lambda g, i, j, k: (g, k, j))
        dims = (((0,), (0,)), ((), ()))
    else:
        a_spec = pl.BlockSpec((None, tm, tk), lambda g, i, j, k: (g, i, k))
        b_spec = pl.BlockSpec((None, tn, tk), lambda g, i, j, k: (g, j, k))
        dims = (((1,), (1,)), ((), ()))

    def product(a_ref, b_ref):
        return lax.dot_general(a_ref[...].astype(BF16), b_ref[...].astype(BF16), dims, preferred_element_type=F32)

    def body_one(a_ref, b_ref, o_ref):
        o_ref[...] = product(a_ref, b_ref).astype(out_dtype)

    def body_acc(a_ref, b_ref, o_ref, acc_ref):
        k = pl.program_id(3)

        @pl.when(k == 0)
        def _():
            acc_ref[...] = product(a_ref, b_ref)

        @pl.when(jnp.logical_and(k > 0, k < nk - 1))
        def _():
            acc_ref[...] += product(a_ref, b_ref)

        @pl.when(k == nk - 1)
        def _():
            o_ref[...] = (acc_ref[...] + product(a_ref, b_ref)).astype(out_dtype)

    (out,), carried = _call(
        body_one if nk == 1 else body_acc, name, [a, b], [a_spec, b_spec],
        [jax.ShapeDtypeStruct((G, M, N), out_dtype)],
        [pl.BlockSpec((None, tm, tn), lambda g, i, j, k: (g, i, j))], (G, M // tm, N // tn, nk),
        [] if nk == 1 else [pltpu.VMEM((tm, tn), F32)], ("parallel", "parallel", "parallel", "arbitrary"), rider)
    out = out[0] if squeeze else out
    return out if rider is None else (out, carried)


def rows(ts, width, colblk=0):
    return pl.BlockSpec((ts, width), lambda i: (i, colblk))


def whole(shape):
    return pl.BlockSpec(shape, lambda i: (0,) * len(shape))


def rowwise(name, fn, S, ts, ins, outs, accs=(), scratch=(), reverse=False):
    n_in, n_out, n_acc = len(ins), len(outs), len(accs)
    nt = S // ts

    def body(*refs):
        step = pl.program_id(0)
        i = nt - 1 - step if reverse else step
        in_refs = refs[:n_in]
        out_refs = refs[n_in:n_in + n_out]
        acc_refs = refs[n_in + n_out:n_in + n_out + n_acc]
        scr = refs[n_in + n_out + n_acc:]

        @pl.when(step == 0)
        def _():
            for r in acc_refs:
                r[...] = jnp.zeros_like(r)

        fn(i, step, in_refs, out_refs, acc_refs, scr)

    def fix(spec):
        if not reverse:
            return spec
        imap = spec.index_map
        return pl.BlockSpec(spec.block_shape, lambda s: imap(nt - 1 - s))

    res = _pcall(
        body, name=name,
        out_shape=[jax.ShapeDtypeStruct(s, d) for s, d, _ in outs] + [jax.ShapeDtypeStruct(s, d) for s, d in accs],
        grid=(nt,),
        in_specs=[fix(sp) for _, sp in ins],
        out_specs=[fix(sp) for _, _, sp in outs] + [whole(s) for s, _ in accs],
        scratch_shapes=list(scratch),
        compiler_params=_cparams("arbitrary"),
    )(*[a for a, _ in ins])
    return res


def _rstd(x):
    return lax.rsqrt(jnp.mean(x * x, axis=-1, keepdims=True) + EPS)


def _rms_bwd(dy, x, g):
    r = _rstd(x)
    xh = x * r
    gdy = dy * g
    dx = r * (gdy - xh * jnp.mean(xh * gdy, axis=-1, keepdims=True))
    return dx, jnp.sum(dy * xh, axis=0, keepdims=True)


def _rope(v, cos_t, sin_a, sin_b, sign):
    return v * cos_t + sign * (pltpu.roll(v, LANE - ROPE_DIM // 2, axis=1) * sin_a
                               + pltpu.roll(v, ROPE_DIM // 2, axis=1) * sin_b)


def rms_fwd(name, h, gain, S, D, ts):
    def fn(i, step, ins, outs, accs, scr):
        x = ins[0][...]
        outs[0][...] = (x * _rstd(x) * ins[1][...]).astype(BF16)
    return rowwise(name, fn, S, ts, [(h, rows(ts, D)), (gain, whole((1, D)))], [((S, D), BF16, rows(ts, D))])[0]


def post_fwd(name, h, out, post_g, ple_g, S, D, ts):
    def fn(i, step, ins, outs, accs, scr):
        o = ins[1][...]
        h1 = ins[0][...] + o * _rstd(o) * ins[2][...]
        outs[0][...] = h1
        outs[1][...] = (h1 * _rstd(h1) * ins[3][...]).astype(BF16)
    return rowwise(name, fn, S, ts,
                   [(h, rows(ts, D)), (out, rows(ts, D)), (post_g, whole((1, D))), (ple_g, whole((1, D)))],
                   [((S, D), F32, rows(ts, D)), ((S, D), BF16, rows(ts, D))])


def ple_fwd(name, a, w_gate, p, w_proj, h1):
    S, D = h1.shape
    PD = p.shape[1]
    tm, tn = _pick(S, (256, 128)), _pick(D, (2048, 1024, 512, 256, 128))

    def body(a_ref, wg_ref, p_ref, wp_ref, h1_ref, h_ref, gl_ref, pp_ref):
        gl = lax.dot_general(a_ref[...], wg_ref[...], _NN, preferred_element_type=F32)
        pp = lax.dot_general(p_ref[...].astype(BF16), wp_ref[...], _NN, preferred_element_type=F32)
        h_ref[...] = h1_ref[...] + pp * _sigmoid(gl)
        gl_ref[...] = gl.astype(BF16)
        pp_ref[...] = pp.astype(BF16)

    tile = pl.BlockSpec((tm, tn), lambda i, j: (i, j))
    res, _ = _call(
        body, name, [a, w_gate, p, w_proj, h1],
        [pl.BlockSpec((tm, D), lambda i, j: (i, 0)), pl.BlockSpec((D, tn), lambda i, j: (0, j)),
         pl.BlockSpec((tm, PD), lambda i, j: (i, 0)), pl.BlockSpec((PD, tn), lambda i, j: (0, j)), tile],
        [jax.ShapeDtypeStruct((S, D), F32), jax.ShapeDtypeStruct((S, D), BF16), jax.ShapeDtypeStruct((S, D), BF16)],
        [tile, tile, tile], (S // tm, D // tn), [], ("parallel", "parallel"))
    return res


def loss_fwd_bwd(name, h, tgt, S, D, ts):
    def fn(i, step, ins, outs, accs, scr):
        e = ins[0][...] - ins[1][...]
        outs[0][...] = e * (1.0 / D)
        accs[0][...] += jnp.broadcast_to(jnp.sum(e * e), (1, LANE))
    return rowwise(name, fn, S, ts, [(h, rows(ts, D)), (tgt, rows(ts, D))], [((S, D), F32, rows(ts, D))],
                   accs=[((1, LANE), F32)])


def ple_bwd(name, dh, pp, gl, S, D, ts):
    def fn(i, step, ins, outs, accs, scr):
        d = ins[0][...]
        gate = _sigmoid(ins[2][...].astype(F32))
        outs[0][...] = (d * gate).astype(BF16)
        outs[1][...] = (d * ins[1][...].astype(F32) * gate * (1.0 - gate)).astype(BF16)
    return rowwise(name, fn, S, ts, [(dh, rows(ts, D)), (pp, rows(ts, D)), (gl, rows(ts, D))],
                   [((S, D), BF16, rows(ts, D)), ((S, D), BF16, rows(ts, D))])


def post_bwd(name, da, dh, h1, out, post_g, ple_g, S, D, ts):
    def fn(i, step, ins, outs, accs, scr):
        dx, dple = _rms_bwd(ins[0][...], ins[2][...], ins[5][...])
        dh1 = ins[1][...] + dx
        dout, dpost = _rms_bwd(dh1, ins[3][...], ins[4][...])
        outs[0][...] = dh1
        outs[1][...] = dout.astype(BF16)
        accs[0][...] += dpost
        accs[1][...] += dple
    return rowwise(name, fn, S, ts,
                   [(da, rows(ts, D)), (dh, rows(ts, D)), (h1, rows(ts, D)), (out, rows(ts, D)),
                    (post_g, whole((1, D))), (ple_g, whole((1, D)))],
                   [((S, D), F32, rows(ts, D)), ((S, D), BF16, rows(ts, D))],
                   accs=[((1, D), F32), ((1, D), F32)])


def pre_bwd(name, dxn, dh1, h, pre_g, S, D, ts):
    def fn(i, step, ins, outs, accs, scr):
        dx, dpre = _rms_bwd(ins[0][...], ins[2][...], ins[3][...])
        outs[0][...] = ins[1][...] + dx
        accs[0][...] += dpre
    return rowwise(name, fn, S, ts,
                   [(dxn, rows(ts, D)), (dh1, rows(ts, D)), (h, rows(ts, D)), (pre_g, whole((1, D)))],
                   [((S, D), F32, rows(ts, D))], accs=[((1, D), F32)])


def _window_sums(ext, w, back):
    n = ext.shape[0]
    s, win = ext, 1
    while win < w:
        s = s + pltpu.roll(s, win if back else n - win, axis=0)
        win *= 2
    return s


def pool_fwd(name, z, S, E, NG, ts):
    G = E // NG

    def fn(i, step, ins, outs, accs, scr):
        carry = scr[0]

        @pl.when(step == 0)
        def _():
            carry[...] = jnp.zeros_like(carry)

        t = i * ts + lax.broadcasted_iota(jnp.int32, (ts, 1), 0)
        for j, w in enumerate(POOL_WINDOWS):
            u = ins[0][:, j * G:(j + 1) * G]
            ext = jnp.concatenate([carry[:, j * G:(j + 1) * G], u], axis=0)
            sw = _window_sums(ext, w, True)[POOL_HALO:, :]
            cnt = jnp.minimum(t + 1, w).astype(F32)
            outs[0][j] = (sw / cnt - u).astype(BF16)
        carry[...] = ins[0][ts - POOL_HALO:, :]

    return rowwise(name, fn, S, ts, [(z, rows(ts, E, 0))],
                   [((NG, S, G), BF16, pl.BlockSpec((NG, ts, G), lambda i: (0, i, 0)))],
                   scratch=[pltpu.VMEM((POOL_HALO, E), F32)])[0]


def pool_bwd(name, dpooled, dg, S, E, NG, ts):
    G = E // NG

    def fn(i, step, ins, outs, accs, scr):
        carry = scr[0]

        @pl.when(step == 0)
        def _():
            carry[...] = jnp.zeros_like(carry)

        t = i * ts + lax.broadcasted_iota(jnp.int32, (ts, 1), 0)
        for j, w in enumerate(POOL_WINDOWS):
            d = ins[0][j]
            e = d / jnp.minimum(t + 1, w).astype(F32)
            ext = jnp.concatenate([e, carry[:, j * G:(j + 1) * G]], axis=0)
            sw = _window_sums(ext, w, False)[:ts, :]
            outs[0][:, j * G:(j + 1) * G] = (sw - d).astype(BF16)
            carry[:, j * G:(j + 1) * G] = e[:POOL_HALO, :]
        outs[0][:, E:] = ins[1][...]

    return rowwise(name, fn, S, ts,
                   [(dpooled, pl.BlockSpec((NG, ts, G), lambda i: (0, i, 0))), (dg, rows(ts, E))],
                   [((S, 2 * E), BF16, rows(ts, 2 * E))],
                   scratch=[pltpu.VMEM((POOL_HALO, E), F32)], reverse=True)[0]


def pool_group_fwd(name, pooled, w_group, z, scale, S, E, NG):
    G = E // NG
    tm = _pick(S, (1024, 512, 256, 128))

    def body(a_ref, b_ref, g_ref, sc_ref, y_ref, mx_ref):
        mx = lax.dot_general(a_ref[...], b_ref[...], _NN, preferred_element_type=F32)
        g = g_ref[...]
        y_ref[...] = (mx * sc_ref[...] * (g * _sigmoid(g))).astype(BF16)
        mx_ref[...] = mx.astype(BF16)

    grp = pl.BlockSpec((None, tm, G), lambda j, i: (j, i, 0))
    res, _ = _call(
        body, name, [pooled, w_group, z, scale],
        [grp, pl.BlockSpec((None, G, G), lambda j, i: (j, 0, 0)), pl.BlockSpec((tm, G), lambda j, i: (i, NG + j)),
         pl.BlockSpec((1, G), lambda j, i: (0, j))],
        [jax.ShapeDtypeStruct((S, E), BF16), jax.ShapeDtypeStruct((NG, S, G), BF16)],
        [pl.BlockSpec((tm, G), lambda j, i: (i, j)), grp], (NG, S // tm), [], ("parallel", "parallel"))
    return res


def pool_gate_bwd(name, dy, mixed, z, scale, S, E, NG, ts):
    G = E // NG

    def fn(i, step, ins, outs, accs, scr):
        for j in range(NG):
            sl = slice(j * G, (j + 1) * G)
            d = ins[0][:, sl]
            mx = ins[1][j].astype(F32)
            g = ins[2][:, sl]
            sc = ins[3][:, sl]
            sg = _sigmoid(g)
            si = g * sg
            outs[0][j] = (d * sc * si).astype(BF16)
            outs[1][:, sl] = (d * mx * sc * (sg * (1.0 + g * (1.0 - sg)))).astype(BF16)
            accs[0][:, sl] += jnp.sum(d * mx * si, axis=0, keepdims=True)

    return rowwise(name, fn, S, ts,
                   [(dy, rows(ts, E)), (mixed, pl.BlockSpec((NG, ts, G), lambda i: (0, i, 0))),
                    (z, rows(ts, E, 1)), (scale, whole((1, E)))],
                   [((NG, S, G), BF16, pl.BlockSpec((NG, ts, G), lambda i: (0, i, 0))), ((S, E), BF16, rows(ts, E))],
                   accs=[((1, E), F32)])


def mla_prep_fwd(name, z, qg, kvg, tabs, S, E, R, ts):
    qb, kb, pb = E // R, E // R + 1, (E + 2 * R) // LANE

    def fn(i, step, ins, outs, accs, scr):
        zq, zkv = ins[0][...], ins[1][...]
        outs[0][...] = (zq * _rstd(zq) * ins[3][...]).astype(BF16)
        outs[1][...] = (zkv * _rstd(zkv) * ins[4][...]).astype(BF16)
        outs[2][...] = _rope(ins[2][...], ins[5][...], ins[6][...], ins[7][...], 1.0)

    return rowwise(name, fn, S, ts,
                   [(z, rows(ts, R, qb)), (z, rows(ts, R, kb)), (z, rows(ts, LANE, pb)),
                    (qg, whole((1, R))), (kvg, whole((1, R)))] + [(t, rows(ts, LANE)) for t in tabs],
                   [((S, R), BF16, rows(ts, R)), ((S, R), BF16, rows(ts, R)), ((S, LANE), F32, rows(ts, LANE))])


def mla_up_fwd(name, qn, kvn, w_uq, w_uk, w_uv, kper, tabs, S, H, R, scale):
    hc = _pick(H, (4, 2, 1))
    tm = _pick(S, (1024, 512, 256, 128))

    def body(qn_ref, kvn_ref, wq_ref, wk_ref, wv_ref, kper_ref, cos_ref, sa_ref, sb_ref, q_ref, k_ref, v_ref):
        cos_t, sin_a, sin_b = cos_ref[...], sa_ref[...], sb_ref[...]
        kvn_t = kvn_ref[...]
        q = lax.dot_general(qn_ref[...], wq_ref[...], _NN, preferred_element_type=F32)
        k = lax.dot_general(kvn_t, wk_ref[...], _NN, preferred_element_type=F32)
        v_ref[...] = lax.dot_general(kvn_t, wv_ref[...], _NN, preferred_element_type=F32).astype(BF16)
        kp = kper_ref[...].astype(BF16)
        for h in range(hc):
            a, b, c = h * HEAD_PAD, h * HEAD_PAD + NOPE_DIM, (h + 1) * HEAD_PAD
            q_ref[:, a:b] = (q[:, a:b] * scale).astype(BF16)
            q_ref[:, b:c] = (_rope(q[:, b:c], cos_t, sin_a, sin_b, 1.0) * scale).astype(BF16)
            k_ref[:, a:b] = k[:, a:b].astype(BF16)
            k_ref[:, b:c] = kp

    row = lambda w: pl.BlockSpec((tm, w), lambda i, j: (i, 0))
    col = lambda w: pl.BlockSpec((R, w), lambda i, j: (0, j))
    out = lambda w: pl.BlockSpec((tm, w), lambda i, j: (i, j))
    W = H * HEAD_PAD
    res, _ = _call(
        body, name, [qn, kvn, w_uq, w_uk, w_uv, kper, *tabs],
        [row(R), row(R), col(hc * HEAD_PAD), col(hc * HEAD_PAD), col(hc * V_DIM), row(LANE), row(LANE), row(LANE),
         row(LANE)],
        [jax.ShapeDtypeStruct((S, W), BF16), jax.ShapeDtypeStruct((S, W), BF16),
         jax.ShapeDtypeStruct((S, H * V_DIM), BF16)],
        [out(hc * HEAD_PAD), out(hc * HEAD_PAD), out(hc * V_DIM)], (S // tm, H // hc), [], ("parallel", "parallel"))
    return res


def mla_gate_fwd(name, o, z, S, E, ts):
    def fn(i, step, ins, outs, accs, scr):
        g = ins[1][...]
        outs[0][...] = (ins[0][...] * (g * _sigmoid(g))).astype(BF16)
    return rowwise(name, fn, S, ts, [(o, rows(ts, E)), (z, rows(ts, E, 0))], [((S, E), BF16, rows(ts, E))])[0]


def mla_gate_bwd(name, dy, o, z, S, E, H, ts):
    def fn(i, step, ins, outs, accs, scr):
        d, ov, g = ins[0][...], ins[1][...], ins[2][...]
        sg = _sigmoid(g)
        outs[0][...] = (d * (g * sg)).astype(BF16)
        outs[1][...] = (d * ov * (sg * (1.0 + g * (1.0 - sg)))).astype(BF16)

    return rowwise(name, fn, S, ts, [(dy, rows(ts, E)), (o, rows(ts, E)), (z, rows(ts, E, 0))],
                   [((S, E), BF16, rows(ts, E)), ((S, E), BF16, rows(ts, E))])


def mla_unpack_q_bwd(name, dqt, tabs, S, H, t, scale):
    W = H * HEAD_PAD

    def fn(i, step, ins, outs, accs, scr):
        cos_t, sin_a, sin_b = ins[1][...], ins[2][...], ins[3][...]
        for h in range(H):
            a, b, c = h * HEAD_PAD, h * HEAD_PAD + NOPE_DIM, (h + 1) * HEAD_PAD
            dq = ins[0][h].T
            outs[0][:, a:b] = (dq[:, :NOPE_DIM] * scale).astype(BF16)
            outs[0][:, b:c] = (_rope(dq[:, NOPE_DIM:], cos_t, sin_a, sin_b, -1.0) * scale).astype(BF16)

    return rowwise(name, fn, S, t,
                   [(dqt, pl.BlockSpec((H, None, HEAD_PAD, t), lambda i: (0, i, 0, 0)))]
                   + [(tb, rows(t, LANE)) for tb in tabs],
                   [((S, W), BF16, rows(t, W))])[0]


def mla_unpack_k_bwd(name, dk, S, H, ts):
    W = H * HEAD_PAD

    def fn(i, step, ins, outs, accs, scr):
        dkpe = jnp.zeros((ts, LANE), F32)
        for h in range(H):
            a, b, c = h * HEAD_PAD, h * HEAD_PAD + NOPE_DIM, (h + 1) * HEAD_PAD
            outs[0][:, a:b] = ins[0][:, a:b].astype(BF16)
            outs[0][:, b:c] = jnp.zeros((ts, LANE), BF16)
            dkpe = dkpe + ins[0][:, b:c]
        outs[1][...] = dkpe

    return rowwise(name, fn, S, ts, [(dk, rows(ts, W))],
                   [((S, W), BF16, rows(ts, W)), ((S, LANE), F32, rows(ts, LANE))])


def mla_prep_bwd(name, dqn, dkvn_k, dkvn_v, z, dkpe, dg, qg, kvg, tabs, S, E, R, ts):
    qb, kb = E // R, E // R + 1
    ZW = E + 2 * R + LANE

    def fn(i, step, ins, outs, accs, scr):
        dzq, dqg = _rms_bwd(ins[0][...], ins[3][...], ins[7][...])
        dzkv, dkvg = _rms_bwd(ins[1][...] + ins[2][...], ins[4][...], ins[8][...])
        outs[0][:, :E] = ins[6][...]
        outs[0][:, E:E + R] = dzq.astype(BF16)
        outs[0][:, E + R:E + 2 * R] = dzkv.astype(BF16)
        outs[0][:, E + 2 * R:] = _rope(ins[5][...], ins[9][...], ins[10][...], ins[11][...], -1.0).astype(BF16)
        accs[0][...] += dqg
        accs[1][...] += dkvg

    return rowwise(name, fn, S, ts,
                   [(dqn, rows(ts, R)), (dkvn_k, rows(ts, R)), (dkvn_v, rows(ts, R)), (z, rows(ts, R, qb)),
                    (z, rows(ts, R, kb)), (dkpe, rows(ts, LANE)), (dg, rows(ts, E)),
                    (qg, whole((1, R))), (kvg, whole((1, R)))] + [(t, rows(ts, LANE)) for t in tabs],
                   [((S, ZW), BF16, rows(ts, ZW))], accs=[((1, R), F32), ((1, R), F32)])


_NT = (((1,), (1,)), ((), ()))
_NN = (((1,), (0,)), ((), ()))
_TN = (((0,), (0,)), ((), ()))


def _causal_mask_t(t):
    return lax.broadcasted_iota(jnp.int32, (t, t), 0) <= lax.broadcasted_iota(jnp.int32, (t, t), 1)


def _tile(i, t):
    return pl.ds(pl.multiple_of(i * t, t), t)


def flash_fwd(name, q, k, v, S, H, t, rider=None):
    nt = S // t

    def body(q_ref, k_ref, v_ref, o_ref, lse_ref, m_sc, l_sc, acc_sc):
        i = pl.program_id(1)
        m_sc[...] = jnp.full_like(m_sc, NEG_INF)
        l_sc[...] = jnp.zeros_like(l_sc)
        acc_sc[...] = jnp.zeros_like(acc_sc)
        q = q_ref[...]

        def tile(j, diag):
            s = lax.dot_general(k_ref[_tile(j, t), :], q, _NT, preferred_element_type=F32)
            if diag:
                s = jnp.where(_causal_mask_t(t), s, NEG_INF)
            m_prev = m_sc[...]
            m_new = jnp.maximum(m_prev, jnp.max(s, axis=0, keepdims=True))
            alpha = jnp.exp(m_prev - m_new)
            p = jnp.exp(s - m_new)
            l_sc[...] = alpha * l_sc[...] + jnp.sum(p, axis=0, keepdims=True)
            acc_sc[...] = alpha * acc_sc[...] + lax.dot_general(v_ref[_tile(j, t), :], p.astype(BF16), _TN,
                                                                 preferred_element_type=F32)
            m_sc[...] = m_new

        def off_diagonal(j, carry):
            tile(j, False)
            return carry

        lax.fori_loop(0, i, off_diagonal, 0)
        tile(i, True)
        l = l_sc[...]
        o_ref[...] = (acc_sc[...] / l).T
        lse_ref[...] = m_sc[...] + jnp.log(l)

    (o, lse), carried = _call(
        body, name, [q, k, v],
        [pl.BlockSpec((t, HEAD_PAD), lambda h, i: (i, h)), pl.BlockSpec((S, HEAD_PAD), lambda h, i: (0, h)),
         pl.BlockSpec((S, V_DIM), lambda h, i: (0, h))],
        [jax.ShapeDtypeStruct((S, H * V_DIM), F32), jax.ShapeDtypeStruct((H, nt, 1, t), F32)],
        [pl.BlockSpec((t, V_DIM), lambda h, i: (i, h)), pl.BlockSpec((None, None, 1, t), lambda h, i: (h, i, 0, 0))],
        (H, nt), [pltpu.VMEM((1, t), F32), pltpu.VMEM((1, t), F32), pltpu.VMEM((V_DIM, t), F32)],
        ("parallel", "parallel"), rider)
    return o, lse, carried


def flash_bwd(name, q, k, v, do, o, lse, S, H, t, rider=None):
    nt = S // t

    def body(q_ref, k_ref, v_ref, do_ref, o_ref, lse_ref, dq_ref, dk_ref, dv_ref, kt_sc, dl_sc, dv_sc):
        j = pl.program_id(1)

        @pl.when(j == 0)
        def _():
            dq_ref[...] = jnp.zeros_like(dq_ref)
            ones = jnp.ones((8, V_DIM), BF16)
            for i in range(nt):
                x = do_ref[i * t:(i + 1) * t, :].astype(F32) * o_ref[i * t:(i + 1) * t, :]
                hi = x.astype(BF16)
                lo = (x - hi.astype(F32)).astype(BF16)
                dl_sc[i] = (lax.dot_general(ones, hi, _NT, preferred_element_type=F32)
                            + lax.dot_general(ones, lo, _NT, preferred_element_type=F32))

        kj, vj = k_ref[...], v_ref[...]
        kt_sc[...] = kj.astype(F32).T.astype(BF16)
        dk_ref[...] = jnp.zeros_like(dk_ref)
        dv_sc[...] = jnp.zeros_like(dv_sc)

        def tile(i, diag):
            qi, doi = q_ref[_tile(i, t), :], do_ref[_tile(i, t), :]
            s = lax.dot_general(kj, qi, _NT, preferred_element_type=F32)
            p = jnp.exp(s - lse_ref[i])
            if diag:
                p = jnp.where(_causal_mask_t(t), p, 0.0)
            dv_sc[...] += lax.dot_general(p.astype(BF16), doi, _NN, preferred_element_type=F32)
            dp = lax.dot_general(vj, doi, _NT, preferred_element_type=F32)
            ds = (p * (dp - dl_sc[i, 0:1, :])).astype(BF16)
            dk_ref[...] += lax.dot_general(ds, qi, _NN, preferred_element_type=F32)
            dq_ref[i] += lax.dot_general(kt_sc[...], ds, _NN, preferred_element_type=F32)

        def off_diagonal(i, carry):
            tile(i, False)
            return carry

        tile(j, True)
        lax.fori_loop(j + 1, nt, off_diagonal, 0)
        dv_ref[...] = dv_sc[...].astype(BF16)

    head = lambda w: pl.BlockSpec((S, w), lambda h, j: (0, h))
    ktile = lambda w: pl.BlockSpec((t, w), lambda h, j: (j, h))
    (dq, dk, dv), carried = _call(
        body, name, [q, k, v, do, o, lse],
        [head(HEAD_PAD), ktile(HEAD_PAD), ktile(V_DIM), head(V_DIM), head(V_DIM),
         pl.BlockSpec((None, nt, 1, t), lambda h, j: (h, 0, 0, 0))],
        [jax.ShapeDtypeStruct((H, nt, HEAD_PAD, t), F32), jax.ShapeDtypeStruct((S, H * HEAD_PAD), F32),
         jax.ShapeDtypeStruct((S, H * V_DIM), BF16)],
        [pl.BlockSpec((None, nt, HEAD_PAD, t), lambda h, j: (h, 0, 0, 0)), ktile(HEAD_PAD), ktile(V_DIM)],
        (H, nt), [pltpu.VMEM((HEAD_PAD, t), BF16), pltpu.VMEM((nt, 8, t), F32), pltpu.VMEM((t, V_DIM), F32)],
        ("parallel", "arbitrary"), rider)
    return dq, dk, dv, carried


def _peers():
    x, y, c = lax.axis_index("x"), lax.axis_index("y"), lax.axis_index("c")
    me = 4 * x + 2 * y + c
    peers = []
    for fx, fy, fc in ((0, 0, 1), (1, 0, 0), (0, 1, 0), (1, 1, 0), (1, 0, 1), (0, 1, 1), (1, 1, 1)):
        px, py, pc = x ^ fx, y ^ fy, c ^ fc
        peers.append(((px, py, pc), 4 * px + 2 * py + pc))
    return me, peers


def _hbm_specs(n):
    return [pl.BlockSpec(memory_space=pl.ANY)] * n


class Rider:
    def __init__(self, arrs, gather):
        self.arrs, self.gather, self.n = list(arrs), gather, len(arrs)
        self.out_shapes = [jax.ShapeDtypeStruct((N_DEV,) + a.shape if gather else a.shape, a.dtype) for a in arrs]
        self.sems = [pltpu.SemaphoreType.DMA((self.n, N_DEV - 1)), pltpu.SemaphoreType.DMA((self.n, N_DEV - 1)),
                     pltpu.SemaphoreType.DMA((self.n,))]

    def _copies(self, srcs, dsts, sems):
        send_sems, recv_sems, local_sems = sems
        x, y, c = lax.axis_index("x"), lax.axis_index("y"), lax.axis_index("c")
        ident = lambda d: 4 * d[0] + 2 * d[1] + d[2]
        me, sibling = (x, y, c), (x, y, 1 - c)
        chips = [(1 - x, y), (x, 1 - y), (1 - x, 1 - y)]
        _, peers = _peers()

        def remote(a, k, incoming):
            if not self.gather:
                target, pid = peers[k]
                src, block = srcs[a].at[pid], (pid if incoming else ident(me))
            elif k == 0:
                target, src, block = sibling, srcs[a], ident(sibling if incoming else me)
            elif k <= 3:
                target = (*chips[k - 1], c)
                src, block = srcs[a], ident(target if incoming else me)
            else:
                landed = ident((*chips[k - 4], c))
                target, src = sibling, dsts[a].at[landed]
                block = ident((*chips[k - 4], 1 - c)) if incoming else landed
            return pltpu.make_async_remote_copy(
                src_ref=src, dst_ref=dsts[a].at[block], send_sem=send_sems.at[a, k], recv_sem=recv_sems.at[a, k],
                device_id=target, device_id_type=MESH)

        def local(a):
            return pltpu.make_async_copy(srcs[a] if self.gather else srcs[a].at[ident(me)], dsts[a].at[ident(me)],
                                         local_sems.at[a])

        return local, remote

    def start(self, srcs, dsts, sems):
        local, remote = self._copies(srcs, dsts, sems)
        for a in range(self.n):
            local(a).start()
            for k in range(4 if self.gather else N_DEV - 1):
                remote(a, k, False).start()

    def relay(self, srcs, dsts, sems):
        if not self.gather:
            return
        local, remote = self._copies(srcs, dsts, sems)
        for a in range(self.n):
            for k in range(1, 4):
                remote(a, k, True).wait_recv()
                remote(a, k + 3, False).start()

    def wait(self, srcs, dsts, sems):
        local, remote = self._copies(srcs, dsts, sems)
        for a in range(self.n):
            for k in range(N_DEV - 1):
                if not (self.gather and 1 <= k <= 3):
                    remote(a, k, True).wait_recv()
        for a in range(self.n):
            for k in range(N_DEV - 1):
                remote(a, k, False).wait_send()
            local(a).wait()


def _carry(body, n_in, n_out, rider, grid):
    n = rider.n
    steps = math.prod(grid)

    def wrapped(*refs):
        ins, r_in = refs[:n_in], refs[n_in:n_in + n]
        outs = refs[n_in + n:n_in + n + n_out]
        r_out = refs[n_in + n + n_out:n_in + 2 * n + n_out]
        scratch, sems = refs[n_in + 2 * n + n_out:-3], refs[-3:]
        step = 0
        for d, g in enumerate(grid):
            step = step * g + pl.program_id(d)

        @pl.when(step == 0)
        def _():
            rider.start(r_in, r_out, sems)

        if rider.gather:
            @pl.when(step == (3 * steps) // 4)
            def _():
                rider.relay(r_in, r_out, sems)

        body(*ins, *outs, *scratch)

        @pl.when(step == steps - 1)
        def _():
            rider.wait(r_in, r_out, sems)

    return wrapped


def _call(body, name, ins, in_specs, out_shape, out_specs, grid, scratch, sem, rider=None):
    if rider is None:
        return _pcall(body, name=name, out_shape=list(out_shape), grid=grid, in_specs=list(in_specs),
                      out_specs=list(out_specs), scratch_shapes=list(scratch), compiler_params=_cparams(*sem))(*ins), None
    res = _pcall(
        _carry(body, len(ins), len(out_shape), rider, grid), name=name,
        out_shape=list(out_shape) + rider.out_shapes, grid=grid,
        in_specs=list(in_specs) + _hbm_specs(rider.n), out_specs=list(out_specs) + _hbm_specs(rider.n),
        scratch_shapes=list(scratch) + rider.sems, compiler_params=_cparams(*(("arbitrary",) * len(grid))),
    )(*ins, *rider.arrs)
    return res[:len(out_shape)], res[len(out_shape):]


def exchange(name, arrs, gather):
    rider = Rider(arrs, gather)

    def body(*refs):
        srcs, dsts, sems = refs[:rider.n], refs[rider.n:2 * rider.n], refs[2 * rider.n:]
        rider.start(srcs, dsts, sems)
        rider.relay(srcs, dsts, sems)
        rider.wait(srcs, dsts, sems)

    return _pcall(body, name=name, out_shape=rider.out_shapes, in_specs=_hbm_specs(rider.n),
                  out_specs=_hbm_specs(rider.n), scratch_shapes=rider.sems)(*arrs)


def adamw(name, gslots, w, m, v, layer=0, prev=None):
    K, R, C = gslots.shape
    per_row = C * (K * gslots.dtype.itemsize + 7 * 4) * 2
    tr = R
    for cand in (1024, 512, 256, 128, 64, 32, 16, 8):
        if R % cand == 0:
            tr = cand
            if cand * per_row <= VMEM_LIMIT_BYTES // 2:
                break
    c1 = 1.0 / (1.0 - ADAM_B1 ** ADAM_STEP)
    c2 = 1.0 / (1.0 - ADAM_B2 ** ADAM_STEP)

    def body(g_ref, w_ref, m_ref, v_ref, *rest):
        go_ref, d_ref, mo_ref, vo_ref = rest[-4:]
        g = g_ref[0].astype(F32)
        for s in range(1, K):
            g = g + g_ref[s].astype(F32)
        mn = ADAM_B1 * m_ref[...] + (1.0 - ADAM_B1) * g
        vn = ADAM_B2 * v_ref[...] + (1.0 - ADAM_B2) * (g * g)
        go_ref[...] = g
        mo_ref[...] = mn
        vo_ref[...] = vn
        d_ref[...] = -ADAM_LR * ((mn * c1) / (jnp.sqrt(vn * c2) + ADAM_EPS) + ADAM_WD * w_ref[...])

    blk = pl.BlockSpec((None, tr, C), lambda i: (layer, i, 0))
    prev = [] if prev is None else list(prev)
    return _pcall(
        body, name=name, out_shape=[jax.ShapeDtypeStruct(w.shape, F32)] * 4, grid=(R // tr,),
        in_specs=[pl.BlockSpec((K, tr, C), lambda i: (0, i, 0)), blk, blk, blk] + _hbm_specs(len(prev)),
        out_specs=[blk] * 4, input_output_aliases={4 + q: q for q in range(len(prev))},
        compiler_params=_cparams("parallel"),
    )(gslots, w, m, v, *prev)


def _from_slots(gathered, ax):
    g = jnp.moveaxis(gathered, 0, ax)
    s = g.shape
    return g.reshape(s[:ax] + (s[ax] * s[ax + 1],) + s[ax + 2:])


def _to_slots(full, ax):
    s = full.shape
    g = full.reshape(s[:ax] + (N_DEV, s[ax] // N_DEV) + s[ax + 1:])
    g = jnp.moveaxis(g, ax, 0)
    return g.reshape(N_DEV, -1, g.shape[-1])


def _rope_tables(pos, S):
    inv_freq = ROPE_THETA ** (-jnp.arange(0, ROPE_DIM, 2, dtype=F32) / ROPE_DIM)
    ang = pos.astype(F32)[:, None] * inv_freq
    cos, sin = jnp.cos(ang), jnp.sin(ang)
    z = jnp.zeros((S, ROPE_DIM // 2), F32)
    cos_t = jnp.concatenate([cos, cos, z, z], axis=1)
    sin_a = jnp.concatenate([-sin, z, z, z], axis=1)
    sin_b = jnp.concatenate([z, sin, z, z], axis=1)
    return cos_t, sin_a, sin_b


def kernel(x, p, positions, pre_norm, post_norm, pool_w_in, pool_w_group, pool_scale, pool_w_out, mla_w_in, mla_q_norm, mla_w_uq, mla_kv_norm, mla_w_ukv, mla_w_out, ple_norm, ple_w_gate, ple_w_proj, loss_target, m_pre_norm, m_post_norm, m_pool_w_in, m_pool_w_group, m_pool_scale, m_pool_w_out, m_mla_w_in, m_mla_q_norm, m_mla_w_uq, m_mla_kv_norm, m_mla_w_ukv, m_mla_w_out, m_ple_norm, m_ple_w_gate, m_ple_w_proj, v_pre_norm, v_post_norm, v_pool_w_in, v_pool_w_group, v_pool_scale, v_pool_w_out, v_mla_w_in, v_mla_q_norm, v_mla_w_uq, v_mla_kv_norm, v_mla_w_ukv, v_mla_w_out, v_ple_norm, v_ple_w_gate, v_ple_w_proj):
    wl = dict(pre_norm=pre_norm, post_norm=post_norm, pool_w_in=pool_w_in, pool_w_group=pool_w_group,
              pool_scale=pool_scale, pool_w_out=pool_w_out, mla_w_in=mla_w_in, mla_q_norm=mla_q_norm,
              mla_w_uq=mla_w_uq, mla_kv_norm=mla_kv_norm, mla_w_ukv=mla_w_ukv, mla_w_out=mla_w_out,
              ple_norm=ple_norm, ple_w_gate=ple_w_gate, ple_w_proj=ple_w_proj)
    ml = dict(pre_norm=m_pre_norm, post_norm=m_post_norm, pool_w_in=m_pool_w_in, pool_w_group=m_pool_w_group,
              pool_scale=m_pool_scale, pool_w_out=m_pool_w_out, mla_w_in=m_mla_w_in, mla_q_norm=m_mla_q_norm,
              mla_w_uq=m_mla_w_uq, mla_kv_norm=m_mla_kv_norm, mla_w_ukv=m_mla_w_ukv, mla_w_out=m_mla_w_out,
              ple_norm=m_ple_norm, ple_w_gate=m_ple_w_gate, ple_w_proj=m_ple_w_proj)
    vl = dict(pre_norm=v_pre_norm, post_norm=v_post_norm, pool_w_in=v_pool_w_in, pool_w_group=v_pool_w_group,
              pool_scale=v_pool_scale, pool_w_out=v_pool_w_out, mla_w_in=v_mla_w_in, mla_q_norm=v_mla_q_norm,
              mla_w_uq=v_mla_w_uq, mla_kv_norm=v_mla_kv_norm, mla_w_ukv=v_mla_w_ukv, mla_w_out=v_mla_w_out,
              ple_norm=v_ple_norm, ple_w_gate=v_ple_w_gate, ple_w_proj=v_ple_w_proj)

    S, D = x.shape[1], x.shape[2]
    L = pre_norm.shape[0]
    E = pool_scale.shape[1]
    NG = pool_w_group.shape[1]
    R = mla_w_uq.shape[1]
    H = D // 128
    EM = H * V_DIM
    PD = p.shape[-1]
    me = 4 * lax.axis_index("x") + 2 * lax.axis_index("y") + lax.axis_index("c")
    ts = min(S, 256)
    tsw = min(S, 128)
    ta = min(S, 512)
    sm_scale = (NOPE_DIM + ROPE_DIM) ** -0.5

    wb = {n: wl[n].astype(BF16) for n in BIG}
    full = {}

    def ag_rider(host):
        return Rider([wb[n][l] for n, l in AG_PLAN[host]], True) if host in AG_PLAN else None

    def ag_done(host, results):
        for (n, l), g in zip(AG_PLAN[host], results):
            full[n, l] = _from_slots(g, SHARD_AXIS[n] - 1)

    small_sh = jnp.concatenate([wl[n].reshape(1, -1) for n in SMALL_SHARD], axis=1)
    g_in0, g_small = exchange("gather_first", [wb['pool_w_in'][0], small_sh], True)
    full['pool_w_in', 0] = _from_slots(g_in0, SHARD_AXIS['pool_w_in'] - 1)
    nq = mla_q_norm.size
    q_norm = _from_slots(g_small[:, 0, :nq].reshape((N_DEV,) + mla_q_norm.shape), 1)
    kv_norm = _from_slots(g_small[:, 0, nq:].reshape((N_DEV,) + mla_kv_norm.shape), 1)

    def mla_kernel_weights(j):
        w_in = full['mla_w_in', j]
        w_in_k = jnp.concatenate([w_in[:, 2 * R + ROPE_DIM:], w_in[:, :2 * R + ROPE_DIM],
                                  jnp.zeros((D, LANE - ROPE_DIM), BF16)], axis=1)
        w_uq_k = jnp.pad(full['mla_w_uq', j].reshape(R, H, NOPE_DIM + ROPE_DIM),
                         ((0, 0), (0, 0), (0, HEAD_PAD - NOPE_DIM - ROPE_DIM))).reshape(R, H * HEAD_PAD)
        w_ukv = full['mla_w_ukv', j].reshape(R, H, NOPE_DIM + V_DIM)
        w_uk_k = jnp.pad(w_ukv[..., :NOPE_DIM], ((0, 0), (0, 0), (0, HEAD_PAD - NOPE_DIM))).reshape(R, H * HEAD_PAD)
        w_uv_k = w_ukv[..., NOPE_DIM:].reshape(R, H * V_DIM)
        return w_in_k, w_uq_k, w_uk_k, w_uv_k

    def fmm(name, a, b, mode, out_dtype=F32):
        if name not in AG_PLAN:
            return mm(name, a, b, mode, out_dtype)
        out, carried = mm(name, a, b, mode, out_dtype, ag_rider(name))
        ag_done(name, carried)
        return out

    mla_w = {}
    tabs = _rope_tables(positions[0], S)

    h = x[0]
    saved = []
    for i in range(L):
        j = i // 2
        sv = dict(h=h)
        xn = rms_fwd(f"pre_norm_{i}", h, pre_norm[i:i + 1], S, D, ts)
        sv['xn'] = xn
        if i % 2 == 0:
            z = fmm(f"pool_in_{i}", xn, full['pool_w_in', j], 'nn')
            pooled = pool_fwd(f"pool_window_{i}", z, S, E, NG, tsw)
            y, mixed = pool_group_fwd(f"pool_group_{i}", pooled, full['pool_w_group', j], z, pool_scale[j:j + 1],
                                      S, E, NG)
            out = fmm(f"pool_out_{i}", y, full['pool_w_out', j], 'nn')
            sv.update(z=z, pooled=pooled, mixed=mixed, y=y)
        else:
            w_in_k, w_uq_k, w_uk_k, w_uv_k = mla_w[j] = mla_kernel_weights(j)
            z = fmm(f"mla_in_{i}", xn, w_in_k, 'nn')
            qn, kvn, kper = mla_prep_fwd(f"mla_prep_{i}", z, q_norm[j:j + 1], kv_norm[j:j + 1], tabs, S, EM, R, ts)
            qp, kp, vv = mla_up_fwd(f"mla_up_{i}", qn, kvn, w_uq_k, w_uk_k, w_uv_k, kper, tabs, S, H, R, sm_scale)
            o, lse, carried = flash_fwd(f"attn_{i}", qp, kp, vv, S, H, ta, ag_rider(f"attn_{i}"))
            if carried is not None:
                ag_done(f"attn_{i}", carried)
            y = mla_gate_fwd(f"mla_gate_{i}", o, z, S, EM, ts)
            out = fmm(f"mla_out_{i}", y, full['mla_w_out', j], 'nn')
            sv.update(z=z, qn=qn, kvn=kvn, qp=qp, kp=kp, vv=vv, o=o, lse=lse, y=y)
        h1, a = post_fwd(f"post_norm_{i}", h, out, post_norm[i:i + 1], ple_norm[i:i + 1], S, D, ts)
        h, gl, pp = ple_fwd(f"ple_{i}", a, full['ple_w_gate', i], p[i, 0], full['ple_w_proj', i], h1)
        sv.update(out=out, h1=h1, a=a, gl=gl, pp=pp)
        saved.append(sv)

    dh, loss_acc = loss_fwd_bwd("loss", h, loss_target[0], S, D, ts)
    loss = lax.psum(loss_acc[0, 0] * (0.5 / D), ("x", "y", "c"))

    gw = {n: [None] * wl[n].shape[0] for n in WEIGHTS}
    recv = {}

    def rs_rider(host):
        if host not in RS_PLAN:
            return None
        return Rider([_to_slots(gw[n][l], SHARD_AXIS[n] - 1).astype(BF16) for n, l in RS_PLAN[host]], False)

    def rs_done(host, results):
        for key, r in zip(RS_PLAN[host], results):
            recv[key] = r

    def bmm(name, a, b, mode):
        out_dtype = BF16 if mode == 'tn' else F32
        if name not in RS_PLAN:
            return mm(name, a, b, mode, out_dtype)
        out, carried = mm(name, a, b, mode, out_dtype, rs_rider(name))
        rs_done(name, carried)
        return out

    for i in reversed(range(L)):
        j = i // 2
        sv = saved[i]
        dpp, dgl = ple_bwd(f"ple_bwd_{i}", dh, sv['pp'], sv['gl'], S, D, ts)
        gw['ple_w_proj'][i] = bmm(f"ple_proj_dw_{i}", p[i, 0], dpp, 'tn')
        gw['ple_w_gate'][i] = bmm(f"ple_gate_dw_{i}", sv['a'], dgl, 'tn')
        da = bmm(f"ple_gate_dx_{i}", dgl, full['ple_w_gate', i], 'nt')
        dh1, dout, dpost, dple = post_bwd(f"post_norm_bwd_{i}", da, dh, sv['h1'], sv['out'],
                                          post_norm[i:i + 1], ple_norm[i:i + 1], S, D, ts)
        gw['post_norm'][i], gw['ple_norm'][i] = dpost[0], dple[0]
        xn = sv['xn']
        if i % 2 == 0:
            gw['pool_w_out'][j] = bmm(f"pool_out_dw_{i}", sv['y'], dout, 'tn')
            dy = bmm(f"pool_out_dx_{i}", dout, full['pool_w_out', j], 'nt')
            dmixed, dg, dscale = pool_gate_bwd(f"pool_gate_bwd_{i}", dy, sv['mixed'], sv['z'], pool_scale[j:j + 1],
                                               S, E, NG, tsw)
            gw['pool_scale'][j] = dscale[0]
            gw['pool_w_group'][j] = bmm(f"pool_group_dw_{i}", sv['pooled'], dmixed, 'tn')
            dpooled = bmm(f"pool_group_dx_{i}", dmixed, full['pool_w_group', j], 'nt')
            dz = pool_bwd(f"pool_window_bwd_{i}", dpooled, dg, S, E, NG, tsw)
            gw['pool_w_in'][j] = bmm(f"pool_in_dw_{i}", xn, dz, 'tn')
            dxn = bmm(f"pool_in_dx_{i}", dz, full['pool_w_in', j], 'nt')
        else:
            w_in_k, w_uq_k, w_uk_k, w_uv_k = mla_w[j]
            gw['mla_w_out'][j] = bmm(f"mla_out_dw_{i}", sv['y'], dout, 'tn')
            dy = bmm(f"mla_out_dx_{i}", dout, full['mla_w_out', j], 'nt')
            do, dg = mla_gate_bwd(f"mla_gate_bwd_{i}", dy, sv['o'], sv['z'], S, EM, H, ts)
            dqt, dkp, dvv, carried = flash_bwd(f"attn_bwd_{i}", sv['qp'], sv['kp'], sv['vv'], do, sv['o'], sv['lse'],
                                               S, H, ta, rs_rider(f"attn_bwd_{i}"))
            if carried is not None:
                rs_done(f"attn_bwd_{i}", carried)
            dq_raw = mla_unpack_q_bwd(f"mla_pack_q_bwd_{i}", dqt, tabs, S, H, ta, sm_scale)
            dk_raw, dkpe = mla_unpack_k_bwd(f"mla_pack_k_bwd_{i}", dkp, S, H, tsw)
            g_uq = bmm(f"mla_uq_dw_{i}", sv['qn'], dq_raw, 'tn')
            g_uk = bmm(f"mla_uk_dw_{i}", sv['kvn'], dk_raw, 'tn')
            g_uv = bmm(f"mla_uv_dw_{i}", sv['kvn'], dvv, 'tn')
            gw['mla_w_uq'][j] = g_uq.reshape(R, H, HEAD_PAD)[:, :, :NOPE_DIM + ROPE_DIM].reshape(R, -1)
            gw['mla_w_ukv'][j] = jnp.concatenate(
                [g_uk.reshape(R, H, HEAD_PAD)[:, :, :NOPE_DIM], g_uv.reshape(R, H, V_DIM)], axis=2).reshape(R, -1)
            dqn = bmm(f"mla_uq_dx_{i}", dq_raw, w_uq_k, 'nt')
            dkvn_k = bmm(f"mla_uk_dx_{i}", dk_raw, w_uk_k, 'nt')
            dkvn_v = bmm(f"mla_uv_dx_{i}", dvv, w_uv_k, 'nt')
            dz, dqg, dkvg = mla_prep_bwd(f"mla_prep_bwd_{i}", dqn, dkvn_k, dkvn_v, sv['z'], dkpe, dg,
                                         q_norm[j:j + 1], kv_norm[j:j + 1], tabs, S, EM, R, ts)
            g_in = bmm(f"mla_in_dw_{i}", xn, dz, 'tn')
            dxn = bmm(f"mla_in_dx_{i}", dz, w_in_k, 'nt')
            gw['mla_q_norm'][j], gw['mla_kv_norm'][j] = dqg[0], dkvg[0]
            gw['mla_w_in'][j] = jnp.concatenate([g_in[:, EM:EM + 2 * R + ROPE_DIM], g_in[:, :EM]], axis=1)
        dh, dpre = pre_bwd(f"pre_norm_bwd_{i}", dxn, dh1, sv['h'], pre_norm[i:i + 1], S, D, ts)
        gw['pre_norm'][i] = dpre[0]
    grad_x = dh[None]

    last = exchange("scatter_last", [_to_slots(gw[n][l], SHARD_AXIS[n] - 1).astype(BF16) for n, l in RS_LAST], False)
    for key, r in zip(RS_LAST, last):
        recv[key] = r
    small_names = SMALL_REPL + SMALL_SHARD
    gw = {n: jnp.stack(gw[n]) for n in small_names}
    small_g = jnp.concatenate([gw[n].reshape(1, -1) for n in small_names], axis=1)
    small_all = exchange("gather_small_grads", [small_g], True)[0]

    outs = {}
    for n in BIG:
        shp = wl[n].shape
        three = lambda a: a.reshape(shp[0], -1, shp[-1])
        res = None
        for l in range(shp[0]):
            res = adamw(f"adamw_{n}_{l}", recv[n, l], three(wl[n]), three(ml[n]), three(vl[n]), l, res)
        outs[n] = [a.reshape(shp) for a in res]

    pieces, off = [], 0
    for n in small_names:
        sz = gw[n].size
        g = small_all[:, :, off:off + sz]
        off += sz
        if n in SMALL_SHARD:
            rows_, cols_ = gw[n].shape
            g = lax.dynamic_slice_in_dim(g.reshape(N_DEV, rows_, cols_), me * (cols_ // N_DEV), cols_ // N_DEV, axis=2)
            g = g.reshape(N_DEV, 1, -1)
        pieces.append(g)
    gs = jnp.concatenate(pieces, axis=2)
    flat = lambda d: jnp.concatenate([d[n].reshape(1, -1) for n in small_names], axis=1)
    res = adamw("adamw_small", gs, flat(wl)[None], flat(ml)[None], flat(vl)[None])
    off = 0
    for n in small_names:
        sz = wl[n].size
        outs[n] = [a[0, :, off:off + sz].reshape(wl[n].shape) for a in res]
        off += sz

    return (loss, grad_x, *[outs[n][0] for n in WEIGHTS], *[outs[n][1] for n in WEIGHTS],
            *[outs[n][2] for n in WEIGHTS], *[outs[n][3] for n in WEIGHTS])
```

```python
import functools
import math

import jax
import jax.numpy as jnp
from jax import lax
from jax.experimental import pallas as pl
from jax.experimental.pallas import tpu as pltpu

F32 = jnp.float32
BF16 = jnp.bfloat16

N_DEV = 8
EPS = 1e-6
ROPE_THETA = 10000.0
NOPE_DIM = 128
ROPE_DIM = 64
V_DIM = 128
HEAD_PAD = 256
LANE = 128
POOL_WINDOWS = (2, 4, 8, 16)
POOL_HALO = 16
NEG_INF = -1e30
ADAM_LR = 0.001
ADAM_B1 = 0.9
ADAM_B2 = 0.999
ADAM_EPS = 1e-08
ADAM_WD = 0.01
ADAM_STEP = 10
VMEM_LIMIT_BYTES = 56 * 1024 * 1024
MM_MAX_TK = 3200
MESH = pl.DeviceIdType.MESH

SHARD_AXIS = dict(pre_norm=None, post_norm=None, pool_w_in=2, pool_w_group=2, pool_scale=None, pool_w_out=1,
                  mla_w_in=2, mla_q_norm=1, mla_w_uq=2, mla_kv_norm=1, mla_w_ukv=2, mla_w_out=1,
                  ple_norm=None, ple_w_gate=1, ple_w_proj=2)
WEIGHTS = tuple(SHARD_AXIS)
BIG = ('pool_w_in', 'pool_w_group', 'pool_w_out', 'mla_w_in', 'mla_w_uq', 'mla_w_ukv', 'mla_w_out',
       'ple_w_gate', 'ple_w_proj')
SMALL_REPL = ('pre_norm', 'post_norm', 'pool_scale', 'ple_norm')
SMALL_SHARD = ('mla_q_norm', 'mla_kv_norm')
SLOT_NATIVE = ('pool_w_in',)

AG_PLAN = {
    "pool_in_0": [("pool_w_group", 0), ("pool_w_out", 0), ("ple_w_gate", 0), ("ple_w_proj", 0)],
    "pool_out_0": [("mla_w_in", 0), ("mla_w_uq", 0), ("mla_w_ukv", 0)],
    "attn_1": [("mla_w_out", 0), ("ple_w_gate", 1), ("ple_w_proj", 1), ("pool_w_in", 1), ("pool_w_group", 1),
               ("pool_w_out", 1), ("ple_w_gate", 2), ("ple_w_proj", 2)],
    "pool_in_2": [("mla_w_in", 1), ("mla_w_uq", 1), ("mla_w_ukv", 1)],
    "attn_3": [("mla_w_out", 1), ("ple_w_gate", 3), ("ple_w_proj", 3)],
}
RS_PLAN = {
    "attn_bwd_3": [("ple_w_gate", 3), ("ple_w_proj", 3), ("mla_w_out", 1)],
    "mla_in_dw_3": [("mla_w_uq", 1), ("mla_w_ukv", 1)],
    "pool_out_dx_2": [("mla_w_in", 1)],
    "pool_in_dw_2": [("ple_w_gate", 2), ("ple_w_proj", 2)],
    "pool_in_dx_2": [("pool_w_out", 1)],
    "attn_bwd_1": [("pool_w_group", 1), ("pool_w_in", 1), ("ple_w_gate", 1), ("ple_w_proj", 1), ("mla_w_out", 0)],
    "mla_in_dw_1": [("mla_w_uq", 0), ("mla_w_ukv", 0)],
    "pool_out_dw_0": [("mla_w_in", 0)],
    "pool_out_dx_0": [("ple_w_gate", 0), ("ple_w_proj", 0)],
    "pool_in_dw_0": [("pool_w_out", 0)],
    "pool_in_dx_0": [("pool_w_group", 0)],
}
RS_LAST = [("pool_w_in", 0)]


def _pcall(body, **kw):
    return pl.pallas_call(body, **kw)


def _cparams(*sem):
    return pltpu.CompilerParams(dimension_semantics=sem, vmem_limit_bytes=VMEM_LIMIT_BYTES)


def _pick(n, cands):
    for c in cands:
        if n % c == 0:
            return c
    return n


def _sigmoid(x):
    return 1.0 / (1.0 + jnp.exp(-x))


def mm(name, a, b, mode, out_dtype=F32, rider=None, slots=False):
    squeeze = a.ndim == 2
    if squeeze:
        a = a[None]
        b = b if slots and mode != 'tn' else b[None]
    G = a.shape[0]
    if mode == 'nn':
        M, K = a.shape[1:]
        N = b.shape[2] * (N_DEV if slots else 1)
    elif mode == 'tn':
        K, M = a.shape[1:]
        N = b.shape[2]
    else:
        M, K = a.shape[1:]
        N = b.shape[1]
    n = (K if mode == 'nt' else N) // N_DEV
    tm = _pick(M, (1024, 512, 256, 128))
    tn = _pick(n if slots and mode != 'nt' else N, (1024, 768, 640, 512, 384, 256, 128))
    if slots and mode == 'nt':
        tk = _pick(n, (2048, 1024, 512, 256, 128))
    else:
        tk = K if K <= MM_MAX_TK else _pick(K, (2048, 1024, 640, 512, 384, 256, 128))
    nk = K // tk
    o_spec = pl.BlockSpec((None, tm, tn), lambda g, i, j, k: (g, i, j))
    o_shape = (G, M, N)
    if mode == 'nn':
        a_spec = pl.BlockSpec((None, tm, tk), lambda g, i, j, k: (g, i, k))
        b_spec = pl.BlockSpec((None, tk, tn), lambda g, i, j, k: (g, k, j))
        if slots:
            b_spec = pl.BlockSpec((None, tk, tn), lambda g, i, j, k: (j // (n // tn), k, j % (n // tn)))
        dims = (((1,), (0,)), ((), ()))
    elif mode == 'tn':
        a_spec = pl.BlockSpec((None, tk, tm), lambda g, i, j, k: (g, k, i))
        b_spec = pl.BlockSpec((None, tk, tn), lambda g, i, j, k: (g, k, j))
        if slots:
            o_spec = pl.BlockSpec((None, tm, tn), lambda g, i, j, k: (j // (n // tn), i, j % (n // tn)))
            o_shape = (N_DEV, M, n)
        dims = (((0,), (0,)), ((), ()))
    else:
        a_spec = pl.BlockSpec((None, tm, tk), lambda g, i, j, k: (g, i, k))
        b_spec = pl.BlockSpec((None, tn, tk), lambda g, i, j, k: (g, j, k))
        if slots:
            b_spec = pl.BlockSpec((None, tn, tk), lambda g, i, j, k: (k // (n // tk), j, k % (n // tk)))
        dims = (((1,), (1,)), ((), ()))

    def product(a_ref, b_ref):
        return lax.dot_general(a_ref[...].astype(BF16), b_ref[...].astype(BF16), dims, preferred_element_type=F32)

    def body_one(a_ref, b_ref, o_ref):
        o_ref[...] = product(a_ref, b_ref).astype(out_dtype)

    def body_acc(a_ref, b_ref, o_ref, acc_ref):
        k = pl.program_id(3)

        @pl.when(k == 0)
        def _():
            acc_ref[...] = product(a_ref, b_ref)

        @pl.when(jnp.logical_and(k > 0, k < nk - 1))
        def _():
            acc_ref[...] += product(a_ref, b_ref)

        @pl.when(k == nk - 1)
        def _():
            o_ref[...] = (acc_ref[...] + product(a_ref, b_ref)).astype(out_dtype)

    (out,), carried = _call(
        body_one if nk == 1 else body_acc, name, [a, b], [a_spec, b_spec],
        [jax.ShapeDtypeStruct(o_shape, out_dtype)], [o_spec], (G, M // tm, N // tn, nk),
        [] if nk == 1 else [pltpu.VMEM((tm, tn), F32)], ("parallel", "parallel", "parallel", "arbitrary"), rider)
    out = out[0] if squeeze and not (slots and mode == 'tn') else out
    return out if rider is None else (out, carried)


def rows(ts, width, colblk=0):
    return pl.BlockSpec((ts, width), lambda i: (i, colblk))


def whole(shape):
    return pl.BlockSpec(shape, lambda i: (0,) * len(shape))


def rowwise(name, fn, S, ts, ins, outs, accs=(), scratch=(), reverse=False):
    n_in, n_out, n_acc = len(ins), len(outs), len(accs)
    nt = S // ts

    def body(*refs):
        step = pl.program_id(0)
        i = nt - 1 - step if reverse else step
        in_refs = refs[:n_in]
        out_refs = refs[n_in:n_in + n_out]
        acc_refs = refs[n_in + n_out:n_in + n_out + n_acc]
        scr = refs[n_in + n_out + n_acc:]

        @pl.when(step == 0)
        def _():
            for r in acc_refs:
                r[...] = jnp.zeros_like(r)

        fn(i, step, in_refs, out_refs, acc_refs, scr)

    def fix(spec):
        if not reverse:
            return spec
        imap = spec.index_map
        return pl.BlockSpec(spec.block_shape, lambda s: imap(nt - 1 - s))

    res = _pcall(
        body, name=name,
        out_shape=[jax.ShapeDtypeStruct(s, d) for s, d, _ in outs] + [jax.ShapeDtypeStruct(s, d) for s, d in accs],
        grid=(nt,),
        in_specs=[fix(sp) for _, sp in ins],
        out_specs=[fix(sp) for _, _, sp in outs] + [whole(s) for s, _ in accs],
        scratch_shapes=list(scratch),
        compiler_params=_cparams("arbitrary"),
    )(*[a for a, _ in ins])
    return res


def _rstd(x):
    return lax.rsqrt(jnp.mean(x * x, axis=-1, keepdims=True) + EPS)


def _rms_bwd(dy, x, g):
    r = _rstd(x)
    xh = x * r
    gdy = dy * g
    dx = r * (gdy - xh * jnp.mean(xh * gdy, axis=-1, keepdims=True))
    return dx, jnp.sum(dy * xh, axis=0, keepdims=True)


def _rope(v, cos_t, sin_a, sin_b, sign):
    return v * cos_t + sign * (pltpu.roll(v, LANE - ROPE_DIM // 2, axis=1) * sin_a
                               + pltpu.roll(v, ROPE_DIM // 2, axis=1) * sin_b)


def rms_fwd(name, h, gain, S, D, ts):
    def fn(i, step, ins, outs, accs, scr):
        x = ins[0][...]
        outs[0][...] = (x * _rstd(x) * ins[1][...]).astype(BF16)
    return rowwise(name, fn, S, ts, [(h, rows(ts, D)), (gain, whole((1, D)))], [((S, D), BF16, rows(ts, D))])[0]


def post_fwd(name, h, out, post_g, ple_g, S, D, ts):
    def fn(i, step, ins, outs, accs, scr):
        o = ins[1][...]
        h1 = ins[0][...] + o * _rstd(o) * ins[2][...]
        outs[0][...] = h1
        outs[1][...] = (h1 * _rstd(h1) * ins[3][...]).astype(BF16)
    return rowwise(name, fn, S, ts,
                   [(h, rows(ts, D)), (out, rows(ts, D)), (post_g, whole((1, D))), (ple_g, whole((1, D)))],
                   [((S, D), F32, rows(ts, D)), ((S, D), BF16, rows(ts, D))])


def ple_fwd(name, a, w_gate, p, w_proj, h1):
    S, D = h1.shape
    PD = p.shape[1]
    tm, tn = _pick(S, (256, 128)), _pick(D, (2048, 1024, 512, 256, 128))

    def body(a_ref, wg_ref, p_ref, wp_ref, h1_ref, h_ref, gl_ref, pp_ref):
        gl = lax.dot_general(a_ref[...], wg_ref[...], _NN, preferred_element_type=F32)
        pp = lax.dot_general(p_ref[...].astype(BF16), wp_ref[...], _NN, preferred_element_type=F32)
        h_ref[...] = h1_ref[...] + pp * _sigmoid(gl)
        gl_ref[...] = gl.astype(BF16)
        pp_ref[...] = pp.astype(BF16)

    tile = pl.BlockSpec((tm, tn), lambda i, j: (i, j))
    res, _ = _call(
        body, name, [a, w_gate, p, w_proj, h1],
        [pl.BlockSpec((tm, D), lambda i, j: (i, 0)), pl.BlockSpec((D, tn), lambda i, j: (0, j)),
         pl.BlockSpec((tm, PD), lambda i, j: (i, 0)), pl.BlockSpec((PD, tn), lambda i, j: (0, j)), tile],
        [jax.ShapeDtypeStruct((S, D), F32), jax.ShapeDtypeStruct((S, D), BF16), jax.ShapeDtypeStruct((S, D), BF16)],
        [tile, tile, tile], (S // tm, D // tn), [], ("parallel", "parallel"))
    return res


def loss_fwd_bwd(name, h, tgt, S, D, ts):
    def fn(i, step, ins, outs, accs, scr):
        e = ins[0][...] - ins[1][...]
        outs[0][...] = e * (1.0 / D)
        accs[0][...] += jnp.broadcast_to(jnp.sum(e * e), (1, LANE))
    return rowwise(name, fn, S, ts, [(h, rows(ts, D)), (tgt, rows(ts, D))], [((S, D), F32, rows(ts, D))],
                   accs=[((1, LANE), F32)])


def ple_bwd(name, dh, pp, gl, S, D, ts):
    def fn(i, step, ins, outs, accs, scr):
        d = ins[0][...]
        gate = _sigmoid(ins[2][...].astype(F32))
        outs[0][...] = (d * gate).astype(BF16)
        outs[1][...] = (d * ins[1][...].astype(F32) * gate * (1.0 - gate)).astype(BF16)
    return rowwise(name, fn, S, ts, [(dh, rows(ts, D)), (pp, rows(ts, D)), (gl, rows(ts, D))],
                   [((S, D), BF16, rows(ts, D)), ((S, D), BF16, rows(ts, D))])


def post_bwd(name, dgl, w_gate, dh, h1, out, post_g, ple_g, S, D, ts):
    def fn(i, step, ins, outs, accs, scr):
        da = lax.dot_general(ins[0][...], ins[1][...], _NT, preferred_element_type=F32)
        dx, dple = _rms_bwd(da, ins[3][...], ins[6][...])
        dh1 = ins[2][...] + dx
        dout, dpost = _rms_bwd(dh1, ins[4][...], ins[5][...])
        outs[0][...] = dh1
        outs[1][...] = dout.astype(BF16)
        accs[0][...] += dpost
        accs[1][...] += dple
    return rowwise(name, fn, S, ts,
                   [(dgl, rows(ts, D)), (w_gate, whole((D, D))), (dh, rows(ts, D)), (h1, rows(ts, D)),
                    (out, rows(ts, D)), (post_g, whole((1, D))), (ple_g, whole((1, D)))],
                   [((S, D), F32, rows(ts, D)), ((S, D), BF16, rows(ts, D))],
                   accs=[((1, D), F32), ((1, D), F32)])


def pre_bwd(name, dxn, dh1, h, pre_g, S, D, ts):
    def fn(i, step, ins, outs, accs, scr):
        dx, dpre = _rms_bwd(ins[0][...], ins[2][...], ins[3][...])
        outs[0][...] = ins[1][...] + dx
        accs[0][...] += dpre
    return rowwise(name, fn, S, ts,
                   [(dxn, rows(ts, D)), (dh1, rows(ts, D)), (h, rows(ts, D)), (pre_g, whole((1, D)))],
                   [((S, D), F32, rows(ts, D))], accs=[((1, D), F32)])


def _window_sums(ext, w, back):
    n = ext.shape[0]
    s, win = ext, 1
    while win < w:
        s = s + pltpu.roll(s, win if back else n - win, axis=0)
        win *= 2
    return s


def pool_fwd(name, z, S, E, NG, ts):
    G = E // NG

    def fn(i, step, ins, outs, accs, scr):
        carry = scr[0]

        @pl.when(step == 0)
        def _():
            carry[...] = jnp.zeros_like(carry)

        t = i * ts + lax.broadcasted_iota(jnp.int32, (ts, 1), 0)
        for j, w in enumerate(POOL_WINDOWS):
            u = ins[0][:, j * G:(j + 1) * G]
            ext = jnp.concatenate([carry[:, j * G:(j + 1) * G], u], axis=0)
            sw = _window_sums(ext, w, True)[POOL_HALO:, :]
            cnt = jnp.minimum(t + 1, w).astype(F32)
            outs[0][j] = (sw / cnt - u).astype(BF16)
        carry[...] = ins[0][ts - POOL_HALO:, :]

    return rowwise(name, fn, S, ts, [(z, rows(ts, E, 0))],
                   [((NG, S, G), BF16, pl.BlockSpec((NG, ts, G), lambda i: (0, i, 0)))],
                   scratch=[pltpu.VMEM((POOL_HALO, E), F32)])[0]


def pool_bwd(name, dpooled, dg, S, E, NG, ts):
    G = E // NG

    def fn(i, step, ins, outs, accs, scr):
        carry = scr[0]

        @pl.when(step == 0)
        def _():
            carry[...] = jnp.zeros_like(carry)

        t = i * ts + lax.broadcasted_iota(jnp.int32, (ts, 1), 0)
        for j, w in enumerate(POOL_WINDOWS):
            d = ins[0][j]
            e = d / jnp.minimum(t + 1, w).astype(F32)
            ext = jnp.concatenate([e, carry[:, j * G:(j + 1) * G]], axis=0)
            sw = _window_sums(ext, w, False)[:ts, :]
            outs[0][:, j * G:(j + 1) * G] = (sw - d).astype(BF16)
            carry[:, j * G:(j + 1) * G] = e[:POOL_HALO, :]
        outs[0][:, E:] = ins[1][...]

    return rowwise(name, fn, S, ts,
                   [(dpooled, pl.BlockSpec((NG, ts, G), lambda i: (0, i, 0))), (dg, rows(ts, E))],
                   [((S, 2 * E), BF16, rows(ts, 2 * E))],
                   scratch=[pltpu.VMEM((POOL_HALO, E), F32)], reverse=True)[0]


def pool_group_fwd(name, pooled, w_group, z, scale, S, E, NG):
    G = E // NG
    tm = _pick(S, (1024, 512, 256, 128))

    def body(a_ref, b_ref, g_ref, sc_ref, y_ref, mx_ref):
        mx = lax.dot_general(a_ref[...], b_ref[...], _NN, preferred_element_type=F32)
        g = g_ref[...]
        y_ref[...] = (mx * sc_ref[...] * (g * _sigmoid(g))).astype(BF16)
        mx_ref[...] = mx.astype(BF16)

    grp = pl.BlockSpec((None, tm, G), lambda j, i: (j, i, 0))
    res, _ = _call(
        body, name, [pooled, w_group, z, scale],
        [grp, pl.BlockSpec((None, G, G), lambda j, i: (j, 0, 0)), pl.BlockSpec((tm, G), lambda j, i: (i, NG + j)),
         pl.BlockSpec((1, G), lambda j, i: (0, j))],
        [jax.ShapeDtypeStruct((S, E), BF16), jax.ShapeDtypeStruct((NG, S, G), BF16)],
        [pl.BlockSpec((tm, G), lambda j, i: (i, j)), grp], (NG, S // tm), [], ("parallel", "parallel"))
    return res


def pool_gate_bwd(name, dy, mixed, z, scale, S, E, NG, ts):
    G = E // NG

    def fn(i, step, ins, outs, accs, scr):
        for j in range(NG):
            sl = slice(j * G, (j + 1) * G)
            d = ins[0][:, sl]
            mx = ins[1][j].astype(F32)
            g = ins[2][:, sl]
            sc = ins[3][:, sl]
            sg = _sigmoid(g)
            si = g * sg
            outs[0][j] = (d * sc * si).astype(BF16)
            outs[1][:, sl] = (d * mx * sc * (sg * (1.0 + g * (1.0 - sg)))).astype(BF16)
            accs[0][:, sl] += jnp.sum(d * mx * si, axis=0, keepdims=True)

    return rowwise(name, fn, S, ts,
                   [(dy, rows(ts, E)), (mixed, pl.BlockSpec((NG, ts, G), lambda i: (0, i, 0))),
                    (z, rows(ts, E, 1)), (scale, whole((1, E)))],
                   [((NG, S, G), BF16, pl.BlockSpec((NG, ts, G), lambda i: (0, i, 0))), ((S, E), BF16, rows(ts, E))],
                   accs=[((1, E), F32)])


def mla_prep_fwd(name, z, qg, kvg, tabs, S, E, R, ts):
    qb, kb, pb = E // R, E // R + 1, (E + 2 * R) // LANE

    def fn(i, step, ins, outs, accs, scr):
        zq, zkv = ins[0][...], ins[1][...]
        outs[0][...] = (zq * _rstd(zq) * ins[3][...]).astype(BF16)
        outs[1][...] = (zkv * _rstd(zkv) * ins[4][...]).astype(BF16)
        outs[2][...] = _rope(ins[2][...], ins[5][...], ins[6][...], ins[7][...], 1.0)

    return rowwise(name, fn, S, ts,
                   [(z, rows(ts, R, qb)), (z, rows(ts, R, kb)), (z, rows(ts, LANE, pb)),
                    (qg, whole((1, R))), (kvg, whole((1, R)))] + [(t, rows(ts, LANE)) for t in tabs],
                   [((S, R), BF16, rows(ts, R)), ((S, R), BF16, rows(ts, R)), ((S, LANE), F32, rows(ts, LANE))])


def mla_up_fwd(name, qn, kvn, w_uq, w_uk, w_uv, kper, tabs, S, H, R, scale):
    hc = _pick(H, (4, 2, 1))
    tm = _pick(S, (1024, 512, 256, 128))

    def body(qn_ref, kvn_ref, wq_ref, wk_ref, wv_ref, kper_ref, cos_ref, sa_ref, sb_ref, q_ref, k_ref, v_ref):
        cos_t, sin_a, sin_b = cos_ref[...], sa_ref[...], sb_ref[...]
        kvn_t = kvn_ref[...]
        q = lax.dot_general(qn_ref[...], wq_ref[...], _NN, preferred_element_type=F32)
        k = lax.dot_general(kvn_t, wk_ref[...], _NN, preferred_element_type=F32)
        v_ref[...] = lax.dot_general(kvn_t, wv_ref[...], _NN, preferred_element_type=F32).astype(BF16)
        kp = kper_ref[...].astype(BF16)
        for h in range(hc):
            a, b, c = h * HEAD_PAD, h * HEAD_PAD + NOPE_DIM, (h + 1) * HEAD_PAD
            q_ref[:, a:b] = (q[:, a:b] * scale).astype(BF16)
            q_ref[:, b:c] = (_rope(q[:, b:c], cos_t, sin_a, sin_b, 1.0) * scale).astype(BF16)
            k_ref[:, a:b] = k[:, a:b].astype(BF16)
            k_ref[:, b:c] = kp

    row = lambda w: pl.BlockSpec((tm, w), lambda i, j: (i, 0))
    col = lambda w: pl.BlockSpec((R, w), lambda i, j: (0, j))
    out = lambda w: pl.BlockSpec((tm, w), lambda i, j: (i, j))
    W = H * HEAD_PAD
    res, _ = _call(
        body, name, [qn, kvn, w_uq, w_uk, w_uv, kper, *tabs],
        [row(R), row(R), col(hc * HEAD_PAD), col(hc * HEAD_PAD), col(hc * V_DIM), row(LANE), row(LANE), row(LANE),
         row(LANE)],
        [jax.ShapeDtypeStruct((S, W), BF16), jax.ShapeDtypeStruct((S, W), BF16),
         jax.ShapeDtypeStruct((S, H * V_DIM), BF16)],
        [out(hc * HEAD_PAD), out(hc * HEAD_PAD), out(hc * V_DIM)], (S // tm, H // hc), [], ("parallel", "parallel"))
    return res


def mla_gate_fwd(name, o, z, S, E, ts):
    def fn(i, step, ins, outs, accs, scr):
        g = ins[1][...]
        outs[0][...] = (ins[0][...] * (g * _sigmoid(g))).astype(BF16)
    return rowwise(name, fn, S, ts, [(o, rows(ts, E)), (z, rows(ts, E, 0))], [((S, E), BF16, rows(ts, E))])[0]


def mla_gate_bwd(name, dy, o, z, S, E, H, ts):
    def fn(i, step, ins, outs, accs, scr):
        d, ov, g = ins[0][...], ins[1][...], ins[2][...]
        sg = _sigmoid(g)
        outs[0][...] = (d * (g * sg)).astype(BF16)
        outs[1][...] = (d * ov * (sg * (1.0 + g * (1.0 - sg)))).astype(BF16)

    return rowwise(name, fn, S, ts, [(dy, rows(ts, E)), (o, rows(ts, E)), (z, rows(ts, E, 0))],
                   [((S, E), BF16, rows(ts, E)), ((S, E), BF16, rows(ts, E))])


def mla_unpack_q_bwd(name, dqt, tabs, S, H, t, scale):
    W = H * HEAD_PAD

    def fn(i, step, ins, outs, accs, scr):
        cos_t, sin_a, sin_b = ins[1][...], ins[2][...], ins[3][...]
        for h in range(H):
            a, b, c = h * HEAD_PAD, h * HEAD_PAD + NOPE_DIM, (h + 1) * HEAD_PAD
            dq = ins[0][h].T
            outs[0][:, a:b] = (dq[:, :NOPE_DIM] * scale).astype(BF16)
            outs[0][:, b:c] = (_rope(dq[:, NOPE_DIM:], cos_t, sin_a, sin_b, -1.0) * scale).astype(BF16)

    return rowwise(name, fn, S, t,
                   [(dqt, pl.BlockSpec((H, None, HEAD_PAD, t), lambda i: (0, i, 0, 0)))]
                   + [(tb, rows(t, LANE)) for tb in tabs],
                   [((S, W), BF16, rows(t, W))])[0]


def mla_unpack_k_bwd(name, dk, S, H, ts):
    W = H * HEAD_PAD

    def fn(i, step, ins, outs, accs, scr):
        dkpe = jnp.zeros((ts, LANE), F32)
        for h in range(H):
            a, b, c = h * HEAD_PAD, h * HEAD_PAD + NOPE_DIM, (h + 1) * HEAD_PAD
            outs[0][:, a:b] = ins[0][:, a:b].astype(BF16)
            outs[0][:, b:c] = jnp.zeros((ts, LANE), BF16)
            dkpe = dkpe + ins[0][:, b:c]
        outs[1][...] = dkpe

    return rowwise(name, fn, S, ts, [(dk, rows(ts, W))],
                   [((S, W), BF16, rows(ts, W)), ((S, LANE), F32, rows(ts, LANE))])


def mla_prep_bwd(name, dqn, dkvn_k, dkvn_v, z, dkpe, dg, qg, kvg, tabs, S, E, R, ts):
    qb, kb = E // R, E // R + 1
    ZW = E + 2 * R + LANE

    def fn(i, step, ins, outs, accs, scr):
        dzq, dqg = _rms_bwd(ins[0][...], ins[3][...], ins[7][...])
        dzkv, dkvg = _rms_bwd(ins[1][...] + ins[2][...], ins[4][...], ins[8][...])
        outs[0][:, :E] = ins[6][...]
        outs[0][:, E:E + R] = dzq.astype(BF16)
        outs[0][:, E + R:E + 2 * R] = dzkv.astype(BF16)
        outs[0][:, E + 2 * R:] = _rope(ins[5][...], ins[9][...], ins[10][...], ins[11][...], -1.0).astype(BF16)
        accs[0][...] += dqg
        accs[1][...] += dkvg

    return rowwise(name, fn, S, ts,
                   [(dqn, rows(ts, R)), (dkvn_k, rows(ts, R)), (dkvn_v, rows(ts, R)), (z, rows(ts, R, qb)),
                    (z, rows(ts, R, kb)), (dkpe, rows(ts, LANE)), (dg, rows(ts, E)),
                    (qg, whole((1, R))), (kvg, whole((1, R)))] + [(t, rows(ts, LANE)) for t in tabs],
                   [((S, ZW), BF16, rows(ts, ZW))], accs=[((1, R), F32), ((1, R), F32)])


_NT = (((1,), (1,)), ((), ()))
_NN = (((1,), (0,)), ((), ()))
_TN = (((0,), (0,)), ((), ()))


def _causal_mask_t(t):
    return lax.broadcasted_iota(jnp.int32, (t, t), 0) <= lax.broadcasted_iota(jnp.int32, (t, t), 1)


def _tile(i, t):
    return pl.ds(pl.multiple_of(i * t, t), t)


def flash_fwd(name, q, k, v, S, H, t, rider=None):
    nt = S // t

    def body(q_ref, k_ref, v_ref, o_ref, lse_ref, m_sc, l_sc, acc_sc):
        i = pl.program_id(1)
        m_sc[...] = jnp.full_like(m_sc, NEG_INF)
        l_sc[...] = jnp.zeros_like(l_sc)
        acc_sc[...] = jnp.zeros_like(acc_sc)
        q = q_ref[...]

        def tile(j, diag):
            s = lax.dot_general(k_ref[_tile(j, t), :], q, _NT, preferred_element_type=F32)
            if diag:
                s = jnp.where(_causal_mask_t(t), s, NEG_INF)
            m_prev = m_sc[...]
            m_new = jnp.maximum(m_prev, jnp.max(s, axis=0, keepdims=True))
            alpha = jnp.exp(m_prev - m_new)
            p = jnp.exp(s - m_new)
            l_sc[...] = alpha * l_sc[...] + jnp.sum(p, axis=0, keepdims=True)
            acc_sc[...] = alpha * acc_sc[...] + lax.dot_general(v_ref[_tile(j, t), :], p.astype(BF16), _TN,
                                                                 preferred_element_type=F32)
            m_sc[...] = m_new

        def off_diagonal(j, carry):
            tile(j, False)
            return carry

        lax.fori_loop(0, i, off_diagonal, 0)
        tile(i, True)
        l = l_sc[...]
        o_ref[...] = (acc_sc[...] / l).T
        lse_ref[...] = m_sc[...] + jnp.log(l)

    (o, lse), carried = _call(
        body, name, [q, k, v],
        [pl.BlockSpec((t, HEAD_PAD), lambda h, i: (i, h)), pl.BlockSpec((S, HEAD_PAD), lambda h, i: (0, h)),
         pl.BlockSpec((S, V_DIM), lambda h, i: (0, h))],
        [jax.ShapeDtypeStruct((S, H * V_DIM), F32), jax.ShapeDtypeStruct((H, nt, 1, t), F32)],
        [pl.BlockSpec((t, V_DIM), lambda h, i: (i, h)), pl.BlockSpec((None, None, 1, t), lambda h, i: (h, i, 0, 0))],
        (H, nt), [pltpu.VMEM((1, t), F32), pltpu.VMEM((1, t), F32), pltpu.VMEM((V_DIM, t), F32)],
        ("parallel", "parallel"), rider)
    return o, lse, carried


def flash_bwd(name, q, k, v, do, o, lse, S, H, t, rider=None):
    nt = S // t

    def body(q_ref, k_ref, v_ref, do_ref, o_ref, lse_ref, dq_ref, dk_ref, dv_ref, kt_sc, dl_sc, dv_sc):
        j = pl.program_id(1)

        @pl.when(j == 0)
        def _():
            dq_ref[...] = jnp.zeros_like(dq_ref)
            ones = jnp.ones((8, V_DIM), BF16)
            for i in range(nt):
                x = do_ref[i * t:(i + 1) * t, :].astype(F32) * o_ref[i * t:(i + 1) * t, :]
                hi = x.astype(BF16)
                lo = (x - hi.astype(F32)).astype(BF16)
                dl_sc[i] = (lax.dot_general(ones, hi, _NT, preferred_element_type=F32)
                            + lax.dot_general(ones, lo, _NT, preferred_element_type=F32))

        kj, vj = k_ref[...], v_ref[...]
        kt_sc[...] = kj.astype(F32).T.astype(BF16)
        dk_ref[...] = jnp.zeros_like(dk_ref)
        dv_sc[...] = jnp.zeros_like(dv_sc)

        def tile(i, diag):
            qi, doi = q_ref[_tile(i, t), :], do_ref[_tile(i, t), :]
            s = lax.dot_general(kj, qi, _NT, preferred_element_type=F32)
            p = jnp.exp(s - lse_ref[i])
            if diag:
                p = jnp.where(_causal_mask_t(t), p, 0.0)
            dv_sc[...] += lax.dot_general(p.astype(BF16), doi, _NN, preferred_element_type=F32)
            dp = lax.dot_general(vj, doi, _NT, preferred_element_type=F32)
            ds = (p * (dp - dl_sc[i, 0:1, :])).astype(BF16)
            dk_ref[...] += lax.dot_general(ds, qi, _NN, preferred_element_type=F32)
            dq_ref[i] += lax.dot_general(kt_sc[...], ds, _NN, preferred_element_type=F32)

        def off_diagonal(i, carry):
            tile(i, False)
            return carry

        tile(j, True)
        lax.fori_loop(j + 1, nt, off_diagonal, 0)
        dv_ref[...] = dv_sc[...].astype(BF16)

    head = lambda w: pl.BlockSpec((S, w), lambda h, j: (0, h))
    ktile = lambda w: pl.BlockSpec((t, w), lambda h, j: (j, h))
    (dq, dk, dv), carried = _call(
        body, name, [q, k, v, do, o, lse],
        [head(HEAD_PAD), ktile(HEAD_PAD), ktile(V_DIM), head(V_DIM), head(V_DIM),
         pl.BlockSpec((None, nt, 1, t), lambda h, j: (h, 0, 0, 0))],
        [jax.ShapeDtypeStruct((H, nt, HEAD_PAD, t), F32), jax.ShapeDtypeStruct((S, H * HEAD_PAD), F32),
         jax.ShapeDtypeStruct((S, H * V_DIM), BF16)],
        [pl.BlockSpec((None, nt, HEAD_PAD, t), lambda h, j: (h, 0, 0, 0)), ktile(HEAD_PAD), ktile(V_DIM)],
        (H, nt), [pltpu.VMEM((HEAD_PAD, t), BF16), pltpu.VMEM((nt, 8, t), F32), pltpu.VMEM((t, V_DIM), F32)],
        ("parallel", "arbitrary"), rider)
    return dq, dk, dv, carried


def _peers():
    x, y, c = lax.axis_index("x"), lax.axis_index("y"), lax.axis_index("c")
    me = 4 * x + 2 * y + c
    peers = []
    for fx, fy, fc in ((0, 0, 1), (1, 0, 0), (0, 1, 0), (1, 1, 0), (1, 0, 1), (0, 1, 1), (1, 1, 1)):
        px, py, pc = x ^ fx, y ^ fy, c ^ fc
        peers.append(((px, py, pc), 4 * px + 2 * py + pc))
    return me, peers


def _hbm_specs(n):
    return [pl.BlockSpec(memory_space=pl.ANY)] * n


class Rider:
    def __init__(self, arrs, gather):
        self.arrs, self.gather, self.n = list(arrs), gather, len(arrs)
        self.out_shapes = [jax.ShapeDtypeStruct((N_DEV,) + a.shape if gather else a.shape, a.dtype) for a in arrs]
        self.sems = [pltpu.SemaphoreType.DMA((self.n, N_DEV - 1)), pltpu.SemaphoreType.DMA((self.n, N_DEV - 1)),
                     pltpu.SemaphoreType.DMA((self.n,))]

    def _copies(self, srcs, dsts, sems):
        send_sems, recv_sems, local_sems = sems
        x, y, c = lax.axis_index("x"), lax.axis_index("y"), lax.axis_index("c")
        ident = lambda d: 4 * d[0] + 2 * d[1] + d[2]
        me, sibling = (x, y, c), (x, y, 1 - c)
        chips = [(1 - x, y), (x, 1 - y), (1 - x, 1 - y)]
        _, peers = _peers()

        def remote(a, k, incoming):
            if not self.gather:
                target, pid = peers[k]
                src, block = srcs[a].at[pid], (pid if incoming else ident(me))
            elif k == 0:
                target, src, block = sibling, srcs[a], ident(sibling if incoming else me)
            elif k <= 3:
                target = (*chips[k - 1], c)
                src, block = srcs[a], ident(target if incoming else me)
            else:
                landed = ident((*chips[k - 4], c))
                target, src = sibling, dsts[a].at[landed]
                block = ident((*chips[k - 4], 1 - c)) if incoming else landed
            return pltpu.make_async_remote_copy(
                src_ref=src, dst_ref=dsts[a].at[block], send_sem=send_sems.at[a, k], recv_sem=recv_sems.at[a, k],
                device_id=target, device_id_type=MESH)

        def local(a):
            return pltpu.make_async_copy(srcs[a] if self.gather else srcs[a].at[ident(me)], dsts[a].at[ident(me)],
                                         local_sems.at[a])

        return local, remote

    def start(self, srcs, dsts, sems):
        local, remote = self._copies(srcs, dsts, sems)
        for a in range(self.n):
            local(a).start()
            for k in range(4 if self.gather else N_DEV - 1):
                remote(a, k, False).start()

    def relay(self, srcs, dsts, sems):
        if not self.gather:
            return
        local, remote = self._copies(srcs, dsts, sems)
        for a in range(self.n):
            for k in range(1, 4):
                remote(a, k, True).wait_recv()
                remote(a, k + 3, False).start()

    def wait(self, srcs, dsts, sems):
        local, remote = self._copies(srcs, dsts, sems)
        for a in range(self.n):
            for k in range(N_DEV - 1):
                if not (self.gather and 1 <= k <= 3):
                    remote(a, k, True).wait_recv()
        for a in range(self.n):
            for k in range(N_DEV - 1):
                remote(a, k, False).wait_send()
            local(a).wait()


def _carry(body, n_in, n_out, rider, grid):
    n = rider.n
    steps = math.prod(grid)

    def wrapped(*refs):
        ins, r_in = refs[:n_in], refs[n_in:n_in + n]
        outs = refs[n_in + n:n_in + n + n_out]
        r_out = refs[n_in + n + n_out:n_in + 2 * n + n_out]
        scratch, sems = refs[n_in + 2 * n + n_out:-3], refs[-3:]
        step = 0
        for d, g in enumerate(grid):
            step = step * g + pl.program_id(d)

        @pl.when(step == 0)
        def _():
            rider.start(r_in, r_out, sems)

        if rider.gather:
            @pl.when(step == (3 * steps) // 4)
            def _():
                rider.relay(r_in, r_out, sems)

        body(*ins, *outs, *scratch)

        @pl.when(step == steps - 1)
        def _():
            rider.wait(r_in, r_out, sems)

    return wrapped


def _call(body, name, ins, in_specs, out_shape, out_specs, grid, scratch, sem, rider=None):
    if rider is None:
        return _pcall(body, name=name, out_shape=list(out_shape), grid=grid, in_specs=list(in_specs),
                      out_specs=list(out_specs), scratch_shapes=list(scratch), compiler_params=_cparams(*sem))(*ins), None
    res = _pcall(
        _carry(body, len(ins), len(out_shape), rider, grid), name=name,
        out_shape=list(out_shape) + rider.out_shapes, grid=grid,
        in_specs=list(in_specs) + _hbm_specs(rider.n), out_specs=list(out_specs) + _hbm_specs(rider.n),
        scratch_shapes=list(scratch) + rider.sems, compiler_params=_cparams(*(("arbitrary",) * len(grid))),
    )(*ins, *rider.arrs)
    return res[:len(out_shape)], res[len(out_shape):]


def exchange(name, arrs, gather):
    rider = Rider(arrs, gather)

    def body(*refs):
        srcs, dsts, sems = refs[:rider.n], refs[rider.n:2 * rider.n], refs[2 * rider.n:]
        rider.start(srcs, dsts, sems)
        rider.relay(srcs, dsts, sems)
        rider.wait(srcs, dsts, sems)

    return _pcall(body, name=name, out_shape=rider.out_shapes, in_specs=_hbm_specs(rider.n),
                  out_specs=_hbm_specs(rider.n), scratch_shapes=rider.sems)(*arrs)


def adamw(name, gslots, w, m, v, layer=0, prev=None):
    K, R, C = gslots.shape
    per_row = C * (K * gslots.dtype.itemsize + 7 * 4) * 2
    tr = R
    for cand in (1024, 512, 256, 128, 64, 32, 16, 8):
        if R % cand == 0:
            tr = cand
            if cand * per_row <= VMEM_LIMIT_BYTES // 2:
                break
    c1 = 1.0 / (1.0 - ADAM_B1 ** ADAM_STEP)
    c2 = 1.0 / (1.0 - ADAM_B2 ** ADAM_STEP)

    def body(g_ref, w_ref, m_ref, v_ref, *rest):
        go_ref, d_ref, mo_ref, vo_ref = rest[-4:]
        g = g_ref[0].astype(F32)
        for s in range(1, K):
            g = g + g_ref[s].astype(F32)
        mn = ADAM_B1 * m_ref[...] + (1.0 - ADAM_B1) * g
        vn = ADAM_B2 * v_ref[...] + (1.0 - ADAM_B2) * (g * g)
        go_ref[...] = g
        mo_ref[...] = mn
        vo_ref[...] = vn
        d_ref[...] = -ADAM_LR * ((mn * c1) / (jnp.sqrt(vn * c2) + ADAM_EPS) + ADAM_WD * w_ref[...])

    blk = pl.BlockSpec((None, tr, C), lambda i: (layer, i, 0))
    prev = [] if prev is None else list(prev)
    return _pcall(
        body, name=name, out_shape=[jax.ShapeDtypeStruct(w.shape, F32)] * 4, grid=(R // tr,),
        in_specs=[pl.BlockSpec((K, tr, C), lambda i: (0, i, 0)), blk, blk, blk] + _hbm_specs(len(prev)),
        out_specs=[blk] * 4, input_output_aliases={4 + q: q for q in range(len(prev))},
        compiler_params=_cparams("parallel"),
    )(gslots, w, m, v, *prev)


def _from_slots(gathered, ax):
    g = jnp.moveaxis(gathered, 0, ax)
    s = g.shape
    return g.reshape(s[:ax] + (s[ax] * s[ax + 1],) + s[ax + 2:])


def _to_slots(full, ax):
    s = full.shape
    g = full.reshape(s[:ax] + (N_DEV, s[ax] // N_DEV) + s[ax + 1:])
    g = jnp.moveaxis(g, ax, 0)
    return g.reshape(N_DEV, -1, g.shape[-1])


def _rope_tables(pos, S):
    inv_freq = ROPE_THETA ** (-jnp.arange(0, ROPE_DIM, 2, dtype=F32) / ROPE_DIM)
    ang = pos.astype(F32)[:, None] * inv_freq
    cos, sin = jnp.cos(ang), jnp.sin(ang)
    z = jnp.zeros((S, ROPE_DIM // 2), F32)
    cos_t = jnp.concatenate([cos, cos, z, z], axis=1)
    sin_a = jnp.concatenate([-sin, z, z, z], axis=1)
    sin_b = jnp.concatenate([z, sin, z, z], axis=1)
    return cos_t, sin_a, sin_b


def kernel(x, p, positions, pre_norm, post_norm, pool_w_in, pool_w_group, pool_scale, pool_w_out, mla_w_in, mla_q_norm, mla_w_uq, mla_kv_norm, mla_w_ukv, mla_w_out, ple_norm, ple_w_gate, ple_w_proj, loss_target, m_pre_norm, m_post_norm, m_pool_w_in, m_pool_w_group, m_pool_scale, m_pool_w_out, m_mla_w_in, m_mla_q_norm, m_mla_w_uq, m_mla_kv_norm, m_mla_w_ukv, m_mla_w_out, m_ple_norm, m_ple_w_gate, m_ple_w_proj, v_pre_norm, v_post_norm, v_pool_w_in, v_pool_w_group, v_pool_scale, v_pool_w_out, v_mla_w_in, v_mla_q_norm, v_mla_w_uq, v_mla_kv_norm, v_mla_w_ukv, v_mla_w_out, v_ple_norm, v_ple_w_gate, v_ple_w_proj):
    wl = dict(pre_norm=pre_norm, post_norm=post_norm, pool_w_in=pool_w_in, pool_w_group=pool_w_group,
              pool_scale=pool_scale, pool_w_out=pool_w_out, mla_w_in=mla_w_in, mla_q_norm=mla_q_norm,
              mla_w_uq=mla_w_uq, mla_kv_norm=mla_kv_norm, mla_w_ukv=mla_w_ukv, mla_w_out=mla_w_out,
              ple_norm=ple_norm, ple_w_gate=ple_w_gate, ple_w_proj=ple_w_proj)
    ml = dict(pre_norm=m_pre_norm, post_norm=m_post_norm, pool_w_in=m_pool_w_in, pool_w_group=m_pool_w_group,
              pool_scale=m_pool_scale, pool_w_out=m_pool_w_out, mla_w_in=m_mla_w_in, mla_q_norm=m_mla_q_norm,
              mla_w_uq=m_mla_w_uq, mla_kv_norm=m_mla_kv_norm, mla_w_ukv=m_mla_w_ukv, mla_w_out=m_mla_w_out,
              ple_norm=m_ple_norm, ple_w_gate=m_ple_w_gate, ple_w_proj=m_ple_w_proj)
    vl = dict(pre_norm=v_pre_norm, post_norm=v_post_norm, pool_w_in=v_pool_w_in, pool_w_group=v_pool_w_group,
              pool_scale=v_pool_scale, pool_w_out=v_pool_w_out, mla_w_in=v_mla_w_in, mla_q_norm=v_mla_q_norm,
              mla_w_uq=v_mla_w_uq, mla_kv_norm=v_mla_kv_norm, mla_w_ukv=v_mla_w_ukv, mla_w_out=v_mla_w_out,
              ple_norm=v_ple_norm, ple_w_gate=v_ple_w_gate, ple_w_proj=v_ple_w_proj)

    S, D = x.shape[1], x.shape[2]
    L = pre_norm.shape[0]
    E = pool_scale.shape[1]
    NG = pool_w_group.shape[1]
    R = mla_w_uq.shape[1]
    H = D // 128
    EM = H * V_DIM
    PD = p.shape[-1]
    me = 4 * lax.axis_index("x") + 2 * lax.axis_index("y") + lax.axis_index("c")
    ts = min(S, 256)
    tsw = min(S, 128)
    ta = min(S, 512)
    sm_scale = (NOPE_DIM + ROPE_DIM) ** -0.5

    wb = {n: wl[n].astype(BF16) for n in BIG}
    full = {}

    def ag_rider(host):
        return Rider([wb[n][l] for n, l in AG_PLAN[host]], True) if host in AG_PLAN else None

    def ag_done(host, results):
        for (n, l), g in zip(AG_PLAN[host], results):
            full[n, l] = g if n in SLOT_NATIVE else _from_slots(g, SHARD_AXIS[n] - 1)

    small_sh = jnp.concatenate([wl[n].reshape(1, -1) for n in SMALL_SHARD], axis=1)
    g_in0, g_small = exchange("gather_first", [wb['pool_w_in'][0], small_sh], True)
    full['pool_w_in', 0] = g_in0
    nq = mla_q_norm.size
    q_norm = _from_slots(g_small[:, 0, :nq].reshape((N_DEV,) + mla_q_norm.shape), 1)
    kv_norm = _from_slots(g_small[:, 0, nq:].reshape((N_DEV,) + mla_kv_norm.shape), 1)

    def mla_kernel_weights(j):
        w_in = full['mla_w_in', j]
        w_in_k = jnp.concatenate([w_in[:, 2 * R + ROPE_DIM:], w_in[:, :2 * R + ROPE_DIM],
                                  jnp.zeros((D, LANE - ROPE_DIM), BF16)], axis=1)
        w_uq_k = jnp.pad(full['mla_w_uq', j].reshape(R, H, NOPE_DIM + ROPE_DIM),
                         ((0, 0), (0, 0), (0, HEAD_PAD - NOPE_DIM - ROPE_DIM))).reshape(R, H * HEAD_PAD)
        w_ukv = full['mla_w_ukv', j].reshape(R, H, NOPE_DIM + V_DIM)
        w_uk_k = jnp.pad(w_ukv[..., :NOPE_DIM], ((0, 0), (0, 0), (0, HEAD_PAD - NOPE_DIM))).reshape(R, H * HEAD_PAD)
        w_uv_k = w_ukv[..., NOPE_DIM:].reshape(R, H * V_DIM)
        return w_in_k, w_uq_k, w_uk_k, w_uv_k

    def fmm(name, a, b, mode, out_dtype=F32, slots=False):
        if name not in AG_PLAN:
            return mm(name, a, b, mode, out_dtype, slots=slots)
        out, carried = mm(name, a, b, mode, out_dtype, ag_rider(name), slots)
        ag_done(name, carried)
        return out

    mla_w = {}
    tabs = _rope_tables(positions[0], S)

    h = x[0]
    saved = []
    for i in range(L):
        j = i // 2
        sv = dict(h=h)
        xn = rms_fwd(f"pre_norm_{i}", h, pre_norm[i:i + 1], S, D, ts)
        sv['xn'] = xn
        if i % 2 == 0:
            z = fmm(f"pool_in_{i}", xn, full['pool_w_in', j], 'nn', slots=True)
            pooled = pool_fwd(f"pool_window_{i}", z, S, E, NG, tsw)
            y, mixed = pool_group_fwd(f"pool_group_{i}", pooled, full['pool_w_group', j], z, pool_scale[j:j + 1],
                                      S, E, NG)
            out = fmm(f"pool_out_{i}", y, full['pool_w_out', j], 'nn')
            sv.update(z=z, pooled=pooled, mixed=mixed, y=y)
        else:
            w_in_k, w_uq_k, w_uk_k, w_uv_k = mla_w[j] = mla_kernel_weights(j)
            z = fmm(f"mla_in_{i}", xn, w_in_k, 'nn')
            qn, kvn, kper = mla_prep_fwd(f"mla_prep_{i}", z, q_norm[j:j + 1], kv_norm[j:j + 1], tabs, S, EM, R, ts)
            qp, kp, vv = mla_up_fwd(f"mla_up_{i}", qn, kvn, w_uq_k, w_uk_k, w_uv_k, kper, tabs, S, H, R, sm_scale)
            o, lse, carried = flash_fwd(f"attn_{i}", qp, kp, vv, S, H, ta, ag_rider(f"attn_{i}"))
            if carried is not None:
                ag_done(f"attn_{i}", carried)
            y = mla_gate_fwd(f"mla_gate_{i}", o, z, S, EM, ts)
            out = fmm(f"mla_out_{i}", y, full['mla_w_out', j], 'nn')
            sv.update(z=z, qn=qn, kvn=kvn, qp=qp, kp=kp, vv=vv, o=o, lse=lse, y=y)
        h1, a = post_fwd(f"post_norm_{i}", h, out, post_norm[i:i + 1], ple_norm[i:i + 1], S, D, ts)
        h, gl, pp = ple_fwd(f"ple_{i}", a, full['ple_w_gate', i], p[i, 0], full['ple_w_proj', i], h1)
        sv.update(out=out, h1=h1, a=a, gl=gl, pp=pp)
        saved.append(sv)

    dh, loss_acc = loss_fwd_bwd("loss", h, loss_target[0], S, D, ts)
    loss = lax.psum(loss_acc[0, 0] * (0.5 / D), ("x", "y", "c"))

    gw = {n: [None] * wl[n].shape[0] for n in WEIGHTS}
    recv = {}

    def rs_rider(host):
        if host not in RS_PLAN:
            return None
        return Rider([send_slots(n, l) for n, l in RS_PLAN[host]], False)

    def send_slots(n, l):
        return gw[n][l] if n in SLOT_NATIVE else _to_slots(gw[n][l], SHARD_AXIS[n] - 1).astype(BF16)

    def rs_done(host, results):
        for key, r in zip(RS_PLAN[host], results):
            recv[key] = r

    def bmm(name, a, b, mode, slots=False):
        out_dtype = BF16 if mode == 'tn' else F32
        if name not in RS_PLAN:
            return mm(name, a, b, mode, out_dtype, slots=slots)
        out, carried = mm(name, a, b, mode, out_dtype, rs_rider(name), slots)
        rs_done(name, carried)
        return out

    for i in reversed(range(L)):
        j = i // 2
        sv = saved[i]
        dpp, dgl = ple_bwd(f"ple_bwd_{i}", dh, sv['pp'], sv['gl'], S, D, ts)
        gw['ple_w_proj'][i] = bmm(f"ple_proj_dw_{i}", p[i, 0], dpp, 'tn')
        gw['ple_w_gate'][i] = bmm(f"ple_gate_dw_{i}", sv['a'], dgl, 'tn')
        dh1, dout, dpost, dple = post_bwd(f"post_norm_bwd_{i}", dgl, full['ple_w_gate', i], dh, sv['h1'], sv['out'],
                                          post_norm[i:i + 1], ple_norm[i:i + 1], S, D, ts)
        gw['post_norm'][i], gw['ple_norm'][i] = dpost[0], dple[0]
        xn = sv['xn']
        if i % 2 == 0:
            gw['pool_w_out'][j] = bmm(f"pool_out_dw_{i}", sv['y'], dout, 'tn')
            dy = bmm(f"pool_out_dx_{i}", dout, full['pool_w_out', j], 'nt')
            dmixed, dg, dscale = pool_gate_bwd(f"pool_gate_bwd_{i}", dy, sv['mixed'], sv['z'], pool_scale[j:j + 1],
                                               S, E, NG, tsw)
            gw['pool_scale'][j] = dscale[0]
            gw['pool_w_group'][j] = bmm(f"pool_group_dw_{i}", sv['pooled'], dmixed, 'tn')
            dpooled = bmm(f"pool_group_dx_{i}", dmixed, full['pool_w_group', j], 'nt')
            dz = pool_bwd(f"pool_window_bwd_{i}", dpooled, dg, S, E, NG, tsw)
            gw['pool_w_in'][j] = bmm(f"pool_in_dw_{i}", xn, dz, 'tn', slots=True)
            dxn = bmm(f"pool_in_dx_{i}", dz, full['pool_w_in', j], 'nt', slots=True)
        else:
            w_in_k, w_uq_k, w_uk_k, w_uv_k = mla_w[j]
            gw['mla_w_out'][j] = bmm(f"mla_out_dw_{i}", sv['y'], dout, 'tn')
            dy = bmm(f"mla_out_dx_{i}", dout, full['mla_w_out', j], 'nt')
            do, dg = mla_gate_bwd(f"mla_gate_bwd_{i}", dy, sv['o'], sv['z'], S, EM, H, ts)
            dqt, dkp, dvv, carried = flash_bwd(f"attn_bwd_{i}", sv['qp'], sv['kp'], sv['vv'], do, sv['o'], sv['lse'],
                                               S, H, ta, rs_rider(f"attn_bwd_{i}"))
            if carried is not None:
                rs_done(f"attn_bwd_{i}", carried)
            dq_raw = mla_unpack_q_bwd(f"mla_pack_q_bwd_{i}", dqt, tabs, S, H, ta, sm_scale)
            dk_raw, dkpe = mla_unpack_k_bwd(f"mla_pack_k_bwd_{i}", dkp, S, H, tsw)
            g_uq = bmm(f"mla_uq_dw_{i}", sv['qn'], dq_raw, 'tn')
            g_uk = bmm(f"mla_uk_dw_{i}", sv['kvn'], dk_raw, 'tn')
            g_uv = bmm(f"mla_uv_dw_{i}", sv['kvn'], dvv, 'tn')
            gw['mla_w_uq'][j] = g_uq.reshape(R, H, HEAD_PAD)[:, :, :NOPE_DIM + ROPE_DIM].reshape(R, -1)
            gw['mla_w_ukv'][j] = jnp.concatenate(
                [g_uk.reshape(R, H, HEAD_PAD)[:, :, :NOPE_DIM], g_uv.reshape(R, H, V_DIM)], axis=2).reshape(R, -1)
            dqn = bmm(f"mla_uq_dx_{i}", dq_raw, w_uq_k, 'nt')
            dkvn_k = bmm(f"mla_uk_dx_{i}", dk_raw, w_uk_k, 'nt')
            dkvn_v = bmm(f"mla_uv_dx_{i}", dvv, w_uv_k, 'nt')
            dz, dqg, dkvg = mla_prep_bwd(f"mla_prep_bwd_{i}", dqn, dkvn_k, dkvn_v, sv['z'], dkpe, dg,
                                         q_norm[j:j + 1], kv_norm[j:j + 1], tabs, S, EM, R, ts)
            g_in = bmm(f"mla_in_dw_{i}", xn, dz, 'tn')
            dxn = bmm(f"mla_in_dx_{i}", dz, w_in_k, 'nt')
            gw['mla_q_norm'][j], gw['mla_kv_norm'][j] = dqg[0], dkvg[0]
            gw['mla_w_in'][j] = jnp.concatenate([g_in[:, EM:EM + 2 * R + ROPE_DIM], g_in[:, :EM]], axis=1)
        dh, dpre = pre_bwd(f"pre_norm_bwd_{i}", dxn, dh1, sv['h'], pre_norm[i:i + 1], S, D, ts)
        gw['pre_norm'][i] = dpre[0]
    grad_x = dh[None]

    last = exchange("scatter_last", [send_slots(n, l) for n, l in RS_LAST], False)
    for key, r in zip(RS_LAST, last):
        recv[key] = r
    small_names = SMALL_REPL + SMALL_SHARD
    gw = {n: jnp.stack(gw[n]) for n in small_names}
    small_g = jnp.concatenate([gw[n].reshape(1, -1) for n in small_names], axis=1)
    small_all = exchange("gather_small_grads", [small_g], True)[0]

    outs = {}
    for n in BIG:
        shp = wl[n].shape
        three = lambda a: a.reshape(shp[0], -1, shp[-1])
        res = None
        for l in range(shp[0]):
            res = adamw(f"adamw_{n}_{l}", recv[n, l], three(wl[n]), three(ml[n]), three(vl[n]), l, res)
        outs[n] = [a.reshape(shp) for a in res]

    pieces, off = [], 0
    for n in small_names:
        sz = gw[n].size
        g = small_all[:, :, off:off + sz]
        off += sz
        if n in SMALL_SHARD:
            rows_, cols_ = gw[n].shape
            g = lax.dynamic_slice_in_dim(g.reshape(N_DEV, rows_, cols_), me * (cols_ // N_DEV), cols_ // N_DEV, axis=2)
            g = g.reshape(N_DEV, 1, -1)
        pieces.append(g)
    gs = jnp.concatenate(pieces, axis=2)
    flat = lambda d: jnp.concatenate([d[n].reshape(1, -1) for n in small_names], axis=1)
    res = adamw("adamw_small", gs, flat(wl)[None], flat(ml)[None], flat(vl)[None])
    off = 0
    for n in small_names:
        sz = wl[n].size
        outs[n] = [a[0, :, off:off + sz].reshape(wl[n].shape) for a in res]
        off += sz

    return (loss, grad_x, *[outs[n][0] for n in WEIGHTS], *[outs[n][1] for n in WEIGHTS],
            *[outs[n][2] for n in WEIGHTS], *[outs[n][3] for n in WEIGHTS])
```

```python
import functools
import math

import jax
import jax.numpy as jnp
from jax import lax
from jax.experimental import pallas as pl
from jax.experimental.pallas import tpu as pltpu

F32 = jnp.float32
BF16 = jnp.bfloat16

N_DEV = 8
EPS = 1e-6
ROPE_THETA = 10000.0
NOPE_DIM = 128
ROPE_DIM = 64
V_DIM = 128
HEAD_PAD = 256
LANE = 128
POOL_WINDOWS = (2, 4, 8, 16)
POOL_HALO = 16
NEG_INF = -1e30
ADAM_LR = 0.001
ADAM_B1 = 0.9
ADAM_B2 = 0.999
ADAM_EPS = 1e-08
ADAM_WD = 0.01
ADAM_STEP = 10
VMEM_LIMIT_BYTES = 56 * 1024 * 1024
MM_MAX_TK = 3200
MM_NT_A_BYTES = 8 * 1024 * 1024
MESH = pl.DeviceIdType.MESH

SHARD_AXIS = dict(pre_norm=None, post_norm=None, pool_w_in=2, pool_w_group=2, pool_scale=None, pool_w_out=1,
                  mla_w_in=2, mla_q_norm=1, mla_w_uq=2, mla_kv_norm=1, mla_w_ukv=2, mla_w_out=1,
                  ple_norm=None, ple_w_gate=1, ple_w_proj=2)
WEIGHTS = tuple(SHARD_AXIS)
BIG = ('pool_w_in', 'pool_w_group', 'pool_w_out', 'mla_w_in', 'mla_w_uq', 'mla_w_ukv', 'mla_w_out',
       'ple_w_gate', 'ple_w_proj')
SMALL_REPL = ('pre_norm', 'post_norm', 'pool_scale', 'ple_norm')
SMALL_SHARD = ('mla_q_norm', 'mla_kv_norm')
SLOT_NATIVE = ('pool_w_in',)

AG_PLAN = {
    "pool_in_0": [("pool_w_group", 0), ("pool_w_out", 0), ("ple_w_gate", 0), ("ple_w_proj", 0)],
    "pool_out_0": [("mla_w_in", 0), ("mla_w_uq", 0), ("mla_w_ukv", 0)],
    "attn_1": [("mla_w_out", 0), ("ple_w_gate", 1), ("ple_w_proj", 1), ("pool_w_in", 1), ("pool_w_group", 1),
               ("mla_w_in", 1)],
    "pool_in_2": [("pool_w_out", 1), ("ple_w_gate", 2), ("ple_w_proj", 2), ("mla_w_uq", 1), ("mla_w_ukv", 1)],
    "attn_3": [("mla_w_out", 1), ("ple_w_gate", 3), ("ple_w_proj", 3)],
}
RS_PLAN = {
    "attn_bwd_3": [("ple_w_gate", 3), ("ple_w_proj", 3), ("mla_w_out", 1)],
    "mla_in_dw_3": [("mla_w_uq", 1), ("mla_w_ukv", 1)],
    "pool_out_dx_2": [("mla_w_in", 1)],
    "pool_in_dw_2": [("ple_w_gate", 2), ("ple_w_proj", 2)],
    "pool_in_dx_2": [("pool_w_out", 1)],
    "attn_bwd_1": [("pool_w_group", 1), ("pool_w_in", 1), ("ple_w_gate", 1), ("ple_w_proj", 1), ("mla_w_out", 0)],
    "mla_in_dw_1": [("mla_w_uq", 0), ("mla_w_ukv", 0)],
    "pool_out_dw_0": [("mla_w_in", 0)],
    "pool_out_dx_0": [("ple_w_gate", 0), ("ple_w_proj", 0)],
    "pool_group_dx_0": [("pool_w_group", 0)],
    "pool_in_dw_0": [("pool_w_out", 0)],
    "pool_in_dx_0": [("pool_w_in", 0, 0)],
}
RS_LAST = [("pool_w_in", 0, 1)]


def _pcall(body, **kw):
    return pl.pallas_call(body, **kw)


def _cparams(*sem):
    return pltpu.CompilerParams(dimension_semantics=sem, vmem_limit_bytes=VMEM_LIMIT_BYTES)


def _pick(n, cands):
    for c in cands:
        if n % c == 0:
            return c
    return n


def _sigmoid(x):
    return 1.0 / (1.0 + jnp.exp(-x))


def mm(name, a, b, mode, out_dtype=F32, rider=None, slots=False):
    squeeze = a.ndim == 2
    if squeeze:
        a = a[None]
        b = b if slots and mode != 'tn' else b[None]
    G = a.shape[0]
    if mode == 'nn':
        M, K = a.shape[1:]
        N = b.shape[2] * (N_DEV if slots else 1)
    elif mode == 'tn':
        K, M = a.shape[1:]
        N = b.shape[2]
    else:
        M, K = a.shape[1:]
        N = b.shape[1]
    n = (K if mode == 'nt' else N) // N_DEV
    tm = _pick(M, (1024, 512, 256, 128))
    tn = _pick(n if slots and mode != 'nt' else N, (1024, 768, 640, 512, 384, 256, 128))
    if slots and mode == 'nt':
        tk = _pick(n, (2048, 1024, 512, 256, 128))
    else:
        tk = K if K <= MM_MAX_TK else _pick(K, (2048, 1024, 640, 512, 384, 256, 128))
    nk = K // tk
    if mode == 'nt' and M % 2048 == 0 and 2048 * tk * a.dtype.itemsize <= MM_NT_A_BYTES:
        tm = 2048
    o_spec = pl.BlockSpec((None, tm, tn), lambda g, i, j, k: (g, i, j))
    o_shape = (G, M, N)
    if mode == 'nn':
        a_spec = pl.BlockSpec((None, tm, tk), lambda g, i, j, k: (g, i, k))
        b_spec = pl.BlockSpec((None, tk, tn), lambda g, i, j, k: (g, k, j))
        if slots:
            b_spec = pl.BlockSpec((None, tk, tn), lambda g, i, j, k: (j // (n // tn), k, j % (n // tn)))
        dims = (((1,), (0,)), ((), ()))
    elif mode == 'tn':
        a_spec = pl.BlockSpec((None, tk, tm), lambda g, i, j, k: (g, k, i))
        b_spec = pl.BlockSpec((None, tk, tn), lambda g, i, j, k: (g, k, j))
        if slots:
            o_spec = pl.BlockSpec((None, tm, tn), lambda g, i, j, k: (j // (n // tn), i, j % (n // tn)))
            o_shape = (N_DEV, M, n)
        dims = (((0,), (0,)), ((), ()))
    else:
        a_spec = pl.BlockSpec((None, tm, tk), lambda g, i, j, k: (g, i, k))
        b_spec = pl.BlockSpec((None, tn, tk), lambda g, i, j, k: (g, j, k))
        if slots:
            b_spec = pl.BlockSpec((None, tn, tk), lambda g, i, j, k: (k // (n // tk), j, k % (n // tk)))
        dims = (((1,), (1,)), ((), ()))

    def product(a_ref, b_ref):
        return lax.dot_general(a_ref[...].astype(BF16), b_ref[...].astype(BF16), dims, preferred_element_type=F32)

    def body_one(a_ref, b_ref, o_ref):
        o_ref[...] = product(a_ref, b_ref).astype(out_dtype)

    def body_acc(a_ref, b_ref, o_ref, acc_ref):
        k = pl.program_id(3)

        @pl.when(k == 0)
        def _():
            acc_ref[...] = product(a_ref, b_ref)

        @pl.when(jnp.logical_and(k > 0, k < nk - 1))
        def _():
            acc_ref[...] += product(a_ref, b_ref)

        @pl.when(k == nk - 1)
        def _():
            o_ref[...] = (acc_ref[...] + product(a_ref, b_ref)).astype(out_dtype)

    (out,), carried = _call(
        body_one if nk == 1 else body_acc, name, [a, b], [a_spec, b_spec],
        [jax.ShapeDtypeStruct(o_shape, out_dtype)], [o_spec], (G, M // tm, N // tn, nk),
        [] if nk == 1 else [pltpu.VMEM((tm, tn), F32)], ("parallel", "parallel", "parallel", "arbitrary"), rider)
    out = out[0] if squeeze and not (slots and mode == 'tn') else out
    return out if rider is None else (out, carried)


def rows(ts, width, colblk=0):
    return pl.BlockSpec((ts, width), lambda i: (i, colblk))


def whole(shape):
    return pl.BlockSpec(shape, lambda i: (0,) * len(shape))


def rowwise(name, fn, S, ts, ins, outs, accs=(), scratch=(), reverse=False):
    n_in, n_out, n_acc = len(ins), len(outs), len(accs)
    nt = S // ts

    def body(*refs):
        step = pl.program_id(0)
        i = nt - 1 - step if reverse else step
        in_refs = refs[:n_in]
        out_refs = refs[n_in:n_in + n_out]
        acc_refs = refs[n_in + n_out:n_in + n_out + n_acc]
        scr = refs[n_in + n_out + n_acc:]

        @pl.when(step == 0)
        def _():
            for r in acc_refs:
                r[...] = jnp.zeros_like(r)

        fn(i, step, in_refs, out_refs, acc_refs, scr)

    def fix(spec):
        if not reverse:
            return spec
        imap = spec.index_map
        return pl.BlockSpec(spec.block_shape, lambda s: imap(nt - 1 - s))

    res = _pcall(
        body, name=name,
        out_shape=[jax.ShapeDtypeStruct(s, d) for s, d, _ in outs] + [jax.ShapeDtypeStruct(s, d) for s, d in accs],
        grid=(nt,),
        in_specs=[fix(sp) for _, sp in ins],
        out_specs=[fix(sp) for _, _, sp in outs] + [whole(s) for s, _ in accs],
        scratch_shapes=list(scratch),
        compiler_params=_cparams("arbitrary"),
    )(*[a for a, _ in ins])
    return res


def _rstd(x):
    return lax.rsqrt(jnp.mean(x * x, axis=-1, keepdims=True) + EPS)


def _rms_bwd(dy, x, g):
    r = _rstd(x)
    xh = x * r
    gdy = dy * g
    dx = r * (gdy - xh * jnp.mean(xh * gdy, axis=-1, keepdims=True))
    return dx, jnp.sum(dy * xh, axis=0, keepdims=True)


def _rope(v, cos_t, sin_a, sin_b, sign):
    return v * cos_t + sign * (pltpu.roll(v, LANE - ROPE_DIM // 2, axis=1) * sin_a
                               + pltpu.roll(v, ROPE_DIM // 2, axis=1) * sin_b)


def rms_fwd(name, h, gain, S, D, ts):
    def fn(i, step, ins, outs, accs, scr):
        x = ins[0][...]
        outs[0][...] = (x * _rstd(x) * ins[1][...]).astype(BF16)
    return rowwise(name, fn, S, ts, [(h, rows(ts, D)), (gain, whole((1, D)))], [((S, D), BF16, rows(ts, D))])[0]


def post_fwd(name, h, out, post_g, ple_g, S, D, ts):
    def fn(i, step, ins, outs, accs, scr):
        o = ins[1][...]
        h1 = ins[0][...] + o * _rstd(o) * ins[2][...]
        outs[0][...] = h1
        outs[1][...] = (h1 * _rstd(h1) * ins[3][...]).astype(BF16)
    return rowwise(name, fn, S, ts,
                   [(h, rows(ts, D)), (out, rows(ts, D)), (post_g, whole((1, D))), (ple_g, whole((1, D)))],
                   [((S, D), F32, rows(ts, D)), ((S, D), BF16, rows(ts, D))])


def ple_fwd(name, a, w_gate, p, w_proj, h1, next_gain=None, target=None):
    S, D = h1.shape
    PD = p.shape[1]
    tm = _pick(S, (256, 128))
    last = target is not None

    def body(a_ref, wg_ref, p_ref, wp_ref, h1_ref, x_ref, gl_ref, pp_ref, o1_ref, o2_ref):
        i = pl.program_id(0)
        gl = lax.dot_general(a_ref[...], wg_ref[...], _NN, preferred_element_type=F32)
        pp = lax.dot_general(p_ref[...].astype(BF16), wp_ref[...], _NN, preferred_element_type=F32)
        h = h1_ref[...] + pp * _sigmoid(gl)
        gl_ref[...] = gl.astype(BF16)
        pp_ref[...] = pp.astype(BF16)
        if last:
            e = h - x_ref[...]
            o1_ref[...] = e * (1.0 / D)

            @pl.when(i == 0)
            def _():
                o2_ref[...] = jnp.zeros_like(o2_ref)

            o2_ref[...] += jnp.broadcast_to(jnp.sum(e * e), (1, LANE))
        else:
            o1_ref[...] = h
            o2_ref[...] = (h * _rstd(h) * x_ref[...]).astype(BF16)

    row = lambda w: pl.BlockSpec((tm, w), lambda i: (i, 0))
    res, _ = _call(
        body, name, [a, w_gate, p, w_proj, h1, target if last else next_gain],
        [row(D), whole((D, D)), row(PD), whole((PD, D)), row(D), row(D) if last else whole((1, D))],
        [jax.ShapeDtypeStruct((S, D), BF16), jax.ShapeDtypeStruct((S, D), BF16), jax.ShapeDtypeStruct((S, D), F32),
         jax.ShapeDtypeStruct((1, LANE), F32) if last else jax.ShapeDtypeStruct((S, D), BF16)],
        [row(D), row(D), row(D), whole((1, LANE)) if last else row(D)], (S // tm,), [],
        ("arbitrary",) if last else ("parallel",))
    return res


def loss_fwd_bwd(name, h, tgt, S, D, ts):
    def fn(i, step, ins, outs, accs, scr):
        e = ins[0][...] - ins[1][...]
        outs[0][...] = e * (1.0 / D)
        accs[0][...] += jnp.broadcast_to(jnp.sum(e * e), (1, LANE))
    return rowwise(name, fn, S, ts, [(h, rows(ts, D)), (tgt, rows(ts, D))], [((S, D), F32, rows(ts, D))],
                   accs=[((1, LANE), F32)])


def ple_bwd(name, dh, pp, gl, S, D, ts):
    def fn(i, step, ins, outs, accs, scr):
        d = ins[0][...]
        gate = _sigmoid(ins[2][...].astype(F32))
        outs[0][...] = (d * gate).astype(BF16)
        outs[1][...] = (d * ins[1][...].astype(F32) * gate * (1.0 - gate)).astype(BF16)
    return rowwise(name, fn, S, ts, [(dh, rows(ts, D)), (pp, rows(ts, D)), (gl, rows(ts, D))],
                   [((S, D), BF16, rows(ts, D)), ((S, D), BF16, rows(ts, D))])


def post_bwd(name, dgl, w_gate, dh, h1, out, post_g, ple_g, S, D, ts):
    def fn(i, step, ins, outs, accs, scr):
        da = lax.dot_general(ins[0][...], ins[1][...], _NT, preferred_element_type=F32)
        dx, dple = _rms_bwd(da, ins[3][...], ins[6][...])
        dh1 = ins[2][...] + dx
        dout, dpost = _rms_bwd(dh1, ins[4][...], ins[5][...])
        outs[0][...] = dh1
        outs[1][...] = dout.astype(BF16)
        accs[0][...] += dpost
        accs[1][...] += dple
    return rowwise(name, fn, S, ts,
                   [(dgl, rows(ts, D)), (w_gate, whole((D, D))), (dh, rows(ts, D)), (h1, rows(ts, D)),
                    (out, rows(ts, D)), (post_g, whole((1, D))), (ple_g, whole((1, D)))],
                   [((S, D), F32, rows(ts, D)), ((S, D), BF16, rows(ts, D))],
                   accs=[((1, D), F32), ((1, D), F32)])


def pre_bwd(name, dxn, dh1, h, pre_g, S, D, ts):
    def fn(i, step, ins, outs, accs, scr):
        dx, dpre = _rms_bwd(ins[0][...], ins[2][...], ins[3][...])
        outs[0][...] = ins[1][...] + dx
        accs[0][...] += dpre
    return rowwise(name, fn, S, ts,
                   [(dxn, rows(ts, D)), (dh1, rows(ts, D)), (h, rows(ts, D)), (pre_g, whole((1, D)))],
                   [((S, D), F32, rows(ts, D))], accs=[((1, D), F32)])


def _window_sums(ext, w, back):
    n = ext.shape[0]
    s, win = ext, 1
    while win < w:
        s = s + pltpu.roll(s, win if back else n - win, axis=0)
        win *= 2
    return s


def pool_fwd(name, z, S, E, NG, ts):
    G = E // NG

    def fn(i, step, ins, outs, accs, scr):
        carry = scr[0]

        @pl.when(step == 0)
        def _():
            carry[...] = jnp.zeros_like(carry)

        t = i * ts + lax.broadcasted_iota(jnp.int32, (ts, 1), 0)
        for j, w in enumerate(POOL_WINDOWS):
            u = ins[0][:, j * G:(j + 1) * G]
            ext = jnp.concatenate([carry[:, j * G:(j + 1) * G], u], axis=0)
            sw = _window_sums(ext, w, True)[POOL_HALO:, :]
            cnt = jnp.minimum(t + 1, w).astype(F32)
            outs[0][j] = (sw / cnt - u).astype(BF16)
        carry[...] = ins[0][ts - POOL_HALO:, :]

    return rowwise(name, fn, S, ts, [(z, rows(ts, E, 0))],
                   [((NG, S, G), BF16, pl.BlockSpec((NG, ts, G), lambda i: (0, i, 0)))],
                   scratch=[pltpu.VMEM((POOL_HALO, E), F32)])[0]


def pool_bwd(name, dpooled, dg, S, E, NG, ts):
    G = E // NG

    def fn(i, step, ins, outs, accs, scr):
        carry = scr[0]

        @pl.when(step == 0)
        def _():
            carry[...] = jnp.zeros_like(carry)

        t = i * ts + lax.broadcasted_iota(jnp.int32, (ts, 1), 0)
        for j, w in enumerate(POOL_WINDOWS):
            d = ins[0][j]
            e = d / jnp.minimum(t + 1, w).astype(F32)
            ext = jnp.concatenate([e, carry[:, j * G:(j + 1) * G]], axis=0)
            sw = _window_sums(ext, w, False)[:ts, :]
            outs[0][:, j * G:(j + 1) * G] = (sw - d).astype(BF16)
            carry[:, j * G:(j + 1) * G] = e[:POOL_HALO, :]
        outs[0][:, E:] = ins[1][...]

    return rowwise(name, fn, S, ts,
                   [(dpooled, pl.BlockSpec((NG, ts, G), lambda i: (0, i, 0))), (dg, rows(ts, E))],
                   [((S, 2 * E), BF16, rows(ts, 2 * E))],
                   scratch=[pltpu.VMEM((POOL_HALO, E), F32)], reverse=True)[0]


def pool_group_fwd(name, pooled, w_group, z, scale, S, E, NG):
    G = E // NG
    tm = _pick(S, (1024, 512, 256, 128))

    def body(a_ref, b_ref, g_ref, sc_ref, y_ref, mx_ref):
        mx = lax.dot_general(a_ref[...], b_ref[...], _NN, preferred_element_type=F32)
        g = g_ref[...]
        y_ref[...] = (mx * sc_ref[...] * (g * _sigmoid(g))).astype(BF16)
        mx_ref[...] = mx.astype(BF16)

    grp = pl.BlockSpec((None, tm, G), lambda j, i: (j, i, 0))
    res, _ = _call(
        body, name, [pooled, w_group, z, scale],
        [grp, pl.BlockSpec((None, G, G), lambda j, i: (j, 0, 0)), pl.BlockSpec((tm, G), lambda j, i: (i, NG + j)),
         pl.BlockSpec((1, G), lambda j, i: (0, j))],
        [jax.ShapeDtypeStruct((S, E), BF16), jax.ShapeDtypeStruct((NG, S, G), BF16)],
        [pl.BlockSpec((tm, G), lambda j, i: (i, j)), grp], (NG, S // tm), [], ("parallel", "parallel"))
    return res


def pool_gate_bwd(name, dy, mixed, z, scale, S, E, NG, ts):
    G = E // NG

    def fn(i, step, ins, outs, accs, scr):
        for j in range(NG):
            sl = slice(j * G, (j + 1) * G)
            d = ins[0][:, sl]
            mx = ins[1][j].astype(F32)
            g = ins[2][:, sl]
            sc = ins[3][:, sl]
            sg = _sigmoid(g)
            si = g * sg
            outs[0][j] = (d * sc * si).astype(BF16)
            outs[1][:, sl] = (d * mx * sc * (sg * (1.0 + g * (1.0 - sg)))).astype(BF16)
            accs[0][:, sl] += jnp.sum(d * mx * si, axis=0, keepdims=True)

    return rowwise(name, fn, S, ts,
                   [(dy, rows(ts, E)), (mixed, pl.BlockSpec((NG, ts, G), lambda i: (0, i, 0))),
                    (z, rows(ts, E, 1)), (scale, whole((1, E)))],
                   [((NG, S, G), BF16, pl.BlockSpec((NG, ts, G), lambda i: (0, i, 0))), ((S, E), BF16, rows(ts, E))],
                   accs=[((1, E), F32)])


def mla_prep_fwd(name, z, qg, kvg, tabs, S, E, R, ts):
    qb, kb, pb = E // R, E // R + 1, (E + 2 * R) // LANE

    def fn(i, step, ins, outs, accs, scr):
        zq, zkv = ins[0][...], ins[1][...]
        outs[0][...] = (zq * _rstd(zq) * ins[3][...]).astype(BF16)
        outs[1][...] = (zkv * _rstd(zkv) * ins[4][...]).astype(BF16)
        outs[2][...] = _rope(ins[2][...], ins[5][...], ins[6][...], ins[7][...], 1.0)

    return rowwise(name, fn, S, ts,
                   [(z, rows(ts, R, qb)), (z, rows(ts, R, kb)), (z, rows(ts, LANE, pb)),
                    (qg, whole((1, R))), (kvg, whole((1, R)))] + [(t, rows(ts, LANE)) for t in tabs],
                   [((S, R), BF16, rows(ts, R)), ((S, R), BF16, rows(ts, R)), ((S, LANE), F32, rows(ts, LANE))])


def mla_up_fwd(name, qn, kvn, w_uq, w_uk, w_uv, kper, tabs, S, H, R, scale):
    hc = _pick(H, (4, 2, 1))
    tm = _pick(S, (1024, 512, 256, 128))

    def body(qn_ref, kvn_ref, wq_ref, wk_ref, wv_ref, kper_ref, cos_ref, sa_ref, sb_ref, q_ref, k_ref, v_ref):
        cos_t, sin_a, sin_b = cos_ref[...], sa_ref[...], sb_ref[...]
        kvn_t = kvn_ref[...]
        q = lax.dot_general(qn_ref[...], wq_ref[...], _NN, preferred_element_type=F32)
        k = lax.dot_general(kvn_t, wk_ref[...], _NN, preferred_element_type=F32)
        v_ref[...] = lax.dot_general(kvn_t, wv_ref[...], _NN, preferred_element_type=F32).astype(BF16)
        kp = kper_ref[...].astype(BF16)
        for h in range(hc):
            a, b, c = h * HEAD_PAD, h * HEAD_PAD + NOPE_DIM, (h + 1) * HEAD_PAD
            q_ref[:, a:b] = (q[:, a:b] * scale).astype(BF16)
            q_ref[:, b:c] = (_rope(q[:, b:c], cos_t, sin_a, sin_b, 1.0) * scale).astype(BF16)
            k_ref[:, a:b] = k[:, a:b].astype(BF16)
            k_ref[:, b:c] = kp

    row = lambda w: pl.BlockSpec((tm, w), lambda i, j: (i, 0))
    col = lambda w: pl.BlockSpec((R, w), lambda i, j: (0, j))
    out = lambda w: pl.BlockSpec((tm, w), lambda i, j: (i, j))
    W = H * HEAD_PAD
    res, _ = _call(
        body, name, [qn, kvn, w_uq, w_uk, w_uv, kper, *tabs],
        [row(R), row(R), col(hc * HEAD_PAD), col(hc * HEAD_PAD), col(hc * V_DIM), row(LANE), row(LANE), row(LANE),
         row(LANE)],
        [jax.ShapeDtypeStruct((S, W), BF16), jax.ShapeDtypeStruct((S, W), BF16),
         jax.ShapeDtypeStruct((S, H * V_DIM), BF16)],
        [out(hc * HEAD_PAD), out(hc * HEAD_PAD), out(hc * V_DIM)], (S // tm, H // hc), [], ("parallel", "parallel"))
    return res


def mla_gate_fwd(name, o, z, S, E, ts):
    def fn(i, step, ins, outs, accs, scr):
        g = ins[1][...]
        outs[0][...] = (ins[0][...] * (g * _sigmoid(g))).astype(BF16)
    return rowwise(name, fn, S, ts, [(o, rows(ts, E)), (z, rows(ts, E, 0))], [((S, E), BF16, rows(ts, E))])[0]


def mla_gate_bwd(name, dy, o, z, S, E, H, ts):
    def fn(i, step, ins, outs, accs, scr):
        d, ov, g = ins[0][...], ins[1][...], ins[2][...]
        sg = _sigmoid(g)
        outs[0][...] = (d * (g * sg)).astype(BF16)
        outs[1][...] = (d * ov * (sg * (1.0 + g * (1.0 - sg)))).astype(BF16)

    return rowwise(name, fn, S, ts, [(dy, rows(ts, E)), (o, rows(ts, E)), (z, rows(ts, E, 0))],
                   [((S, E), BF16, rows(ts, E)), ((S, E), BF16, rows(ts, E))])


def mla_unpack_q_bwd(name, dqt, tabs, S, H, t, scale):
    W = H * HEAD_PAD

    def fn(i, step, ins, outs, accs, scr):
        cos_t, sin_a, sin_b = ins[1][...], ins[2][...], ins[3][...]
        for h in range(H):
            a, b, c = h * HEAD_PAD, h * HEAD_PAD + NOPE_DIM, (h + 1) * HEAD_PAD
            dq = ins[0][h].T
            outs[0][:, a:b] = (dq[:, :NOPE_DIM] * scale).astype(BF16)
            outs[0][:, b:c] = (_rope(dq[:, NOPE_DIM:], cos_t, sin_a, sin_b, -1.0) * scale).astype(BF16)

    return rowwise(name, fn, S, t,
                   [(dqt, pl.BlockSpec((H, None, HEAD_PAD, t), lambda i: (0, i, 0, 0)))]
                   + [(tb, rows(t, LANE)) for tb in tabs],
                   [((S, W), BF16, rows(t, W))])[0]


def mla_unpack_k_bwd(name, dk, S, H, ts):
    W = H * HEAD_PAD

    def fn(i, step, ins, outs, accs, scr):
        dkpe = jnp.zeros((ts, LANE), F32)
        for h in range(H):
            a, b, c = h * HEAD_PAD, h * HEAD_PAD + NOPE_DIM, (h + 1) * HEAD_PAD
            outs[0][:, a:b] = ins[0][:, a:b].astype(BF16)
            outs[0][:, b:c] = jnp.zeros((ts, LANE), BF16)
            dkpe = dkpe + ins[0][:, b:c]
        outs[1][...] = dkpe

    return rowwise(name, fn, S, ts, [(dk, rows(ts, W))],
                   [((S, W), BF16, rows(ts, W)), ((S, LANE), F32, rows(ts, LANE))])


def mla_prep_bwd(name, dqn, dkvn_k, dkvn_v, z, dkpe, dg, qg, kvg, tabs, S, E, R, ts):
    qb, kb = E // R, E // R + 1
    ZW = E + 2 * R + LANE

    def fn(i, step, ins, outs, accs, scr):
        dzq, dqg = _rms_bwd(ins[0][...], ins[3][...], ins[7][...])
        dzkv, dkvg = _rms_bwd(ins[1][...] + ins[2][...], ins[4][...], ins[8][...])
        outs[0][:, :E] = ins[6][...]
        outs[0][:, E:E + R] = dzq.astype(BF16)
        outs[0][:, E + R:E + 2 * R] = dzkv.astype(BF16)
        outs[0][:, E + 2 * R:] = _rope(ins[5][...], ins[9][...], ins[10][...], ins[11][...], -1.0).astype(BF16)
        accs[0][...] += dqg
        accs[1][...] += dkvg

    return rowwise(name, fn, S, ts,
                   [(dqn, rows(ts, R)), (dkvn_k, rows(ts, R)), (dkvn_v, rows(ts, R)), (z, rows(ts, R, qb)),
                    (z, rows(ts, R, kb)), (dkpe, rows(ts, LANE)), (dg, rows(ts, E)),
                    (qg, whole((1, R))), (kvg, whole((1, R)))] + [(t, rows(ts, LANE)) for t in tabs],
                   [((S, ZW), BF16, rows(ts, ZW))], accs=[((1, R), F32), ((1, R), F32)])


_NT = (((1,), (1,)), ((), ()))
_NN = (((1,), (0,)), ((), ()))
_TN = (((0,), (0,)), ((), ()))


def _causal_mask_t(t):
    return lax.broadcasted_iota(jnp.int32, (t, t), 0) <= lax.broadcasted_iota(jnp.int32, (t, t), 1)


def _tile(i, t):
    return pl.ds(pl.multiple_of(i * t, t), t)


def flash_fwd(name, q, k, v, S, H, t, rider=None):
    nt = S // t

    def body(q_ref, k_ref, v_ref, o_ref, lse_ref, m_sc, l_sc, acc_sc):
        i = pl.program_id(1)
        m_sc[...] = jnp.full_like(m_sc, NEG_INF)
        l_sc[...] = jnp.zeros_like(l_sc)
        acc_sc[...] = jnp.zeros_like(acc_sc)
        q = q_ref[...]

        def tile(j, diag):
            s = lax.dot_general(k_ref[_tile(j, t), :], q, _NT, preferred_element_type=F32)
            if diag:
                s = jnp.where(_causal_mask_t(t), s, NEG_INF)
            m_prev = m_sc[...]
            m_new = jnp.maximum(m_prev, jnp.max(s, axis=0, keepdims=True))
            alpha = jnp.exp(m_prev - m_new)
            p = jnp.exp(s - m_new)
            l_sc[...] = alpha * l_sc[...] + jnp.sum(p, axis=0, keepdims=True)
            acc_sc[...] = alpha * acc_sc[...] + lax.dot_general(v_ref[_tile(j, t), :], p.astype(BF16), _TN,
                                                                 preferred_element_type=F32)
            m_sc[...] = m_new

        def off_diagonal(j, carry):
            tile(j, False)
            return carry

        lax.fori_loop(0, i, off_diagonal, 0)
        tile(i, True)
        l = l_sc[...]
        o_ref[...] = (acc_sc[...] / l).T
        lse_ref[...] = m_sc[...] + jnp.log(l)

    (o, lse), carried = _call(
        body, name, [q, k, v],
        [pl.BlockSpec((t, HEAD_PAD), lambda h, i: (i, h)), pl.BlockSpec((S, HEAD_PAD), lambda h, i: (0, h)),
         pl.BlockSpec((S, V_DIM), lambda h, i: (0, h))],
        [jax.ShapeDtypeStruct((S, H * V_DIM), F32), jax.ShapeDtypeStruct((H, nt, 1, t), F32)],
        [pl.BlockSpec((t, V_DIM), lambda h, i: (i, h)), pl.BlockSpec((None, None, 1, t), lambda h, i: (h, i, 0, 0))],
        (H, nt), [pltpu.VMEM((1, t), F32), pltpu.VMEM((1, t), F32), pltpu.VMEM((V_DIM, t), F32)],
        ("parallel", "parallel"), rider)
    return o, lse, carried


def flash_bwd(name, q, k, v, do, o, lse, S, H, t, rider=None):
    nt = S // t

    def body(q_ref, k_ref, v_ref, do_ref, o_ref, lse_ref, dq_ref, dk_ref, dv_ref, kt_sc, dl_sc, dv_sc):
        j = pl.program_id(1)

        @pl.when(j == 0)
        def _():
            dq_ref[...] = jnp.zeros_like(dq_ref)
            ones = jnp.ones((8, V_DIM), BF16)
            for i in range(nt):
                x = do_ref[i * t:(i + 1) * t, :].astype(F32) * o_ref[i * t:(i + 1) * t, :]
                hi = x.astype(BF16)
                lo = (x - hi.astype(F32)).astype(BF16)
                dl_sc[i] = (lax.dot_general(ones, hi, _NT, preferred_element_type=F32)
                            + lax.dot_general(ones, lo, _NT, preferred_element_type=F32))

        kj, vj = k_ref[...], v_ref[...]
        kt_sc[...] = kj.astype(F32).T.astype(BF16)
        dk_ref[...] = jnp.zeros_like(dk_ref)
        dv_sc[...] = jnp.zeros_like(dv_sc)

        def tile(i, diag):
            qi, doi = q_ref[_tile(i, t), :], do_ref[_tile(i, t), :]
            s = lax.dot_general(kj, qi, _NT, preferred_element_type=F32)
            p = jnp.exp(s - lse_ref[i])
            if diag:
                p = jnp.where(_causal_mask_t(t), p, 0.0)
            dv_sc[...] += lax.dot_general(p.astype(BF16), doi, _NN, preferred_element_type=F32)
            dp = lax.dot_general(vj, doi, _NT, preferred_element_type=F32)
            ds = (p * (dp - dl_sc[i, 0:1, :])).astype(BF16)
            dk_ref[...] += lax.dot_general(ds, qi, _NN, preferred_element_type=F32)
            dq_ref[i] += lax.dot_general(kt_sc[...], ds, _NN, preferred_element_type=F32)

        def off_diagonal(i, carry):
            tile(i, False)
            return carry

        tile(j, True)
        lax.fori_loop(j + 1, nt, off_diagonal, 0)
        dv_ref[...] = dv_sc[...].astype(BF16)

    head = lambda w: pl.BlockSpec((S, w), lambda h, j: (0, h))
    ktile = lambda w: pl.BlockSpec((t, w), lambda h, j: (j, h))
    (dq, dk, dv), carried = _call(
        body, name, [q, k, v, do, o, lse],
        [head(HEAD_PAD), ktile(HEAD_PAD), ktile(V_DIM), head(V_DIM), head(V_DIM),
         pl.BlockSpec((None, nt, 1, t), lambda h, j: (h, 0, 0, 0))],
        [jax.ShapeDtypeStruct((H, nt, HEAD_PAD, t), F32), jax.ShapeDtypeStruct((S, H * HEAD_PAD), F32),
         jax.ShapeDtypeStruct((S, H * V_DIM), BF16)],
        [pl.BlockSpec((None, nt, HEAD_PAD, t), lambda h, j: (h, 0, 0, 0)), ktile(HEAD_PAD), ktile(V_DIM)],
        (H, nt), [pltpu.VMEM((HEAD_PAD, t), BF16), pltpu.VMEM((nt, 8, t), F32), pltpu.VMEM((t, V_DIM), F32)],
        ("parallel", "arbitrary"), rider)
    return dq, dk, dv, carried


def _peers():
    x, y, c = lax.axis_index("x"), lax.axis_index("y"), lax.axis_index("c")
    me = 4 * x + 2 * y + c
    peers = []
    for fx, fy, fc in ((0, 0, 1), (1, 0, 0), (0, 1, 0), (1, 1, 0), (1, 0, 1), (0, 1, 1), (1, 1, 1)):
        px, py, pc = x ^ fx, y ^ fy, c ^ fc
        peers.append(((px, py, pc), 4 * px + 2 * py + pc))
    return me, peers


def _hbm_specs(n):
    return [pl.BlockSpec(memory_space=pl.ANY)] * n


class Rider:
    def __init__(self, arrs, gather, windows=None):
        self.arrs, self.gather, self.n = list(arrs), gather, len(arrs)
        self.windows = list(windows) if windows is not None else [None] * self.n
        assert not (gather and any(w is not None for w in self.windows))
        self.out_shapes = [
            jax.ShapeDtypeStruct((N_DEV,) + a.shape if gather else
                                 a.shape if w is None else (N_DEV, w[1]) + a.shape[2:], a.dtype)
            for a, w in zip(arrs, self.windows)]
        self.sems = [pltpu.SemaphoreType.DMA((self.n, N_DEV - 1)), pltpu.SemaphoreType.DMA((self.n, N_DEV - 1)),
                     pltpu.SemaphoreType.DMA((self.n,))]

    def _copies(self, srcs, dsts, sems):
        send_sems, recv_sems, local_sems = sems
        x, y, c = lax.axis_index("x"), lax.axis_index("y"), lax.axis_index("c")
        ident = lambda d: 4 * d[0] + 2 * d[1] + d[2]
        me, sibling = (x, y, c), (x, y, 1 - c)
        chips = [(1 - x, y), (x, 1 - y), (1 - x, 1 - y)]
        _, peers = _peers()

        def slot(a, pid):
            w = self.windows[a]
            return srcs[a].at[pid] if w is None else srcs[a].at[pid, pl.ds(w[0], w[1])]

        def remote(a, k, incoming):
            if not self.gather:
                target, pid = peers[k]
                src, block = slot(a, pid), (pid if incoming else ident(me))
            elif k == 0:
                target, src, block = sibling, srcs[a], ident(sibling if incoming else me)
            elif k <= 3:
                target = (*chips[k - 1], c)
                src, block = srcs[a], ident(target if incoming else me)
            else:
                landed = ident((*chips[k - 4], c))
                target, src = sibling, dsts[a].at[landed]
                block = ident((*chips[k - 4], 1 - c)) if incoming else landed
            return pltpu.make_async_remote_copy(
                src_ref=src, dst_ref=dsts[a].at[block], send_sem=send_sems.at[a, k], recv_sem=recv_sems.at[a, k],
                device_id=target, device_id_type=MESH)

        def local(a):
            return pltpu.make_async_copy(srcs[a] if self.gather else slot(a, ident(me)), dsts[a].at[ident(me)],
                                         local_sems.at[a])

        return local, remote

    def start(self, srcs, dsts, sems):
        local, remote = self._copies(srcs, dsts, sems)
        for a in range(self.n):
            local(a).start()
            for k in range(4 if self.gather else N_DEV - 1):
                remote(a, k, False).start()

    def relay(self, srcs, dsts, sems):
        if not self.gather:
            return
        local, remote = self._copies(srcs, dsts, sems)
        for a in range(self.n):
            for k in range(1, 4):
                remote(a, k, True).wait_recv()
                remote(a, k + 3, False).start()

    def wait(self, srcs, dsts, sems):
        local, remote = self._copies(srcs, dsts, sems)
        for a in range(self.n):
            for k in range(N_DEV - 1):
                if not (self.gather and 1 <= k <= 3):
                    remote(a, k, True).wait_recv()
        for a in range(self.n):
            for k in range(N_DEV - 1):
                remote(a, k, False).wait_send()
            local(a).wait()


def _carry(body, n_in, n_out, rider, grid):
    n = rider.n
    steps = math.prod(grid)

    def wrapped(*refs):
        ins, r_in = refs[:n_in], refs[n_in:n_in + n]
        outs = refs[n_in + n:n_in + n + n_out]
        r_out = refs[n_in + n + n_out:n_in + 2 * n + n_out]
        scratch, sems = refs[n_in + 2 * n + n_out:-3], refs[-3:]
        step = 0
        for d, g in enumerate(grid):
            step = step * g + pl.program_id(d)

        @pl.when(step == 0)
        def _():
            rider.start(r_in, r_out, sems)

        if rider.gather:
            @pl.when(step == (3 * steps) // 4)
            def _():
                rider.relay(r_in, r_out, sems)

        body(*ins, *outs, *scratch)

        @pl.when(step == steps - 1)
        def _():
            rider.wait(r_in, r_out, sems)

    return wrapped


def _call(body, name, ins, in_specs, out_shape, out_specs, grid, scratch, sem, rider=None):
    if rider is None:
        return _pcall(body, name=name, out_shape=list(out_shape), grid=grid, in_specs=list(in_specs),
                      out_specs=list(out_specs), scratch_shapes=list(scratch), compiler_params=_cparams(*sem))(*ins), None
    res = _pcall(
        _carry(body, len(ins), len(out_shape), rider, grid), name=name,
        out_shape=list(out_shape) + rider.out_shapes, grid=grid,
        in_specs=list(in_specs) + _hbm_specs(rider.n), out_specs=list(out_specs) + _hbm_specs(rider.n),
        scratch_shapes=list(scratch) + rider.sems, compiler_params=_cparams(*(("arbitrary",) * len(grid))),
    )(*ins, *rider.arrs)
    return res[:len(out_shape)], res[len(out_shape):]


def exchange(name, rider):
    def body(*refs):
        srcs, dsts, sems = refs[:rider.n], refs[rider.n:2 * rider.n], refs[2 * rider.n:]
        rider.start(srcs, dsts, sems)
        rider.relay(srcs, dsts, sems)
        rider.wait(srcs, dsts, sems)

    return _pcall(body, name=name, out_shape=rider.out_shapes, in_specs=_hbm_specs(rider.n),
                  out_specs=_hbm_specs(rider.n), scratch_shapes=rider.sems)(*rider.arrs)


def adamw(name, gslots, w, m, v, layer=0, prev=None):
    K, R, C = gslots.shape
    per_row = C * (K * gslots.dtype.itemsize + 7 * 4) * 2
    tr = R
    for cand in (1024, 512, 256, 128, 64, 32, 16, 8):
        if R % cand == 0:
            tr = cand
            if cand * per_row <= VMEM_LIMIT_BYTES // 2:
                break
    c1 = 1.0 / (1.0 - ADAM_B1 ** ADAM_STEP)
    c2 = 1.0 / (1.0 - ADAM_B2 ** ADAM_STEP)

    def body(g_ref, w_ref, m_ref, v_ref, *rest):
        go_ref, d_ref, mo_ref, vo_ref = rest[-4:]
        g = g_ref[0].astype(F32)
        for s in range(1, K):
            g = g + g_ref[s].astype(F32)
        mn = ADAM_B1 * m_ref[...] + (1.0 - ADAM_B1) * g
        vn = ADAM_B2 * v_ref[...] + (1.0 - ADAM_B2) * (g * g)
        go_ref[...] = g
        mo_ref[...] = mn
        vo_ref[...] = vn
        d_ref[...] = -ADAM_LR * ((mn * c1) / (jnp.sqrt(vn * c2) + ADAM_EPS) + ADAM_WD * w_ref[...])

    blk = pl.BlockSpec((None, tr, C), lambda i: (layer, i, 0))
    prev = [] if prev is None else list(prev)
    return _pcall(
        body, name=name, out_shape=[jax.ShapeDtypeStruct(w.shape, F32)] * 4, grid=(R // tr,),
        in_specs=[pl.BlockSpec((K, tr, C), lambda i: (0, i, 0)), blk, blk, blk] + _hbm_specs(len(prev)),
        out_specs=[blk] * 4, input_output_aliases={4 + q: q for q in range(len(prev))},
        compiler_params=_cparams("parallel"),
    )(gslots, w, m, v, *prev)


def _from_slots(gathered, ax):
    g = jnp.moveaxis(gathered, 0, ax)
    s = g.shape
    return g.reshape(s[:ax] + (s[ax] * s[ax + 1],) + s[ax + 2:])


def _to_slots(full, ax):
    s = full.shape
    g = full.reshape(s[:ax] + (N_DEV, s[ax] // N_DEV) + s[ax + 1:])
    g = jnp.moveaxis(g, ax, 0)
    return g.reshape(N_DEV, -1, g.shape[-1])


def _rope_tables(pos, S):
    inv_freq = ROPE_THETA ** (-jnp.arange(0, ROPE_DIM, 2, dtype=F32) / ROPE_DIM)
    ang = pos.astype(F32)[:, None] * inv_freq
    cos, sin = jnp.cos(ang), jnp.sin(ang)
    z = jnp.zeros((S, ROPE_DIM // 2), F32)
    cos_t = jnp.concatenate([cos, cos, z, z], axis=1)
    sin_a = jnp.concatenate([-sin, z, z, z], axis=1)
    sin_b = jnp.concatenate([z, sin, z, z], axis=1)
    return cos_t, sin_a, sin_b


def kernel(x, p, positions, pre_norm, post_norm, pool_w_in, pool_w_group, pool_scale, pool_w_out, mla_w_in, mla_q_norm, mla_w_uq, mla_kv_norm, mla_w_ukv, mla_w_out, ple_norm, ple_w_gate, ple_w_proj, loss_target, m_pre_norm, m_post_norm, m_pool_w_in, m_pool_w_group, m_pool_scale, m_pool_w_out, m_mla_w_in, m_mla_q_norm, m_mla_w_uq, m_mla_kv_norm, m_mla_w_ukv, m_mla_w_out, m_ple_norm, m_ple_w_gate, m_ple_w_proj, v_pre_norm, v_post_norm, v_pool_w_in, v_pool_w_group, v_pool_scale, v_pool_w_out, v_mla_w_in, v_mla_q_norm, v_mla_w_uq, v_mla_kv_norm, v_mla_w_ukv, v_mla_w_out, v_ple_norm, v_ple_w_gate, v_ple_w_proj):
    wl = dict(pre_norm=pre_norm, post_norm=post_norm, pool_w_in=pool_w_in, pool_w_group=pool_w_group,
              pool_scale=pool_scale, pool_w_out=pool_w_out, mla_w_in=mla_w_in, mla_q_norm=mla_q_norm,
              mla_w_uq=mla_w_uq, mla_kv_norm=mla_kv_norm, mla_w_ukv=mla_w_ukv, mla_w_out=mla_w_out,
              ple_norm=ple_norm, ple_w_gate=ple_w_gate, ple_w_proj=ple_w_proj)
    ml = dict(pre_norm=m_pre_norm, post_norm=m_post_norm, pool_w_in=m_pool_w_in, pool_w_group=m_pool_w_group,
              pool_scale=m_pool_scale, pool_w_out=m_pool_w_out, mla_w_in=m_mla_w_in, mla_q_norm=m_mla_q_norm,
              mla_w_uq=m_mla_w_uq, mla_kv_norm=m_mla_kv_norm, mla_w_ukv=m_mla_w_ukv, mla_w_out=m_mla_w_out,
              ple_norm=m_ple_norm, ple_w_gate=m_ple_w_gate, ple_w_proj=m_ple_w_proj)
    vl = dict(pre_norm=v_pre_norm, post_norm=v_post_norm, pool_w_in=v_pool_w_in, pool_w_group=v_pool_w_group,
              pool_scale=v_pool_scale, pool_w_out=v_pool_w_out, mla_w_in=v_mla_w_in, mla_q_norm=v_mla_q_norm,
              mla_w_uq=v_mla_w_uq, mla_kv_norm=v_mla_kv_norm, mla_w_ukv=v_mla_w_ukv, mla_w_out=v_mla_w_out,
              ple_norm=v_ple_norm, ple_w_gate=v_ple_w_gate, ple_w_proj=v_ple_w_proj)

    S, D = x.shape[1], x.shape[2]
    L = pre_norm.shape[0]
    E = pool_scale.shape[1]
    NG = pool_w_group.shape[1]
    R = mla_w_uq.shape[1]
    H = D // 128
    EM = H * V_DIM
    PD = p.shape[-1]
    me = 4 * lax.axis_index("x") + 2 * lax.axis_index("y") + lax.axis_index("c")
    ts = min(S, 256)
    tsw = min(S, 128)
    ta = min(S, 512)
    sm_scale = (NOPE_DIM + ROPE_DIM) ** -0.5

    wb = {n: wl[n].astype(BF16) for n in BIG}
    full = {}

    def ag_rider(host):
        return Rider([wb[n][l] for n, l in AG_PLAN[host]], True) if host in AG_PLAN else None

    def ag_done(host, results):
        for (n, l), g in zip(AG_PLAN[host], results):
            full[n, l] = g if n in SLOT_NATIVE else _from_slots(g, SHARD_AXIS[n] - 1)

    small_sh = jnp.concatenate([wl[n].reshape(1, -1) for n in SMALL_SHARD], axis=1)
    g_in0, g_small = exchange("gather_first", Rider([wb['pool_w_in'][0], small_sh], True))
    full['pool_w_in', 0] = g_in0
    nq = mla_q_norm.size
    q_norm = _from_slots(g_small[:, 0, :nq].reshape((N_DEV,) + mla_q_norm.shape), 1)
    kv_norm = _from_slots(g_small[:, 0, nq:].reshape((N_DEV,) + mla_kv_norm.shape), 1)

    def mla_kernel_weights(j):
        w_in = full['mla_w_in', j]
        w_in_k = jnp.concatenate([w_in[:, 2 * R + ROPE_DIM:], w_in[:, :2 * R + ROPE_DIM],
                                  jnp.zeros((D, LANE - ROPE_DIM), BF16)], axis=1)
        w_uq_k = jnp.pad(full['mla_w_uq', j].reshape(R, H, NOPE_DIM + ROPE_DIM),
                         ((0, 0), (0, 0), (0, HEAD_PAD - NOPE_DIM - ROPE_DIM))).reshape(R, H * HEAD_PAD)
        w_ukv = full['mla_w_ukv', j].reshape(R, H, NOPE_DIM + V_DIM)
        w_uk_k = jnp.pad(w_ukv[..., :NOPE_DIM], ((0, 0), (0, 0), (0, HEAD_PAD - NOPE_DIM))).reshape(R, H * HEAD_PAD)
        w_uv_k = w_ukv[..., NOPE_DIM:].reshape(R, H * V_DIM)
        return w_in_k, w_uq_k, w_uk_k, w_uv_k

    def fmm(name, a, b, mode, out_dtype=F32, slots=False):
        if name not in AG_PLAN:
            return mm(name, a, b, mode, out_dtype, slots=slots)
        out, carried = mm(name, a, b, mode, out_dtype, ag_rider(name), slots)
        ag_done(name, carried)
        return out

    mla_w = {}
    tabs = _rope_tables(positions[0], S)

    h = x[0]
    saved = []
    for i in range(L):
        j = i // 2
        sv = dict(h=h)
        if i == 0:
            xn = rms_fwd(f"pre_norm_{i}", h, pre_norm[i:i + 1], S, D, ts)
        sv['xn'] = xn
        if i % 2 == 0:
            z = fmm(f"pool_in_{i}", xn, full['pool_w_in', j], 'nn', slots=True)
            pooled = pool_fwd(f"pool_window_{i}", z, S, E, NG, tsw)
            y, mixed = pool_group_fwd(f"pool_group_{i}", pooled, full['pool_w_group', j], z, pool_scale[j:j + 1],
                                      S, E, NG)
            out = fmm(f"pool_out_{i}", y, full['pool_w_out', j], 'nn')
            sv.update(z=z, pooled=pooled, mixed=mixed, y=y)
        else:
            w_in_k, w_uq_k, w_uk_k, w_uv_k = mla_w[j] = mla_kernel_weights(j)
            z = fmm(f"mla_in_{i}", xn, w_in_k, 'nn')
            qn, kvn, kper = mla_prep_fwd(f"mla_prep_{i}", z, q_norm[j:j + 1], kv_norm[j:j + 1], tabs, S, EM, R, ts)
            qp, kp, vv = mla_up_fwd(f"mla_up_{i}", qn, kvn, w_uq_k, w_uk_k, w_uv_k, kper, tabs, S, H, R, sm_scale)
            o, lse, carried = flash_fwd(f"attn_{i}", qp, kp, vv, S, H, ta, ag_rider(f"attn_{i}"))
            if carried is not None:
                ag_done(f"attn_{i}", carried)
            y = mla_gate_fwd(f"mla_gate_{i}", o, z, S, EM, ts)
            out = fmm(f"mla_out_{i}", y, full['mla_w_out', j], 'nn')
            sv.update(z=z, qn=qn, kvn=kvn, qp=qp, kp=kp, vv=vv, o=o, lse=lse, y=y)
        h1, a = post_fwd(f"post_norm_{i}", h, out, post_norm[i:i + 1], ple_norm[i:i + 1], S, D, ts)
        if i < L - 1:
            gl, pp, h, xn = ple_fwd(f"ple_{i}", a, full['ple_w_gate', i], p[i, 0], full['ple_w_proj', i], h1,
                                    next_gain=pre_norm[i + 1:i + 2])
        else:
            gl, pp, dh, loss_acc = ple_fwd(f"ple_{i}", a, full['ple_w_gate', i], p[i, 0], full['ple_w_proj', i], h1,
                                           target=loss_target[0])
        sv.update(out=out, h1=h1, a=a, gl=gl, pp=pp)
        saved.append(sv)

    loss = lax.psum(loss_acc[0, 0] * (0.5 / D), ("x", "y", "c"))

    gw = {n: [None] * wl[n].shape[0] for n in WEIGHTS}
    recv = {}

    def rs_rider(host):
        if host not in RS_PLAN:
            return None
        return scatter_rider(RS_PLAN[host])

    def scatter_rider(keys):
        arrs = [gw[n][l] if n in SLOT_NATIVE else _to_slots(gw[n][l], SHARD_AXIS[n] - 1).astype(BF16)
                for n, l, *_ in keys]
        halves = [key[2] if len(key) == 3 else None for key in keys]
        return Rider(arrs, False, [None if h is None else (h * (a.shape[1] // 2), a.shape[1] // 2)
                                   for a, h in zip(arrs, halves)])

    def rs_done(host, results):
        for key, r in zip(RS_PLAN[host], results):
            recv[key] = r

    def bmm(name, a, b, mode, slots=False):
        out_dtype = BF16 if mode == 'tn' else F32
        if name not in RS_PLAN:
            return mm(name, a, b, mode, out_dtype, slots=slots)
        out, carried = mm(name, a, b, mode, out_dtype, rs_rider(name), slots)
        rs_done(name, carried)
        return out

    for i in reversed(range(L)):
        j = i // 2
        sv = saved[i]
        dpp, dgl = ple_bwd(f"ple_bwd_{i}", dh, sv['pp'], sv['gl'], S, D, ts)
        gw['ple_w_proj'][i] = bmm(f"ple_proj_dw_{i}", p[i, 0], dpp, 'tn')
        gw['ple_w_gate'][i] = bmm(f"ple_gate_dw_{i}", sv['a'], dgl, 'tn')
        dh1, dout, dpost, dple = post_bwd(f"post_norm_bwd_{i}", dgl, full['ple_w_gate', i], dh, sv['h1'], sv['out'],
                                          post_norm[i:i + 1], ple_norm[i:i + 1], S, D, ts)
        gw['post_norm'][i], gw['ple_norm'][i] = dpost[0], dple[0]
        xn = sv['xn']
        if i % 2 == 0:
            gw['pool_w_out'][j] = bmm(f"pool_out_dw_{i}", sv['y'], dout, 'tn')
            dy = bmm(f"pool_out_dx_{i}", dout, full['pool_w_out', j], 'nt')
            dmixed, dg, dscale = pool_gate_bwd(f"pool_gate_bwd_{i}", dy, sv['mixed'], sv['z'], pool_scale[j:j + 1],
                                               S, E, NG, tsw)
            gw['pool_scale'][j] = dscale[0]
            gw['pool_w_group'][j] = bmm(f"pool_group_dw_{i}", sv['pooled'], dmixed, 'tn')
            dpooled = bmm(f"pool_group_dx_{i}", dmixed, full['pool_w_group', j], 'nt')
            dz = pool_bwd(f"pool_window_bwd_{i}", dpooled, dg, S, E, NG, tsw)
            gw['pool_w_in'][j] = bmm(f"pool_in_dw_{i}", xn, dz, 'tn', slots=True)
            dxn = bmm(f"pool_in_dx_{i}", dz, full['pool_w_in', j], 'nt', slots=True)
        else:
            w_in_k, w_uq_k, w_uk_k, w_uv_k = mla_w[j]
            gw['mla_w_out'][j] = bmm(f"mla_out_dw_{i}", sv['y'], dout, 'tn')
            dy = bmm(f"mla_out_dx_{i}", dout, full['mla_w_out', j], 'nt')
            do, dg = mla_gate_bwd(f"mla_gate_bwd_{i}", dy, sv['o'], sv['z'], S, EM, H, ts)
            dqt, dkp, dvv, carried = flash_bwd(f"attn_bwd_{i}", sv['qp'], sv['kp'], sv['vv'], do, sv['o'], sv['lse'],
                                               S, H, ta, rs_rider(f"attn_bwd_{i}"))
            if carried is not None:
                rs_done(f"attn_bwd_{i}", carried)
            dq_raw = mla_unpack_q_bwd(f"mla_pack_q_bwd_{i}", dqt, tabs, S, H, ta, sm_scale)
            dk_raw, dkpe = mla_unpack_k_bwd(f"mla_pack_k_bwd_{i}", dkp, S, H, tsw)
            g_uq = bmm(f"mla_uq_dw_{i}", sv['qn'], dq_raw, 'tn')
            g_uk = bmm(f"mla_uk_dw_{i}", sv['kvn'], dk_raw, 'tn')
            g_uv = bmm(f"mla_uv_dw_{i}", sv['kvn'], dvv, 'tn')
            gw['mla_w_uq'][j] = g_uq.reshape(R, H, HEAD_PAD)[:, :, :NOPE_DIM + ROPE_DIM].reshape(R, -1)
            gw['mla_w_ukv'][j] = jnp.concatenate(
                [g_uk.reshape(R, H, HEAD_PAD)[:, :, :NOPE_DIM], g_uv.reshape(R, H, V_DIM)], axis=2).reshape(R, -1)
            dqn = bmm(f"mla_uq_dx_{i}", dq_raw, w_uq_k, 'nt')
            dkvn_k = bmm(f"mla_uk_dx_{i}", dk_raw, w_uk_k, 'nt')
            dkvn_v = bmm(f"mla_uv_dx_{i}", dvv, w_uv_k, 'nt')
            dz, dqg, dkvg = mla_prep_bwd(f"mla_prep_bwd_{i}", dqn, dkvn_k, dkvn_v, sv['z'], dkpe, dg,
                                         q_norm[j:j + 1], kv_norm[j:j + 1], tabs, S, EM, R, ts)
            g_in = bmm(f"mla_in_dw_{i}", xn, dz, 'tn')
            dxn = bmm(f"mla_in_dx_{i}", dz, w_in_k, 'nt')
            gw['mla_q_norm'][j], gw['mla_kv_norm'][j] = dqg[0], dkvg[0]
            gw['mla_w_in'][j] = jnp.concatenate([g_in[:, EM:EM + 2 * R + ROPE_DIM], g_in[:, :EM]], axis=1)
        dh, dpre = pre_bwd(f"pre_norm_bwd_{i}", dxn, dh1, sv['h'], pre_norm[i:i + 1], S, D, ts)
        gw['pre_norm'][i] = dpre[0]
    grad_x = dh[None]

    last = exchange("scatter_last", scatter_rider(RS_LAST))
    for key, r in zip(RS_LAST, last):
        recv[key] = r
    for n, l in {key[:2] for key in recv if len(key) == 3}:
        recv[n, l] = jnp.concatenate([recv[n, l, 0], recv[n, l, 1]], axis=1)
    small_names = SMALL_REPL + SMALL_SHARD
    gw = {n: jnp.stack(gw[n]) for n in small_names}
    small_g = jnp.concatenate([gw[n].reshape(1, -1) for n in small_names], axis=1)
    small_all = exchange("gather_small_grads", Rider([small_g], True))[0]

    outs = {}
    for n in BIG:
        shp = wl[n].shape
        three = lambda a: a.reshape(shp[0], -1, shp[-1])
        res = None
        for l in range(shp[0]):
            res = adamw(f"adamw_{n}_{l}", recv[n, l], three(wl[n]), three(ml[n]), three(vl[n]), l, res)
        outs[n] = [a.reshape(shp) for a in res]

    pieces, off = [], 0
    for n in small_names:
        sz = gw[n].size
        g = small_all[:, :, off:off + sz]
        off += sz
        if n in SMALL_SHARD:
            rows_, cols_ = gw[n].shape
            g = lax.dynamic_slice_in_dim(g.reshape(N_DEV, rows_, cols_), me * (cols_ // N_DEV), cols_ // N_DEV, axis=2)
            g = g.reshape(N_DEV, 1, -1)
        pieces.append(g)
    gs = jnp.concatenate(pieces, axis=2)
    flat = lambda d: jnp.concatenate([d[n].reshape(1, -1) for n in small_names], axis=1)
    res = adamw("adamw_small", gs, flat(wl)[None], flat(ml)[None], flat(vl)[None])
    off = 0
    for n in small_names:
        sz = wl[n].size
        outs[n] = [a[0, :, off:off + sz].reshape(wl[n].shape) for a in res]
        off += sz

    return (loss, grad_x, *[outs[n][0] for n in WEIGHTS], *[outs[n][1] for n in WEIGHTS],
            *[outs[n][2] for n in WEIGHTS], *[outs[n][3] for n in WEIGHTS])
```

```python
import math

import jax
import jax.numpy as jnp
from jax import lax
from jax.experimental import pallas as pl
from jax.experimental.pallas import tpu as pltpu

F32 = jnp.float32
BF16 = jnp.bfloat16

N_DEV = 8
EPS = 1e-6
ROPE_THETA = 10000.0
NOPE_DIM = 128
ROPE_DIM = 64
V_DIM = 128
HEAD_PAD = 256
LANE = 128
POOL_WINDOWS = (2, 4, 8, 16)
POOL_HALO = 16
NEG_INF = -1e30
ADAM_LR = 0.001
ADAM_B1 = 0.9
ADAM_B2 = 0.999
ADAM_EPS = 1e-08
ADAM_WD = 0.01
ADAM_STEP = 10
VMEM_LIMIT_BYTES = 56 * 1024 * 1024
MM_MAX_TK = 3200
MESH = pl.DeviceIdType.MESH

SHARD_AXIS = dict(pre_norm=None, post_norm=None, pool_w_in=2, pool_w_group=2, pool_scale=None, pool_w_out=1,
                  mla_w_in=2, mla_q_norm=1, mla_w_uq=2, mla_kv_norm=1, mla_w_ukv=2, mla_w_out=1,
                  ple_norm=None, ple_w_gate=1, ple_w_proj=2)
WEIGHTS = tuple(SHARD_AXIS)
BIG = ('pool_w_in', 'pool_w_group', 'pool_w_out', 'mla_w_in', 'mla_w_uq', 'mla_w_ukv', 'mla_w_out',
       'ple_w_gate', 'ple_w_proj')
SMALL_REPL = ('pre_norm', 'post_norm', 'pool_scale', 'ple_norm')
SMALL_SHARD = ('mla_q_norm', 'mla_kv_norm')
SLOT_NATIVE = ('pool_w_in',)

AG_PLAN = {
    "pool_in_0": [("pool_w_group", 0), ("pool_w_out", 0), ("ple_w_gate", 0), ("ple_w_proj", 0)],
    "pool_out_0": [("mla_w_in", 0), ("mla_w_uq", 0), ("mla_w_ukv", 0)],
    "attn_1": [("mla_w_out", 0), ("ple_w_gate", 1), ("ple_w_proj", 1), ("pool_w_in", 1), ("pool_w_group", 1),
               ("mla_w_in", 1)],
    "pool_in_2": [("pool_w_out", 1), ("ple_w_gate", 2), ("ple_w_proj", 2), ("mla_w_uq", 1), ("mla_w_ukv", 1)],
    "attn_3": [("mla_w_out", 1), ("ple_w_gate", 3), ("ple_w_proj", 3)],
}
RS_PLAN = {
    "attn_bwd_3": [("ple_w_gate", 3), ("ple_w_proj", 3), ("mla_w_out", 1)],
    "mla_in_dw_3": [("mla_w_uq", 1), ("mla_w_ukv", 1)],
    "pool_out_dx_2": [("mla_w_in", 1)],
    "pool_in_dw_2": [("ple_w_gate", 2), ("ple_w_proj", 2)],
    "pool_in_dx_2": [("pool_w_out", 1)],
    "attn_bwd_1": [("pool_w_group", 1), ("pool_w_in", 1), ("ple_w_gate", 1), ("ple_w_proj", 1), ("mla_w_out", 0)],
    "mla_in_dw_1": [("mla_w_uq", 0), ("mla_w_ukv", 0)],
    "pool_out_dw_0": [("mla_w_in", 0)],
    "pool_out_dx_0": [("ple_w_gate", 0), ("ple_w_proj", 0)],
    "pool_group_dx_0": [("pool_w_group", 0)],
    "pool_in_dw_0": [("pool_w_out", 0)],
    "pool_in_dx_0": [("pool_w_in", 0, 0)],
}
RS_LAST = [("pool_w_in", 0, 1)]


def _pcall(body, **kw):
    return pl.pallas_call(body, **kw)


def _cparams(*sem):
    return pltpu.CompilerParams(dimension_semantics=sem, vmem_limit_bytes=VMEM_LIMIT_BYTES)


def _pick(n, cands):
    for c in cands:
        if n % c == 0:
            return c
    return n


def _sigmoid(x):
    return 1.0 / (1.0 + jnp.exp(-x))


def mm(name, a, b, mode, out_dtype=F32, rider=None, slots=False):
    squeeze = a.ndim == 2
    if squeeze:
        a = a[None]
        b = b if slots and mode != 'tn' else b[None]
    G = a.shape[0]
    if mode == 'nn':
        M, K = a.shape[1:]
        N = b.shape[2] * (N_DEV if slots else 1)
    elif mode == 'tn':
        K, M = a.shape[1:]
        N = b.shape[2]
    else:
        M, K = a.shape[1:]
        N = b.shape[1]
    n = (K if mode == 'nt' else N) // N_DEV
    tm = _pick(M, (1024, 512, 256, 128))
    tn = _pick(n if slots and mode != 'nt' else N, (1024, 768, 640, 512, 384, 256, 128))
    if slots and mode == 'nt':
        tk = _pick(n, (2048, 1024, 512, 256, 128))
    else:
        tk = K if K <= MM_MAX_TK else _pick(K, (2048, 1024, 640, 512, 384, 256, 128))
    nk = K // tk
    o_spec = pl.BlockSpec((None, tm, tn), lambda g, i, j, k: (g, i, j))
    o_shape = (G, M, N)
    if mode == 'nn':
        a_spec = pl.BlockSpec((None, tm, tk), lambda g, i, j, k: (g, i, k))
        b_spec = pl.BlockSpec((None, tk, tn), lambda g, i, j, k: (g, k, j))
        if slots:
            b_spec = pl.BlockSpec((None, tk, tn), lambda g, i, j, k: (j // (n // tn), k, j % (n // tn)))
        dims = (((1,), (0,)), ((), ()))
    elif mode == 'tn':
        a_spec = pl.BlockSpec((None, tk, tm), lambda g, i, j, k: (g, k, i))
        b_spec = pl.BlockSpec((None, tk, tn), lambda g, i, j, k: (g, k, j))
        if slots:
            o_spec = pl.BlockSpec((None, tm, tn), lambda g, i, j, k: (j // (n // tn), i, j % (n // tn)))
            o_shape = (N_DEV, M, n)
        dims = (((0,), (0,)), ((), ()))
    else:
        a_spec = pl.BlockSpec((None, tm, tk), lambda g, i, j, k: (g, i, k))
        b_spec = pl.BlockSpec((None, tn, tk), lambda g, i, j, k: (g, j, k))
        if slots:
            b_spec = pl.BlockSpec((None, tn, tk), lambda g, i, j, k: (k // (n // tk), j, k % (n // tk)))
        dims = (((1,), (1,)), ((), ()))

    def product(a_ref, b_ref):
        return lax.dot_general(a_ref[...].astype(BF16), b_ref[...].astype(BF16), dims, preferred_element_type=F32)

    def body_one(a_ref, b_ref, o_ref):
        o_ref[...] = product(a_ref, b_ref).astype(out_dtype)

    def body_acc(a_ref, b_ref, o_ref, acc_ref):
        k = pl.program_id(3)

        @pl.when(k == 0)
        def _():
            acc_ref[...] = product(a_ref, b_ref)

        @pl.when(jnp.logical_and(k > 0, k < nk - 1))
        def _():
            acc_ref[...] += product(a_ref, b_ref)

        @pl.when(k == nk - 1)
        def _():
            o_ref[...] = (acc_ref[...] + product(a_ref, b_ref)).astype(out_dtype)

    (out,), carried = _call(
        body_one if nk == 1 else body_acc, name, [a, b], [a_spec, b_spec],
        [jax.ShapeDtypeStruct(o_shape, out_dtype)], [o_spec], (G, M // tm, N // tn, nk),
        [] if nk == 1 else [pltpu.VMEM((tm, tn), F32)], ("parallel", "parallel", "parallel", "arbitrary"), rider)
    out = out[0] if squeeze and not (slots and mode == 'tn') else out
    return out if rider is None else (out, carried)


def rows(ts, width, colblk=0):
    return pl.BlockSpec((ts, width), lambda i: (i, colblk))


def whole(shape):
    return pl.BlockSpec(shape, lambda i: (0,) * len(shape))


def rowwise(name, fn, S, ts, ins, outs, accs=(), scratch=(), reverse=False):
    n_in, n_out, n_acc = len(ins), len(outs), len(accs)
    nt = S // ts

    def body(*refs):
        step = pl.program_id(0)
        i = nt - 1 - step if reverse else step
        in_refs = refs[:n_in]
        out_refs = refs[n_in:n_in + n_out]
        acc_refs = refs[n_in + n_out:n_in + n_out + n_acc]
        scr = refs[n_in + n_out + n_acc:]

        @pl.when(step == 0)
        def _():
            for r in acc_refs:
                r[...] = jnp.zeros_like(r)

        fn(i, step, in_refs, out_refs, acc_refs, scr)

    def fix(spec):
        if not reverse:
            return spec
        imap = spec.index_map
        return pl.BlockSpec(spec.block_shape, lambda s: imap(nt - 1 - s))

    res = _pcall(
        body, name=name,
        out_shape=[jax.ShapeDtypeStruct(s, d) for s, d, _ in outs] + [jax.ShapeDtypeStruct(s, d) for s, d in accs],
        grid=(nt,),
        in_specs=[fix(sp) for _, sp in ins],
        out_specs=[fix(sp) for _, _, sp in outs] + [whole(s) for s, _ in accs],
        scratch_shapes=list(scratch),
        compiler_params=_cparams("arbitrary"),
    )(*[a for a, _ in ins])
    return res


def _rstd(x):
    return lax.rsqrt(jnp.mean(x * x, axis=-1, keepdims=True) + EPS)


def _rms_bwd(dy, x, g):
    r = _rstd(x)
    xh = x * r
    gdy = dy * g
    dx = r * (gdy - xh * jnp.mean(xh * gdy, axis=-1, keepdims=True))
    return dx, jnp.sum(dy * xh, axis=0, keepdims=True)


def _rope(v, cos_t, sin_a, sin_b, sign):
    return v * cos_t + sign * (pltpu.roll(v, LANE - ROPE_DIM // 2, axis=1) * sin_a
                               + pltpu.roll(v, ROPE_DIM // 2, axis=1) * sin_b)


def rms_fwd(name, h, gain, S, D, ts):
    def fn(i, step, ins, outs, accs, scr):
        x = ins[0][...]
        outs[0][...] = (x * _rstd(x) * ins[1][...]).astype(BF16)
    return rowwise(name, fn, S, ts, [(h, rows(ts, D)), (gain, whole((1, D)))], [((S, D), BF16, rows(ts, D))])[0]


def post_fwd(name, h, out, post_g, ple_g, S, D, ts):
    def fn(i, step, ins, outs, accs, scr):
        o = ins[1][...]
        h1 = ins[0][...] + o * _rstd(o) * ins[2][...]
        outs[0][...] = h1
        outs[1][...] = (h1 * _rstd(h1) * ins[3][...]).astype(BF16)
    return rowwise(name, fn, S, ts,
                   [(h, rows(ts, D)), (out, rows(ts, D)), (post_g, whole((1, D))), (ple_g, whole((1, D)))],
                   [((S, D), F32, rows(ts, D)), ((S, D), BF16, rows(ts, D))])


def ple_fwd(name, a, w_gate, p, w_proj, h1, next_gain=None, target=None):
    S, D = h1.shape
    PD = p.shape[1]
    tm = _pick(S, (256, 128))
    last = target is not None

    def body(a_ref, wg_ref, p_ref, wp_ref, h1_ref, x_ref, b1_ref, b2_ref, o1_ref, o2_ref):
        i = pl.program_id(0)
        gl = lax.dot_general(a_ref[...], wg_ref[...], _NN, preferred_element_type=F32)
        pp = lax.dot_general(p_ref[...].astype(BF16), wp_ref[...], _NN, preferred_element_type=F32)
        h = h1_ref[...] + pp * _sigmoid(gl)
        if last:
            e = h - x_ref[...]
            dh = e * (1.0 / D)
            o1_ref[...] = dh
            b1_ref[...], b2_ref[...] = _ple_bwd(dh, pp, gl)

            @pl.when(i == 0)
            def _():
                o2_ref[...] = jnp.zeros_like(o2_ref)

            o2_ref[...] += jnp.broadcast_to(jnp.sum(e * e), (1, LANE))
        else:
            b1_ref[...] = gl.astype(BF16)
            b2_ref[...] = pp.astype(BF16)
            o1_ref[...] = h
            o2_ref[...] = (h * _rstd(h) * x_ref[...]).astype(BF16)

    row = lambda w: pl.BlockSpec((tm, w), lambda i: (i, 0))
    res, _ = _call(
        body, name, [a, w_gate, p, w_proj, h1, target if last else next_gain],
        [row(D), whole((D, D)), row(PD), whole((PD, D)), row(D), row(D) if last else whole((1, D))],
        [jax.ShapeDtypeStruct((S, D), BF16), jax.ShapeDtypeStruct((S, D), BF16), jax.ShapeDtypeStruct((S, D), F32),
         jax.ShapeDtypeStruct((1, LANE), F32) if last else jax.ShapeDtypeStruct((S, D), BF16)],
        [row(D), row(D), row(D), whole((1, LANE)) if last else row(D)], (S // tm,), [],
        ("arbitrary",) if last else ("parallel",))
    return res


def _ple_bwd(dh, pp, gl):
    gate = _sigmoid(gl)
    return (dh * gate).astype(BF16), (dh * pp * gate * (1.0 - gate)).astype(BF16)


def post_bwd(name, dgl, w_gate, dh, h1, out, post_g, ple_g, S, D, ts):
    def fn(i, step, ins, outs, accs, scr):
        da = lax.dot_general(ins[0][...], ins[1][...], _NT, preferred_element_type=F32)
        dx, dple = _rms_bwd(da, ins[3][...], ins[6][...])
        dh1 = ins[2][...] + dx
        dout, dpost = _rms_bwd(dh1, ins[4][...], ins[5][...])
        outs[0][...] = dh1
        outs[1][...] = dout.astype(BF16)
        accs[0][...] += dpost
        accs[1][...] += dple
    return rowwise(name, fn, S, ts,
                   [(dgl, rows(ts, D)), (w_gate, whole((D, D))), (dh, rows(ts, D)), (h1, rows(ts, D)),
                    (out, rows(ts, D)), (post_g, whole((1, D))), (ple_g, whole((1, D)))],
                   [((S, D), F32, rows(ts, D)), ((S, D), BF16, rows(ts, D))],
                   accs=[((1, D), F32), ((1, D), F32)])


def pre_bwd(name, dxn, dh1, h, pre_g, S, D, ts, below=None):
    def fn(i, step, ins, outs, accs, scr):
        dx, dpre = _rms_bwd(ins[0][...], ins[2][...], ins[3][...])
        dh = ins[1][...] + dx
        outs[0][...] = dh
        accs[0][...] += dpre
        if below is not None:
            outs[1][...], outs[2][...] = _ple_bwd(dh, ins[4][...].astype(F32), ins[5][...].astype(F32))
    more = [] if below is None else [(below[0], rows(ts, D)), (below[1], rows(ts, D))]
    return rowwise(name, fn, S, ts,
                   [(dxn, rows(ts, D)), (dh1, rows(ts, D)), (h, rows(ts, D)), (pre_g, whole((1, D)))] + more,
                   [((S, D), F32, rows(ts, D))] + [((S, D), BF16, rows(ts, D))] * len(more), accs=[((1, D), F32)])


def _window_sums(ext, w, back):
    n = ext.shape[0]
    s, win = ext, 1
    while win < w:
        s = s + pltpu.roll(s, win if back else n - win, axis=0)
        win *= 2
    return s


def pool_fwd(name, z, S, E, NG, ts):
    G = E // NG

    def fn(i, step, ins, outs, accs, scr):
        carry = scr[0]

        @pl.when(step == 0)
        def _():
            carry[...] = jnp.zeros_like(carry)

        t = i * ts + lax.broadcasted_iota(jnp.int32, (ts, 1), 0)
        for j, w in enumerate(POOL_WINDOWS):
            u = ins[0][:, j * G:(j + 1) * G]
            ext = jnp.concatenate([carry[:, j * G:(j + 1) * G], u], axis=0)
            sw = _window_sums(ext, w, True)[POOL_HALO:, :]
            cnt = jnp.minimum(t + 1, w).astype(F32)
            outs[0][j] = (sw / cnt - u).astype(BF16)
        carry[...] = ins[0][ts - POOL_HALO:, :]

    return rowwise(name, fn, S, ts, [(z, rows(ts, E, 0))],
                   [((NG, S, G), BF16, pl.BlockSpec((NG, ts, G), lambda i: (0, i, 0)))],
                   scratch=[pltpu.VMEM((POOL_HALO, E), F32)])[0]


def pool_bwd(name, dpooled, dg, S, E, NG, ts):
    G = E // NG

    def fn(i, step, ins, outs, accs, scr):
        carry = scr[0]

        @pl.when(step == 0)
        def _():
            carry[...] = jnp.zeros_like(carry)

        t = i * ts + lax.broadcasted_iota(jnp.int32, (ts, 1), 0)
        for j, w in enumerate(POOL_WINDOWS):
            d = ins[0][j]
            e = d / jnp.minimum(t + 1, w).astype(F32)
            ext = jnp.concatenate([e, carry[:, j * G:(j + 1) * G]], axis=0)
            sw = _window_sums(ext, w, False)[:ts, :]
            outs[0][:, j * G:(j + 1) * G] = (sw - d).astype(BF16)
            carry[:, j * G:(j + 1) * G] = e[:POOL_HALO, :]
        outs[0][:, E:] = ins[1][...]

    return rowwise(name, fn, S, ts,
                   [(dpooled, pl.BlockSpec((NG, ts, G), lambda i: (0, i, 0))), (dg, rows(ts, E))],
                   [((S, 2 * E), BF16, rows(ts, 2 * E))],
                   scratch=[pltpu.VMEM((POOL_HALO, E), F32)], reverse=True)[0]


def pool_group_fwd(name, pooled, w_group, z, scale, S, E, NG):
    G = E // NG
    tm = _pick(S, (1024, 512, 256, 128))

    def body(a_ref, b_ref, g_ref, sc_ref, y_ref, mx_ref):
        mx = lax.dot_general(a_ref[...], b_ref[...], _NN, preferred_element_type=F32)
        g = g_ref[...]
        y_ref[...] = (mx * sc_ref[...] * (g * _sigmoid(g))).astype(BF16)
        mx_ref[...] = mx.astype(BF16)

    grp = pl.BlockSpec((None, tm, G), lambda j, i: (j, i, 0))
    res, _ = _call(
        body, name, [pooled, w_group, z, scale],
        [grp, pl.BlockSpec((None, G, G), lambda j, i: (j, 0, 0)), pl.BlockSpec((tm, G), lambda j, i: (i, NG + j)),
         pl.BlockSpec((1, G), lambda j, i: (0, j))],
        [jax.ShapeDtypeStruct((S, E), BF16), jax.ShapeDtypeStruct((NG, S, G), BF16)],
        [pl.BlockSpec((tm, G), lambda j, i: (i, j)), grp], (NG, S // tm), [], ("parallel", "parallel"))
    return res


def pool_gate_bwd(name, dy, mixed, z, scale, S, E, NG, ts):
    G = E // NG

    def fn(i, step, ins, outs, accs, scr):
        for j in range(NG):
            sl = slice(j * G, (j + 1) * G)
            d = ins[0][:, sl]
            mx = ins[1][j].astype(F32)
            g = ins[2][:, sl]
            sc = ins[3][:, sl]
            sg = _sigmoid(g)
            si = g * sg
            outs[0][j] = (d * sc * si).astype(BF16)
            outs[1][:, sl] = (d * mx * sc * (sg * (1.0 + g * (1.0 - sg)))).astype(BF16)
            accs[0][:, sl] += jnp.sum(d * mx * si, axis=0, keepdims=True)

    return rowwise(name, fn, S, ts,
                   [(dy, rows(ts, E)), (mixed, pl.BlockSpec((NG, ts, G), lambda i: (0, i, 0))),
                    (z, rows(ts, E, 1)), (scale, whole((1, E)))],
                   [((NG, S, G), BF16, pl.BlockSpec((NG, ts, G), lambda i: (0, i, 0))), ((S, E), BF16, rows(ts, E))],
                   accs=[((1, E), F32)])


def mla_prep_fwd(name, z, qg, kvg, tabs, S, E, R, ts):
    qb, kb, pb = E // R, E // R + 1, (E + 2 * R) // LANE

    def fn(i, step, ins, outs, accs, scr):
        zq, zkv = ins[0][...], ins[1][...]
        outs[0][...] = (zq * _rstd(zq) * ins[3][...]).astype(BF16)
        outs[1][...] = (zkv * _rstd(zkv) * ins[4][...]).astype(BF16)
        outs[2][...] = _rope(ins[2][...], ins[5][...], ins[6][...], ins[7][...], 1.0)

    return rowwise(name, fn, S, ts,
                   [(z, rows(ts, R, qb)), (z, rows(ts, R, kb)), (z, rows(ts, LANE, pb)),
                    (qg, whole((1, R))), (kvg, whole((1, R)))] + [(t, rows(ts, LANE)) for t in tabs],
                   [((S, R), BF16, rows(ts, R)), ((S, R), BF16, rows(ts, R)), ((S, LANE), F32, rows(ts, LANE))])


def mla_up_fwd(name, qn, kvn, w_uq, w_uk, w_uv, kper, tabs, S, H, R, scale):
    hc = _pick(H, (4, 2, 1))
    tm = _pick(S, (1024, 512, 256, 128))

    def body(qn_ref, kvn_ref, wq_ref, wk_ref, wv_ref, kper_ref, cos_ref, sa_ref, sb_ref, q_ref, k_ref, v_ref):
        cos_t, sin_a, sin_b = cos_ref[...], sa_ref[...], sb_ref[...]
        kvn_t = kvn_ref[...]
        q = lax.dot_general(qn_ref[...], wq_ref[...], _NN, preferred_element_type=F32)
        k = lax.dot_general(kvn_t, wk_ref[...], _NN, preferred_element_type=F32)
        v_ref[...] = lax.dot_general(kvn_t, wv_ref[...], _NN, preferred_element_type=F32).astype(BF16)
        kp = kper_ref[...].astype(BF16)
        for h in range(hc):
            a, b, c = h * HEAD_PAD, h * HEAD_PAD + NOPE_DIM, (h + 1) * HEAD_PAD
            q_ref[:, a:b] = (q[:, a:b] * scale).astype(BF16)
            q_ref[:, b:c] = (_rope(q[:, b:c], cos_t, sin_a, sin_b, 1.0) * scale).astype(BF16)
            k_ref[:, a:b] = k[:, a:b].astype(BF16)
            k_ref[:, b:c] = kp

    row = lambda w: pl.BlockSpec((tm, w), lambda i, j: (i, 0))
    col = lambda w: pl.BlockSpec((R, w), lambda i, j: (0, j))
    out = lambda w: pl.BlockSpec((tm, w), lambda i, j: (i, j))
    W = H * HEAD_PAD
    res, _ = _call(
        body, name, [qn, kvn, w_uq, w_uk, w_uv, kper, *tabs],
        [row(R), row(R), col(hc * HEAD_PAD), col(hc * HEAD_PAD), col(hc * V_DIM), row(LANE), row(LANE), row(LANE),
         row(LANE)],
        [jax.ShapeDtypeStruct((S, W), BF16), jax.ShapeDtypeStruct((S, W), BF16),
         jax.ShapeDtypeStruct((S, H * V_DIM), BF16)],
        [out(hc * HEAD_PAD), out(hc * HEAD_PAD), out(hc * V_DIM)], (S // tm, H // hc), [], ("parallel", "parallel"))
    return res


def mla_gate_bwd(name, dy, o, z, S, E, ts):
    def fn(i, step, ins, outs, accs, scr):
        d, ov, g = ins[0][...], ins[1][...], ins[2][...]
        sg = _sigmoid(g)
        outs[0][...] = (d * (g * sg)).astype(BF16)
        outs[1][...] = (d * ov * (sg * (1.0 + g * (1.0 - sg)))).astype(BF16)

    return rowwise(name, fn, S, ts, [(dy, rows(ts, E)), (o, rows(ts, E)), (z, rows(ts, E, 0))],
                   [((S, E), BF16, rows(ts, E)), ((S, E), BF16, rows(ts, E))])


def mla_unpack_q_bwd(name, dqt, tabs, S, H, t, scale):
    W = H * HEAD_PAD

    def fn(i, step, ins, outs, accs, scr):
        cos_t, sin_a, sin_b = ins[1][...], ins[2][...], ins[3][...]
        for h in range(H):
            a, b, c = h * HEAD_PAD, h * HEAD_PAD + NOPE_DIM, (h + 1) * HEAD_PAD
            dq = ins[0][h].T
            outs[0][:, a:b] = (dq[:, :NOPE_DIM] * scale).astype(BF16)
            outs[0][:, b:c] = (_rope(dq[:, NOPE_DIM:], cos_t, sin_a, sin_b, -1.0) * scale).astype(BF16)

    return rowwise(name, fn, S, t,
                   [(dqt, pl.BlockSpec((H, None, HEAD_PAD, t), lambda i: (0, i, 0, 0)))]
                   + [(tb, rows(t, LANE)) for tb in tabs],
                   [((S, W), BF16, rows(t, W))])[0]


def mla_unpack_k_bwd(name, dk, S, H, ts):
    W = H * HEAD_PAD

    def fn(i, step, ins, outs, accs, scr):
        dkpe = jnp.zeros((ts, LANE), F32)
        for h in range(H):
            a, b, c = h * HEAD_PAD, h * HEAD_PAD + NOPE_DIM, (h + 1) * HEAD_PAD
            outs[0][:, a:b] = ins[0][:, a:b].astype(BF16)
            outs[0][:, b:c] = jnp.zeros((ts, LANE), BF16)
            dkpe = dkpe + ins[0][:, b:c]
        outs[1][...] = dkpe

    return rowwise(name, fn, S, ts, [(dk, rows(ts, W))],
                   [((S, W), BF16, rows(ts, W)), ((S, LANE), F32, rows(ts, LANE))])


def mla_prep_bwd(name, dqn, dkvn_k, dkvn_v, z, dkpe, dg, qg, kvg, tabs, S, E, R, ts):
    qb, kb = E // R, E // R + 1
    ZW = E + 2 * R + LANE

    def fn(i, step, ins, outs, accs, scr):
        dzq, dqg = _rms_bwd(ins[0][...], ins[3][...], ins[7][...])
        dzkv, dkvg = _rms_bwd(ins[1][...] + ins[2][...], ins[4][...], ins[8][...])
        outs[0][:, :E] = ins[6][...]
        outs[0][:, E:E + R] = dzq.astype(BF16)
        outs[0][:, E + R:E + 2 * R] = dzkv.astype(BF16)
        outs[0][:, E + 2 * R:] = _rope(ins[5][...], ins[9][...], ins[10][...], ins[11][...], -1.0).astype(BF16)
        accs[0][...] += dqg
        accs[1][...] += dkvg

    return rowwise(name, fn, S, ts,
                   [(dqn, rows(ts, R)), (dkvn_k, rows(ts, R)), (dkvn_v, rows(ts, R)), (z, rows(ts, R, qb)),
                    (z, rows(ts, R, kb)), (dkpe, rows(ts, LANE)), (dg, rows(ts, E)),
                    (qg, whole((1, R))), (kvg, whole((1, R)))] + [(t, rows(ts, LANE)) for t in tabs],
                   [((S, ZW), BF16, rows(ts, ZW))], accs=[((1, R), F32), ((1, R), F32)])


_NT = (((1,), (1,)), ((), ()))
_NN = (((1,), (0,)), ((), ()))
_TN = (((0,), (0,)), ((), ()))


def _causal_mask_t(t):
    return lax.broadcasted_iota(jnp.int32, (t, t), 0) <= lax.broadcasted_iota(jnp.int32, (t, t), 1)


def _tile(i, t):
    return pl.ds(pl.multiple_of(i * t, t), t)


def flash_fwd(name, q, k, v, z, S, H, t, rider=None):
    nt = S // t

    def body(q_ref, k_ref, v_ref, g_ref, o_ref, y_ref, lse_ref, m_sc, l_sc, acc_sc):
        i = pl.program_id(1)
        m_sc[...] = jnp.full_like(m_sc, NEG_INF)
        l_sc[...] = jnp.zeros_like(l_sc)
        acc_sc[...] = jnp.zeros_like(acc_sc)
        q = q_ref[...]

        def tile(j, diag):
            s = lax.dot_general(k_ref[_tile(j, t), :], q, _NT, preferred_element_type=F32)
            if diag:
                s = jnp.where(_causal_mask_t(t), s, NEG_INF)
            m_prev = m_sc[...]
            m_new = jnp.maximum(m_prev, jnp.max(s, axis=0, keepdims=True))
            alpha = jnp.exp(m_prev - m_new)
            p = jnp.exp(s - m_new)
            l_sc[...] = alpha * l_sc[...] + jnp.sum(p, axis=0, keepdims=True)
            acc_sc[...] = alpha * acc_sc[...] + lax.dot_general(v_ref[_tile(j, t), :], p.astype(BF16), _TN,
                                                                 preferred_element_type=F32)
            m_sc[...] = m_new

        def off_diagonal(j, carry):
            tile(j, False)
            return carry

        lax.fori_loop(0, i, off_diagonal, 0)
        tile(i, True)
        l = l_sc[...]
        o = (acc_sc[...] / l).T
        g = g_ref[...]
        o_ref[...] = o
        y_ref[...] = (o * (g * _sigmoid(g))).astype(BF16)
        lse_ref[...] = m_sc[...] + jnp.log(l)

    qtile = pl.BlockSpec((t, V_DIM), lambda h, i: (i, h))
    (o, y, lse), carried = _call(
        body, name, [q, k, v, z],
        [pl.BlockSpec((t, HEAD_PAD), lambda h, i: (i, h)), pl.BlockSpec((S, HEAD_PAD), lambda h, i: (0, h)),
         pl.BlockSpec((S, V_DIM), lambda h, i: (0, h)), qtile],
        [jax.ShapeDtypeStruct((S, H * V_DIM), F32), jax.ShapeDtypeStruct((S, H * V_DIM), BF16),
         jax.ShapeDtypeStruct((H, nt, 1, t), F32)],
        [qtile, qtile, pl.BlockSpec((None, None, 1, t), lambda h, i: (h, i, 0, 0))],
        (H, nt), [pltpu.VMEM((1, t), F32), pltpu.VMEM((1, t), F32), pltpu.VMEM((V_DIM, t), F32)],
        ("parallel", "parallel"), rider)
    return o, y, lse, carried


def flash_bwd(name, q, k, v, do, o, lse, S, H, t, rider=None):
    nt = S // t

    def body(q_ref, k_ref, v_ref, do_ref, o_ref, lse_ref, dq_ref, dk_ref, dv_ref, kt_sc, dl_sc, dv_sc):
        j = pl.program_id(1)

        @pl.when(j == 0)
        def _():
            dq_ref[...] = jnp.zeros_like(dq_ref)
            ones = jnp.ones((8, V_DIM), BF16)
            for i in range(nt):
                x = do_ref[i * t:(i + 1) * t, :].astype(F32) * o_ref[i * t:(i + 1) * t, :]
                hi = x.astype(BF16)
                lo = (x - hi.astype(F32)).astype(BF16)
                dl_sc[i] = (lax.dot_general(ones, hi, _NT, preferred_element_type=F32)
                            + lax.dot_general(ones, lo, _NT, preferred_element_type=F32))

        kj, vj = k_ref[...], v_ref[...]
        kt_sc[...] = kj.astype(F32).T.astype(BF16)
        dk_ref[...] = jnp.zeros_like(dk_ref)
        dv_sc[...] = jnp.zeros_like(dv_sc)

        def tile(i, diag):
            qi, doi = q_ref[_tile(i, t), :], do_ref[_tile(i, t), :]
            s = lax.dot_general(kj, qi, _NT, preferred_element_type=F32)
            p = jnp.exp(s - lse_ref[i])
            if diag:
                p = jnp.where(_causal_mask_t(t), p, 0.0)
            dv_sc[...] += lax.dot_general(p.astype(BF16), doi, _NN, preferred_element_type=F32)
            dp = lax.dot_general(vj, doi, _NT, preferred_element_type=F32)
            ds = (p * (dp - dl_sc[i, 0:1, :])).astype(BF16)
            dk_ref[...] += lax.dot_general(ds, qi, _NN, preferred_element_type=F32)
            dq_ref[i] += lax.dot_general(kt_sc[...], ds, _NN, preferred_element_type=F32)

        def off_diagonal(i, carry):
            tile(i, False)
            return carry

        tile(j, True)
        lax.fori_loop(j + 1, nt, off_diagonal, 0)
        dv_ref[...] = dv_sc[...].astype(BF16)

    head = lambda w: pl.BlockSpec((S, w), lambda h, j: (0, h))
    ktile = lambda w: pl.BlockSpec((t, w), lambda h, j: (j, h))
    (dq, dk, dv), carried = _call(
        body, name, [q, k, v, do, o, lse],
        [head(HEAD_PAD), ktile(HEAD_PAD), ktile(V_DIM), head(V_DIM), head(V_DIM),
         pl.BlockSpec((None, nt, 1, t), lambda h, j: (h, 0, 0, 0))],
        [jax.ShapeDtypeStruct((H, nt, HEAD_PAD, t), F32), jax.ShapeDtypeStruct((S, H * HEAD_PAD), F32),
         jax.ShapeDtypeStruct((S, H * V_DIM), BF16)],
        [pl.BlockSpec((None, nt, HEAD_PAD, t), lambda h, j: (h, 0, 0, 0)), ktile(HEAD_PAD), ktile(V_DIM)],
        (H, nt), [pltpu.VMEM((HEAD_PAD, t), BF16), pltpu.VMEM((nt, 8, t), F32), pltpu.VMEM((t, V_DIM), F32)],
        ("parallel", "arbitrary"), rider)
    return dq, dk, dv, carried


def _peers():
    x, y, c = lax.axis_index("x"), lax.axis_index("y"), lax.axis_index("c")
    me = 4 * x + 2 * y + c
    peers = []
    for fx, fy, fc in ((0, 0, 1), (1, 0, 0), (0, 1, 0), (1, 1, 0), (1, 0, 1), (0, 1, 1), (1, 1, 1)):
        px, py, pc = x ^ fx, y ^ fy, c ^ fc
        peers.append(((px, py, pc), 4 * px + 2 * py + pc))
    return me, peers


def _hbm_specs(n):
    return [pl.BlockSpec(memory_space=pl.ANY)] * n


class Rider:
    def __init__(self, arrs, gather, windows=None):
        self.arrs, self.gather, self.n = list(arrs), gather, len(arrs)
        self.windows = list(windows) if windows is not None else [None] * self.n
        assert not (gather and any(w is not None for w in self.windows))
        self.out_shapes = [
            jax.ShapeDtypeStruct((N_DEV,) + a.shape if gather else
                                 a.shape if w is None else (N_DEV, w[1]) + a.shape[2:], a.dtype)
            for a, w in zip(arrs, self.windows)]
        self.sems = [pltpu.SemaphoreType.DMA((self.n, N_DEV - 1)), pltpu.SemaphoreType.DMA((self.n, N_DEV - 1)),
                     pltpu.SemaphoreType.DMA((self.n,))]

    def _copies(self, srcs, dsts, sems):
        send_sems, recv_sems, local_sems = sems
        x, y, c = lax.axis_index("x"), lax.axis_index("y"), lax.axis_index("c")
        ident = lambda d: 4 * d[0] + 2 * d[1] + d[2]
        me, sibling = (x, y, c), (x, y, 1 - c)
        chips = [(1 - x, y), (x, 1 - y), (1 - x, 1 - y)]
        _, peers = _peers()

        def slot(a, pid):
            w = self.windows[a]
            return srcs[a].at[pid] if w is None else srcs[a].at[pid, pl.ds(w[0], w[1])]

        def remote(a, k, incoming):
            if not self.gather:
                target, pid = peers[k]
                src, block = slot(a, pid), (pid if incoming else ident(me))
            elif k == 0:
                target, src, block = sibling, srcs[a], ident(sibling if incoming else me)
            elif k <= 3:
                target = (*chips[k - 1], c)
                src, block = srcs[a], ident(target if incoming else me)
            else:
                landed = ident((*chips[k - 4], c))
                target, src = sibling, dsts[a].at[landed]
                block = ident((*chips[k - 4], 1 - c)) if incoming else landed
            return pltpu.make_async_remote_copy(
                src_ref=src, dst_ref=dsts[a].at[block], send_sem=send_sems.at[a, k], recv_sem=recv_sems.at[a, k],
                device_id=target, device_id_type=MESH)

        def local(a):
            return pltpu.make_async_copy(srcs[a] if self.gather else slot(a, ident(me)), dsts[a].at[ident(me)],
                                         local_sems.at[a])

        return local, remote

    def start(self, srcs, dsts, sems):
        local, remote = self._copies(srcs, dsts, sems)
        for a in range(self.n):
            local(a).start()
            for k in range(4 if self.gather else N_DEV - 1):
                remote(a, k, False).start()

    def relay(self, srcs, dsts, sems):
        if not self.gather:
            return
        local, remote = self._copies(srcs, dsts, sems)
        for a in range(self.n):
            for k in range(1, 4):
                remote(a, k, True).wait_recv()
                remote(a, k + 3, False).start()

    def wait(self, srcs, dsts, sems):
        local, remote = self._copies(srcs, dsts, sems)
        for a in range(self.n):
            for k in range(N_DEV - 1):
                if not (self.gather and 1 <= k <= 3):
                    remote(a, k, True).wait_recv()
        for a in range(self.n):
            for k in range(N_DEV - 1):
                remote(a, k, False).wait_send()
            local(a).wait()


def _carry(body, n_in, n_out, rider, grid):
    n = rider.n
    steps = math.prod(grid)

    def wrapped(*refs):
        ins, r_in = refs[:n_in], refs[n_in:n_in + n]
        outs = refs[n_in + n:n_in + n + n_out]
        r_out = refs[n_in + n + n_out:n_in + 2 * n + n_out]
        scratch, sems = refs[n_in + 2 * n + n_out:-3], refs[-3:]
        step = 0
        for d, g in enumerate(grid):
            step = step * g + pl.program_id(d)

        @pl.when(step == 0)
        def _():
            rider.start(r_in, r_out, sems)

        if rider.gather:
            @pl.when(step == (3 * steps) // 4)
            def _():
                rider.relay(r_in, r_out, sems)

        body(*ins, *outs, *scratch)

        @pl.when(step == steps - 1)
        def _():
            rider.wait(r_in, r_out, sems)

    return wrapped


def _call(body, name, ins, in_specs, out_shape, out_specs, grid, scratch, sem, rider=None):
    if rider is None:
        return _pcall(body, name=name, out_shape=list(out_shape), grid=grid, in_specs=list(in_specs),
                      out_specs=list(out_specs), scratch_shapes=list(scratch), compiler_params=_cparams(*sem))(*ins), None
    res = _pcall(
        _carry(body, len(ins), len(out_shape), rider, grid), name=name,
        out_shape=list(out_shape) + rider.out_shapes, grid=grid,
        in_specs=list(in_specs) + _hbm_specs(rider.n), out_specs=list(out_specs) + _hbm_specs(rider.n),
        scratch_shapes=list(scratch) + rider.sems, compiler_params=_cparams(*(("arbitrary",) * len(grid))),
    )(*ins, *rider.arrs)
    return res[:len(out_shape)], res[len(out_shape):]


def exchange(name, rider):
    def body(*refs):
        srcs, dsts, sems = refs[:rider.n], refs[rider.n:2 * rider.n], refs[2 * rider.n:]
        rider.start(srcs, dsts, sems)
        rider.relay(srcs, dsts, sems)
        rider.wait(srcs, dsts, sems)

    return _pcall(body, name=name, out_shape=rider.out_shapes, in_specs=_hbm_specs(rider.n),
                  out_specs=_hbm_specs(rider.n), scratch_shapes=rider.sems)(*rider.arrs)


def adamw(name, gslots, w, m, v, layer=0, prev=None):
    K, R, C = gslots.shape
    per_row = C * (K * gslots.dtype.itemsize + 7 * 4) * 2
    tr = R
    for cand in (1024, 512, 256, 128, 64, 32, 16, 8):
        if R % cand == 0:
            tr = cand
            if cand * per_row <= VMEM_LIMIT_BYTES // 2:
                break
    c1 = 1.0 / (1.0 - ADAM_B1 ** ADAM_STEP)
    c2 = 1.0 / (1.0 - ADAM_B2 ** ADAM_STEP)

    def body(g_ref, w_ref, m_ref, v_ref, *rest):
        go_ref, d_ref, mo_ref, vo_ref = rest[-4:]
        g = g_ref[0].astype(F32)
        for s in range(1, K):
            g = g + g_ref[s].astype(F32)
        mn = ADAM_B1 * m_ref[...] + (1.0 - ADAM_B1) * g
        vn = ADAM_B2 * v_ref[...] + (1.0 - ADAM_B2) * (g * g)
        go_ref[...] = g
        mo_ref[...] = mn
        vo_ref[...] = vn
        d_ref[...] = -ADAM_LR * ((mn * c1) / (jnp.sqrt(vn * c2) + ADAM_EPS) + ADAM_WD * w_ref[...])

    blk = pl.BlockSpec((None, tr, C), lambda i: (layer, i, 0))
    prev = [] if prev is None else list(prev)
    return _pcall(
        body, name=name, out_shape=[jax.ShapeDtypeStruct(w.shape, F32)] * 4, grid=(R // tr,),
        in_specs=[pl.BlockSpec((K, tr, C), lambda i: (0, i, 0)), blk, blk, blk] + _hbm_specs(len(prev)),
        out_specs=[blk] * 4, input_output_aliases={4 + q: q for q in range(len(prev))},
        compiler_params=_cparams("parallel"),
    )(gslots, w, m, v, *prev)


def _from_slots(gathered, ax):
    g = jnp.moveaxis(gathered, 0, ax)
    s = g.shape
    return g.reshape(s[:ax] + (s[ax] * s[ax + 1],) + s[ax + 2:])


def _to_slots(full, ax):
    s = full.shape
    g = full.reshape(s[:ax] + (N_DEV, s[ax] // N_DEV) + s[ax + 1:])
    g = jnp.moveaxis(g, ax, 0)
    return g.reshape(N_DEV, -1, g.shape[-1])


def _rope_tables(pos, S):
    inv_freq = ROPE_THETA ** (-jnp.arange(0, ROPE_DIM, 2, dtype=F32) / ROPE_DIM)
    ang = pos.astype(F32)[:, None] * inv_freq
    cos, sin = jnp.cos(ang), jnp.sin(ang)
    z = jnp.zeros((S, ROPE_DIM // 2), F32)
    cos_t = jnp.concatenate([cos, cos, z, z], axis=1)
    sin_a = jnp.concatenate([-sin, z, z, z], axis=1)
    sin_b = jnp.concatenate([z, sin, z, z], axis=1)
    return cos_t, sin_a, sin_b


def kernel(x, p, positions, pre_norm, post_norm, pool_w_in, pool_w_group, pool_scale, pool_w_out, mla_w_in, mla_q_norm, mla_w_uq, mla_kv_norm, mla_w_ukv, mla_w_out, ple_norm, ple_w_gate, ple_w_proj, loss_target, m_pre_norm, m_post_norm, m_pool_w_in, m_pool_w_group, m_pool_scale, m_pool_w_out, m_mla_w_in, m_mla_q_norm, m_mla_w_uq, m_mla_kv_norm, m_mla_w_ukv, m_mla_w_out, m_ple_norm, m_ple_w_gate, m_ple_w_proj, v_pre_norm, v_post_norm, v_pool_w_in, v_pool_w_group, v_pool_scale, v_pool_w_out, v_mla_w_in, v_mla_q_norm, v_mla_w_uq, v_mla_kv_norm, v_mla_w_ukv, v_mla_w_out, v_ple_norm, v_ple_w_gate, v_ple_w_proj):
    wl = dict(pre_norm=pre_norm, post_norm=post_norm, pool_w_in=pool_w_in, pool_w_group=pool_w_group,
              pool_scale=pool_scale, pool_w_out=pool_w_out, mla_w_in=mla_w_in, mla_q_norm=mla_q_norm,
              mla_w_uq=mla_w_uq, mla_kv_norm=mla_kv_norm, mla_w_ukv=mla_w_ukv, mla_w_out=mla_w_out,
              ple_norm=ple_norm, ple_w_gate=ple_w_gate, ple_w_proj=ple_w_proj)
    ml = dict(pre_norm=m_pre_norm, post_norm=m_post_norm, pool_w_in=m_pool_w_in, pool_w_group=m_pool_w_group,
              pool_scale=m_pool_scale, pool_w_out=m_pool_w_out, mla_w_in=m_mla_w_in, mla_q_norm=m_mla_q_norm,
              mla_w_uq=m_mla_w_uq, mla_kv_norm=m_mla_kv_norm, mla_w_ukv=m_mla_w_ukv, mla_w_out=m_mla_w_out,
              ple_norm=m_ple_norm, ple_w_gate=m_ple_w_gate, ple_w_proj=m_ple_w_proj)
    vl = dict(pre_norm=v_pre_norm, post_norm=v_post_norm, pool_w_in=v_pool_w_in, pool_w_group=v_pool_w_group,
              pool_scale=v_pool_scale, pool_w_out=v_pool_w_out, mla_w_in=v_mla_w_in, mla_q_norm=v_mla_q_norm,
              mla_w_uq=v_mla_w_uq, mla_kv_norm=v_mla_kv_norm, mla_w_ukv=v_mla_w_ukv, mla_w_out=v_mla_w_out,
              ple_norm=v_ple_norm, ple_w_gate=v_ple_w_gate, ple_w_proj=v_ple_w_proj)

    S, D = x.shape[1], x.shape[2]
    L = pre_norm.shape[0]
    E = pool_scale.shape[1]
    NG = pool_w_group.shape[1]
    R = mla_w_uq.shape[1]
    H = D // 128
    EM = H * V_DIM
    PD = p.shape[-1]
    me = 4 * lax.axis_index("x") + 2 * lax.axis_index("y") + lax.axis_index("c")
    ts = min(S, 256)
    tsw = min(S, 128)
    ta = min(S, 512)
    sm_scale = (NOPE_DIM + ROPE_DIM) ** -0.5

    wb = {n: wl[n].astype(BF16) for n in BIG}
    full = {}

    def ag_rider(host):
        return Rider([wb[n][l] for n, l in AG_PLAN[host]], True) if host in AG_PLAN else None

    def ag_done(host, results):
        for (n, l), g in zip(AG_PLAN[host], results):
            full[n, l] = g if n in SLOT_NATIVE else _from_slots(g, SHARD_AXIS[n] - 1)

    small_sh = jnp.concatenate([wl[n].reshape(1, -1) for n in SMALL_SHARD], axis=1)
    g_in0, g_small = exchange("gather_first", Rider([wb['pool_w_in'][0], small_sh], True))
    full['pool_w_in', 0] = g_in0
    nq = mla_q_norm.size
    q_norm = _from_slots(g_small[:, 0, :nq].reshape((N_DEV,) + mla_q_norm.shape), 1)
    kv_norm = _from_slots(g_small[:, 0, nq:].reshape((N_DEV,) + mla_kv_norm.shape), 1)

    def mla_kernel_weights(j):
        w_in = full['mla_w_in', j]
        w_in_k = jnp.concatenate([w_in[:, 2 * R + ROPE_DIM:], w_in[:, :2 * R + ROPE_DIM],
                                  jnp.zeros((D, LANE - ROPE_DIM), BF16)], axis=1)
        w_uq_k = jnp.pad(full['mla_w_uq', j].reshape(R, H, NOPE_DIM + ROPE_DIM),
                         ((0, 0), (0, 0), (0, HEAD_PAD - NOPE_DIM - ROPE_DIM))).reshape(R, H * HEAD_PAD)
        w_ukv = full['mla_w_ukv', j].reshape(R, H, NOPE_DIM + V_DIM)
        w_uk_k = jnp.pad(w_ukv[..., :NOPE_DIM], ((0, 0), (0, 0), (0, HEAD_PAD - NOPE_DIM))).reshape(R, H * HEAD_PAD)
        w_uv_k = w_ukv[..., NOPE_DIM:].reshape(R, H * V_DIM)
        return w_in_k, w_uq_k, w_uk_k, w_uv_k

    def fmm(name, a, b, mode, out_dtype=F32, slots=False):
        if name not in AG_PLAN:
            return mm(name, a, b, mode, out_dtype, slots=slots)
        out, carried = mm(name, a, b, mode, out_dtype, ag_rider(name), slots)
        ag_done(name, carried)
        return out

    mla_w = {}
    tabs = _rope_tables(positions[0], S)

    h = x[0]
    saved = []
    for i in range(L):
        j = i // 2
        sv = dict(h=h)
        if i == 0:
            xn = rms_fwd(f"pre_norm_{i}", h, pre_norm[i:i + 1], S, D, ts)
        sv['xn'] = xn
        if i % 2 == 0:
            z = fmm(f"pool_in_{i}", xn, full['pool_w_in', j], 'nn', slots=True)
            pooled = pool_fwd(f"pool_window_{i}", z, S, E, NG, tsw)
            y, mixed = pool_group_fwd(f"pool_group_{i}", pooled, full['pool_w_group', j], z, pool_scale[j:j + 1],
                                      S, E, NG)
            out = fmm(f"pool_out_{i}", y, full['pool_w_out', j], 'nn')
            sv.update(z=z, pooled=pooled, mixed=mixed, y=y)
        else:
            w_in_k, w_uq_k, w_uk_k, w_uv_k = mla_w[j] = mla_kernel_weights(j)
            z = fmm(f"mla_in_{i}", xn, w_in_k, 'nn')
            qn, kvn, kper = mla_prep_fwd(f"mla_prep_{i}", z, q_norm[j:j + 1], kv_norm[j:j + 1], tabs, S, EM, R, ts)
            qp, kp, vv = mla_up_fwd(f"mla_up_{i}", qn, kvn, w_uq_k, w_uk_k, w_uv_k, kper, tabs, S, H, R, sm_scale)
            o, y, lse, carried = flash_fwd(f"attn_{i}", qp, kp, vv, z, S, H, ta, ag_rider(f"attn_{i}"))
            if carried is not None:
                ag_done(f"attn_{i}", carried)
            out = fmm(f"mla_out_{i}", y, full['mla_w_out', j], 'nn')
            sv.update(z=z, qn=qn, kvn=kvn, qp=qp, kp=kp, vv=vv, o=o, lse=lse, y=y)
        h1, a = post_fwd(f"post_norm_{i}", h, out, post_norm[i:i + 1], ple_norm[i:i + 1], S, D, ts)
        if i < L - 1:
            gl, pp, h, xn = ple_fwd(f"ple_{i}", a, full['ple_w_gate', i], p[i, 0], full['ple_w_proj', i], h1,
                                    next_gain=pre_norm[i + 1:i + 2])
        else:
            gl = pp = None
            dpp, dgl, dh, loss_acc = ple_fwd(f"ple_{i}", a, full['ple_w_gate', i], p[i, 0], full['ple_w_proj', i], h1,
                                           target=loss_target[0])
        sv.update(out=out, h1=h1, a=a, gl=gl, pp=pp)
        saved.append(sv)

    loss = lax.psum(loss_acc[0, 0] * (0.5 / D), ("x", "y", "c"))

    gw = {n: [None] * wl[n].shape[0] for n in WEIGHTS}
    recv = {}

    def rs_rider(host):
        if host not in RS_PLAN:
            return None
        return scatter_rider(RS_PLAN[host])

    def scatter_rider(keys):
        arrs = [gw[n][l] if n in SLOT_NATIVE else _to_slots(gw[n][l], SHARD_AXIS[n] - 1).astype(BF16)
                for n, l, *_ in keys]
        halves = [key[2] if len(key) == 3 else None for key in keys]
        return Rider(arrs, False, [None if h is None else (h * (a.shape[1] // 2), a.shape[1] // 2)
                                   for a, h in zip(arrs, halves)])

    def rs_done(host, results):
        for key, r in zip(RS_PLAN[host], results):
            recv[key] = r

    def bmm(name, a, b, mode, slots=False):
        out_dtype = BF16 if mode == 'tn' else F32
        if name not in RS_PLAN:
            return mm(name, a, b, mode, out_dtype, slots=slots)
        out, carried = mm(name, a, b, mode, out_dtype, rs_rider(name), slots)
        rs_done(name, carried)
        return out

    for i in reversed(range(L)):
        j = i // 2
        sv = saved[i]
        gw['ple_w_proj'][i] = bmm(f"ple_proj_dw_{i}", p[i, 0], dpp, 'tn')
        gw['ple_w_gate'][i] = bmm(f"ple_gate_dw_{i}", sv['a'], dgl, 'tn')
        dh1, dout, dpost, dple = post_bwd(f"post_norm_bwd_{i}", dgl, full['ple_w_gate', i], dh, sv['h1'], sv['out'],
                                          post_norm[i:i + 1], ple_norm[i:i + 1], S, D, ts)
        gw['post_norm'][i], gw['ple_norm'][i] = dpost[0], dple[0]
        xn = sv['xn']
        if i % 2 == 0:
            gw['pool_w_out'][j] = bmm(f"pool_out_dw_{i}", sv['y'], dout, 'tn')
            dy = bmm(f"pool_out_dx_{i}", dout, full['pool_w_out', j], 'nt')
            dmixed, dg, dscale = pool_gate_bwd(f"pool_gate_bwd_{i}", dy, sv['mixed'], sv['z'], pool_scale[j:j + 1],
                                               S, E, NG, tsw)
            gw['pool_scale'][j] = dscale[0]
            gw['pool_w_group'][j] = bmm(f"pool_group_dw_{i}", sv['pooled'], dmixed, 'tn')
            dpooled = bmm(f"pool_group_dx_{i}", dmixed, full['pool_w_group', j], 'nt')
            dz = pool_bwd(f"pool_window_bwd_{i}", dpooled, dg, S, E, NG, tsw)
            gw['pool_w_in'][j] = bmm(f"pool_in_dw_{i}", xn, dz, 'tn', slots=True)
            dxn = bmm(f"pool_in_dx_{i}", dz, full['pool_w_in', j], 'nt', slots=True)
        else:
            w_in_k, w_uq_k, w_uk_k, w_uv_k = mla_w[j]
            gw['mla_w_out'][j] = bmm(f"mla_out_dw_{i}", sv['y'], dout, 'tn')
            dy = bmm(f"mla_out_dx_{i}", dout, full['mla_w_out', j], 'nt')
            do, dg = mla_gate_bwd(f"mla_gate_bwd_{i}", dy, sv['o'], sv['z'], S, EM, ts)
            dqt, dkp, dvv, carried = flash_bwd(f"attn_bwd_{i}", sv['qp'], sv['kp'], sv['vv'], do, sv['o'], sv['lse'],
                                               S, H, ta, rs_rider(f"attn_bwd_{i}"))
            if carried is not None:
                rs_done(f"attn_bwd_{i}", carried)
            dq_raw = mla_unpack_q_bwd(f"mla_pack_q_bwd_{i}", dqt, tabs, S, H, ta, sm_scale)
            dk_raw, dkpe = mla_unpack_k_bwd(f"mla_pack_k_bwd_{i}", dkp, S, H, tsw)
            g_uq = bmm(f"mla_uq_dw_{i}", sv['qn'], dq_raw, 'tn')
            g_uk = bmm(f"mla_uk_dw_{i}", sv['kvn'], dk_raw, 'tn')
            g_uv = bmm(f"mla_uv_dw_{i}", sv['kvn'], dvv, 'tn')
            gw['mla_w_uq'][j] = g_uq.reshape(R, H, HEAD_PAD)[:, :, :NOPE_DIM + ROPE_DIM].reshape(R, -1)
            gw['mla_w_ukv'][j] = jnp.concatenate(
                [g_uk.reshape(R, H, HEAD_PAD)[:, :, :NOPE_DIM], g_uv.reshape(R, H, V_DIM)], axis=2).reshape(R, -1)
            dqn = bmm(f"mla_uq_dx_{i}", dq_raw, w_uq_k, 'nt')
            dkvn_k = bmm(f"mla_uk_dx_{i}", dk_raw, w_uk_k, 'nt')
            dkvn_v = bmm(f"mla_uv_dx_{i}", dvv, w_uv_k, 'nt')
            dz, dqg, dkvg = mla_prep_bwd(f"mla_prep_bwd_{i}", dqn, dkvn_k, dkvn_v, sv['z'], dkpe, dg,
                                         q_norm[j:j + 1], kv_norm[j:j + 1], tabs, S, EM, R, ts)
            g_in = bmm(f"mla_in_dw_{i}", xn, dz, 'tn')
            dxn = bmm(f"mla_in_dx_{i}", dz, w_in_k, 'nt')
            gw['mla_q_norm'][j], gw['mla_kv_norm'][j] = dqg[0], dkvg[0]
            gw['mla_w_in'][j] = jnp.concatenate([g_in[:, EM:EM + 2 * R + ROPE_DIM], g_in[:, :EM]], axis=1)
        if i > 0:
            dh, dpp, dgl, dpre = pre_bwd(f"pre_norm_bwd_{i}", dxn, dh1, sv['h'], pre_norm[i:i + 1], S, D, ts,
                                         below=(saved[i - 1]['pp'], saved[i - 1]['gl']))
        else:
            dh, dpre = pre_bwd(f"pre_norm_bwd_{i}", dxn, dh1, sv['h'], pre_norm[i:i + 1], S, D, ts)
        gw['pre_norm'][i] = dpre[0]
    grad_x = dh[None]

    last = exchange("scatter_last", scatter_rider(RS_LAST))
    for key, r in zip(RS_LAST, last):
        recv[key] = r
    for n, l in {key[:2] for key in recv if len(key) == 3}:
        recv[n, l] = jnp.concatenate([recv[n, l, 0], recv[n, l, 1]], axis=1)
    small_names = SMALL_REPL + SMALL_SHARD
    gw = {n: jnp.stack(gw[n]) for n in small_names}
    small_g = jnp.concatenate([gw[n].reshape(1, -1) for n in small_names], axis=1)
    small_all = exchange("gather_small_grads", Rider([small_g], True))[0]

    outs = {}
    for n in BIG:
        shp = wl[n].shape
        three = lambda a: a.reshape(shp[0], -1, shp[-1])
        res = None
        for l in range(shp[0]):
            res = adamw(f"adamw_{n}_{l}", recv[n, l], three(wl[n]), three(ml[n]), three(vl[n]), l, res)
        outs[n] = [a.reshape(shp) for a in res]

    pieces, off = [], 0
    for n in small_names:
        sz = gw[n].size
        g = small_all[:, :, off:off + sz]
        off += sz
        if n in SMALL_SHARD:
            rows_, cols_ = gw[n].shape
            g = lax.dynamic_slice_in_dim(g.reshape(N_DEV, rows_, cols_), me * (cols_ // N_DEV), cols_ // N_DEV, axis=2)
            g = g.reshape(N_DEV, 1, -1)
        pieces.append(g)
    gs = jnp.concatenate(pieces, axis=2)
    flat = lambda d: jnp.concatenate([d[n].reshape(1, -1) for n in small_names], axis=1)
    res = adamw("adamw_small", gs, flat(wl)[None], flat(ml)[None], flat(vl)[None])
    off = 0
    for n in small_names:
        sz = wl[n].size
        outs[n] = [a[0, :, off:off + sz].reshape(wl[n].shape) for a in res]
        off += sz

    return (loss, grad_x, *[outs[n][0] for n in WEIGHTS], *[outs[n][1] for n in WEIGHTS],
            *[outs[n][2] for n in WEIGHTS], *[outs[n][3] for n in WEIGHTS])
```

```python
import math

import jax
import jax.numpy as jnp
from jax import lax
from jax.experimental import pallas as pl
from jax.experimental.pallas import tpu as pltpu

F32 = jnp.float32
BF16 = jnp.bfloat16

N_DEV = 8
EPS = 1e-6
ROPE_THETA = 10000.0
NOPE_DIM = 128
ROPE_DIM = 64
V_DIM = 128
HEAD_PAD = 256
LANE = 128
POOL_WINDOWS = (2, 4, 8, 16)
POOL_HALO = 16
NEG_INF = -1e30
ADAM_LR = 0.001
ADAM_B1 = 0.9
ADAM_B2 = 0.999
ADAM_EPS = 1e-08
ADAM_WD = 0.01
ADAM_STEP = 10
VMEM_LIMIT_BYTES = 56 * 1024 * 1024
MM_MAX_TK = 3200
MESH = pl.DeviceIdType.MESH

SHARD_AXIS = dict(pre_norm=None, post_norm=None, pool_w_in=2, pool_w_group=2, pool_scale=None, pool_w_out=1,
                  mla_w_in=2, mla_q_norm=1, mla_w_uq=2, mla_kv_norm=1, mla_w_ukv=2, mla_w_out=1,
                  ple_norm=None, ple_w_gate=1, ple_w_proj=2)
WEIGHTS = tuple(SHARD_AXIS)
BIG = ('pool_w_in', 'pool_w_group', 'pool_w_out', 'mla_w_in', 'mla_w_uq', 'mla_w_ukv', 'mla_w_out',
       'ple_w_gate', 'ple_w_proj')
SMALL_REPL = ('pre_norm', 'post_norm', 'pool_scale', 'ple_norm')
SMALL_SHARD = ('mla_q_norm', 'mla_kv_norm')
SLOT_NATIVE = ('pool_w_in',)

AG_PLAN = {
    "pool_in_0": [("pool_w_group", 0), ("pool_w_out", 0), ("ple_w_gate", 0), ("ple_w_proj", 0)],
    "pool_out_0": [("mla_w_in", 0), ("mla_w_uq", 0), ("mla_w_ukv", 0)],
    "attn_1": [("mla_w_out", 0), ("ple_w_gate", 1), ("ple_w_proj", 1), ("pool_w_in", 1), ("pool_w_group", 1),
               ("mla_w_in", 1)],
    "pool_in_2": [("pool_w_out", 1), ("ple_w_gate", 2), ("ple_w_proj", 2), ("mla_w_uq", 1), ("mla_w_ukv", 1)],
    "attn_3": [("mla_w_out", 1), ("ple_w_gate", 3), ("ple_w_proj", 3)],
}
RS_PLAN = {
    "attn_bwd_3": [("ple_w_gate", 3), ("ple_w_proj", 3), ("mla_w_out", 1)],
    "mla_in_dw_3": [("mla_w_uq", 1), ("mla_w_ukv", 1)],
    "pool_out_dx_2": [("mla_w_in", 1)],
    "pool_in_dw_2": [("ple_w_gate", 2), ("ple_w_proj", 2)],
    "pool_in_dx_2": [("pool_w_out", 1)],
    "attn_bwd_1": [("pool_w_group", 1), ("pool_w_in", 1), ("ple_w_gate", 1), ("ple_w_proj", 1), ("mla_w_out", 0)],
    "mla_in_dw_1": [("mla_w_uq", 0), ("mla_w_ukv", 0)],
    "pool_out_dw_0": [("mla_w_in", 0)],
    "pool_out_dx_0": [("ple_w_gate", 0), ("ple_w_proj", 0)],
    "pool_group_dx_0": [("pool_w_group", 0)],
    "pool_in_dw_0": [("pool_w_out", 0)],
    "pool_in_dx_0": [("pool_w_in", 0, 0)],
}
RS_LAST = [("pool_w_in", 0, 1)]


def _pcall(body, **kw):
    return pl.pallas_call(body, **kw)


def _cparams(*sem):
    return pltpu.CompilerParams(dimension_semantics=sem, vmem_limit_bytes=VMEM_LIMIT_BYTES)


def _pick(n, cands):
    for c in cands:
        if n % c == 0:
            return c
    return n


def _sigmoid(x):
    return 1.0 / (1.0 + jnp.exp(-x))


def mm(name, a, b, mode, out_dtype=F32, rider=None, slots=False):
    squeeze = a.ndim == 2
    if squeeze:
        a = a[None]
        b = b if slots and mode != 'tn' else b[None]
    G = a.shape[0]
    if mode == 'nn':
        M, K = a.shape[1:]
        N = b.shape[2] * (N_DEV if slots else 1)
    elif mode == 'tn':
        K, M = a.shape[1:]
        N = b.shape[2]
    else:
        M, K = a.shape[1:]
        N = b.shape[1]
    n = (K if mode == 'nt' else N) // N_DEV
    tm = _pick(M, (1024, 512, 256, 128))
    tn = _pick(n if slots and mode != 'nt' else N, (1024, 768, 640, 512, 384, 256, 128))
    if slots and mode == 'nt':
        tk = _pick(n, (2048, 1024, 512, 256, 128))
    else:
        tk = K if K <= MM_MAX_TK else _pick(K, (2048, 1024, 640, 512, 384, 256, 128))
    nk = K // tk
    o_spec = pl.BlockSpec((None, tm, tn), lambda g, i, j, k: (g, i, j))
    o_shape = (G, M, N)
    if mode == 'nn':
        a_spec = pl.BlockSpec((None, tm, tk), lambda g, i, j, k: (g, i, k))
        b_spec = pl.BlockSpec((None, tk, tn), lambda g, i, j, k: (g, k, j))
        if slots:
            b_spec = pl.BlockSpec((None, tk, tn), lambda g, i, j, k: (j // (n // tn), k, j % (n // tn)))
        dims = (((1,), (0,)), ((), ()))
    elif mode == 'tn':
        a_spec = pl.BlockSpec((None, tk, tm), lambda g, i, j, k: (g, k, i))
        b_spec = pl.BlockSpec((None, tk, tn), lambda g, i, j, k: (g, k, j))
        if slots:
            o_spec = pl.BlockSpec((None, tm, tn), lambda g, i, j, k: (j // (n // tn), i, j % (n // tn)))
            o_shape = (N_DEV, M, n)
        dims = (((0,), (0,)), ((), ()))
    else:
        a_spec = pl.BlockSpec((None, tm, tk), lambda g, i, j, k: (g, i, k))
        b_spec = pl.BlockSpec((None, tn, tk), lambda g, i, j, k: (g, j, k))
        if slots:
            b_spec = pl.BlockSpec((None, tn, tk), lambda g, i, j, k: (k // (n // tk), j, k % (n // tk)))
        dims = (((1,), (1,)), ((), ()))

    def product(a_ref, b_ref):
        return lax.dot_general(a_ref[...].astype(BF16), b_ref[...].astype(BF16), dims, preferred_element_type=F32)

    def body_one(a_ref, b_ref, o_ref):
        o_ref[...] = product(a_ref, b_ref).astype(out_dtype)

    def body_acc(a_ref, b_ref, o_ref, acc_ref):
        k = pl.program_id(3)

        @pl.when(k == 0)
        def _():
            acc_ref[...] = product(a_ref, b_ref)

        @pl.when(jnp.logical_and(k > 0, k < nk - 1))
        def _():
            acc_ref[...] += product(a_ref, b_ref)

        @pl.when(k == nk - 1)
        def _():
            o_ref[...] = (acc_ref[...] + product(a_ref, b_ref)).astype(out_dtype)

    (out,), carried = _call(
        body_one if nk == 1 else body_acc, name, [a, b], [a_spec, b_spec],
        [jax.ShapeDtypeStruct(o_shape, out_dtype)], [o_spec], (G, M // tm, N // tn, nk),
        [] if nk == 1 else [pltpu.VMEM((tm, tn), F32)], ("parallel", "parallel", "parallel", "arbitrary"), rider)
    out = out[0] if squeeze and not (slots and mode == 'tn') else out
    return out if rider is None else (out, carried)


def rows(ts, width, colblk=0):
    return pl.BlockSpec((ts, width), lambda i: (i, colblk))


def whole(shape):
    return pl.BlockSpec(shape, lambda i: (0,) * len(shape))


def rowwise(name, fn, S, ts, ins, outs, accs=(), scratch=(), reverse=False):
    n_in, n_out, n_acc = len(ins), len(outs), len(accs)
    nt = S // ts

    def body(*refs):
        step = pl.program_id(0)
        i = nt - 1 - step if reverse else step
        in_refs = refs[:n_in]
        out_refs = refs[n_in:n_in + n_out]
        acc_refs = refs[n_in + n_out:n_in + n_out + n_acc]
        scr = refs[n_in + n_out + n_acc:]

        @pl.when(step == 0)
        def _():
            for r in acc_refs:
                r[...] = jnp.zeros_like(r)

        fn(i, step, in_refs, out_refs, acc_refs, scr)

    def fix(spec):
        if not reverse:
            return spec
        imap = spec.index_map
        return pl.BlockSpec(spec.block_shape, lambda s: imap(nt - 1 - s))

    res = _pcall(
        body, name=name,
        out_shape=[jax.ShapeDtypeStruct(s, d) for s, d, _ in outs] + [jax.ShapeDtypeStruct(s, d) for s, d in accs],
        grid=(nt,),
        in_specs=[fix(sp) for _, sp in ins],
        out_specs=[fix(sp) for _, _, sp in outs] + [whole(s) for s, _ in accs],
        scratch_shapes=list(scratch),
        compiler_params=_cparams("arbitrary"),
    )(*[a for a, _ in ins])
    return res


def _rstd(x):
    return lax.rsqrt(jnp.mean(x * x, axis=-1, keepdims=True) + EPS)


def _rms_bwd(dy, x, g):
    r = _rstd(x)
    xh = x * r
    gdy = dy * g
    dx = r * (gdy - xh * jnp.mean(xh * gdy, axis=-1, keepdims=True))
    return dx, jnp.sum(dy * xh, axis=0, keepdims=True)


def _rope(v, cos_t, sin_a, sin_b, sign):
    return v * cos_t + sign * (pltpu.roll(v, LANE - ROPE_DIM // 2, axis=1) * sin_a
                               + pltpu.roll(v, ROPE_DIM // 2, axis=1) * sin_b)


def rms_fwd(name, h, gain, S, D, ts):
    def fn(i, step, ins, outs, accs, scr):
        x = ins[0][...]
        outs[0][...] = (x * _rstd(x) * ins[1][...]).astype(BF16)
    return rowwise(name, fn, S, ts, [(h, rows(ts, D)), (gain, whole((1, D)))], [((S, D), BF16, rows(ts, D))])[0]


def post_fwd(name, h, out, post_g, ple_g, S, D, ts):
    def fn(i, step, ins, outs, accs, scr):
        o = ins[1][...]
        h1 = ins[0][...] + o * _rstd(o) * ins[2][...]
        outs[0][...] = h1
        outs[1][...] = (h1 * _rstd(h1) * ins[3][...]).astype(BF16)
    return rowwise(name, fn, S, ts,
                   [(h, rows(ts, D)), (out, rows(ts, D)), (post_g, whole((1, D))), (ple_g, whole((1, D)))],
                   [((S, D), F32, rows(ts, D)), ((S, D), BF16, rows(ts, D))])


def out_post_fwd(name, y, w_out, h, post_g, ple_g, S, D, ts):
    E = y.shape[1]

    def fn(i, step, ins, outs, accs, scr):
        o = lax.dot_general(ins[0][...], ins[1][...], _NN, preferred_element_type=F32)
        h1 = ins[2][...] + o * _rstd(o) * ins[3][...]
        outs[0][...] = o
        outs[1][...] = h1
        outs[2][...] = (h1 * _rstd(h1) * ins[4][...]).astype(BF16)
    return rowwise(name, fn, S, ts,
                   [(y, rows(ts, E)), (w_out, whole((E, D))), (h, rows(ts, D)), (post_g, whole((1, D))),
                    (ple_g, whole((1, D)))],
                   [((S, D), F32, rows(ts, D)), ((S, D), F32, rows(ts, D)), ((S, D), BF16, rows(ts, D))])


def ple_fwd(name, a, w_gate, p, w_proj, h1, next_gain=None, target=None):
    S, D = h1.shape
    PD = p.shape[1]
    tm = _pick(S, (256, 128))
    last = target is not None

    def body(a_ref, wg_ref, p_ref, wp_ref, h1_ref, x_ref, b1_ref, b2_ref, o1_ref, o2_ref):
        i = pl.program_id(0)
        gl = lax.dot_general(a_ref[...], wg_ref[...], _NN, preferred_element_type=F32)
        pp = lax.dot_general(p_ref[...].astype(BF16), wp_ref[...], _NN, preferred_element_type=F32)
        h = h1_ref[...] + pp * _sigmoid(gl)
        if last:
            e = h - x_ref[...]
            dh = e * (1.0 / D)
            o1_ref[...] = dh
            b1_ref[...], b2_ref[...] = _ple_bwd(dh, pp, gl)

            @pl.when(i == 0)
            def _():
                o2_ref[...] = jnp.zeros_like(o2_ref)

            o2_ref[...] += jnp.broadcast_to(jnp.sum(e * e), (1, LANE))
        else:
            b1_ref[...] = gl.astype(BF16)
            b2_ref[...] = pp.astype(BF16)
            o1_ref[...] = h
            o2_ref[...] = (h * _rstd(h) * x_ref[...]).astype(BF16)

    row = lambda w: pl.BlockSpec((tm, w), lambda i: (i, 0))
    res, _ = _call(
        body, name, [a, w_gate, p, w_proj, h1, target if last else next_gain],
        [row(D), whole((D, D)), row(PD), whole((PD, D)), row(D), row(D) if last else whole((1, D))],
        [jax.ShapeDtypeStruct((S, D), BF16), jax.ShapeDtypeStruct((S, D), BF16), jax.ShapeDtypeStruct((S, D), F32),
         jax.ShapeDtypeStruct((1, LANE), F32) if last else jax.ShapeDtypeStruct((S, D), BF16)],
        [row(D), row(D), row(D), whole((1, LANE)) if last else row(D)], (S // tm,), [],
        ("arbitrary",) if last else ("parallel",))
    return res


def _ple_bwd(dh, pp, gl):
    gate = _sigmoid(gl)
    return (dh * gate).astype(BF16), (dh * pp * gate * (1.0 - gate)).astype(BF16)


def post_bwd(name, dgl, w_gate, dh, h1, out, post_g, ple_g, S, D, ts):
    def fn(i, step, ins, outs, accs, scr):
        da = lax.dot_general(ins[0][...], ins[1][...], _NT, preferred_element_type=F32)
        dx, dple = _rms_bwd(da, ins[3][...], ins[6][...])
        dh1 = ins[2][...] + dx
        dout, dpost = _rms_bwd(dh1, ins[4][...], ins[5][...])
        outs[0][...] = dh1
        outs[1][...] = dout.astype(BF16)
        accs[0][...] += dpost
        accs[1][...] += dple
    return rowwise(name, fn, S, ts,
                   [(dgl, rows(ts, D)), (w_gate, whole((D, D))), (dh, rows(ts, D)), (h1, rows(ts, D)),
                    (out, rows(ts, D)), (post_g, whole((1, D))), (ple_g, whole((1, D)))],
                   [((S, D), F32, rows(ts, D)), ((S, D), BF16, rows(ts, D))],
                   accs=[((1, D), F32), ((1, D), F32)])


def pre_bwd(name, dxn, dh1, h, pre_g, S, D, ts, below=None):
    def fn(i, step, ins, outs, accs, scr):
        dx, dpre = _rms_bwd(ins[0][...], ins[2][...], ins[3][...])
        dh = ins[1][...] + dx
        outs[0][...] = dh
        accs[0][...] += dpre
        if below is not None:
            outs[1][...], outs[2][...] = _ple_bwd(dh, ins[4][...].astype(F32), ins[5][...].astype(F32))
    more = [] if below is None else [(below[0], rows(ts, D)), (below[1], rows(ts, D))]
    return rowwise(name, fn, S, ts,
                   [(dxn, rows(ts, D)), (dh1, rows(ts, D)), (h, rows(ts, D)), (pre_g, whole((1, D)))] + more,
                   [((S, D), F32, rows(ts, D))] + [((S, D), BF16, rows(ts, D))] * len(more), accs=[((1, D), F32)])


def _window_sums(ext, w, back):
    n = ext.shape[0]
    s, win = ext, 1
    while win < w:
        s = s + pltpu.roll(s, win if back else n - win, axis=0)
        win *= 2
    return s


def pool_fwd(name, z, S, E, NG, ts):
    G = E // NG

    def fn(i, step, ins, outs, accs, scr):
        carry = scr[0]

        @pl.when(step == 0)
        def _():
            carry[...] = jnp.zeros_like(carry)

        t = i * ts + lax.broadcasted_iota(jnp.int32, (ts, 1), 0)
        for j, w in enumerate(POOL_WINDOWS):
            u = ins[0][:, j * G:(j + 1) * G]
            ext = jnp.concatenate([carry[:, j * G:(j + 1) * G], u], axis=0)
            sw = _window_sums(ext, w, True)[POOL_HALO:, :]
            cnt = jnp.minimum(t + 1, w).astype(F32)
            outs[0][j] = (sw / cnt - u).astype(BF16)
        carry[...] = ins[0][ts - POOL_HALO:, :]

    return rowwise(name, fn, S, ts, [(z, rows(ts, E, 0))],
                   [((NG, S, G), BF16, pl.BlockSpec((NG, ts, G), lambda i: (0, i, 0)))],
                   scratch=[pltpu.VMEM((POOL_HALO, E), F32)])[0]


def pool_bwd(name, dpooled, dg, S, E, NG, ts):
    G = E // NG

    def fn(i, step, ins, outs, accs, scr):
        carry = scr[0]

        @pl.when(step == 0)
        def _():
            carry[...] = jnp.zeros_like(carry)

        t = i * ts + lax.broadcasted_iota(jnp.int32, (ts, 1), 0)
        for j, w in enumerate(POOL_WINDOWS):
            d = ins[0][j]
            e = d / jnp.minimum(t + 1, w).astype(F32)
            ext = jnp.concatenate([e, carry[:, j * G:(j + 1) * G]], axis=0)
            sw = _window_sums(ext, w, False)[:ts, :]
            outs[0][:, j * G:(j + 1) * G] = (sw - d).astype(BF16)
            carry[:, j * G:(j + 1) * G] = e[:POOL_HALO, :]
        outs[0][:, E:] = ins[1][...]

    return rowwise(name, fn, S, ts,
                   [(dpooled, pl.BlockSpec((NG, ts, G), lambda i: (0, i, 0))), (dg, rows(ts, E))],
                   [((S, 2 * E), BF16, rows(ts, 2 * E))],
                   scratch=[pltpu.VMEM((POOL_HALO, E), F32)], reverse=True)[0]


def pool_group_fwd(name, pooled, w_group, z, scale, S, E, NG):
    G = E // NG
    tm = _pick(S, (1024, 512, 256, 128))

    def body(a_ref, b_ref, g_ref, sc_ref, y_ref, mx_ref):
        mx = lax.dot_general(a_ref[...], b_ref[...], _NN, preferred_element_type=F32)
        g = g_ref[...]
        y_ref[...] = (mx * sc_ref[...] * (g * _sigmoid(g))).astype(BF16)
        mx_ref[...] = mx.astype(BF16)

    grp = pl.BlockSpec((None, tm, G), lambda j, i: (j, i, 0))
    res, _ = _call(
        body, name, [pooled, w_group, z, scale],
        [grp, pl.BlockSpec((None, G, G), lambda j, i: (j, 0, 0)), pl.BlockSpec((tm, G), lambda j, i: (i, NG + j)),
         pl.BlockSpec((1, G), lambda j, i: (0, j))],
        [jax.ShapeDtypeStruct((S, E), BF16), jax.ShapeDtypeStruct((NG, S, G), BF16)],
        [pl.BlockSpec((tm, G), lambda j, i: (i, j)), grp], (NG, S // tm), [], ("parallel", "parallel"))
    return res


def pool_out_bwd(name, dout, w_out, mixed, z, scale, S, E, NG, rider=None):
    G, D = E // NG, dout.shape[1]
    tm = _pick(S, (512, 256, 128))

    def body(d_ref, w_ref, mx_ref, g_ref, sc_ref, dmx_ref, dg_ref, dsc_ref):
        i = pl.program_id(1)
        d = lax.dot_general(d_ref[...], w_ref[...], _NT, preferred_element_type=F32)
        mx, g, sc = mx_ref[...].astype(F32), g_ref[...], sc_ref[...]
        sg = _sigmoid(g)
        si = g * sg
        dmx_ref[...] = (d * sc * si).astype(BF16)
        dg_ref[...] = (d * mx * sc * (sg * (1.0 + g * (1.0 - sg)))).astype(BF16)

        @pl.when(i == 0)
        def _():
            dsc_ref[...] = jnp.zeros_like(dsc_ref)

        dsc_ref[...] += jnp.sum(d * mx * si, axis=0, keepdims=True)

    grp = pl.BlockSpec((None, tm, G), lambda j, i: (j, i, 0))
    return _call(
        body, name, [dout, w_out, mixed, z, scale],
        [pl.BlockSpec((tm, D), lambda j, i: (i, 0)), pl.BlockSpec((G, D), lambda j, i: (j, 0)), grp,
         pl.BlockSpec((tm, G), lambda j, i: (i, NG + j)), pl.BlockSpec((1, G), lambda j, i: (0, j))],
        [jax.ShapeDtypeStruct((NG, S, G), BF16), jax.ShapeDtypeStruct((S, E), BF16), jax.ShapeDtypeStruct((1, E), F32)],
        [grp, pl.BlockSpec((tm, G), lambda j, i: (i, j)), pl.BlockSpec((1, G), lambda j, i: (0, j))],
        (NG, S // tm), [], ("parallel", "arbitrary"), rider)


def mla_prep_fwd(name, z, qg, kvg, tabs, S, E, R, ts):
    qb, kb, pb = E // R, E // R + 1, (E + 2 * R) // LANE

    def fn(i, step, ins, outs, accs, scr):
        zq, zkv = ins[0][...], ins[1][...]
        outs[0][...] = (zq * _rstd(zq) * ins[3][...]).astype(BF16)
        outs[1][...] = (zkv * _rstd(zkv) * ins[4][...]).astype(BF16)
        outs[2][...] = _rope(ins[2][...], ins[5][...], ins[6][...], ins[7][...], 1.0)

    return rowwise(name, fn, S, ts,
                   [(z, rows(ts, R, qb)), (z, rows(ts, R, kb)), (z, rows(ts, LANE, pb)),
                    (qg, whole((1, R))), (kvg, whole((1, R)))] + [(t, rows(ts, LANE)) for t in tabs],
                   [((S, R), BF16, rows(ts, R)), ((S, R), BF16, rows(ts, R)), ((S, LANE), F32, rows(ts, LANE))])


def mla_up_fwd(name, qn, kvn, w_uq, w_uk, w_uv, kper, tabs, S, H, R, scale):
    hc = _pick(H, (4, 2, 1))
    tm = _pick(S, (1024, 512, 256, 128))

    def body(qn_ref, kvn_ref, wq_ref, wk_ref, wv_ref, kper_ref, cos_ref, sa_ref, sb_ref, q_ref, k_ref, v_ref):
        cos_t, sin_a, sin_b = cos_ref[...], sa_ref[...], sb_ref[...]
        kvn_t = kvn_ref[...]
        q = lax.dot_general(qn_ref[...], wq_ref[...], _NN, preferred_element_type=F32)
        k = lax.dot_general(kvn_t, wk_ref[...], _NN, preferred_element_type=F32)
        v_ref[...] = lax.dot_general(kvn_t, wv_ref[...], _NN, preferred_element_type=F32).astype(BF16)
        kp = kper_ref[...].astype(BF16)
        for h in range(hc):
            a, b, c = h * HEAD_PAD, h * HEAD_PAD + NOPE_DIM, (h + 1) * HEAD_PAD
            q_ref[:, a:b] = (q[:, a:b] * scale).astype(BF16)
            q_ref[:, b:c] = (_rope(q[:, b:c], cos_t, sin_a, sin_b, 1.0) * scale).astype(BF16)
            k_ref[:, a:b] = k[:, a:b].astype(BF16)
            k_ref[:, b:c] = kp

    row = lambda w: pl.BlockSpec((tm, w), lambda i, j: (i, 0))
    col = lambda w: pl.BlockSpec((R, w), lambda i, j: (0, j))
    out = lambda w: pl.BlockSpec((tm, w), lambda i, j: (i, j))
    W = H * HEAD_PAD
    res, _ = _call(
        body, name, [qn, kvn, w_uq, w_uk, w_uv, kper, *tabs],
        [row(R), row(R), col(hc * HEAD_PAD), col(hc * HEAD_PAD), col(hc * V_DIM), row(LANE), row(LANE), row(LANE),
         row(LANE)],
        [jax.ShapeDtypeStruct((S, W), BF16), jax.ShapeDtypeStruct((S, W), BF16),
         jax.ShapeDtypeStruct((S, H * V_DIM), BF16)],
        [out(hc * HEAD_PAD), out(hc * HEAD_PAD), out(hc * V_DIM)], (S // tm, H // hc), [], ("parallel", "parallel"))
    return res


def mla_gate_bwd(name, dy, o, z, S, E, ts):
    def fn(i, step, ins, outs, accs, scr):
        d, ov, g = ins[0][...], ins[1][...], ins[2][...]
        sg = _sigmoid(g)
        outs[0][...] = (d * (g * sg)).astype(BF16)
        outs[1][...] = (d * ov * (sg * (1.0 + g * (1.0 - sg)))).astype(BF16)

    return rowwise(name, fn, S, ts, [(dy, rows(ts, E)), (o, rows(ts, E)), (z, rows(ts, E, 0))],
                   [((S, E), BF16, rows(ts, E)), ((S, E), BF16, rows(ts, E))])


def mla_unpack_q_bwd(name, dqt, tabs, S, H, t, scale):
    W = H * HEAD_PAD

    def fn(i, step, ins, outs, accs, scr):
        cos_t, sin_a, sin_b = ins[1][...], ins[2][...], ins[3][...]
        for h in range(H):
            a, b, c = h * HEAD_PAD, h * HEAD_PAD + NOPE_DIM, (h + 1) * HEAD_PAD
            dq = ins[0][h].T
            outs[0][:, a:b] = (dq[:, :NOPE_DIM] * scale).astype(BF16)
            outs[0][:, b:c] = (_rope(dq[:, NOPE_DIM:], cos_t, sin_a, sin_b, -1.0) * scale).astype(BF16)

    return rowwise(name, fn, S, t,
                   [(dqt, pl.BlockSpec((H, None, HEAD_PAD, t), lambda i: (0, i, 0, 0)))]
                   + [(tb, rows(t, LANE)) for tb in tabs],
                   [((S, W), BF16, rows(t, W))])[0]


def mla_unpack_k_bwd(name, dk, S, H, ts):
    W = H * HEAD_PAD

    def fn(i, step, ins, outs, accs, scr):
        dkpe = jnp.zeros((ts, LANE), F32)
        for h in range(H):
            a, b, c = h * HEAD_PAD, h * HEAD_PAD + NOPE_DIM, (h + 1) * HEAD_PAD
            outs[0][:, a:b] = ins[0][:, a:b].astype(BF16)
            outs[0][:, b:c] = jnp.zeros((ts, LANE), BF16)
            dkpe = dkpe + ins[0][:, b:c]
        outs[1][...] = dkpe

    return rowwise(name, fn, S, ts, [(dk, rows(ts, W))],
                   [((S, W), BF16, rows(ts, W)), ((S, LANE), F32, rows(ts, LANE))])


def mla_prep_bwd(name, dqn, dkvn_k, dkvn_v, z, dkpe, dg, qg, kvg, tabs, S, E, R, ts):
    qb, kb = E // R, E // R + 1
    ZW = E + 2 * R + LANE

    def fn(i, step, ins, outs, accs, scr):
        dzq, dqg = _rms_bwd(ins[0][...], ins[3][...], ins[7][...])
        dzkv, dkvg = _rms_bwd(ins[1][...] + ins[2][...], ins[4][...], ins[8][...])
        outs[0][:, :E] = ins[6][...]
        outs[0][:, E:E + R] = dzq.astype(BF16)
        outs[0][:, E + R:E + 2 * R] = dzkv.astype(BF16)
        outs[0][:, E + 2 * R:] = _rope(ins[5][...], ins[9][...], ins[10][...], ins[11][...], -1.0).astype(BF16)
        accs[0][...] += dqg
        accs[1][...] += dkvg

    return rowwise(name, fn, S, ts,
                   [(dqn, rows(ts, R)), (dkvn_k, rows(ts, R)), (dkvn_v, rows(ts, R)), (z, rows(ts, R, qb)),
                    (z, rows(ts, R, kb)), (dkpe, rows(ts, LANE)), (dg, rows(ts, E)),
                    (qg, whole((1, R))), (kvg, whole((1, R)))] + [(t, rows(ts, LANE)) for t in tabs],
                   [((S, ZW), BF16, rows(ts, ZW))], accs=[((1, R), F32), ((1, R), F32)])


_NT = (((1,), (1,)), ((), ()))
_NN = (((1,), (0,)), ((), ()))
_TN = (((0,), (0,)), ((), ()))


def _causal_mask_t(t):
    return lax.broadcasted_iota(jnp.int32, (t, t), 0) <= lax.broadcasted_iota(jnp.int32, (t, t), 1)


def _tile(i, t):
    return pl.ds(pl.multiple_of(i * t, t), t)


def flash_fwd(name, q, k, v, z, S, H, t, rider=None):
    nt = S // t

    def body(q_ref, k_ref, v_ref, g_ref, o_ref, y_ref, lse_ref, m_sc, l_sc, acc_sc):
        i = pl.program_id(1)
        m_sc[...] = jnp.full_like(m_sc, NEG_INF)
        l_sc[...] = jnp.zeros_like(l_sc)
        acc_sc[...] = jnp.zeros_like(acc_sc)
        q = q_ref[...]

        def tile(j, diag):
            s = lax.dot_general(k_ref[_tile(j, t), :], q, _NT, preferred_element_type=F32)
            if diag:
                s = jnp.where(_causal_mask_t(t), s, NEG_INF)
            m_prev = m_sc[...]
            m_new = jnp.maximum(m_prev, jnp.max(s, axis=0, keepdims=True))
            alpha = jnp.exp(m_prev - m_new)
            p = jnp.exp(s - m_new)
            l_sc[...] = alpha * l_sc[...] + jnp.sum(p, axis=0, keepdims=True)
            acc_sc[...] = alpha * acc_sc[...] + lax.dot_general(v_ref[_tile(j, t), :], p.astype(BF16), _TN,
                                                                 preferred_element_type=F32)
            m_sc[...] = m_new

        def off_diagonal(j, carry):
            tile(j, False)
            return carry

        lax.fori_loop(0, i, off_diagonal, 0)
        tile(i, True)
        l = l_sc[...]
        o = (acc_sc[...] / l).T
        g = g_ref[...]
        o_ref[...] = o
        y_ref[...] = (o * (g * _sigmoid(g))).astype(BF16)
        lse_ref[...] = m_sc[...] + jnp.log(l)

    qtile = pl.BlockSpec((t, V_DIM), lambda h, i: (i, h))
    (o, y, lse), carried = _call(
        body, name, [q, k, v, z],
        [pl.BlockSpec((t, HEAD_PAD), lambda h, i: (i, h)), pl.BlockSpec((S, HEAD_PAD), lambda h, i: (0, h)),
         pl.BlockSpec((S, V_DIM), lambda h, i: (0, h)), qtile],
        [jax.ShapeDtypeStruct((S, H * V_DIM), F32), jax.ShapeDtypeStruct((S, H * V_DIM), BF16),
         jax.ShapeDtypeStruct((H, nt, 1, t), F32)],
        [qtile, qtile, pl.BlockSpec((None, None, 1, t), lambda h, i: (h, i, 0, 0))],
        (H, nt), [pltpu.VMEM((1, t), F32), pltpu.VMEM((1, t), F32), pltpu.VMEM((V_DIM, t), F32)],
        ("parallel", "parallel"), rider)
    return o, y, lse, carried


def flash_bwd(name, q, k, v, do, o, lse, S, H, t, rider=None):
    nt = S // t

    def body(q_ref, k_ref, v_ref, do_ref, o_ref, lse_ref, dq_ref, dk_ref, dv_ref, kt_sc, dl_sc, dv_sc):
        j = pl.program_id(1)

        @pl.when(j == 0)
        def _():
            dq_ref[...] = jnp.zeros_like(dq_ref)
            ones = jnp.ones((8, V_DIM), BF16)
            for i in range(nt):
                x = do_ref[i * t:(i + 1) * t, :].astype(F32) * o_ref[i * t:(i + 1) * t, :]
                hi = x.astype(BF16)
                lo = (x - hi.astype(F32)).astype(BF16)
                dl_sc[i] = (lax.dot_general(ones, hi, _NT, preferred_element_type=F32)
                            + lax.dot_general(ones, lo, _NT, preferred_element_type=F32))

        kj, vj = k_ref[...], v_ref[...]
        kt_sc[...] = kj.astype(F32).T.astype(BF16)
        dk_ref[...] = jnp.zeros_like(dk_ref)
        dv_sc[...] = jnp.zeros_like(dv_sc)

        def tile(i, diag):
            qi, doi = q_ref[_tile(i, t), :], do_ref[_tile(i, t), :]
            s = lax.dot_general(kj, qi, _NT, preferred_element_type=F32)
            p = jnp.exp(s - lse_ref[i])
            if diag:
                p = jnp.where(_causal_mask_t(t), p, 0.0)
            dv_sc[...] += lax.dot_general(p.astype(BF16), doi, _NN, preferred_element_type=F32)
            dp = lax.dot_general(vj, doi, _NT, preferred_element_type=F32)
            ds = (p * (dp - dl_sc[i, 0:1, :])).astype(BF16)
            dk_ref[...] += lax.dot_general(ds, qi, _NN, preferred_element_type=F32)
            dq_ref[i] += lax.dot_general(kt_sc[...], ds, _NN, preferred_element_type=F32)

        def off_diagonal(i, carry):
            tile(i, False)
            return carry

        tile(j, True)
        lax.fori_loop(j + 1, nt, off_diagonal, 0)
        dv_ref[...] = dv_sc[...].astype(BF16)

    head = lambda w: pl.BlockSpec((S, w), lambda h, j: (0, h))
    ktile = lambda w: pl.BlockSpec((t, w), lambda h, j: (j, h))
    (dq, dk, dv), carried = _call(
        body, name, [q, k, v, do, o, lse],
        [head(HEAD_PAD), ktile(HEAD_PAD), ktile(V_DIM), head(V_DIM), head(V_DIM),
         pl.BlockSpec((None, nt, 1, t), lambda h, j: (h, 0, 0, 0))],
        [jax.ShapeDtypeStruct((H, nt, HEAD_PAD, t), F32), jax.ShapeDtypeStruct((S, H * HEAD_PAD), F32),
         jax.ShapeDtypeStruct((S, H * V_DIM), BF16)],
        [pl.BlockSpec((None, nt, HEAD_PAD, t), lambda h, j: (h, 0, 0, 0)), ktile(HEAD_PAD), ktile(V_DIM)],
        (H, nt), [pltpu.VMEM((HEAD_PAD, t), BF16), pltpu.VMEM((nt, 8, t), F32), pltpu.VMEM((t, V_DIM), F32)],
        ("parallel", "arbitrary"), rider)
    return dq, dk, dv, carried


def _peers():
    x, y, c = lax.axis_index("x"), lax.axis_index("y"), lax.axis_index("c")
    me = 4 * x + 2 * y + c
    peers = []
    for fx, fy, fc in ((0, 0, 1), (1, 0, 0), (0, 1, 0), (1, 1, 0), (1, 0, 1), (0, 1, 1), (1, 1, 1)):
        px, py, pc = x ^ fx, y ^ fy, c ^ fc
        peers.append(((px, py, pc), 4 * px + 2 * py + pc))
    return me, peers


def _hbm_specs(n):
    return [pl.BlockSpec(memory_space=pl.ANY)] * n


class Rider:
    def __init__(self, arrs, gather, windows=None):
        self.arrs, self.gather, self.n = list(arrs), gather, len(arrs)
        self.windows = list(windows) if windows is not None else [None] * self.n
        assert not (gather and any(w is not None for w in self.windows))
        self.out_shapes = [
            jax.ShapeDtypeStruct((N_DEV,) + a.shape if gather else
                                 a.shape if w is None else (N_DEV, w[1]) + a.shape[2:], a.dtype)
            for a, w in zip(arrs, self.windows)]
        self.sems = [pltpu.SemaphoreType.DMA((self.n, N_DEV - 1)), pltpu.SemaphoreType.DMA((self.n, N_DEV - 1)),
                     pltpu.SemaphoreType.DMA((self.n,))]

    def _copies(self, srcs, dsts, sems):
        send_sems, recv_sems, local_sems = sems
        x, y, c = lax.axis_index("x"), lax.axis_index("y"), lax.axis_index("c")
        ident = lambda d: 4 * d[0] + 2 * d[1] + d[2]
        me, sibling = (x, y, c), (x, y, 1 - c)
        chips = [(1 - x, y), (x, 1 - y), (1 - x, 1 - y)]
        _, peers = _peers()

        def slot(a, pid):
            w = self.windows[a]
            return srcs[a].at[pid] if w is None else srcs[a].at[pid, pl.ds(w[0], w[1])]

        def remote(a, k, incoming):
            if not self.gather:
                target, pid = peers[k]
                src, block = slot(a, pid), (pid if incoming else ident(me))
            elif k == 0:
                target, src, block = sibling, srcs[a], ident(sibling if incoming else me)
            elif k <= 3:
                target = (*chips[k - 1], c)
                src, block = srcs[a], ident(target if incoming else me)
            else:
                landed = ident((*chips[k - 4], c))
                target, src = sibling, dsts[a].at[landed]
                block = ident((*chips[k - 4], 1 - c)) if incoming else landed
            return pltpu.make_async_remote_copy(
                src_ref=src, dst_ref=dsts[a].at[block], send_sem=send_sems.at[a, k], recv_sem=recv_sems.at[a, k],
                device_id=target, device_id_type=MESH)

        def local(a):
            return pltpu.make_async_copy(srcs[a] if self.gather else slot(a, ident(me)), dsts[a].at[ident(me)],
                                         local_sems.at[a])

        return local, remote

    def start(self, srcs, dsts, sems):
        local, remote = self._copies(srcs, dsts, sems)
        for a in range(self.n):
            local(a).start()
            for k in range(4 if self.gather else N_DEV - 1):
                remote(a, k, False).start()

    def relay(self, srcs, dsts, sems):
        if not self.gather:
            return
        local, remote = self._copies(srcs, dsts, sems)
        for a in range(self.n):
            for k in range(1, 4):
                remote(a, k, True).wait_recv()
                remote(a, k + 3, False).start()

    def wait(self, srcs, dsts, sems):
        local, remote = self._copies(srcs, dsts, sems)
        for a in range(self.n):
            for k in range(N_DEV - 1):
                if not (self.gather and 1 <= k <= 3):
                    remote(a, k, True).wait_recv()
        for a in range(self.n):
            for k in range(N_DEV - 1):
                remote(a, k, False).wait_send()
            local(a).wait()


def _carry(body, n_in, n_out, rider, grid):
    n = rider.n
    steps = math.prod(grid)

    def wrapped(*refs):
        ins, r_in = refs[:n_in], refs[n_in:n_in + n]
        outs = refs[n_in + n:n_in + n + n_out]
        r_out = refs[n_in + n + n_out:n_in + 2 * n + n_out]
        scratch, sems = refs[n_in + 2 * n + n_out:-3], refs[-3:]
        step = 0
        for d, g in enumerate(grid):
            step = step * g + pl.program_id(d)

        @pl.when(step == 0)
        def _():
            rider.start(r_in, r_out, sems)

        if rider.gather:
            @pl.when(step == (3 * steps) // 4)
            def _():
                rider.relay(r_in, r_out, sems)

        body(*ins, *outs, *scratch)

        @pl.when(step == steps - 1)
        def _():
            rider.wait(r_in, r_out, sems)

    return wrapped


def _call(body, name, ins, in_specs, out_shape, out_specs, grid, scratch, sem, rider=None):
    if rider is None:
        return _pcall(body, name=name, out_shape=list(out_shape), grid=grid, in_specs=list(in_specs),
                      out_specs=list(out_specs), scratch_shapes=list(scratch), compiler_params=_cparams(*sem))(*ins), None
    res = _pcall(
        _carry(body, len(ins), len(out_shape), rider, grid), name=name,
        out_shape=list(out_shape) + rider.out_shapes, grid=grid,
        in_specs=list(in_specs) + _hbm_specs(rider.n), out_specs=list(out_specs) + _hbm_specs(rider.n),
        scratch_shapes=list(scratch) + rider.sems, compiler_params=_cparams(*(("arbitrary",) * len(grid))),
    )(*ins, *rider.arrs)
    return res[:len(out_shape)], res[len(out_shape):]


def exchange(name, rider):
    def body(*refs):
        srcs, dsts, sems = refs[:rider.n], refs[rider.n:2 * rider.n], refs[2 * rider.n:]
        rider.start(srcs, dsts, sems)
        rider.relay(srcs, dsts, sems)
        rider.wait(srcs, dsts, sems)

    return _pcall(body, name=name, out_shape=rider.out_shapes, in_specs=_hbm_specs(rider.n),
                  out_specs=_hbm_specs(rider.n), scratch_shapes=rider.sems)(*rider.arrs)


def adamw(name, gslots, w, m, v, layer=0, prev=None):
    K, R, C = gslots.shape
    per_row = C * (K * gslots.dtype.itemsize + 7 * 4) * 2
    tr = R
    for cand in (1024, 512, 256, 128, 64, 32, 16, 8):
        if R % cand == 0:
            tr = cand
            if cand * per_row <= VMEM_LIMIT_BYTES // 2:
                break
    c1 = 1.0 / (1.0 - ADAM_B1 ** ADAM_STEP)
    c2 = 1.0 / (1.0 - ADAM_B2 ** ADAM_STEP)

    def body(g_ref, w_ref, m_ref, v_ref, *rest):
        go_ref, d_ref, mo_ref, vo_ref = rest[-4:]
        g = g_ref[0].astype(F32)
        for s in range(1, K):
            g = g + g_ref[s].astype(F32)
        mn = ADAM_B1 * m_ref[...] + (1.0 - ADAM_B1) * g
        vn = ADAM_B2 * v_ref[...] + (1.0 - ADAM_B2) * (g * g)
        go_ref[...] = g
        mo_ref[...] = mn
        vo_ref[...] = vn
        d_ref[...] = -ADAM_LR * ((mn * c1) / (jnp.sqrt(vn * c2) + ADAM_EPS) + ADAM_WD * w_ref[...])

    blk = pl.BlockSpec((None, tr, C), lambda i: (layer, i, 0))
    prev = [] if prev is None else list(prev)
    return _pcall(
        body, name=name, out_shape=[jax.ShapeDtypeStruct(w.shape, F32)] * 4, grid=(R // tr,),
        in_specs=[pl.BlockSpec((K, tr, C), lambda i: (0, i, 0)), blk, blk, blk] + _hbm_specs(len(prev)),
        out_specs=[blk] * 4, input_output_aliases={4 + q: q for q in range(len(prev))},
        compiler_params=_cparams("parallel"),
    )(gslots, w, m, v, *prev)


def _from_slots(gathered, ax):
    g = jnp.moveaxis(gathered, 0, ax)
    s = g.shape
    return g.reshape(s[:ax] + (s[ax] * s[ax + 1],) + s[ax + 2:])


def _to_slots(full, ax):
    s = full.shape
    g = full.reshape(s[:ax] + (N_DEV, s[ax] // N_DEV) + s[ax + 1:])
    g = jnp.moveaxis(g, ax, 0)
    return g.reshape(N_DEV, -1, g.shape[-1])


def _rope_tables(pos, S):
    inv_freq = ROPE_THETA ** (-jnp.arange(0, ROPE_DIM, 2, dtype=F32) / ROPE_DIM)
    ang = pos.astype(F32)[:, None] * inv_freq
    cos, sin = jnp.cos(ang), jnp.sin(ang)
    z = jnp.zeros((S, ROPE_DIM // 2), F32)
    cos_t = jnp.concatenate([cos, cos, z, z], axis=1)
    sin_a = jnp.concatenate([-sin, z, z, z], axis=1)
    sin_b = jnp.concatenate([z, sin, z, z], axis=1)
    return cos_t, sin_a, sin_b


def kernel(x, p, positions, pre_norm, post_norm, pool_w_in, pool_w_group, pool_scale, pool_w_out, mla_w_in, mla_q_norm, mla_w_uq, mla_kv_norm, mla_w_ukv, mla_w_out, ple_norm, ple_w_gate, ple_w_proj, loss_target, m_pre_norm, m_post_norm, m_pool_w_in, m_pool_w_group, m_pool_scale, m_pool_w_out, m_mla_w_in, m_mla_q_norm, m_mla_w_uq, m_mla_kv_norm, m_mla_w_ukv, m_mla_w_out, m_ple_norm, m_ple_w_gate, m_ple_w_proj, v_pre_norm, v_post_norm, v_pool_w_in, v_pool_w_group, v_pool_scale, v_pool_w_out, v_mla_w_in, v_mla_q_norm, v_mla_w_uq, v_mla_kv_norm, v_mla_w_ukv, v_mla_w_out, v_ple_norm, v_ple_w_gate, v_ple_w_proj):
    wl = dict(pre_norm=pre_norm, post_norm=post_norm, pool_w_in=pool_w_in, pool_w_group=pool_w_group,
              pool_scale=pool_scale, pool_w_out=pool_w_out, mla_w_in=mla_w_in, mla_q_norm=mla_q_norm,
              mla_w_uq=mla_w_uq, mla_kv_norm=mla_kv_norm, mla_w_ukv=mla_w_ukv, mla_w_out=mla_w_out,
              ple_norm=ple_norm, ple_w_gate=ple_w_gate, ple_w_proj=ple_w_proj)
    ml = dict(pre_norm=m_pre_norm, post_norm=m_post_norm, pool_w_in=m_pool_w_in, pool_w_group=m_pool_w_group,
              pool_scale=m_pool_scale, pool_w_out=m_pool_w_out, mla_w_in=m_mla_w_in, mla_q_norm=m_mla_q_norm,
              mla_w_uq=m_mla_w_uq, mla_kv_norm=m_mla_kv_norm, mla_w_ukv=m_mla_w_ukv, mla_w_out=m_mla_w_out,
              ple_norm=m_ple_norm, ple_w_gate=m_ple_w_gate, ple_w_proj=m_ple_w_proj)
    vl = dict(pre_norm=v_pre_norm, post_norm=v_post_norm, pool_w_in=v_pool_w_in, pool_w_group=v_pool_w_group,
              pool_scale=v_pool_scale, pool_w_out=v_pool_w_out, mla_w_in=v_mla_w_in, mla_q_norm=v_mla_q_norm,
              mla_w_uq=v_mla_w_uq, mla_kv_norm=v_mla_kv_norm, mla_w_ukv=v_mla_w_ukv, mla_w_out=v_mla_w_out,
              ple_norm=v_ple_norm, ple_w_gate=v_ple_w_gate, ple_w_proj=v_ple_w_proj)

    S, D = x.shape[1], x.shape[2]
    L = pre_norm.shape[0]
    E = pool_scale.shape[1]
    NG = pool_w_group.shape[1]
    R = mla_w_uq.shape[1]
    H = D // 128
    EM = H * V_DIM
    PD = p.shape[-1]
    me = 4 * lax.axis_index("x") + 2 * lax.axis_index("y") + lax.axis_index("c")
    ts = min(S, 256)
    tsw = min(S, 128)
    ta = min(S, 512)
    sm_scale = (NOPE_DIM + ROPE_DIM) ** -0.5

    wb = {n: wl[n].astype(BF16) for n in BIG}
    full = {}

    def ag_rider(host):
        return Rider([wb[n][l] for n, l in AG_PLAN[host]], True) if host in AG_PLAN else None

    def ag_done(host, results):
        for (n, l), g in zip(AG_PLAN[host], results):
            full[n, l] = g if n in SLOT_NATIVE else _from_slots(g, SHARD_AXIS[n] - 1)

    small_sh = jnp.concatenate([wl[n].reshape(1, -1) for n in SMALL_SHARD], axis=1)
    g_in0, g_small = exchange("gather_first", Rider([wb['pool_w_in'][0], small_sh], True))
    full['pool_w_in', 0] = g_in0
    nq = mla_q_norm.size
    q_norm = _from_slots(g_small[:, 0, :nq].reshape((N_DEV,) + mla_q_norm.shape), 1)
    kv_norm = _from_slots(g_small[:, 0, nq:].reshape((N_DEV,) + mla_kv_norm.shape), 1)

    def mla_kernel_weights(j):
        w_in = full['mla_w_in', j]
        w_in_k = jnp.concatenate([w_in[:, 2 * R + ROPE_DIM:], w_in[:, :2 * R + ROPE_DIM],
                                  jnp.zeros((D, LANE - ROPE_DIM), BF16)], axis=1)
        w_uq_k = jnp.pad(full['mla_w_uq', j].reshape(R, H, NOPE_DIM + ROPE_DIM),
                         ((0, 0), (0, 0), (0, HEAD_PAD - NOPE_DIM - ROPE_DIM))).reshape(R, H * HEAD_PAD)
        w_ukv = full['mla_w_ukv', j].reshape(R, H, NOPE_DIM + V_DIM)
        w_uk_k = jnp.pad(w_ukv[..., :NOPE_DIM], ((0, 0), (0, 0), (0, HEAD_PAD - NOPE_DIM))).reshape(R, H * HEAD_PAD)
        w_uv_k = w_ukv[..., NOPE_DIM:].reshape(R, H * V_DIM)
        return w_in_k, w_uq_k, w_uk_k, w_uv_k

    def fmm(name, a, b, mode, out_dtype=F32, slots=False):
        if name not in AG_PLAN:
            return mm(name, a, b, mode, out_dtype, slots=slots)
        out, carried = mm(name, a, b, mode, out_dtype, ag_rider(name), slots)
        ag_done(name, carried)
        return out

    mla_w = {}
    tabs = _rope_tables(positions[0], S)

    h = x[0]
    saved = []
    for i in range(L):
        j = i // 2
        sv = dict(h=h)
        if i == 0:
            xn = rms_fwd(f"pre_norm_{i}", h, pre_norm[i:i + 1], S, D, ts)
        sv['xn'] = xn
        if i % 2 == 0:
            z = fmm(f"pool_in_{i}", xn, full['pool_w_in', j], 'nn', slots=True)
            pooled = pool_fwd(f"pool_window_{i}", z, S, E, NG, tsw)
            y, mixed = pool_group_fwd(f"pool_group_{i}", pooled, full['pool_w_group', j], z, pool_scale[j:j + 1],
                                      S, E, NG)
            out = fmm(f"pool_out_{i}", y, full['pool_w_out', j], 'nn')
            sv.update(z=z, pooled=pooled, mixed=mixed, y=y)
        else:
            w_in_k, w_uq_k, w_uk_k, w_uv_k = mla_w[j] = mla_kernel_weights(j)
            z = fmm(f"mla_in_{i}", xn, w_in_k, 'nn')
            qn, kvn, kper = mla_prep_fwd(f"mla_prep_{i}", z, q_norm[j:j + 1], kv_norm[j:j + 1], tabs, S, EM, R, ts)
            qp, kp, vv = mla_up_fwd(f"mla_up_{i}", qn, kvn, w_uq_k, w_uk_k, w_uv_k, kper, tabs, S, H, R, sm_scale)
            o, y, lse, carried = flash_fwd(f"attn_{i}", qp, kp, vv, z, S, H, ta, ag_rider(f"attn_{i}"))
            if carried is not None:
                ag_done(f"attn_{i}", carried)
            out, h1, a = out_post_fwd(f"mla_out_{i}", y, full['mla_w_out', j], h, post_norm[i:i + 1],
                                      ple_norm[i:i + 1], S, D, ts)
            sv.update(z=z, qn=qn, kvn=kvn, qp=qp, kp=kp, vv=vv, o=o, lse=lse, y=y)
        if i % 2 == 0:
            h1, a = post_fwd(f"post_norm_{i}", h, out, post_norm[i:i + 1], ple_norm[i:i + 1], S, D, ts)
        if i < L - 1:
            gl, pp, h, xn = ple_fwd(f"ple_{i}", a, full['ple_w_gate', i], p[i, 0], full['ple_w_proj', i], h1,
                                    next_gain=pre_norm[i + 1:i + 2])
        else:
            gl = pp = None
            dpp, dgl, dh, loss_acc = ple_fwd(f"ple_{i}", a, full['ple_w_gate', i], p[i, 0], full['ple_w_proj', i], h1,
                                           target=loss_target[0])
        sv.update(out=out, h1=h1, a=a, gl=gl, pp=pp)
        saved.append(sv)

    loss = lax.psum(loss_acc[0, 0] * (0.5 / D), ("x", "y", "c"))

    gw = {n: [None] * wl[n].shape[0] for n in WEIGHTS}
    recv = {}

    def rs_rider(host):
        if host not in RS_PLAN:
            return None
        return scatter_rider(RS_PLAN[host])

    def scatter_rider(keys):
        arrs = [gw[n][l] if n in SLOT_NATIVE else _to_slots(gw[n][l], SHARD_AXIS[n] - 1).astype(BF16)
                for n, l, *_ in keys]
        halves = [key[2] if len(key) == 3 else None for key in keys]
        return Rider(arrs, False, [None if h is None else (h * (a.shape[1] // 2), a.shape[1] // 2)
                                   for a, h in zip(arrs, halves)])

    def rs_done(host, results):
        for key, r in zip(RS_PLAN[host], results):
            recv[key] = r

    def bmm(name, a, b, mode, slots=False):
        out_dtype = BF16 if mode == 'tn' else F32
        if name not in RS_PLAN:
            return mm(name, a, b, mode, out_dtype, slots=slots)
        out, carried = mm(name, a, b, mode, out_dtype, rs_rider(name), slots)
        rs_done(name, carried)
        return out

    for i in reversed(range(L)):
        j = i // 2
        sv = saved[i]
        gw['ple_w_proj'][i] = bmm(f"ple_proj_dw_{i}", p[i, 0], dpp, 'tn')
        gw['ple_w_gate'][i] = bmm(f"ple_gate_dw_{i}", sv['a'], dgl, 'tn')
        dh1, dout, dpost, dple = post_bwd(f"post_norm_bwd_{i}", dgl, full['ple_w_gate', i], dh, sv['h1'], sv['out'],
                                          post_norm[i:i + 1], ple_norm[i:i + 1], S, D, ts)
        gw['post_norm'][i], gw['ple_norm'][i] = dpost[0], dple[0]
        xn = sv['xn']
        if i % 2 == 0:
            gw['pool_w_out'][j] = bmm(f"pool_out_dw_{i}", sv['y'], dout, 'tn')
            (dmixed, dg, dscale), carried = pool_out_bwd(
                f"pool_out_dx_{i}", dout, full['pool_w_out', j], sv['mixed'], sv['z'], pool_scale[j:j + 1], S, E, NG,
                rs_rider(f"pool_out_dx_{i}"))
            if carried is not None:
                rs_done(f"pool_out_dx_{i}", carried)
            gw['pool_scale'][j] = dscale[0]
            gw['pool_w_group'][j] = bmm(f"pool_group_dw_{i}", sv['pooled'], dmixed, 'tn')
            dpooled = bmm(f"pool_group_dx_{i}", dmixed, full['pool_w_group', j], 'nt')
            dz = pool_bwd(f"pool_window_bwd_{i}", dpooled, dg, S, E, NG, tsw)
            gw['pool_w_in'][j] = bmm(f"pool_in_dw_{i}", xn, dz, 'tn', slots=True)
            dxn = bmm(f"pool_in_dx_{i}", dz, full['pool_w_in', j], 'nt', slots=True)
        else:
            w_in_k, w_uq_k, w_uk_k, w_uv_k = mla_w[j]
            gw['mla_w_out'][j] = bmm(f"mla_out_dw_{i}", sv['y'], dout, 'tn')
            dy = bmm(f"mla_out_dx_{i}", dout, full['mla_w_out', j], 'nt')
            do, dg = mla_gate_bwd(f"mla_gate_bwd_{i}", dy, sv['o'], sv['z'], S, EM, ts)
            dqt, dkp, dvv, carried = flash_bwd(f"attn_bwd_{i}", sv['qp'], sv['kp'], sv['vv'], do, sv['o'], sv['lse'],
                                               S, H, ta, rs_rider(f"attn_bwd_{i}"))
            if carried is not None:
                rs_done(f"attn_bwd_{i}", carried)
            dq_raw = mla_unpack_q_bwd(f"mla_pack_q_bwd_{i}", dqt, tabs, S, H, ta, sm_scale)
            dk_raw, dkpe = mla_unpack_k_bwd(f"mla_pack_k_bwd_{i}", dkp, S, H, tsw)
            g_uq = bmm(f"mla_uq_dw_{i}", sv['qn'], dq_raw, 'tn')
            g_uk = bmm(f"mla_uk_dw_{i}", sv['kvn'], dk_raw, 'tn')
            g_uv = bmm(f"mla_uv_dw_{i}", sv['kvn'], dvv, 'tn')
            gw['mla_w_uq'][j] = g_uq.reshape(R, H, HEAD_PAD)[:, :, :NOPE_DIM + ROPE_DIM].reshape(R, -1)
            gw['mla_w_ukv'][j] = jnp.concatenate(
                [g_uk.reshape(R, H, HEAD_PAD)[:, :, :NOPE_DIM], g_uv.reshape(R, H, V_DIM)], axis=2).reshape(R, -1)
            dqn = bmm(f"mla_uq_dx_{i}", dq_raw, w_uq_k, 'nt')
            dkvn_k = bmm(f"mla_uk_dx_{i}", dk_raw, w_uk_k, 'nt')
            dkvn_v = bmm(f"mla_uv_dx_{i}", dvv, w_uv_k, 'nt')
            dz, dqg, dkvg = mla_prep_bwd(f"mla_prep_bwd_{i}", dqn, dkvn_k, dkvn_v, sv['z'], dkpe, dg,
                                         q_norm[j:j + 1], kv_norm[j:j + 1], tabs, S, EM, R, ts)
            g_in = bmm(f"mla_in_dw_{i}", xn, dz, 'tn')
            dxn = bmm(f"mla_in_dx_{i}", dz, w_in_k, 'nt')
            gw['mla_q_norm'][j], gw['mla_kv_norm'][j] = dqg[0], dkvg[0]
            gw['mla_w_in'][j] = jnp.concatenate([g_in[:, EM:EM + 2 * R + ROPE_DIM], g_in[:, :EM]], axis=1)
        if i > 0:
            dh, dpp, dgl, dpre = pre_bwd(f"pre_norm_bwd_{i}", dxn, dh1, sv['h'], pre_norm[i:i + 1], S, D, ts,
                                         below=(saved[i - 1]['pp'], saved[i - 1]['gl']))
        else:
            dh, dpre = pre_bwd(f"pre_norm_bwd_{i}", dxn, dh1, sv['h'], pre_norm[i:i + 1], S, D, ts)
        gw['pre_norm'][i] = dpre[0]
    grad_x = dh[None]

    last = exchange("scatter_last", scatter_rider(RS_LAST))
    for key, r in zip(RS_LAST, last):
        recv[key] = r
    for n, l in {key[:2] for key in recv if len(key) == 3}:
        recv[n, l] = jnp.concatenate([recv[n, l, 0], recv[n, l, 1]], axis=1)
    small_names = SMALL_REPL + SMALL_SHARD
    gw = {n: jnp.stack(gw[n]) for n in small_names}
    small_g = jnp.concatenate([gw[n].reshape(1, -1) for n in small_names], axis=1)
    small_all = exchange("gather_small_grads", Rider([small_g], True))[0]

    outs = {}
    for n in BIG:
        shp = wl[n].shape
        three = lambda a: a.reshape(shp[0], -1, shp[-1])
        res = None
        for l in range(shp[0]):
            res = adamw(f"adamw_{n}_{l}", recv[n, l], three(wl[n]), three(ml[n]), three(vl[n]), l, res)
        outs[n] = [a.reshape(shp) for a in res]

    pieces, off = [], 0
    for n in small_names:
        sz = gw[n].size
        g = small_all[:, :, off:off + sz]
        off += sz
        if n in SMALL_SHARD:
            rows_, cols_ = gw[n].shape
            g = lax.dynamic_slice_in_dim(g.reshape(N_DEV, rows_, cols_), me * (cols_ // N_DEV), cols_ // N_DEV, axis=2)
            g = g.reshape(N_DEV, 1, -1)
        pieces.append(g)
    gs = jnp.concatenate(pieces, axis=2)
    flat = lambda d: jnp.concatenate([d[n].reshape(1, -1) for n in small_names], axis=1)
    res = adamw("adamw_small", gs, flat(wl)[None], flat(ml)[None], flat(vl)[None])
    off = 0
    for n in small_names:
        sz = wl[n].size
        outs[n] = [a[0, :, off:off + sz].reshape(wl[n].shape) for a in res]
        off += sz

    return (loss, grad_x, *[outs[n][0] for n in WEIGHTS], *[outs[n][1] for n in WEIGHTS],
            *[outs[n][2] for n in WEIGHTS], *[outs[n][3] for n in WEIGHTS])
```

```python
import math

import jax
import jax.numpy as jnp
from jax import lax
from jax.experimental import pallas as pl
from jax.experimental.pallas import tpu as pltpu

F32 = jnp.float32
BF16 = jnp.bfloat16

N_DEV = 8
EPS = 1e-6
ROPE_THETA = 10000.0
NOPE_DIM = 128
ROPE_DIM = 64
V_DIM = 128
HEAD_PAD = 256
LANE = 128
POOL_WINDOWS = (2, 4, 8, 16)
POOL_HALO = 16
NEG_INF = -1e30
ADAM_LR = 0.001
ADAM_B1 = 0.9
ADAM_B2 = 0.999
ADAM_EPS = 1e-08
ADAM_WD = 0.01
ADAM_STEP = 10
VMEM_LIMIT_BYTES = 56 * 1024 * 1024
MM_MAX_TK = 3200
MESH = pl.DeviceIdType.MESH

SHARD_AXIS = dict(pre_norm=None, post_norm=None, pool_w_in=2, pool_w_group=2, pool_scale=None, pool_w_out=1,
                  mla_w_in=2, mla_q_norm=1, mla_w_uq=2, mla_kv_norm=1, mla_w_ukv=2, mla_w_out=1,
                  ple_norm=None, ple_w_gate=1, ple_w_proj=2)
WEIGHTS = tuple(SHARD_AXIS)
BIG = ('pool_w_in', 'pool_w_group', 'pool_w_out', 'mla_w_in', 'mla_w_uq', 'mla_w_ukv', 'mla_w_out',
       'ple_w_gate', 'ple_w_proj')
SMALL_REPL = ('pre_norm', 'post_norm', 'pool_scale', 'ple_norm')
SMALL_SHARD = ('mla_q_norm', 'mla_kv_norm')
SLOT_NATIVE = ('pool_w_in',)

AG_PLAN = {
    "pool_in_0": [("pool_w_group", 0), ("pool_w_out", 0), ("ple_w_gate", 0), ("ple_w_proj", 0)],
    "pool_out_0": [("mla_w_in", 0), ("mla_w_uq", 0), ("mla_w_ukv", 0)],
    "attn_1": [("mla_w_out", 0), ("ple_w_gate", 1), ("ple_w_proj", 1), ("pool_w_in", 1), ("pool_w_group", 1),
               ("mla_w_in", 1)],
    "pool_in_2": [("pool_w_out", 1), ("ple_w_gate", 2), ("ple_w_proj", 2), ("mla_w_uq", 1), ("mla_w_ukv", 1)],
    "attn_3": [("mla_w_out", 1), ("ple_w_gate", 3), ("ple_w_proj", 3)],
}
RS_PLAN = {
    "attn_bwd_3": [("ple_w_gate", 3), ("ple_w_proj", 3), ("mla_w_out", 1)],
    "mla_in_dw_3": [("mla_w_uq", 1), ("mla_w_ukv", 1)],
    "pool_out_dx_2": [("mla_w_in", 1)],
    "pool_in_dw_2": [("ple_w_gate", 2), ("ple_w_proj", 2)],
    "pool_in_dx_2": [("pool_w_out", 1)],
    "attn_bwd_1": [("pool_w_group", 1), ("pool_w_in", 1), ("ple_w_gate", 1), ("ple_w_proj", 1), ("mla_w_out", 0)],
    "mla_in_dw_1": [("mla_w_uq", 0), ("mla_w_ukv", 0)],
    "pool_out_dw_0": [("mla_w_in", 0)],
    "pool_out_dx_0": [("ple_w_gate", 0), ("ple_w_proj", 0)],
    "pool_group_dx_0": [("pool_w_group", 0)],
    "pool_in_dw_0": [("pool_w_out", 0)],
    "pool_in_dx_0": [("pool_w_in", 0, 0)],
}
RS_LAST = [("pool_w_in", 0, 1)]


def _pcall(body, **kw):
    return pl.pallas_call(body, **kw)


def _cparams(*sem):
    return pltpu.CompilerParams(dimension_semantics=sem, vmem_limit_bytes=VMEM_LIMIT_BYTES)


def _pick(n, cands):
    for c in cands:
        if n % c == 0:
            return c
    return n


def _sigmoid(x):
    return 1.0 / (1.0 + jnp.exp(-x))


def mm(name, a, b, mode, out_dtype=F32, rider=None, slots=False):
    squeeze = a.ndim == 2
    if squeeze:
        a = a[None]
        b = b if slots and mode != 'tn' else b[None]
    G = a.shape[0]
    if mode == 'nn':
        M, K = a.shape[1:]
        N = b.shape[2] * (N_DEV if slots else 1)
    elif mode == 'tn':
        K, M = a.shape[1:]
        N = b.shape[2]
    else:
        M, K = a.shape[1:]
        N = b.shape[1]
    n = (K if mode == 'nt' else N) // N_DEV
    tm = _pick(M, (1024, 512, 256, 128))
    tn = _pick(n if slots and mode != 'nt' else N, (1024, 768, 640, 512, 384, 256, 128))
    if slots and mode == 'nt':
        tk = _pick(n, (2048, 1024, 512, 256, 128))
    else:
        tk = K if K <= MM_MAX_TK else _pick(K, (2048, 1024, 640, 512, 384, 256, 128))
    nk = K // tk
    o_spec = pl.BlockSpec((None, tm, tn), lambda g, i, j, k: (g, i, j))
    o_shape = (G, M, N)
    if mode == 'nn':
        a_spec = pl.BlockSpec((None, tm, tk), lambda g, i, j, k: (g, i, k))
        b_spec = pl.BlockSpec((None, tk, tn), lambda g, i, j, k: (g, k, j))
        if slots:
            b_spec = pl.BlockSpec((None, tk, tn), lambda g, i, j, k: (j // (n // tn), k, j % (n // tn)))
        dims = (((1,), (0,)), ((), ()))
    elif mode == 'tn':
        a_spec = pl.BlockSpec((None, tk, tm), lambda g, i, j, k: (g, k, i))
        b_spec = pl.BlockSpec((None, tk, tn), lambda g, i, j, k: (g, k, j))
        if slots:
            o_spec = pl.BlockSpec((None, tm, tn), lambda g, i, j, k: (j // (n // tn), i, j % (n // tn)))
            o_shape = (N_DEV, M, n)
        dims = (((0,), (0,)), ((), ()))
    else:
        a_spec = pl.BlockSpec((None, tm, tk), lambda g, i, j, k: (g, i, k))
        b_spec = pl.BlockSpec((None, tn, tk), lambda g, i, j, k: (g, j, k))
        if slots:
            b_spec = pl.BlockSpec((None, tn, tk), lambda g, i, j, k: (k // (n // tk), j, k % (n // tk)))
        dims = (((1,), (1,)), ((), ()))

    def product(a_ref, b_ref):
        return lax.dot_general(a_ref[...].astype(BF16), b_ref[...].astype(BF16), dims, preferred_element_type=F32)

    def body_one(a_ref, b_ref, o_ref):
        o_ref[...] = product(a_ref, b_ref).astype(out_dtype)

    def body_acc(a_ref, b_ref, o_ref, acc_ref):
        k = pl.program_id(3)

        @pl.when(k == 0)
        def _():
            acc_ref[...] = product(a_ref, b_ref)

        @pl.when(jnp.logical_and(k > 0, k < nk - 1))
        def _():
            acc_ref[...] += product(a_ref, b_ref)

        @pl.when(k == nk - 1)
        def _():
            o_ref[...] = (acc_ref[...] + product(a_ref, b_ref)).astype(out_dtype)

    (out,), carried = _call(
        body_one if nk == 1 else body_acc, name, [a, b], [a_spec, b_spec],
        [jax.ShapeDtypeStruct(o_shape, out_dtype)], [o_spec], (G, M // tm, N // tn, nk),
        [] if nk == 1 else [pltpu.VMEM((tm, tn), F32)], ("parallel", "parallel", "parallel", "arbitrary"), rider)
    out = out[0] if squeeze and not (slots and mode == 'tn') else out
    return out if rider is None else (out, carried)


def rows(ts, width, colblk=0):
    return pl.BlockSpec((ts, width), lambda i: (i, colblk))


def whole(shape):
    return pl.BlockSpec(shape, lambda i: (0,) * len(shape))


def rowwise(name, fn, S, ts, ins, outs, accs=(), scratch=(), reverse=False):
    n_in, n_out, n_acc = len(ins), len(outs), len(accs)
    nt = S // ts

    def body(*refs):
        step = pl.program_id(0)
        i = nt - 1 - step if reverse else step
        in_refs = refs[:n_in]
        out_refs = refs[n_in:n_in + n_out]
        acc_refs = refs[n_in + n_out:n_in + n_out + n_acc]
        scr = refs[n_in + n_out + n_acc:]

        @pl.when(step == 0)
        def _():
            for r in acc_refs:
                r[...] = jnp.zeros_like(r)

        fn(i, step, in_refs, out_refs, acc_refs, scr)

    def fix(spec):
        if not reverse:
            return spec
        imap = spec.index_map
        return pl.BlockSpec(spec.block_shape, lambda s: imap(nt - 1 - s))

    res = _pcall(
        body, name=name,
        out_shape=[jax.ShapeDtypeStruct(s, d) for s, d, _ in outs] + [jax.ShapeDtypeStruct(s, d) for s, d in accs],
        grid=(nt,),
        in_specs=[fix(sp) for _, sp in ins],
        out_specs=[fix(sp) for _, _, sp in outs] + [whole(s) for s, _ in accs],
        scratch_shapes=list(scratch),
        compiler_params=_cparams("arbitrary"),
    )(*[a for a, _ in ins])
    return res


def _rstd(x):
    return lax.rsqrt(jnp.mean(x * x, axis=-1, keepdims=True) + EPS)


def _rms_bwd(dy, x, g):
    r = _rstd(x)
    xh = x * r
    gdy = dy * g
    dx = r * (gdy - xh * jnp.mean(xh * gdy, axis=-1, keepdims=True))
    return dx, jnp.sum(dy * xh, axis=0, keepdims=True)


def _rope(v, cos_t, sin_a, sin_b, sign):
    return v * cos_t + sign * (pltpu.roll(v, LANE - ROPE_DIM // 2, axis=1) * sin_a
                               + pltpu.roll(v, ROPE_DIM // 2, axis=1) * sin_b)


def rms_fwd(name, h, gain, S, D, ts):
    def fn(i, step, ins, outs, accs, scr):
        x = ins[0][...]
        outs[0][...] = (x * _rstd(x) * ins[1][...]).astype(BF16)
    return rowwise(name, fn, S, ts, [(h, rows(ts, D)), (gain, whole((1, D)))], [((S, D), BF16, rows(ts, D))])[0]


def post_fwd(name, h, out, post_g, ple_g, S, D, ts):
    def fn(i, step, ins, outs, accs, scr):
        o = ins[1][...]
        h1 = ins[0][...] + o * _rstd(o) * ins[2][...]
        outs[0][...] = h1
        outs[1][...] = (h1 * _rstd(h1) * ins[3][...]).astype(BF16)
    return rowwise(name, fn, S, ts,
                   [(h, rows(ts, D)), (out, rows(ts, D)), (post_g, whole((1, D))), (ple_g, whole((1, D)))],
                   [((S, D), F32, rows(ts, D)), ((S, D), BF16, rows(ts, D))])


def out_post_fwd(name, y, w_out, h, post_g, ple_g, S, D, ts):
    E = y.shape[1]

    def fn(i, step, ins, outs, accs, scr):
        o = lax.dot_general(ins[0][...], ins[1][...], _NN, preferred_element_type=F32)
        h1 = ins[2][...] + o * _rstd(o) * ins[3][...]
        outs[0][...] = o
        outs[1][...] = h1
        outs[2][...] = (h1 * _rstd(h1) * ins[4][...]).astype(BF16)
    return rowwise(name, fn, S, ts,
                   [(y, rows(ts, E)), (w_out, whole((E, D))), (h, rows(ts, D)), (post_g, whole((1, D))),
                    (ple_g, whole((1, D)))],
                   [((S, D), F32, rows(ts, D)), ((S, D), F32, rows(ts, D)), ((S, D), BF16, rows(ts, D))])


def ple_fwd(name, a, w_gate, p, w_proj, h1, next_gain=None, target=None):
    S, D = h1.shape
    PD = p.shape[1]
    tm = _pick(S, (256, 128))
    last = target is not None

    def body(a_ref, wg_ref, p_ref, wp_ref, h1_ref, x_ref, b1_ref, b2_ref, o1_ref, o2_ref):
        i = pl.program_id(0)
        gl = lax.dot_general(a_ref[...], wg_ref[...], _NN, preferred_element_type=F32)
        pp = lax.dot_general(p_ref[...].astype(BF16), wp_ref[...], _NN, preferred_element_type=F32)
        h = h1_ref[...] + pp * _sigmoid(gl)
        if last:
            e = h - x_ref[...]
            dh = e * (1.0 / D)
            o1_ref[...] = dh
            b1_ref[...], b2_ref[...] = _ple_bwd(dh, pp, gl)

            @pl.when(i == 0)
            def _():
                o2_ref[...] = jnp.zeros_like(o2_ref)

            o2_ref[...] += jnp.broadcast_to(jnp.sum(e * e), (1, LANE))
        else:
            b1_ref[...] = gl.astype(BF16)
            b2_ref[...] = pp.astype(BF16)
            o1_ref[...] = h
            o2_ref[...] = (h * _rstd(h) * x_ref[...]).astype(BF16)

    row = lambda w: pl.BlockSpec((tm, w), lambda i: (i, 0))
    res, _ = _call(
        body, name, [a, w_gate, p, w_proj, h1, target if last else next_gain],
        [row(D), whole((D, D)), row(PD), whole((PD, D)), row(D), row(D) if last else whole((1, D))],
        [jax.ShapeDtypeStruct((S, D), BF16), jax.ShapeDtypeStruct((S, D), BF16), jax.ShapeDtypeStruct((S, D), F32),
         jax.ShapeDtypeStruct((1, LANE), F32) if last else jax.ShapeDtypeStruct((S, D), BF16)],
        [row(D), row(D), row(D), whole((1, LANE)) if last else row(D)], (S // tm,), [],
        ("arbitrary",) if last else ("parallel",))
    return res


def _ple_bwd(dh, pp, gl):
    gate = _sigmoid(gl)
    return (dh * gate).astype(BF16), (dh * pp * gate * (1.0 - gate)).astype(BF16)


def post_bwd(name, dgl, w_gate, dh, h1, out, post_g, ple_g, S, D, ts):
    def fn(i, step, ins, outs, accs, scr):
        da = lax.dot_general(ins[0][...], ins[1][...], _NT, preferred_element_type=F32)
        dx, dple = _rms_bwd(da, ins[3][...], ins[6][...])
        dh1 = ins[2][...] + dx
        dout, dpost = _rms_bwd(dh1, ins[4][...], ins[5][...])
        outs[0][...] = dh1
        outs[1][...] = dout.astype(BF16)
        accs[0][...] += dpost
        accs[1][...] += dple
    return rowwise(name, fn, S, ts,
                   [(dgl, rows(ts, D)), (w_gate, whole((D, D))), (dh, rows(ts, D)), (h1, rows(ts, D)),
                    (out, rows(ts, D)), (post_g, whole((1, D))), (ple_g, whole((1, D)))],
                   [((S, D), F32, rows(ts, D)), ((S, D), BF16, rows(ts, D))],
                   accs=[((1, D), F32), ((1, D), F32)])


def pre_bwd(name, dxn, dh1, h, pre_g, S, D, ts, below=None):
    def fn(i, step, ins, outs, accs, scr):
        dx, dpre = _rms_bwd(ins[0][...], ins[2][...], ins[3][...])
        dh = ins[1][...] + dx
        outs[0][...] = dh
        accs[0][...] += dpre
        if below is not None:
            outs[1][...], outs[2][...] = _ple_bwd(dh, ins[4][...].astype(F32), ins[5][...].astype(F32))
    more = [] if below is None else [(below[0], rows(ts, D)), (below[1], rows(ts, D))]
    return rowwise(name, fn, S, ts,
                   [(dxn, rows(ts, D)), (dh1, rows(ts, D)), (h, rows(ts, D)), (pre_g, whole((1, D)))] + more,
                   [((S, D), F32, rows(ts, D))] + [((S, D), BF16, rows(ts, D))] * len(more), accs=[((1, D), F32)])


def _window_sums(ext, w, back):
    n = ext.shape[0]
    s, win = ext, 1
    while win < w:
        s = s + pltpu.roll(s, win if back else n - win, axis=0)
        win *= 2
    return s


def pool_fwd(name, z, S, E, NG, ts):
    G = E // NG

    def fn(i, step, ins, outs, accs, scr):
        carry = scr[0]

        @pl.when(step == 0)
        def _():
            carry[...] = jnp.zeros_like(carry)

        t = i * ts + lax.broadcasted_iota(jnp.int32, (ts, 1), 0)
        for j, w in enumerate(POOL_WINDOWS):
            u = ins[0][:, j * G:(j + 1) * G]
            ext = jnp.concatenate([carry[:, j * G:(j + 1) * G], u], axis=0)
            sw = _window_sums(ext, w, True)[POOL_HALO:, :]
            cnt = jnp.minimum(t + 1, w).astype(F32)
            outs[0][j] = (sw / cnt - u).astype(BF16)
        carry[...] = ins[0][ts - POOL_HALO:, :]

    return rowwise(name, fn, S, ts, [(z, rows(ts, E, 0))],
                   [((NG, S, G), BF16, pl.BlockSpec((NG, ts, G), lambda i: (0, i, 0)))],
                   scratch=[pltpu.VMEM((POOL_HALO, E), F32)])[0]


def pool_bwd(name, dpooled, dg, S, E, NG, ts):
    G = E // NG

    def fn(i, step, ins, outs, accs, scr):
        carry = scr[0]

        @pl.when(step == 0)
        def _():
            carry[...] = jnp.zeros_like(carry)

        t = i * ts + lax.broadcasted_iota(jnp.int32, (ts, 1), 0)
        for j, w in enumerate(POOL_WINDOWS):
            d = ins[0][j]
            e = d / jnp.minimum(t + 1, w).astype(F32)
            ext = jnp.concatenate([e, carry[:, j * G:(j + 1) * G]], axis=0)
            sw = _window_sums(ext, w, False)[:ts, :]
            outs[0][:, j * G:(j + 1) * G] = (sw - d).astype(BF16)
            carry[:, j * G:(j + 1) * G] = e[:POOL_HALO, :]
        outs[0][:, E:] = ins[1][...]

    return rowwise(name, fn, S, ts,
                   [(dpooled, pl.BlockSpec((NG, ts, G), lambda i: (0, i, 0))), (dg, rows(ts, E))],
                   [((S, 2 * E), BF16, rows(ts, 2 * E))],
                   scratch=[pltpu.VMEM((POOL_HALO, E), F32)], reverse=True)[0]


def pool_group_fwd(name, pooled, w_group, z, scale, S, E, NG):
    G = E // NG
    tm = _pick(S, (1024, 512, 256, 128))

    def body(a_ref, b_ref, g_ref, sc_ref, y_ref, mx_ref):
        mx = lax.dot_general(a_ref[...], b_ref[...], _NN, preferred_element_type=F32)
        g = g_ref[...]
        y_ref[...] = (mx * sc_ref[...] * (g * _sigmoid(g))).astype(BF16)
        mx_ref[...] = mx.astype(BF16)

    grp = pl.BlockSpec((None, tm, G), lambda j, i: (j, i, 0))
    res, _ = _call(
        body, name, [pooled, w_group, z, scale],
        [grp, pl.BlockSpec((None, G, G), lambda j, i: (j, 0, 0)), pl.BlockSpec((tm, G), lambda j, i: (i, NG + j)),
         pl.BlockSpec((1, G), lambda j, i: (0, j))],
        [jax.ShapeDtypeStruct((S, E), BF16), jax.ShapeDtypeStruct((NG, S, G), BF16)],
        [pl.BlockSpec((tm, G), lambda j, i: (i, j)), grp], (NG, S // tm), [], ("parallel", "parallel"))
    return res


def pool_out_bwd(name, dout, w_out, mixed, z, scale, S, E, NG, rider=None):
    G, D = E // NG, dout.shape[1]
    tm = _pick(S, (512, 256, 128))

    def body(d_ref, w_ref, mx_ref, g_ref, sc_ref, dmx_ref, dg_ref, dsc_ref):
        i = pl.program_id(1)
        d = lax.dot_general(d_ref[...], w_ref[...], _NT, preferred_element_type=F32)
        mx, g, sc = mx_ref[...].astype(F32), g_ref[...], sc_ref[...]
        sg = _sigmoid(g)
        si = g * sg
        dmx_ref[...] = (d * sc * si).astype(BF16)
        dg_ref[...] = (d * mx * sc * (sg * (1.0 + g * (1.0 - sg)))).astype(BF16)

        @pl.when(i == 0)
        def _():
            dsc_ref[...] = jnp.zeros_like(dsc_ref)

        dsc_ref[...] += jnp.sum(d * mx * si, axis=0, keepdims=True)

    grp = pl.BlockSpec((None, tm, G), lambda j, i: (j, i, 0))
    return _call(
        body, name, [dout, w_out, mixed, z, scale],
        [pl.BlockSpec((tm, D), lambda j, i: (i, 0)), pl.BlockSpec((G, D), lambda j, i: (j, 0)), grp,
         pl.BlockSpec((tm, G), lambda j, i: (i, NG + j)), pl.BlockSpec((1, G), lambda j, i: (0, j))],
        [jax.ShapeDtypeStruct((NG, S, G), BF16), jax.ShapeDtypeStruct((S, E), BF16), jax.ShapeDtypeStruct((1, E), F32)],
        [grp, pl.BlockSpec((tm, G), lambda j, i: (i, j)), pl.BlockSpec((1, G), lambda j, i: (0, j))],
        (NG, S // tm), [], ("parallel", "arbitrary"), rider)


def mla_prep_fwd(name, z, qg, kvg, tabs, S, E, R, ts):
    qb, kb, pb = E // R, E // R + 1, (E + 2 * R) // LANE

    def fn(i, step, ins, outs, accs, scr):
        zq, zkv = ins[0][...], ins[1][...]
        outs[0][...] = (zq * _rstd(zq) * ins[3][...]).astype(BF16)
        outs[1][...] = (zkv * _rstd(zkv) * ins[4][...]).astype(BF16)
        outs[2][...] = _rope(ins[2][...], ins[5][...], ins[6][...], ins[7][...], 1.0)

    return rowwise(name, fn, S, ts,
                   [(z, rows(ts, R, qb)), (z, rows(ts, R, kb)), (z, rows(ts, LANE, pb)),
                    (qg, whole((1, R))), (kvg, whole((1, R)))] + [(t, rows(ts, LANE)) for t in tabs],
                   [((S, R), BF16, rows(ts, R)), ((S, R), BF16, rows(ts, R)), ((S, LANE), F32, rows(ts, LANE))])


def mla_up_fwd(name, qn, kvn, w_uq, w_uk, w_uv, kper, tabs, S, H, R, scale):
    hc = _pick(H, (4, 2, 1))
    tm = _pick(S, (1024, 512, 256, 128))

    def body(qn_ref, kvn_ref, wq_ref, wk_ref, wv_ref, kper_ref, cos_ref, sa_ref, sb_ref, q_ref, k_ref, v_ref):
        cos_t, sin_a, sin_b = cos_ref[...], sa_ref[...], sb_ref[...]
        kvn_t = kvn_ref[...]
        q = lax.dot_general(qn_ref[...], wq_ref[...], _NN, preferred_element_type=F32)
        k = lax.dot_general(kvn_t, wk_ref[...], _NN, preferred_element_type=F32)
        v_ref[...] = lax.dot_general(kvn_t, wv_ref[...], _NN, preferred_element_type=F32).astype(BF16)
        kp = kper_ref[...].astype(BF16)
        for h in range(hc):
            a, b, c = h * HEAD_PAD, h * HEAD_PAD + NOPE_DIM, (h + 1) * HEAD_PAD
            q_ref[:, a:b] = (q[:, a:b] * scale).astype(BF16)
            q_ref[:, b:c] = (_rope(q[:, b:c], cos_t, sin_a, sin_b, 1.0) * scale).astype(BF16)
            k_ref[:, a:b] = k[:, a:b].astype(BF16)
            k_ref[:, b:c] = kp

    row = lambda w: pl.BlockSpec((tm, w), lambda i, j: (i, 0))
    col = lambda w: pl.BlockSpec((R, w), lambda i, j: (0, j))
    out = lambda w: pl.BlockSpec((tm, w), lambda i, j: (i, j))
    W = H * HEAD_PAD
    res, _ = _call(
        body, name, [qn, kvn, w_uq, w_uk, w_uv, kper, *tabs],
        [row(R), row(R), col(hc * HEAD_PAD), col(hc * HEAD_PAD), col(hc * V_DIM), row(LANE), row(LANE), row(LANE),
         row(LANE)],
        [jax.ShapeDtypeStruct((S, W), BF16), jax.ShapeDtypeStruct((S, W), BF16),
         jax.ShapeDtypeStruct((S, H * V_DIM), BF16)],
        [out(hc * HEAD_PAD), out(hc * HEAD_PAD), out(hc * V_DIM)], (S // tm, H // hc), [], ("parallel", "parallel"))
    return res


def mla_gate_bwd(name, dy, o, z, S, E, ts):
    def fn(i, step, ins, outs, accs, scr):
        d, ov, g = ins[0][...], ins[1][...], ins[2][...]
        sg = _sigmoid(g)
        outs[0][...] = (d * (g * sg)).astype(BF16)
        outs[1][...] = (d * ov * (sg * (1.0 + g * (1.0 - sg)))).astype(BF16)

    return rowwise(name, fn, S, ts, [(dy, rows(ts, E)), (o, rows(ts, E)), (z, rows(ts, E, 0))],
                   [((S, E), BF16, rows(ts, E)), ((S, E), BF16, rows(ts, E))])


def mla_unpack_q_bwd(name, dqt, tabs, S, H, t, scale):
    W = H * HEAD_PAD

    def fn(i, step, ins, outs, accs, scr):
        cos_t, sin_a, sin_b = ins[1][...], ins[2][...], ins[3][...]
        for h in range(H):
            a, b, c = h * HEAD_PAD, h * HEAD_PAD + NOPE_DIM, (h + 1) * HEAD_PAD
            dq = ins[0][h].T
            outs[0][:, a:b] = (dq[:, :NOPE_DIM] * scale).astype(BF16)
            outs[0][:, b:c] = (_rope(dq[:, NOPE_DIM:], cos_t, sin_a, sin_b, -1.0) * scale).astype(BF16)

    return rowwise(name, fn, S, t,
                   [(dqt, pl.BlockSpec((H, None, HEAD_PAD, t), lambda i: (0, i, 0, 0)))]
                   + [(tb, rows(t, LANE)) for tb in tabs],
                   [((S, W), BF16, rows(t, W))])[0]


def mla_unpack_k_bwd(name, dk, S, H, ts):
    W = H * HEAD_PAD

    def fn(i, step, ins, outs, accs, scr):
        dkpe = jnp.zeros((ts, LANE), F32)
        for h in range(H):
            a, b, c = h * HEAD_PAD, h * HEAD_PAD + NOPE_DIM, (h + 1) * HEAD_PAD
            outs[0][:, a:b] = ins[0][:, a:b].astype(BF16)
            outs[0][:, b:c] = jnp.zeros((ts, LANE), BF16)
            dkpe = dkpe + ins[0][:, b:c]
        outs[1][...] = dkpe

    return rowwise(name, fn, S, ts, [(dk, rows(ts, W))],
                   [((S, W), BF16, rows(ts, W)), ((S, LANE), F32, rows(ts, LANE))])


def mla_prep_bwd(name, dqn, dkvn_k, dkvn_v, z, dkpe, dg, qg, kvg, tabs, S, E, R, ts):
    qb, kb = E // R, E // R + 1
    ZW = E + 2 * R + LANE

    def fn(i, step, ins, outs, accs, scr):
        dzq, dqg = _rms_bwd(ins[0][...], ins[3][...], ins[7][...])
        dzkv, dkvg = _rms_bwd(ins[1][...] + ins[2][...], ins[4][...], ins[8][...])
        outs[0][:, :E] = ins[6][...]
        outs[0][:, E:E + R] = dzq.astype(BF16)
        outs[0][:, E + R:E + 2 * R] = dzkv.astype(BF16)
        outs[0][:, E + 2 * R:] = _rope(ins[5][...], ins[9][...], ins[10][...], ins[11][...], -1.0).astype(BF16)
        accs[0][...] += dqg
        accs[1][...] += dkvg

    return rowwise(name, fn, S, ts,
                   [(dqn, rows(ts, R)), (dkvn_k, rows(ts, R)), (dkvn_v, rows(ts, R)), (z, rows(ts, R, qb)),
                    (z, rows(ts, R, kb)), (dkpe, rows(ts, LANE)), (dg, rows(ts, E)),
                    (qg, whole((1, R))), (kvg, whole((1, R)))] + [(t, rows(ts, LANE)) for t in tabs],
                   [((S, ZW), BF16, rows(ts, ZW))], accs=[((1, R), F32), ((1, R), F32)])


_NT = (((1,), (1,)), ((), ()))
_NN = (((1,), (0,)), ((), ()))
_TN = (((0,), (0,)), ((), ()))


def _causal_mask_t(t):
    return lax.broadcasted_iota(jnp.int32, (t, t), 0) <= lax.broadcasted_iota(jnp.int32, (t, t), 1)


def _tile(i, t):
    return pl.ds(pl.multiple_of(i * t, t), t)


def flash_fwd(name, q, k, v, z, S, H, t, rider=None):
    nt = S // t

    def body(q_ref, k_ref, v_ref, g_ref, o_ref, y_ref, lse_ref, m_sc, l_sc, acc_sc):
        i = pl.program_id(1)
        m_sc[...] = jnp.full_like(m_sc, NEG_INF)
        l_sc[...] = jnp.zeros_like(l_sc)
        acc_sc[...] = jnp.zeros_like(acc_sc)
        q = q_ref[...]

        def tile(j, diag):
            s = lax.dot_general(k_ref[_tile(j, t), :], q, _NT, preferred_element_type=F32)
            if diag:
                s = jnp.where(_causal_mask_t(t), s, NEG_INF)
            m_prev = m_sc[...]
            m_new = jnp.maximum(m_prev, jnp.max(s, axis=0, keepdims=True))
            alpha = jnp.exp(m_prev - m_new)
            p = jnp.exp(s - m_new)
            l_sc[...] = alpha * l_sc[...] + jnp.sum(p, axis=0, keepdims=True)
            acc_sc[...] = alpha * acc_sc[...] + lax.dot_general(v_ref[_tile(j, t), :], p.astype(BF16), _TN,
                                                                 preferred_element_type=F32)
            m_sc[...] = m_new

        def off_diagonal(j, carry):
            tile(j, False)
            return carry

        lax.fori_loop(0, i, off_diagonal, 0)
        tile(i, True)
        l = l_sc[...]
        o = (acc_sc[...] / l).T
        g = g_ref[...]
        o_ref[...] = o
        y_ref[...] = (o * (g * _sigmoid(g))).astype(BF16)
        lse_ref[...] = m_sc[...] + jnp.log(l)

    qtile = pl.BlockSpec((t, V_DIM), lambda h, i: (i, h))
    (o, y, lse), carried = _call(
        body, name, [q, k, v, z],
        [pl.BlockSpec((t, HEAD_PAD), lambda h, i: (i, h)), pl.BlockSpec((S, HEAD_PAD), lambda h, i: (0, h)),
         pl.BlockSpec((S, V_DIM), lambda h, i: (0, h)), qtile],
        [jax.ShapeDtypeStruct((S, H * V_DIM), F32), jax.ShapeDtypeStruct((S, H * V_DIM), BF16),
         jax.ShapeDtypeStruct((H, nt, 1, t), F32)],
        [qtile, qtile, pl.BlockSpec((None, None, 1, t), lambda h, i: (h, i, 0, 0))],
        (H, nt), [pltpu.VMEM((1, t), F32), pltpu.VMEM((1, t), F32), pltpu.VMEM((V_DIM, t), F32)],
        ("parallel", "parallel"), rider)
    return o, y, lse, carried


def flash_bwd(name, q, k, v, do, o, lse, S, H, t, rider=None):
    nt = S // t

    def body(q_ref, k_ref, v_ref, do_ref, o_ref, lse_ref, dq_ref, dk_ref, dv_ref, kt_sc, dl_sc, dv_sc):
        j = pl.program_id(1)

        @pl.when(j == 0)
        def _():
            dq_ref[...] = jnp.zeros_like(dq_ref)
            ones = jnp.ones((8, V_DIM), BF16)
            for i in range(nt):
                x = do_ref[i * t:(i + 1) * t, :].astype(F32) * o_ref[i * t:(i + 1) * t, :]
                hi = x.astype(BF16)
                lo = (x - hi.astype(F32)).astype(BF16)
                dl_sc[i] = (lax.dot_general(ones, hi, _NT, preferred_element_type=F32)
                            + lax.dot_general(ones, lo, _NT, preferred_element_type=F32))

        kj, vj = k_ref[...], v_ref[...]
        kt_sc[...] = kj.astype(F32).T.astype(BF16)
        dk_ref[...] = jnp.zeros_like(dk_ref)
        dv_sc[...] = jnp.zeros_like(dv_sc)

        def tile(i, diag):
            qi, doi = q_ref[_tile(i, t), :], do_ref[_tile(i, t), :]
            s = lax.dot_general(kj, qi, _NT, preferred_element_type=F32)
            p = jnp.exp(s - lse_ref[i])
            if diag:
                p = jnp.where(_causal_mask_t(t), p, 0.0)
            dv_sc[...] += lax.dot_general(p.astype(BF16), doi, _NN, preferred_element_type=F32)
            dp = lax.dot_general(vj, doi, _NT, preferred_element_type=F32)
            ds = (p * (dp - dl_sc[i, 0:1, :])).astype(BF16)
            dk_ref[...] += lax.dot_general(ds, qi, _NN, preferred_element_type=F32)
            dq_ref[i] += lax.dot_general(kt_sc[...], ds, _NN, preferred_element_type=F32)

        def off_diagonal(i, carry):
            tile(i, False)
            return carry

        tile(j, True)
        lax.fori_loop(j + 1, nt, off_diagonal, 0)
        dv_ref[...] = dv_sc[...].astype(BF16)

    head = lambda w: pl.BlockSpec((S, w), lambda h, j: (0, h))
    ktile = lambda w: pl.BlockSpec((t, w), lambda h, j: (j, h))
    (dq, dk, dv), carried = _call(
        body, name, [q, k, v, do, o, lse],
        [head(HEAD_PAD), ktile(HEAD_PAD), ktile(V_DIM), head(V_DIM), head(V_DIM),
         pl.BlockSpec((None, nt, 1, t), lambda h, j: (h, 0, 0, 0))],
        [jax.ShapeDtypeStruct((H, nt, HEAD_PAD, t), F32), jax.ShapeDtypeStruct((S, H * HEAD_PAD), F32),
         jax.ShapeDtypeStruct((S, H * V_DIM), BF16)],
        [pl.BlockSpec((None, nt, HEAD_PAD, t), lambda h, j: (h, 0, 0, 0)), ktile(HEAD_PAD), ktile(V_DIM)],
        (H, nt), [pltpu.VMEM((HEAD_PAD, t), BF16), pltpu.VMEM((nt, 8, t), F32), pltpu.VMEM((t, V_DIM), F32)],
        ("parallel", "arbitrary"), rider)
    return dq, dk, dv, carried


def _peers():
    x, y, c = lax.axis_index("x"), lax.axis_index("y"), lax.axis_index("c")
    me = 4 * x + 2 * y + c
    peers = []
    for fx, fy, fc in ((0, 0, 1), (1, 0, 0), (0, 1, 0), (1, 1, 0), (1, 0, 1), (0, 1, 1), (1, 1, 1)):
        px, py, pc = x ^ fx, y ^ fy, c ^ fc
        peers.append(((px, py, pc), 4 * px + 2 * py + pc))
    return me, peers


def _hbm_specs(n):
    return [pl.BlockSpec(memory_space=pl.ANY)] * n


class Rider:
    def __init__(self, arrs, gather, windows=None, layers=None):
        self.arrs, self.gather, self.n = list(arrs), gather, len(arrs)
        self.windows = list(windows) if windows is not None else [None] * self.n
        self.layers = list(layers) if layers is not None else [None] * self.n
        assert not (gather and any(w is not None for w in self.windows))
        assert gather or all(l is None for l in self.layers)
        self.out_shapes = [
            jax.ShapeDtypeStruct((N_DEV,) + (a.shape if l is None else a.shape[1:]) if gather else
                                 a.shape if w is None else (N_DEV, w[1]) + a.shape[2:], a.dtype)
            for a, w, l in zip(arrs, self.windows, self.layers)]
        self.sems = [pltpu.SemaphoreType.DMA((self.n, N_DEV - 1)), pltpu.SemaphoreType.DMA((self.n, N_DEV - 1)),
                     pltpu.SemaphoreType.DMA((self.n,))]

    def _copies(self, srcs, dsts, sems):
        send_sems, recv_sems, local_sems = sems
        x, y, c = lax.axis_index("x"), lax.axis_index("y"), lax.axis_index("c")
        ident = lambda d: 4 * d[0] + 2 * d[1] + d[2]
        me, sibling = (x, y, c), (x, y, 1 - c)
        chips = [(1 - x, y), (x, 1 - y), (1 - x, 1 - y)]
        _, peers = _peers()

        def slot(a, pid):
            w = self.windows[a]
            return srcs[a].at[pid] if w is None else srcs[a].at[pid, pl.ds(w[0], w[1])]

        def own(a):
            return srcs[a] if self.layers[a] is None else srcs[a].at[self.layers[a]]

        def remote(a, k, incoming):
            if not self.gather:
                target, pid = peers[k]
                src, block = slot(a, pid), (pid if incoming else ident(me))
            elif k == 0:
                target, src, block = sibling, own(a), ident(sibling if incoming else me)
            elif k <= 3:
                target = (*chips[k - 1], c)
                src, block = own(a), ident(target if incoming else me)
            else:
                landed = ident((*chips[k - 4], c))
                target, src = sibling, dsts[a].at[landed]
                block = ident((*chips[k - 4], 1 - c)) if incoming else landed
            return pltpu.make_async_remote_copy(
                src_ref=src, dst_ref=dsts[a].at[block], send_sem=send_sems.at[a, k], recv_sem=recv_sems.at[a, k],
                device_id=target, device_id_type=MESH)

        def local(a):
            return pltpu.make_async_copy(own(a) if self.gather else slot(a, ident(me)), dsts[a].at[ident(me)],
                                         local_sems.at[a])

        return local, remote

    def start(self, srcs, dsts, sems):
        local, remote = self._copies(srcs, dsts, sems)
        for a in range(self.n):
            local(a).start()
            for k in range(4 if self.gather else N_DEV - 1):
                remote(a, k, False).start()

    def relay(self, srcs, dsts, sems):
        if not self.gather:
            return
        local, remote = self._copies(srcs, dsts, sems)
        for a in range(self.n):
            for k in range(1, 4):
                remote(a, k, True).wait_recv()
                remote(a, k + 3, False).start()

    def wait(self, srcs, dsts, sems):
        local, remote = self._copies(srcs, dsts, sems)
        for a in range(self.n):
            for k in range(N_DEV - 1):
                if not (self.gather and 1 <= k <= 3):
                    remote(a, k, True).wait_recv()
        for a in range(self.n):
            for k in range(N_DEV - 1):
                remote(a, k, False).wait_send()
            local(a).wait()


def _carry(body, n_in, n_out, rider, grid):
    n = rider.n
    steps = math.prod(grid)

    def wrapped(*refs):
        ins, r_in = refs[:n_in], refs[n_in:n_in + n]
        outs = refs[n_in + n:n_in + n + n_out]
        r_out = refs[n_in + n + n_out:n_in + 2 * n + n_out]
        scratch, sems = refs[n_in + 2 * n + n_out:-3], refs[-3:]
        step = 0
        for d, g in enumerate(grid):
            step = step * g + pl.program_id(d)

        @pl.when(step == 0)
        def _():
            rider.start(r_in, r_out, sems)

        if rider.gather:
            @pl.when(step == (3 * steps) // 4)
            def _():
                rider.relay(r_in, r_out, sems)

        body(*ins, *outs, *scratch)

        @pl.when(step == steps - 1)
        def _():
            rider.wait(r_in, r_out, sems)

    return wrapped


def _call(body, name, ins, in_specs, out_shape, out_specs, grid, scratch, sem, rider=None):
    if rider is None:
        return _pcall(body, name=name, out_shape=list(out_shape), grid=grid, in_specs=list(in_specs),
                      out_specs=list(out_specs), scratch_shapes=list(scratch), compiler_params=_cparams(*sem))(*ins), None
    res = _pcall(
        _carry(body, len(ins), len(out_shape), rider, grid), name=name,
        out_shape=list(out_shape) + rider.out_shapes, grid=grid,
        in_specs=list(in_specs) + _hbm_specs(rider.n), out_specs=list(out_specs) + _hbm_specs(rider.n),
        scratch_shapes=list(scratch) + rider.sems, compiler_params=_cparams(*(("arbitrary",) * len(grid))),
    )(*ins, *rider.arrs)
    return res[:len(out_shape)], res[len(out_shape):]


def exchange(name, rider):
    def body(*refs):
        srcs, dsts, sems = refs[:rider.n], refs[rider.n:2 * rider.n], refs[2 * rider.n:]
        rider.start(srcs, dsts, sems)
        rider.relay(srcs, dsts, sems)
        rider.wait(srcs, dsts, sems)

    return _pcall(body, name=name, out_shape=rider.out_shapes, in_specs=_hbm_specs(rider.n),
                  out_specs=_hbm_specs(rider.n), scratch_shapes=rider.sems)(*rider.arrs)


def adamw(name, gslots, w, m, v, layer=0, prev=None):
    K, R, C = gslots.shape
    per_row = C * (K * gslots.dtype.itemsize + 7 * 4) * 2
    tr = R
    for cand in (1024, 512, 256, 128, 64, 32, 16, 8):
        if R % cand == 0:
            tr = cand
            if cand * per_row <= VMEM_LIMIT_BYTES // 2:
                break
    c1 = 1.0 / (1.0 - ADAM_B1 ** ADAM_STEP)
    c2 = 1.0 / (1.0 - ADAM_B2 ** ADAM_STEP)

    def body(g_ref, w_ref, m_ref, v_ref, *rest):
        go_ref, d_ref, mo_ref, vo_ref = rest[-4:]
        g = g_ref[0].astype(F32)
        for s in range(1, K):
            g = g + g_ref[s].astype(F32)
        mn = ADAM_B1 * m_ref[...] + (1.0 - ADAM_B1) * g
        vn = ADAM_B2 * v_ref[...] + (1.0 - ADAM_B2) * (g * g)
        go_ref[...] = g
        mo_ref[...] = mn
        vo_ref[...] = vn
        d_ref[...] = -ADAM_LR * ((mn * c1) / (jnp.sqrt(vn * c2) + ADAM_EPS) + ADAM_WD * w_ref[...])

    blk = pl.BlockSpec((None, tr, C), lambda i: (layer, i, 0))
    prev = [] if prev is None else list(prev)
    return _pcall(
        body, name=name, out_shape=[jax.ShapeDtypeStruct(w.shape, F32)] * 4, grid=(R // tr,),
        in_specs=[pl.BlockSpec((K, tr, C), lambda i: (0, i, 0)), blk, blk, blk] + _hbm_specs(len(prev)),
        out_specs=[blk] * 4, input_output_aliases={4 + q: q for q in range(len(prev))},
        compiler_params=_cparams("parallel"),
    )(gslots, w, m, v, *prev)


def _from_slots(gathered, ax):
    g = jnp.moveaxis(gathered, 0, ax)
    s = g.shape
    return g.reshape(s[:ax] + (s[ax] * s[ax + 1],) + s[ax + 2:])


def _to_slots(full, ax):
    s = full.shape
    g = full.reshape(s[:ax] + (N_DEV, s[ax] // N_DEV) + s[ax + 1:])
    g = jnp.moveaxis(g, ax, 0)
    return g.reshape(N_DEV, -1, g.shape[-1])


def _rope_tables(pos, S):
    inv_freq = ROPE_THETA ** (-jnp.arange(0, ROPE_DIM, 2, dtype=F32) / ROPE_DIM)
    ang = pos.astype(F32)[:, None] * inv_freq
    cos, sin = jnp.cos(ang), jnp.sin(ang)
    z = jnp.zeros((S, ROPE_DIM // 2), F32)
    cos_t = jnp.concatenate([cos, cos, z, z], axis=1)
    sin_a = jnp.concatenate([-sin, z, z, z], axis=1)
    sin_b = jnp.concatenate([z, sin, z, z], axis=1)
    return cos_t, sin_a, sin_b


def kernel(x, p, positions, pre_norm, post_norm, pool_w_in, pool_w_group, pool_scale, pool_w_out, mla_w_in, mla_q_norm, mla_w_uq, mla_kv_norm, mla_w_ukv, mla_w_out, ple_norm, ple_w_gate, ple_w_proj, loss_target, m_pre_norm, m_post_norm, m_pool_w_in, m_pool_w_group, m_pool_scale, m_pool_w_out, m_mla_w_in, m_mla_q_norm, m_mla_w_uq, m_mla_kv_norm, m_mla_w_ukv, m_mla_w_out, m_ple_norm, m_ple_w_gate, m_ple_w_proj, v_pre_norm, v_post_norm, v_pool_w_in, v_pool_w_group, v_pool_scale, v_pool_w_out, v_mla_w_in, v_mla_q_norm, v_mla_w_uq, v_mla_kv_norm, v_mla_w_ukv, v_mla_w_out, v_ple_norm, v_ple_w_gate, v_ple_w_proj):
    wl = dict(pre_norm=pre_norm, post_norm=post_norm, pool_w_in=pool_w_in, pool_w_group=pool_w_group,
              pool_scale=pool_scale, pool_w_out=pool_w_out, mla_w_in=mla_w_in, mla_q_norm=mla_q_norm,
              mla_w_uq=mla_w_uq, mla_kv_norm=mla_kv_norm, mla_w_ukv=mla_w_ukv, mla_w_out=mla_w_out,
              ple_norm=ple_norm, ple_w_gate=ple_w_gate, ple_w_proj=ple_w_proj)
    ml = dict(pre_norm=m_pre_norm, post_norm=m_post_norm, pool_w_in=m_pool_w_in, pool_w_group=m_pool_w_group,
              pool_scale=m_pool_scale, pool_w_out=m_pool_w_out, mla_w_in=m_mla_w_in, mla_q_norm=m_mla_q_norm,
              mla_w_uq=m_mla_w_uq, mla_kv_norm=m_mla_kv_norm, mla_w_ukv=m_mla_w_ukv, mla_w_out=m_mla_w_out,
              ple_norm=m_ple_norm, ple_w_gate=m_ple_w_gate, ple_w_proj=m_ple_w_proj)
    vl = dict(pre_norm=v_pre_norm, post_norm=v_post_norm, pool_w_in=v_pool_w_in, pool_w_group=v_pool_w_group,
              pool_scale=v_pool_scale, pool_w_out=v_pool_w_out, mla_w_in=v_mla_w_in, mla_q_norm=v_mla_q_norm,
              mla_w_uq=v_mla_w_uq, mla_kv_norm=v_mla_kv_norm, mla_w_ukv=v_mla_w_ukv, mla_w_out=v_mla_w_out,
              ple_norm=v_ple_norm, ple_w_gate=v_ple_w_gate, ple_w_proj=v_ple_w_proj)

    S, D = x.shape[1], x.shape[2]
    L = pre_norm.shape[0]
    E = pool_scale.shape[1]
    NG = pool_w_group.shape[1]
    R = mla_w_uq.shape[1]
    H = D // 128
    EM = H * V_DIM
    PD = p.shape[-1]
    me = 4 * lax.axis_index("x") + 2 * lax.axis_index("y") + lax.axis_index("c")
    ts = min(S, 256)
    tsw = min(S, 256)
    ta = min(S, 512)
    sm_scale = (NOPE_DIM + ROPE_DIM) ** -0.5

    wb = {n: wl[n].astype(BF16) for n in BIG}
    full = {}

    def ag_rider(host):
        if host not in AG_PLAN:
            return None
        return Rider([wb[n] for n, _ in AG_PLAN[host]], True, layers=[l for _, l in AG_PLAN[host]])

    def ag_done(host, results):
        for (n, l), g in zip(AG_PLAN[host], results):
            full[n, l] = g if n in SLOT_NATIVE else _from_slots(g, SHARD_AXIS[n] - 1)

    small_sh = jnp.concatenate([wl[n].reshape(1, -1) for n in SMALL_SHARD], axis=1)
    g_in0, g_small = exchange("gather_first", Rider([wb['pool_w_in'], small_sh], True, layers=[0, None]))
    full['pool_w_in', 0] = g_in0
    nq = mla_q_norm.size
    q_norm = _from_slots(g_small[:, 0, :nq].reshape((N_DEV,) + mla_q_norm.shape), 1)
    kv_norm = _from_slots(g_small[:, 0, nq:].reshape((N_DEV,) + mla_kv_norm.shape), 1)

    def mla_kernel_weights(j):
        w_in = full['mla_w_in', j]
        w_in_k = jnp.concatenate([w_in[:, 2 * R + ROPE_DIM:], w_in[:, :2 * R + ROPE_DIM],
                                  jnp.zeros((D, LANE - ROPE_DIM), BF16)], axis=1)
        w_uq_k = jnp.pad(full['mla_w_uq', j].reshape(R, H, NOPE_DIM + ROPE_DIM),
                         ((0, 0), (0, 0), (0, HEAD_PAD - NOPE_DIM - ROPE_DIM))).reshape(R, H * HEAD_PAD)
        w_ukv = full['mla_w_ukv', j].reshape(R, H, NOPE_DIM + V_DIM)
        w_uk_k = jnp.pad(w_ukv[..., :NOPE_DIM], ((0, 0), (0, 0), (0, HEAD_PAD - NOPE_DIM))).reshape(R, H * HEAD_PAD)
        w_uv_k = w_ukv[..., NOPE_DIM:].reshape(R, H * V_DIM)
        return w_in_k, w_uq_k, w_uk_k, w_uv_k

    def fmm(name, a, b, mode, out_dtype=F32, slots=False):
        if name not in AG_PLAN:
            return mm(name, a, b, mode, out_dtype, slots=slots)
        out, carried = mm(name, a, b, mode, out_dtype, ag_rider(name), slots)
        ag_done(name, carried)
        return out

    mla_w = {}
    tabs = _rope_tables(positions[0], S)

    h = x[0]
    saved = []
    for i in range(L):
        j = i // 2
        sv = dict(h=h)
        if i == 0:
            xn = rms_fwd(f"pre_norm_{i}", h, pre_norm[i:i + 1], S, D, ts)
        sv['xn'] = xn
        if i % 2 == 0:
            z = fmm(f"pool_in_{i}", xn, full['pool_w_in', j], 'nn', slots=True)
            pooled = pool_fwd(f"pool_window_{i}", z, S, E, NG, tsw)
            y, mixed = pool_group_fwd(f"pool_group_{i}", pooled, full['pool_w_group', j], z, pool_scale[j:j + 1],
                                      S, E, NG)
            out = fmm(f"pool_out_{i}", y, full['pool_w_out', j], 'nn')
            sv.update(z=z, pooled=pooled, mixed=mixed, y=y)
        else:
            w_in_k, w_uq_k, w_uk_k, w_uv_k = mla_w[j] = mla_kernel_weights(j)
            z = fmm(f"mla_in_{i}", xn, w_in_k, 'nn')
            qn, kvn, kper = mla_prep_fwd(f"mla_prep_{i}", z, q_norm[j:j + 1], kv_norm[j:j + 1], tabs, S, EM, R, ts)
            qp, kp, vv = mla_up_fwd(f"mla_up_{i}", qn, kvn, w_uq_k, w_uk_k, w_uv_k, kper, tabs, S, H, R, sm_scale)
            o, y, lse, carried = flash_fwd(f"attn_{i}", qp, kp, vv, z, S, H, ta, ag_rider(f"attn_{i}"))
            if carried is not None:
                ag_done(f"attn_{i}", carried)
            out, h1, a = out_post_fwd(f"mla_out_{i}", y, full['mla_w_out', j], h, post_norm[i:i + 1],
                                      ple_norm[i:i + 1], S, D, ts)
            sv.update(z=z, qn=qn, kvn=kvn, qp=qp, kp=kp, vv=vv, o=o, lse=lse, y=y)
        if i % 2 == 0:
            h1, a = post_fwd(f"post_norm_{i}", h, out, post_norm[i:i + 1], ple_norm[i:i + 1], S, D, ts)
        if i < L - 1:
            gl, pp, h, xn = ple_fwd(f"ple_{i}", a, full['ple_w_gate', i], p[i, 0], full['ple_w_proj', i], h1,
                                    next_gain=pre_norm[i + 1:i + 2])
        else:
            gl = pp = None
            dpp, dgl, dh, loss_acc = ple_fwd(f"ple_{i}", a, full['ple_w_gate', i], p[i, 0], full['ple_w_proj', i], h1,
                                           target=loss_target[0])
        sv.update(out=out, h1=h1, a=a, gl=gl, pp=pp)
        saved.append(sv)

    loss = lax.psum(loss_acc[0, 0] * (0.5 / D), ("x", "y", "c"))

    gw = {n: [None] * wl[n].shape[0] for n in WEIGHTS}
    recv = {}

    def rs_rider(host):
        if host not in RS_PLAN:
            return None
        return scatter_rider(RS_PLAN[host])

    def scatter_rider(keys):
        arrs = [gw[n][l] if n in SLOT_NATIVE else _to_slots(gw[n][l], SHARD_AXIS[n] - 1).astype(BF16)
                for n, l, *_ in keys]
        halves = [key[2] if len(key) == 3 else None for key in keys]
        return Rider(arrs, False, [None if h is None else (h * (a.shape[1] // 2), a.shape[1] // 2)
                                   for a, h in zip(arrs, halves)])

    def rs_done(host, results):
        for key, r in zip(RS_PLAN[host], results):
            recv[key] = r

    def bmm(name, a, b, mode, slots=False):
        out_dtype = BF16 if mode == 'tn' else F32
        if name not in RS_PLAN:
            return mm(name, a, b, mode, out_dtype, slots=slots)
        out, carried = mm(name, a, b, mode, out_dtype, rs_rider(name), slots)
        rs_done(name, carried)
        return out

    for i in reversed(range(L)):
        j = i // 2
        sv = saved[i]
        gw['ple_w_proj'][i] = bmm(f"ple_proj_dw_{i}", p[i, 0], dpp, 'tn')
        gw['ple_w_gate'][i] = bmm(f"ple_gate_dw_{i}", sv['a'], dgl, 'tn')
        dh1, dout, dpost, dple = post_bwd(f"post_norm_bwd_{i}", dgl, full['ple_w_gate', i], dh, sv['h1'], sv['out'],
                                          post_norm[i:i + 1], ple_norm[i:i + 1], S, D, ts)
        gw['post_norm'][i], gw['ple_norm'][i] = dpost[0], dple[0]
        xn = sv['xn']
        if i % 2 == 0:
            gw['pool_w_out'][j] = bmm(f"pool_out_dw_{i}", sv['y'], dout, 'tn')
            (dmixed, dg, dscale), carried = pool_out_bwd(
                f"pool_out_dx_{i}", dout, full['pool_w_out', j], sv['mixed'], sv['z'], pool_scale[j:j + 1], S, E, NG,
                rs_rider(f"pool_out_dx_{i}"))
            if carried is not None:
                rs_done(f"pool_out_dx_{i}", carried)
            gw['pool_scale'][j] = dscale[0]
            gw['pool_w_group'][j] = bmm(f"pool_group_dw_{i}", sv['pooled'], dmixed, 'tn')
            dpooled = bmm(f"pool_group_dx_{i}", dmixed, full['pool_w_group', j], 'nt')
            dz = pool_bwd(f"pool_window_bwd_{i}", dpooled, dg, S, E, NG, tsw)
            gw['pool_w_in'][j] = bmm(f"pool_in_dw_{i}", xn, dz, 'tn', slots=True)
            dxn = bmm(f"pool_in_dx_{i}", dz, full['pool_w_in', j], 'nt', slots=True)
        else:
            w_in_k, w_uq_k, w_uk_k, w_uv_k = mla_w[j]
            gw['mla_w_out'][j] = bmm(f"mla_out_dw_{i}", sv['y'], dout, 'tn')
            dy = bmm(f"mla_out_dx_{i}", dout, full['mla_w_out', j], 'nt')
            do, dg = mla_gate_bwd(f"mla_gate_bwd_{i}", dy, sv['o'], sv['z'], S, EM, ts)
            dqt, dkp, dvv, carried = flash_bwd(f"attn_bwd_{i}", sv['qp'], sv['kp'], sv['vv'], do, sv['o'], sv['lse'],
                                               S, H, ta, rs_rider(f"attn_bwd_{i}"))
            if carried is not None:
                rs_done(f"attn_bwd_{i}", carried)
            dq_raw = mla_unpack_q_bwd(f"mla_pack_q_bwd_{i}", dqt, tabs, S, H, ta, sm_scale)
            dk_raw, dkpe = mla_unpack_k_bwd(f"mla_pack_k_bwd_{i}", dkp, S, H, tsw)
            g_uq = bmm(f"mla_uq_dw_{i}", sv['qn'], dq_raw, 'tn')
            g_uk = bmm(f"mla_uk_dw_{i}", sv['kvn'], dk_raw, 'tn')
            g_uv = bmm(f"mla_uv_dw_{i}", sv['kvn'], dvv, 'tn')
            gw['mla_w_uq'][j] = g_uq.reshape(R, H, HEAD_PAD)[:, :, :NOPE_DIM + ROPE_DIM].reshape(R, -1)
            gw['mla_w_ukv'][j] = jnp.concatenate(
                [g_uk.reshape(R, H, HEAD_PAD)[:, :, :NOPE_DIM], g_uv.reshape(R, H, V_DIM)], axis=2).reshape(R, -1)
            dqn = bmm(f"mla_uq_dx_{i}", dq_raw, w_uq_k, 'nt')
            dkvn_k = bmm(f"mla_uk_dx_{i}", dk_raw, w_uk_k, 'nt')
            dkvn_v = bmm(f"mla_uv_dx_{i}", dvv, w_uv_k, 'nt')
            dz, dqg, dkvg = mla_prep_bwd(f"mla_prep_bwd_{i}", dqn, dkvn_k, dkvn_v, sv['z'], dkpe, dg,
                                         q_norm[j:j + 1], kv_norm[j:j + 1], tabs, S, EM, R, ts)
            g_in = bmm(f"mla_in_dw_{i}", xn, dz, 'tn')
            dxn = bmm(f"mla_in_dx_{i}", dz, w_in_k, 'nt')
            gw['mla_q_norm'][j], gw['mla_kv_norm'][j] = dqg[0], dkvg[0]
            gw['mla_w_in'][j] = jnp.concatenate([g_in[:, EM:EM + 2 * R + ROPE_DIM], g_in[:, :EM]], axis=1)
        if i > 0:
            dh, dpp, dgl, dpre = pre_bwd(f"pre_norm_bwd_{i}", dxn, dh1, sv['h'], pre_norm[i:i + 1], S, D, ts,
                                         below=(saved[i - 1]['pp'], saved[i - 1]['gl']))
        else:
            dh, dpre = pre_bwd(f"pre_norm_bwd_{i}", dxn, dh1, sv['h'], pre_norm[i:i + 1], S, D, ts)
        gw['pre_norm'][i] = dpre[0]
    grad_x = dh[None]

    last = exchange("scatter_last", scatter_rider(RS_LAST))
    for key, r in zip(RS_LAST, last):
        recv[key] = r
    for n, l in {key[:2] for key in recv if len(key) == 3}:
        recv[n, l] = jnp.concatenate([recv[n, l, 0], recv[n, l, 1]], axis=1)
    small_names = SMALL_REPL + SMALL_SHARD
    gw = {n: jnp.stack(gw[n]) for n in small_names}
    small_g = jnp.concatenate([gw[n].reshape(1, -1) for n in small_names], axis=1)
    small_all = exchange("gather_small_grads", Rider([small_g], True))[0]

    outs = {}
    for n in BIG:
        shp = wl[n].shape
        three = lambda a: a.reshape(shp[0], -1, shp[-1])
        res = None
        for l in range(shp[0]):
            res = adamw(f"adamw_{n}_{l}", recv[n, l], three(wl[n]), three(ml[n]), three(vl[n]), l, res)
        outs[n] = [a.reshape(shp) for a in res]

    pieces, off = [], 0
    for n in small_names:
        sz = gw[n].size
        g = small_all[:, :, off:off + sz]
        off += sz
        if n in SMALL_SHARD:
            rows_, cols_ = gw[n].shape
            g = lax.dynamic_slice_in_dim(g.reshape(N_DEV, rows_, cols_), me * (cols_ // N_DEV), cols_ // N_DEV, axis=2)
            g = g.reshape(N_DEV, 1, -1)
        pieces.append(g)
    gs = jnp.concatenate(pieces, axis=2)
    flat = lambda d: jnp.concatenate([d[n].reshape(1, -1) for n in small_names], axis=1)
    res = adamw("adamw_small", gs, flat(wl)[None], flat(ml)[None], flat(vl)[None])
    off = 0
    for n in small_names:
        sz = wl[n].size
        outs[n] = [a[0, :, off:off + sz].reshape(wl[n].shape) for a in res]
        off += sz

    return (loss, grad_x, *[outs[n][0] for n in WEIGHTS], *[outs[n][1] for n in WEIGHTS],
            *[outs[n][2] for n in WEIGHTS], *[outs[n][3] for n in WEIGHTS])
```

```python
import math

import jax
import jax.numpy as jnp
from jax import lax
from jax.experimental import pallas as pl
from jax.experimental.pallas import tpu as pltpu

F32 = jnp.float32
BF16 = jnp.bfloat16

N_DEV = 8
EPS = 1e-6
ROPE_THETA = 10000.0
NOPE_DIM = 128
ROPE_DIM = 64
V_DIM = 128
HEAD_PAD = 256
LANE = 128
POOL_WINDOWS = (2, 4, 8, 16)
POOL_HALO = 16
NEG_INF = -1e30
ADAM_LR = 0.001
ADAM_B1 = 0.9
ADAM_B2 = 0.999
ADAM_EPS = 1e-08
ADAM_WD = 0.01
ADAM_STEP = 10
VMEM_LIMIT_BYTES = 56 * 1024 * 1024
MM_MAX_TK = 3200
MESH = pl.DeviceIdType.MESH

SHARD_AXIS = dict(pre_norm=None, post_norm=None, pool_w_in=2, pool_w_group=2, pool_scale=None, pool_w_out=1,
                  mla_w_in=2, mla_q_norm=1, mla_w_uq=2, mla_kv_norm=1, mla_w_ukv=2, mla_w_out=1,
                  ple_norm=None, ple_w_gate=1, ple_w_proj=2)
WEIGHTS = tuple(SHARD_AXIS)
BIG = ('pool_w_in', 'pool_w_group', 'pool_w_out', 'mla_w_in', 'mla_w_uq', 'mla_w_ukv', 'mla_w_out',
       'ple_w_gate', 'ple_w_proj')
SMALL_REPL = ('pre_norm', 'post_norm', 'pool_scale', 'ple_norm')
SMALL_SHARD = ('mla_q_norm', 'mla_kv_norm')
SLOT_NATIVE = ('pool_w_in',)

AG_PLAN = {
    "pool_in_0": [("pool_w_group", 0), ("pool_w_out", 0), ("ple_w_gate", 0), ("ple_w_proj", 0)],
    "pool_out_0": [("mla_w_in", 0), ("mla_w_uq", 0), ("mla_w_ukv", 0)],
    "attn_1": [("mla_w_out", 0), ("ple_w_gate", 1), ("ple_w_proj", 1), ("pool_w_in", 1), ("pool_w_group", 1),
               ("mla_w_in", 1)],
    "pool_in_2": [("pool_w_out", 1), ("ple_w_gate", 2), ("ple_w_proj", 2), ("mla_w_uq", 1), ("mla_w_ukv", 1)],
    "attn_3": [("mla_w_out", 1), ("ple_w_gate", 3), ("ple_w_proj", 3)],
}
RS_PLAN = {
    "attn_bwd_3": [("ple_w_gate", 3), ("ple_w_proj", 3), ("mla_w_out", 1)],
    "mla_in_dw_3": [("mla_w_uq", 1), ("mla_w_ukv", 1)],
    "pool_out_dx_2": [("mla_w_in", 1)],
    "pool_in_dw_2": [("ple_w_gate", 2), ("ple_w_proj", 2)],
    "pool_in_dx_2": [("pool_w_out", 1)],
    "attn_bwd_1": [("pool_w_group", 1), ("pool_w_in", 1), ("ple_w_gate", 1), ("ple_w_proj", 1), ("mla_w_out", 0)],
    "mla_in_dw_1": [("mla_w_uq", 0), ("mla_w_ukv", 0)],
    "pool_out_dw_0": [("mla_w_in", 0)],
    "pool_out_dx_0": [("ple_w_gate", 0), ("ple_w_proj", 0)],
    "pool_group_dx_0": [("pool_w_group", 0)],
    "pool_in_dw_0": [("pool_w_out", 0)],
    "pool_in_dx_0": [("pool_w_in", 0, 0)],
}
RS_LAST = [("pool_w_in", 0, 1)]


def _pcall(body, **kw):
    return pl.pallas_call(body, **kw)


def _cparams(*sem):
    return pltpu.CompilerParams(dimension_semantics=sem, vmem_limit_bytes=VMEM_LIMIT_BYTES)


def _pick(n, cands):
    for c in cands:
        if n % c == 0:
            return c
    return n


def _sigmoid(x):
    return 1.0 / (1.0 + jnp.exp(-x))


def mm(name, a, b, mode, out_dtype=F32, rider=None, slots=False):
    squeeze = a.ndim == 2
    if squeeze:
        a = a[None]
        b = b if slots and mode != 'tn' else b[None]
    G = a.shape[0]
    if mode == 'nn':
        M, K = a.shape[1:]
        N = b.shape[2] * (N_DEV if slots else 1)
    elif mode == 'tn':
        K, M = a.shape[1:]
        N = b.shape[2]
    else:
        M, K = a.shape[1:]
        N = b.shape[1]
    n = (K if mode == 'nt' else N) // N_DEV
    tm = _pick(M, (1024, 512, 256, 128))
    tn = _pick(n if slots and mode != 'nt' else N, (1024, 768, 640, 512, 384, 256, 128))
    if slots and mode == 'nt':
        tk = _pick(n, (2048, 1024, 512, 256, 128))
    else:
        tk = K if K <= MM_MAX_TK else _pick(K, (2048, 1024, 640, 512, 384, 256, 128))
    nk = K // tk
    o_spec = pl.BlockSpec((None, tm, tn), lambda g, i, j, k: (g, i, j))
    o_shape = (G, M, N)
    if mode == 'nn':
        a_spec = pl.BlockSpec((None, tm, tk), lambda g, i, j, k: (g, i, k))
        b_spec = pl.BlockSpec((None, tk, tn), lambda g, i, j, k: (g, k, j))
        if slots:
            b_spec = pl.BlockSpec((None, tk, tn), lambda g, i, j, k: (j // (n // tn), k, j % (n // tn)))
        dims = (((1,), (0,)), ((), ()))
    elif mode == 'tn':
        a_spec = pl.BlockSpec((None, tk, tm), lambda g, i, j, k: (g, k, i))
        b_spec = pl.BlockSpec((None, tk, tn), lambda g, i, j, k: (g, k, j))
        if slots:
            o_spec = pl.BlockSpec((None, tm, tn), lambda g, i, j, k: (j // (n // tn), i, j % (n // tn)))
            o_shape = (N_DEV, M, n)
        dims = (((0,), (0,)), ((), ()))
    else:
        a_spec = pl.BlockSpec((None, tm, tk), lambda g, i, j, k: (g, i, k))
        b_spec = pl.BlockSpec((None, tn, tk), lambda g, i, j, k: (g, j, k))
        if slots:
            b_spec = pl.BlockSpec((None, tn, tk), lambda g, i, j, k: (k // (n // tk), j, k % (n // tk)))
        dims = (((1,), (1,)), ((), ()))

    def product(a_ref, b_ref):
        return lax.dot_general(a_ref[...].astype(BF16), b_ref[...].astype(BF16), dims, preferred_element_type=F32)

    def body_one(a_ref, b_ref, o_ref):
        o_ref[...] = product(a_ref, b_ref).astype(out_dtype)

    def body_acc(a_ref, b_ref, o_ref, acc_ref):
        k = pl.program_id(3)

        @pl.when(k == 0)
        def _():
            acc_ref[...] = product(a_ref, b_ref)

        @pl.when(jnp.logical_and(k > 0, k < nk - 1))
        def _():
            acc_ref[...] += product(a_ref, b_ref)

        @pl.when(k == nk - 1)
        def _():
            o_ref[...] = (acc_ref[...] + product(a_ref, b_ref)).astype(out_dtype)

    (out,), carried = _call(
        body_one if nk == 1 else body_acc, name, [a, b], [a_spec, b_spec],
        [jax.ShapeDtypeStruct(o_shape, out_dtype)], [o_spec], (G, M // tm, N // tn, nk),
        [] if nk == 1 else [pltpu.VMEM((tm, tn), F32)], ("parallel", "parallel", "parallel", "arbitrary"), rider)
    out = out[0] if squeeze and not (slots and mode == 'tn') else out
    return out if rider is None else (out, carried)


def rows(ts, width, colblk=0):
    return pl.BlockSpec((ts, width), lambda i: (i, colblk))


def whole(shape):
    return pl.BlockSpec(shape, lambda i: (0,) * len(shape))


def rowwise(name, fn, S, ts, ins, outs, accs=(), scratch=(), reverse=False):
    n_in, n_out, n_acc = len(ins), len(outs), len(accs)
    nt = S // ts

    def body(*refs):
        step = pl.program_id(0)
        i = nt - 1 - step if reverse else step
        in_refs = refs[:n_in]
        out_refs = refs[n_in:n_in + n_out]
        acc_refs = refs[n_in + n_out:n_in + n_out + n_acc]
        scr = refs[n_in + n_out + n_acc:]

        @pl.when(step == 0)
        def _():
            for r in acc_refs:
                r[...] = jnp.zeros_like(r)

        fn(i, step, in_refs, out_refs, acc_refs, scr)

    def fix(spec):
        if not reverse:
            return spec
        imap = spec.index_map
        return pl.BlockSpec(spec.block_shape, lambda s: imap(nt - 1 - s))

    res = _pcall(
        body, name=name,
        out_shape=[jax.ShapeDtypeStruct(s, d) for s, d, _ in outs] + [jax.ShapeDtypeStruct(s, d) for s, d in accs],
        grid=(nt,),
        in_specs=[fix(sp) for _, sp in ins],
        out_specs=[fix(sp) for _, _, sp in outs] + [whole(s) for s, _ in accs],
        scratch_shapes=list(scratch),
        compiler_params=_cparams("arbitrary"),
    )(*[a for a, _ in ins])
    return res


def _rstd(x):
    return lax.rsqrt(jnp.mean(x * x, axis=-1, keepdims=True) + EPS)


def _rms_bwd(dy, x, g):
    r = _rstd(x)
    xh = x * r
    gdy = dy * g
    dx = r * (gdy - xh * jnp.mean(xh * gdy, axis=-1, keepdims=True))
    return dx, jnp.sum(dy * xh, axis=0, keepdims=True)


def _rope(v, cos_t, sin_a, sin_b, sign):
    return v * cos_t + sign * (pltpu.roll(v, LANE - ROPE_DIM // 2, axis=1) * sin_a
                               + pltpu.roll(v, ROPE_DIM // 2, axis=1) * sin_b)


def rms_fwd(name, h, gain, S, D, ts):
    def fn(i, step, ins, outs, accs, scr):
        x = ins[0][...]
        outs[0][...] = (x * _rstd(x) * ins[1][...]).astype(BF16)
    return rowwise(name, fn, S, ts, [(h, rows(ts, D)), (gain, whole((1, D)))], [((S, D), BF16, rows(ts, D))])[0]


def post_fwd(name, h, out, post_g, ple_g, S, D, ts):
    def fn(i, step, ins, outs, accs, scr):
        o = ins[1][...]
        h1 = ins[0][...] + o * _rstd(o) * ins[2][...]
        outs[0][...] = h1
        outs[1][...] = (h1 * _rstd(h1) * ins[3][...]).astype(BF16)
    return rowwise(name, fn, S, ts,
                   [(h, rows(ts, D)), (out, rows(ts, D)), (post_g, whole((1, D))), (ple_g, whole((1, D)))],
                   [((S, D), F32, rows(ts, D)), ((S, D), BF16, rows(ts, D))])


def out_post_fwd(name, y, w_out, h, post_g, ple_g, S, D, ts):
    E = y.shape[1]

    def fn(i, step, ins, outs, accs, scr):
        o = lax.dot_general(ins[0][...], ins[1][...], _NN, preferred_element_type=F32)
        h1 = ins[2][...] + o * _rstd(o) * ins[3][...]
        outs[0][...] = o
        outs[1][...] = h1
        outs[2][...] = (h1 * _rstd(h1) * ins[4][...]).astype(BF16)
    return rowwise(name, fn, S, ts,
                   [(y, rows(ts, E)), (w_out, whole((E, D))), (h, rows(ts, D)), (post_g, whole((1, D))),
                    (ple_g, whole((1, D)))],
                   [((S, D), F32, rows(ts, D)), ((S, D), F32, rows(ts, D)), ((S, D), BF16, rows(ts, D))])


def ple_fwd(name, a, w_gate, p, w_proj, h1, next_gain=None, target=None):
    S, D = h1.shape
    PD = p.shape[1]
    tm = _pick(S, (256, 128))
    last = target is not None

    def body(a_ref, wg_ref, p_ref, wp_ref, h1_ref, x_ref, b1_ref, b2_ref, o1_ref, o2_ref):
        i = pl.program_id(0)
        gl = lax.dot_general(a_ref[...], wg_ref[...], _NN, preferred_element_type=F32)
        pp = lax.dot_general(p_ref[...].astype(BF16), wp_ref[...], _NN, preferred_element_type=F32)
        h = h1_ref[...] + pp * _sigmoid(gl)
        if last:
            e = h - x_ref[...]
            dh = e * (1.0 / D)
            o1_ref[...] = dh
            b1_ref[...], b2_ref[...] = _ple_bwd(dh, pp, gl)

            @pl.when(i == 0)
            def _():
                o2_ref[...] = jnp.zeros_like(o2_ref)

            o2_ref[...] += jnp.broadcast_to(jnp.sum(e * e), (1, LANE))
        else:
            b1_ref[...] = gl.astype(BF16)
            b2_ref[...] = pp.astype(BF16)
            o1_ref[...] = h
            o2_ref[...] = (h * _rstd(h) * x_ref[...]).astype(BF16)

    row = lambda w: pl.BlockSpec((tm, w), lambda i: (i, 0))
    res, _ = _call(
        body, name, [a, w_gate, p, w_proj, h1, target if last else next_gain],
        [row(D), whole((D, D)), row(PD), whole((PD, D)), row(D), row(D) if last else whole((1, D))],
        [jax.ShapeDtypeStruct((S, D), BF16), jax.ShapeDtypeStruct((S, D), BF16), jax.ShapeDtypeStruct((S, D), F32),
         jax.ShapeDtypeStruct((1, LANE), F32) if last else jax.ShapeDtypeStruct((S, D), BF16)],
        [row(D), row(D), row(D), whole((1, LANE)) if last else row(D)], (S // tm,), [],
        ("arbitrary",) if last else ("parallel",))
    return res


def _ple_bwd(dh, pp, gl):
    gate = _sigmoid(gl)
    return (dh * gate).astype(BF16), (dh * pp * gate * (1.0 - gate)).astype(BF16)


def post_bwd(name, dgl, w_gate, dh, h1, out, post_g, ple_g, S, D, ts):
    def fn(i, step, ins, outs, accs, scr):
        da = lax.dot_general(ins[0][...], ins[1][...], _NT, preferred_element_type=F32)
        dx, dple = _rms_bwd(da, ins[3][...], ins[6][...])
        dh1 = ins[2][...] + dx
        dout, dpost = _rms_bwd(dh1, ins[4][...], ins[5][...])
        outs[0][...] = dh1
        outs[1][...] = dout.astype(BF16)
        accs[0][...] += dpost
        accs[1][...] += dple
    return rowwise(name, fn, S, ts,
                   [(dgl, rows(ts, D)), (w_gate, whole((D, D))), (dh, rows(ts, D)), (h1, rows(ts, D)),
                    (out, rows(ts, D)), (post_g, whole((1, D))), (ple_g, whole((1, D)))],
                   [((S, D), F32, rows(ts, D)), ((S, D), BF16, rows(ts, D))],
                   accs=[((1, D), F32), ((1, D), F32)])


def pre_bwd(name, dxn, dh1, h, pre_g, S, D, ts, below=None):
    def fn(i, step, ins, outs, accs, scr):
        dx, dpre = _rms_bwd(ins[0][...], ins[2][...], ins[3][...])
        dh = ins[1][...] + dx
        outs[0][...] = dh
        accs[0][...] += dpre
        if below is not None:
            outs[1][...], outs[2][...] = _ple_bwd(dh, ins[4][...].astype(F32), ins[5][...].astype(F32))
    more = [] if below is None else [(below[0], rows(ts, D)), (below[1], rows(ts, D))]
    return rowwise(name, fn, S, ts,
                   [(dxn, rows(ts, D)), (dh1, rows(ts, D)), (h, rows(ts, D)), (pre_g, whole((1, D)))] + more,
                   [((S, D), F32, rows(ts, D))] + [((S, D), BF16, rows(ts, D))] * len(more), accs=[((1, D), F32)])


def _window_sums(ext, w, back):
    n = ext.shape[0]
    s, win = ext, 1
    while win < w:
        s = s + pltpu.roll(s, win if back else n - win, axis=0)
        win *= 2
    return s


def pool_fwd(name, z, S, E, NG, ts):
    G = E // NG

    def fn(i, step, ins, outs, accs, scr):
        carry = scr[0]

        @pl.when(step == 0)
        def _():
            carry[...] = jnp.zeros_like(carry)

        t = i * ts + lax.broadcasted_iota(jnp.int32, (ts, 1), 0)
        for j, w in enumerate(POOL_WINDOWS):
            u = ins[0][:, j * G:(j + 1) * G]
            ext = jnp.concatenate([carry[:, j * G:(j + 1) * G], u], axis=0)
            sw = _window_sums(ext, w, True)[POOL_HALO:, :]
            cnt = jnp.minimum(t + 1, w).astype(F32)
            outs[0][j] = (sw / cnt - u).astype(BF16)
        carry[...] = ins[0][ts - POOL_HALO:, :]

    return rowwise(name, fn, S, ts, [(z, rows(ts, E, 0))],
                   [((NG, S, G), BF16, pl.BlockSpec((NG, ts, G), lambda i: (0, i, 0)))],
                   scratch=[pltpu.VMEM((POOL_HALO, E), F32)])[0]


def pool_bwd(name, dpooled, dg, S, E, NG, ts):
    G = E // NG

    def fn(i, step, ins, outs, accs, scr):
        carry = scr[0]

        @pl.when(step == 0)
        def _():
            carry[...] = jnp.zeros_like(carry)

        t = i * ts + lax.broadcasted_iota(jnp.int32, (ts, 1), 0)
        for j, w in enumerate(POOL_WINDOWS):
            d = ins[0][j]
            e = d / jnp.minimum(t + 1, w).astype(F32)
            ext = jnp.concatenate([e, carry[:, j * G:(j + 1) * G]], axis=0)
            sw = _window_sums(ext, w, False)[:ts, :]
            outs[0][:, j * G:(j + 1) * G] = (sw - d).astype(BF16)
            carry[:, j * G:(j + 1) * G] = e[:POOL_HALO, :]
        outs[0][:, E:] = ins[1][...]

    return rowwise(name, fn, S, ts,
                   [(dpooled, pl.BlockSpec((NG, ts, G), lambda i: (0, i, 0))), (dg, rows(ts, E))],
                   [((S, 2 * E), BF16, rows(ts, 2 * E))],
                   scratch=[pltpu.VMEM((POOL_HALO, E), F32)], reverse=True)[0]


def pool_group_fwd(name, pooled, w_group, z, scale, S, E, NG):
    G = E // NG
    tm = _pick(S, (1024, 512, 256, 128))

    def body(a_ref, b_ref, g_ref, sc_ref, y_ref, mx_ref):
        mx = lax.dot_general(a_ref[...], b_ref[...], _NN, preferred_element_type=F32)
        g = g_ref[...]
        y_ref[...] = (mx * sc_ref[...] * (g * _sigmoid(g))).astype(BF16)
        mx_ref[...] = mx.astype(BF16)

    grp = pl.BlockSpec((None, tm, G), lambda j, i: (j, i, 0))
    res, _ = _call(
        body, name, [pooled, w_group, z, scale],
        [grp, pl.BlockSpec((None, G, G), lambda j, i: (j, 0, 0)), pl.BlockSpec((tm, G), lambda j, i: (i, NG + j)),
         pl.BlockSpec((1, G), lambda j, i: (0, j))],
        [jax.ShapeDtypeStruct((S, E), BF16), jax.ShapeDtypeStruct((NG, S, G), BF16)],
        [pl.BlockSpec((tm, G), lambda j, i: (i, j)), grp], (NG, S // tm), [], ("parallel", "parallel"))
    return res


def pool_out_bwd(name, dout, w_out, mixed, z, scale, S, E, NG, rider=None):
    G, D = E // NG, dout.shape[1]
    tm = _pick(S, (512, 256, 128))

    def body(d_ref, w_ref, mx_ref, g_ref, sc_ref, dmx_ref, dg_ref, dsc_ref):
        i = pl.program_id(1)
        d = lax.dot_general(d_ref[...], w_ref[...], _NT, preferred_element_type=F32)
        mx, g, sc = mx_ref[...].astype(F32), g_ref[...], sc_ref[...]
        sg = _sigmoid(g)
        si = g * sg
        dmx_ref[...] = (d * sc * si).astype(BF16)
        dg_ref[...] = (d * mx * sc * (sg * (1.0 + g * (1.0 - sg)))).astype(BF16)

        @pl.when(i == 0)
        def _():
            dsc_ref[...] = jnp.zeros_like(dsc_ref)

        dsc_ref[...] += jnp.sum(d * mx * si, axis=0, keepdims=True)

    grp = pl.BlockSpec((None, tm, G), lambda j, i: (j, i, 0))
    return _call(
        body, name, [dout, w_out, mixed, z, scale],
        [pl.BlockSpec((tm, D), lambda j, i: (i, 0)), pl.BlockSpec((G, D), lambda j, i: (j, 0)), grp,
         pl.BlockSpec((tm, G), lambda j, i: (i, NG + j)), pl.BlockSpec((1, G), lambda j, i: (0, j))],
        [jax.ShapeDtypeStruct((NG, S, G), BF16), jax.ShapeDtypeStruct((S, E), BF16), jax.ShapeDtypeStruct((1, E), F32)],
        [grp, pl.BlockSpec((tm, G), lambda j, i: (i, j)), pl.BlockSpec((1, G), lambda j, i: (0, j))],
        (NG, S // tm), [], ("parallel", "arbitrary"), rider)


def mla_prep_fwd(name, z, qg, kvg, tabs, S, E, R, ts):
    qb, kb, pb = E // R, E // R + 1, (E + 2 * R) // LANE

    def fn(i, step, ins, outs, accs, scr):
        zq, zkv = ins[0][...], ins[1][...]
        outs[0][...] = (zq * _rstd(zq) * ins[3][...]).astype(BF16)
        outs[1][...] = (zkv * _rstd(zkv) * ins[4][...]).astype(BF16)
        outs[2][...] = _rope(ins[2][...], ins[5][...], ins[6][...], ins[7][...], 1.0)

    return rowwise(name, fn, S, ts,
                   [(z, rows(ts, R, qb)), (z, rows(ts, R, kb)), (z, rows(ts, LANE, pb)),
                    (qg, whole((1, R))), (kvg, whole((1, R)))] + [(t, rows(ts, LANE)) for t in tabs],
                   [((S, R), BF16, rows(ts, R)), ((S, R), BF16, rows(ts, R)), ((S, LANE), F32, rows(ts, LANE))])


def mla_up_fwd(name, qn, kvn, w_uq, w_uk, w_uv, kper, tabs, S, H, R, scale):
    hc = _pick(H, (4, 2, 1))
    tm = _pick(S, (1024, 512, 256, 128))

    def body(qn_ref, kvn_ref, wq_ref, wk_ref, wv_ref, kper_ref, cos_ref, sa_ref, sb_ref, q_ref, k_ref, v_ref):
        cos_t, sin_a, sin_b = cos_ref[...], sa_ref[...], sb_ref[...]
        kvn_t = kvn_ref[...]
        q = lax.dot_general(qn_ref[...], wq_ref[...], _NN, preferred_element_type=F32)
        k = lax.dot_general(kvn_t, wk_ref[...], _NN, preferred_element_type=F32)
        v_ref[...] = lax.dot_general(kvn_t, wv_ref[...], _NN, preferred_element_type=F32).astype(BF16)
        kp = kper_ref[...].astype(BF16)
        for h in range(hc):
            a, b, c = h * HEAD_PAD, h * HEAD_PAD + NOPE_DIM, (h + 1) * HEAD_PAD
            q_ref[:, a:b] = (q[:, a:b] * scale).astype(BF16)
            q_ref[:, b:c] = (_rope(q[:, b:c], cos_t, sin_a, sin_b, 1.0) * scale).astype(BF16)
            k_ref[:, a:b] = k[:, a:b].astype(BF16)
            k_ref[:, b:c] = kp

    row = lambda w: pl.BlockSpec((tm, w), lambda i, j: (i, 0))
    col = lambda w: pl.BlockSpec((R, w), lambda i, j: (0, j))
    out = lambda w: pl.BlockSpec((tm, w), lambda i, j: (i, j))
    W = H * HEAD_PAD
    res, _ = _call(
        body, name, [qn, kvn, w_uq, w_uk, w_uv, kper, *tabs],
        [row(R), row(R), col(hc * HEAD_PAD), col(hc * HEAD_PAD), col(hc * V_DIM), row(LANE), row(LANE), row(LANE),
         row(LANE)],
        [jax.ShapeDtypeStruct((S, W), BF16), jax.ShapeDtypeStruct((S, W), BF16),
         jax.ShapeDtypeStruct((S, H * V_DIM), BF16)],
        [out(hc * HEAD_PAD), out(hc * HEAD_PAD), out(hc * V_DIM)], (S // tm, H // hc), [], ("parallel", "parallel"))
    return res


def mla_gate_bwd(name, dy, o, z, S, E, ts):
    def fn(i, step, ins, outs, accs, scr):
        d, ov, g = ins[0][...], ins[1][...], ins[2][...]
        sg = _sigmoid(g)
        outs[0][...] = (d * (g * sg)).astype(BF16)
        outs[1][...] = (d * ov * (sg * (1.0 + g * (1.0 - sg)))).astype(BF16)

    return rowwise(name, fn, S, ts, [(dy, rows(ts, E)), (o, rows(ts, E)), (z, rows(ts, E, 0))],
                   [((S, E), BF16, rows(ts, E)), ((S, E), BF16, rows(ts, E))])


def mla_unpack_q_bwd(name, dqt, tabs, S, H, t, scale):
    W = H * HEAD_PAD

    def fn(i, step, ins, outs, accs, scr):
        cos_t, sin_a, sin_b = ins[1][...], ins[2][...], ins[3][...]
        for h in range(H):
            a, b, c = h * HEAD_PAD, h * HEAD_PAD + NOPE_DIM, (h + 1) * HEAD_PAD
            dq = ins[0][h].T
            outs[0][:, a:b] = (dq[:, :NOPE_DIM] * scale).astype(BF16)
            outs[0][:, b:c] = (_rope(dq[:, NOPE_DIM:], cos_t, sin_a, sin_b, -1.0) * scale).astype(BF16)

    return rowwise(name, fn, S, t,
                   [(dqt, pl.BlockSpec((H, None, HEAD_PAD, t), lambda i: (0, i, 0, 0)))]
                   + [(tb, rows(t, LANE)) for tb in tabs],
                   [((S, W), BF16, rows(t, W))])[0]


def mla_unpack_k_bwd(name, dk, S, H, ts):
    W = H * HEAD_PAD

    def fn(i, step, ins, outs, accs, scr):
        dkpe = jnp.zeros((ts, LANE), F32)
        for h in range(H):
            a, b, c = h * HEAD_PAD, h * HEAD_PAD + NOPE_DIM, (h + 1) * HEAD_PAD
            outs[0][:, a:b] = ins[0][:, a:b].astype(BF16)
            outs[0][:, b:c] = jnp.zeros((ts, LANE), BF16)
            dkpe = dkpe + ins[0][:, b:c]
        outs[1][...] = dkpe

    return rowwise(name, fn, S, ts, [(dk, rows(ts, W))],
                   [((S, W), BF16, rows(ts, W)), ((S, LANE), F32, rows(ts, LANE))])


def mla_prep_bwd(name, dqn, dkvn_k, dkvn_v, z, dkpe, dg, qg, kvg, tabs, S, E, R, ts):
    qb, kb = E // R, E // R + 1
    ZW = E + 2 * R + LANE

    def fn(i, step, ins, outs, accs, scr):
        dzq, dqg = _rms_bwd(ins[0][...], ins[3][...], ins[7][...])
        dzkv, dkvg = _rms_bwd(ins[1][...] + ins[2][...], ins[4][...], ins[8][...])
        outs[0][:, :E] = ins[6][...]
        outs[0][:, E:E + R] = dzq.astype(BF16)
        outs[0][:, E + R:E + 2 * R] = dzkv.astype(BF16)
        outs[0][:, E + 2 * R:] = _rope(ins[5][...], ins[9][...], ins[10][...], ins[11][...], -1.0).astype(BF16)
        accs[0][...] += dqg
        accs[1][...] += dkvg

    return rowwise(name, fn, S, ts,
                   [(dqn, rows(ts, R)), (dkvn_k, rows(ts, R)), (dkvn_v, rows(ts, R)), (z, rows(ts, R, qb)),
                    (z, rows(ts, R, kb)), (dkpe, rows(ts, LANE)), (dg, rows(ts, E)),
                    (qg, whole((1, R))), (kvg, whole((1, R)))] + [(t, rows(ts, LANE)) for t in tabs],
                   [((S, ZW), BF16, rows(ts, ZW))], accs=[((1, R), F32), ((1, R), F32)])


_NT = (((1,), (1,)), ((), ()))
_NN = (((1,), (0,)), ((), ()))
_TN = (((0,), (0,)), ((), ()))


def _causal_mask_t(t):
    return lax.broadcasted_iota(jnp.int32, (t, t), 0) <= lax.broadcasted_iota(jnp.int32, (t, t), 1)


def _tile(i, t):
    return pl.ds(pl.multiple_of(i * t, t), t)


def flash_fwd(name, q, k, v, z, S, H, t, rider=None):
    nt = S // t

    def body(q_ref, k_ref, v_ref, g_ref, o_ref, y_ref, lse_ref, m_sc, l_sc, acc_sc):
        i = pl.program_id(1)
        m_sc[...] = jnp.full_like(m_sc, NEG_INF)
        l_sc[...] = jnp.zeros_like(l_sc)
        acc_sc[...] = jnp.zeros_like(acc_sc)
        q = q_ref[...]

        def tile(j, diag):
            s = lax.dot_general(k_ref[_tile(j, t), :], q, _NT, preferred_element_type=F32)
            if diag:
                s = jnp.where(_causal_mask_t(t), s, NEG_INF)
            m_prev = m_sc[...]
            m_new = jnp.maximum(m_prev, jnp.max(s, axis=0, keepdims=True))
            alpha = jnp.exp(m_prev - m_new)
            p = jnp.exp(s - m_new)
            l_sc[...] = alpha * l_sc[...] + jnp.sum(p, axis=0, keepdims=True)
            acc_sc[...] = alpha * acc_sc[...] + lax.dot_general(v_ref[_tile(j, t), :], p.astype(BF16), _TN,
                                                                 preferred_element_type=F32)
            m_sc[...] = m_new

        def off_diagonal(j, carry):
            tile(j, False)
            return carry

        lax.fori_loop(0, i, off_diagonal, 0)
        tile(i, True)
        l = l_sc[...]
        o = (acc_sc[...] / l).T
        g = g_ref[...]
        o_ref[...] = o
        y_ref[...] = (o * (g * _sigmoid(g))).astype(BF16)
        lse_ref[...] = m_sc[...] + jnp.log(l)

    qtile = pl.BlockSpec((t, V_DIM), lambda h, i: (i, h))
    (o, y, lse), carried = _call(
        body, name, [q, k, v, z],
        [pl.BlockSpec((t, HEAD_PAD), lambda h, i: (i, h)), pl.BlockSpec((S, HEAD_PAD), lambda h, i: (0, h)),
         pl.BlockSpec((S, V_DIM), lambda h, i: (0, h)), qtile],
        [jax.ShapeDtypeStruct((S, H * V_DIM), F32), jax.ShapeDtypeStruct((S, H * V_DIM), BF16),
         jax.ShapeDtypeStruct((H, nt, 1, t), F32)],
        [qtile, qtile, pl.BlockSpec((None, None, 1, t), lambda h, i: (h, i, 0, 0))],
        (H, nt), [pltpu.VMEM((1, t), F32), pltpu.VMEM((1, t), F32), pltpu.VMEM((V_DIM, t), F32)],
        ("parallel", "parallel"), rider)
    return o, y, lse, carried


def flash_bwd(name, q, k, v, do, o, lse, S, H, t, rider=None):
    nt = S // t

    def body(q_ref, k_ref, v_ref, do_ref, o_ref, lse_ref, dq_ref, dk_ref, dv_ref, kt_sc, dl_sc, dv_sc):
        j = pl.program_id(1)

        @pl.when(j == 0)
        def _():
            dq_ref[...] = jnp.zeros_like(dq_ref)
            ones = jnp.ones((8, V_DIM), BF16)
            for i in range(nt):
                x = do_ref[i * t:(i + 1) * t, :].astype(F32) * o_ref[i * t:(i + 1) * t, :]
                hi = x.astype(BF16)
                lo = (x - hi.astype(F32)).astype(BF16)
                dl_sc[i] = (lax.dot_general(ones, hi, _NT, preferred_element_type=F32)
                            + lax.dot_general(ones, lo, _NT, preferred_element_type=F32))

        kj, vj = k_ref[...], v_ref[...]
        kt_sc[...] = kj.astype(F32).T.astype(BF16)
        dk_ref[...] = jnp.zeros_like(dk_ref)
        dv_sc[...] = jnp.zeros_like(dv_sc)

        def tile(i, diag):
            qi, doi = q_ref[_tile(i, t), :], do_ref[_tile(i, t), :]
            s = lax.dot_general(kj, qi, _NT, preferred_element_type=F32)
            p = jnp.exp(s - lse_ref[i])
            if diag:
                p = jnp.where(_causal_mask_t(t), p, 0.0)
            dv_sc[...] += lax.dot_general(p.astype(BF16), doi, _NN, preferred_element_type=F32)
            dp = lax.dot_general(vj, doi, _NT, preferred_element_type=F32)
            ds = (p * (dp - dl_sc[i, 0:1, :])).astype(BF16)
            dk_ref[...] += lax.dot_general(ds, qi, _NN, preferred_element_type=F32)
            dq_ref[i] += lax.dot_general(kt_sc[...], ds, _NN, preferred_element_type=F32)

        def off_diagonal(i, carry):
            tile(i, False)
            return carry

        tile(j, True)
        lax.fori_loop(j + 1, nt, off_diagonal, 0)
        dv_ref[...] = dv_sc[...].astype(BF16)

    head = lambda w: pl.BlockSpec((S, w), lambda h, j: (0, h))
    ktile = lambda w: pl.BlockSpec((t, w), lambda h, j: (j, h))
    (dq, dk, dv), carried = _call(
        body, name, [q, k, v, do, o, lse],
        [head(HEAD_PAD), ktile(HEAD_PAD), ktile(V_DIM), head(V_DIM), head(V_DIM),
         pl.BlockSpec((None, nt, 1, t), lambda h, j: (h, 0, 0, 0))],
        [jax.ShapeDtypeStruct((H, nt, HEAD_PAD, t), F32), jax.ShapeDtypeStruct((S, H * HEAD_PAD), F32),
         jax.ShapeDtypeStruct((S, H * V_DIM), BF16)],
        [pl.BlockSpec((None, nt, HEAD_PAD, t), lambda h, j: (h, 0, 0, 0)), ktile(HEAD_PAD), ktile(V_DIM)],
        (H, nt), [pltpu.VMEM((HEAD_PAD, t), BF16), pltpu.VMEM((nt, 8, t), F32), pltpu.VMEM((t, V_DIM), F32)],
        ("parallel", "arbitrary"), rider)
    return dq, dk, dv, carried


def _peers():
    x, y, c = lax.axis_index("x"), lax.axis_index("y"), lax.axis_index("c")
    me = 4 * x + 2 * y + c
    peers = []
    for fx, fy, fc in ((0, 0, 1), (1, 0, 0), (0, 1, 0), (1, 1, 0), (1, 0, 1), (0, 1, 1), (1, 1, 1)):
        px, py, pc = x ^ fx, y ^ fy, c ^ fc
        peers.append(((px, py, pc), 4 * px + 2 * py + pc))
    return me, peers


def _hbm_specs(n):
    return [pl.BlockSpec(memory_space=pl.ANY)] * n


class Rider:
    def __init__(self, arrs, gather, windows=None, layers=None):
        self.arrs, self.gather, self.n = list(arrs), gather, len(arrs)
        self.windows = list(windows) if windows is not None else [None] * self.n
        self.layers = list(layers) if layers is not None else [None] * self.n
        assert not (gather and any(w is not None for w in self.windows))
        assert gather or all(l is None for l in self.layers)
        self.out_shapes = [
            jax.ShapeDtypeStruct((N_DEV,) + (a.shape if l is None else a.shape[1:]) if gather else
                                 a.shape if w is None else (N_DEV, w[1]) + a.shape[2:], a.dtype)
            for a, w, l in zip(arrs, self.windows, self.layers)]
        self.sems = [pltpu.SemaphoreType.DMA((self.n, N_DEV - 1)), pltpu.SemaphoreType.DMA((self.n, N_DEV - 1)),
                     pltpu.SemaphoreType.DMA((self.n,))]

    def _copies(self, srcs, dsts, sems):
        send_sems, recv_sems, local_sems = sems
        x, y, c = lax.axis_index("x"), lax.axis_index("y"), lax.axis_index("c")
        ident = lambda d: 4 * d[0] + 2 * d[1] + d[2]
        me, sibling = (x, y, c), (x, y, 1 - c)
        chips = [(1 - x, y), (x, 1 - y), (1 - x, 1 - y)]
        _, peers = _peers()

        def slot(a, pid):
            w = self.windows[a]
            return srcs[a].at[pid] if w is None else srcs[a].at[pid, pl.ds(w[0], w[1])]

        def own(a):
            return srcs[a] if self.layers[a] is None else srcs[a].at[self.layers[a]]

        def remote(a, k, incoming):
            if not self.gather:
                target, pid = peers[k]
                src, block = slot(a, pid), (pid if incoming else ident(me))
            elif k == 0:
                target, src, block = sibling, own(a), ident(sibling if incoming else me)
            elif k <= 3:
                target = (*chips[k - 1], c)
                src, block = own(a), ident(target if incoming else me)
            else:
                landed = ident((*chips[k - 4], c))
                target, src = sibling, dsts[a].at[landed]
                block = ident((*chips[k - 4], 1 - c)) if incoming else landed
            return pltpu.make_async_remote_copy(
                src_ref=src, dst_ref=dsts[a].at[block], send_sem=send_sems.at[a, k], recv_sem=recv_sems.at[a, k],
                device_id=target, device_id_type=MESH)

        def local(a):
            return pltpu.make_async_copy(own(a) if self.gather else slot(a, ident(me)), dsts[a].at[ident(me)],
                                         local_sems.at[a])

        return local, remote

    def start(self, srcs, dsts, sems, only=None):
        local, remote = self._copies(srcs, dsts, sems)
        for a in range(self.n) if only is None else (only,):
            local(a).start()
            for k in range(4 if self.gather else N_DEV - 1):
                remote(a, k, False).start()

    def relay(self, srcs, dsts, sems):
        if not self.gather:
            return
        local, remote = self._copies(srcs, dsts, sems)
        for a in range(self.n):
            for k in range(1, 4):
                remote(a, k, True).wait_recv()
                remote(a, k + 3, False).start()

    def wait(self, srcs, dsts, sems):
        local, remote = self._copies(srcs, dsts, sems)
        for a in range(self.n):
            for k in range(N_DEV - 1):
                if not (self.gather and 1 <= k <= 3):
                    remote(a, k, True).wait_recv()
        for a in range(self.n):
            for k in range(N_DEV - 1):
                remote(a, k, False).wait_send()
            local(a).wait()


def _carry(body, n_in, n_out, rider, grid):
    n = rider.n
    steps = math.prod(grid)
    stride = max(1, (steps // 2) // n)
    first = [min(a * stride, max(0, steps // 2 - 1)) for a in range(n)]

    def wrapped(*refs):
        ins, r_in = refs[:n_in], refs[n_in:n_in + n]
        outs = refs[n_in + n:n_in + n + n_out]
        r_out = refs[n_in + n + n_out:n_in + 2 * n + n_out]
        scratch, sems = refs[n_in + 2 * n + n_out:-3], refs[-3:]
        step = 0
        for d, g in enumerate(grid):
            step = step * g + pl.program_id(d)

        for s0 in sorted(set(first)):
            @pl.when(step == s0)
            def _(s0=s0):
                for a in range(n):
                    if first[a] == s0:
                        rider.start(r_in, r_out, sems, only=a)

        if rider.gather:
            @pl.when(step == (3 * steps) // 4)
            def _():
                rider.relay(r_in, r_out, sems)

        body(*ins, *outs, *scratch)

        @pl.when(step == steps - 1)
        def _():
            rider.wait(r_in, r_out, sems)

    return wrapped


def _call(body, name, ins, in_specs, out_shape, out_specs, grid, scratch, sem, rider=None):
    if rider is None:
        return _pcall(body, name=name, out_shape=list(out_shape), grid=grid, in_specs=list(in_specs),
                      out_specs=list(out_specs), scratch_shapes=list(scratch), compiler_params=_cparams(*sem))(*ins), None
    res = _pcall(
        _carry(body, len(ins), len(out_shape), rider, grid), name=name,
        out_shape=list(out_shape) + rider.out_shapes, grid=grid,
        in_specs=list(in_specs) + _hbm_specs(rider.n), out_specs=list(out_specs) + _hbm_specs(rider.n),
        scratch_shapes=list(scratch) + rider.sems, compiler_params=_cparams(*(("arbitrary",) * len(grid))),
    )(*ins, *rider.arrs)
    return res[:len(out_shape)], res[len(out_shape):]


def exchange(name, rider):
    def body(*refs):
        srcs, dsts, sems = refs[:rider.n], refs[rider.n:2 * rider.n], refs[2 * rider.n:]
        rider.start(srcs, dsts, sems)
        rider.relay(srcs, dsts, sems)
        rider.wait(srcs, dsts, sems)

    return _pcall(body, name=name, out_shape=rider.out_shapes, in_specs=_hbm_specs(rider.n),
                  out_specs=_hbm_specs(rider.n), scratch_shapes=rider.sems)(*rider.arrs)


def adamw(name, gslots, w, m, v, layer=0, prev=None):
    K, R, C = gslots.shape
    per_row = C * (K * gslots.dtype.itemsize + 7 * 4) * 2
    tr = R
    for cand in (1024, 512, 256, 128, 64, 32, 16, 8):
        if R % cand == 0:
            tr = cand
            if cand * per_row <= VMEM_LIMIT_BYTES // 2:
                break
    c1 = 1.0 / (1.0 - ADAM_B1 ** ADAM_STEP)
    c2 = 1.0 / (1.0 - ADAM_B2 ** ADAM_STEP)

    def body(g_ref, w_ref, m_ref, v_ref, *rest):
        go_ref, d_ref, mo_ref, vo_ref = rest[-4:]
        g = g_ref[0].astype(F32)
        for s in range(1, K):
            g = g + g_ref[s].astype(F32)
        mn = ADAM_B1 * m_ref[...] + (1.0 - ADAM_B1) * g
        vn = ADAM_B2 * v_ref[...] + (1.0 - ADAM_B2) * (g * g)
        go_ref[...] = g
        mo_ref[...] = mn
        vo_ref[...] = vn
        d_ref[...] = -ADAM_LR * ((mn * c1) / (jnp.sqrt(vn * c2) + ADAM_EPS) + ADAM_WD * w_ref[...])

    blk = pl.BlockSpec((None, tr, C), lambda i: (layer, i, 0))
    prev = [] if prev is None else list(prev)
    return _pcall(
        body, name=name, out_shape=[jax.ShapeDtypeStruct(w.shape, F32)] * 4, grid=(R // tr,),
        in_specs=[pl.BlockSpec((K, tr, C), lambda i: (0, i, 0)), blk, blk, blk] + _hbm_specs(len(prev)),
        out_specs=[blk] * 4, input_output_aliases={4 + q: q for q in range(len(prev))},
        compiler_params=_cparams("parallel"),
    )(gslots, w, m, v, *prev)


def _from_slots(gathered, ax):
    g = jnp.moveaxis(gathered, 0, ax)
    s = g.shape
    return g.reshape(s[:ax] + (s[ax] * s[ax + 1],) + s[ax + 2:])


def _to_slots(full, ax):
    s = full.shape
    g = full.reshape(s[:ax] + (N_DEV, s[ax] // N_DEV) + s[ax + 1:])
    g = jnp.moveaxis(g, ax, 0)
    return g.reshape(N_DEV, -1, g.shape[-1])


def _rope_tables(pos, S):
    inv_freq = ROPE_THETA ** (-jnp.arange(0, ROPE_DIM, 2, dtype=F32) / ROPE_DIM)
    ang = pos.astype(F32)[:, None] * inv_freq
    cos, sin = jnp.cos(ang), jnp.sin(ang)
    z = jnp.zeros((S, ROPE_DIM // 2), F32)
    cos_t = jnp.concatenate([cos, cos, z, z], axis=1)
    sin_a = jnp.concatenate([-sin, z, z, z], axis=1)
    sin_b = jnp.concatenate([z, sin, z, z], axis=1)
    return cos_t, sin_a, sin_b


def kernel(x, p, positions, pre_norm, post_norm, pool_w_in, pool_w_group, pool_scale, pool_w_out, mla_w_in, mla_q_norm, mla_w_uq, mla_kv_norm, mla_w_ukv, mla_w_out, ple_norm, ple_w_gate, ple_w_proj, loss_target, m_pre_norm, m_post_norm, m_pool_w_in, m_pool_w_group, m_pool_scale, m_pool_w_out, m_mla_w_in, m_mla_q_norm, m_mla_w_uq, m_mla_kv_norm, m_mla_w_ukv, m_mla_w_out, m_ple_norm, m_ple_w_gate, m_ple_w_proj, v_pre_norm, v_post_norm, v_pool_w_in, v_pool_w_group, v_pool_scale, v_pool_w_out, v_mla_w_in, v_mla_q_norm, v_mla_w_uq, v_mla_kv_norm, v_mla_w_ukv, v_mla_w_out, v_ple_norm, v_ple_w_gate, v_ple_w_proj):
    wl = dict(pre_norm=pre_norm, post_norm=post_norm, pool_w_in=pool_w_in, pool_w_group=pool_w_group,
              pool_scale=pool_scale, pool_w_out=pool_w_out, mla_w_in=mla_w_in, mla_q_norm=mla_q_norm,
              mla_w_uq=mla_w_uq, mla_kv_norm=mla_kv_norm, mla_w_ukv=mla_w_ukv, mla_w_out=mla_w_out,
              ple_norm=ple_norm, ple_w_gate=ple_w_gate, ple_w_proj=ple_w_proj)
    ml = dict(pre_norm=m_pre_norm, post_norm=m_post_norm, pool_w_in=m_pool_w_in, pool_w_group=m_pool_w_group,
              pool_scale=m_pool_scale, pool_w_out=m_pool_w_out, mla_w_in=m_mla_w_in, mla_q_norm=m_mla_q_norm,
              mla_w_uq=m_mla_w_uq, mla_kv_norm=m_mla_kv_norm, mla_w_ukv=m_mla_w_ukv, mla_w_out=m_mla_w_out,
              ple_norm=m_ple_norm, ple_w_gate=m_ple_w_gate, ple_w_proj=m_ple_w_proj)
    vl = dict(pre_norm=v_pre_norm, post_norm=v_post_norm, pool_w_in=v_pool_w_in, pool_w_group=v_pool_w_group,
              pool_scale=v_pool_scale, pool_w_out=v_pool_w_out, mla_w_in=v_mla_w_in, mla_q_norm=v_mla_q_norm,
              mla_w_uq=v_mla_w_uq, mla_kv_norm=v_mla_kv_norm, mla_w_ukv=v_mla_w_ukv, mla_w_out=v_mla_w_out,
              ple_norm=v_ple_norm, ple_w_gate=v_ple_w_gate, ple_w_proj=v_ple_w_proj)

    S, D = x.shape[1], x.shape[2]
    L = pre_norm.shape[0]
    E = pool_scale.shape[1]
    NG = pool_w_group.shape[1]
    R = mla_w_uq.shape[1]
    H = D // 128
    EM = H * V_DIM
    PD = p.shape[-1]
    me = 4 * lax.axis_index("x") + 2 * lax.axis_index("y") + lax.axis_index("c")
    ts = min(S, 256)
    tsw = min(S, 256)
    ta = min(S, 512)
    sm_scale = (NOPE_DIM + ROPE_DIM) ** -0.5

    wb = {n: wl[n].astype(BF16) for n in BIG}
    full = {}

    def ag_rider(host):
        if host not in AG_PLAN:
            return None
        return Rider([wb[n] for n, _ in AG_PLAN[host]], True, layers=[l for _, l in AG_PLAN[host]])

    def ag_done(host, results):
        for (n, l), g in zip(AG_PLAN[host], results):
            full[n, l] = g if n in SLOT_NATIVE else _from_slots(g, SHARD_AXIS[n] - 1)

    small_sh = jnp.concatenate([wl[n].reshape(1, -1) for n in SMALL_SHARD], axis=1)
    g_in0, g_small = exchange("gather_first", Rider([wb['pool_w_in'], small_sh], True, layers=[0, None]))
    full['pool_w_in', 0] = g_in0
    nq = mla_q_norm.size
    q_norm = _from_slots(g_small[:, 0, :nq].reshape((N_DEV,) + mla_q_norm.shape), 1)
    kv_norm = _from_slots(g_small[:, 0, nq:].reshape((N_DEV,) + mla_kv_norm.shape), 1)

    def mla_kernel_weights(j):
        w_in = full['mla_w_in', j]
        w_in_k = jnp.concatenate([w_in[:, 2 * R + ROPE_DIM:], w_in[:, :2 * R + ROPE_DIM],
                                  jnp.zeros((D, LANE - ROPE_DIM), BF16)], axis=1)
        w_uq_k = jnp.pad(full['mla_w_uq', j].reshape(R, H, NOPE_DIM + ROPE_DIM),
                         ((0, 0), (0, 0), (0, HEAD_PAD - NOPE_DIM - ROPE_DIM))).reshape(R, H * HEAD_PAD)
        w_ukv = full['mla_w_ukv', j].reshape(R, H, NOPE_DIM + V_DIM)
        w_uk_k = jnp.pad(w_ukv[..., :NOPE_DIM], ((0, 0), (0, 0), (0, HEAD_PAD - NOPE_DIM))).reshape(R, H * HEAD_PAD)
        w_uv_k = w_ukv[..., NOPE_DIM:].reshape(R, H * V_DIM)
        return w_in_k, w_uq_k, w_uk_k, w_uv_k

    def fmm(name, a, b, mode, out_dtype=F32, slots=False):
        if name not in AG_PLAN:
            return mm(name, a, b, mode, out_dtype, slots=slots)
        out, carried = mm(name, a, b, mode, out_dtype, ag_rider(name), slots)
        ag_done(name, carried)
        return out

    mla_w = {}
    tabs = _rope_tables(positions[0], S)

    h = x[0]
    saved = []
    for i in range(L):
        j = i // 2
        sv = dict(h=h)
        if i == 0:
            xn = rms_fwd(f"pre_norm_{i}", h, pre_norm[i:i + 1], S, D, ts)
        sv['xn'] = xn
        if i % 2 == 0:
            z = fmm(f"pool_in_{i}", xn, full['pool_w_in', j], 'nn', slots=True)
            pooled = pool_fwd(f"pool_window_{i}", z, S, E, NG, tsw)
            y, mixed = pool_group_fwd(f"pool_group_{i}", pooled, full['pool_w_group', j], z, pool_scale[j:j + 1],
                                      S, E, NG)
            out = fmm(f"pool_out_{i}", y, full['pool_w_out', j], 'nn')
            sv.update(z=z, pooled=pooled, mixed=mixed, y=y)
        else:
            w_in_k, w_uq_k, w_uk_k, w_uv_k = mla_w[j] = mla_kernel_weights(j)
            z = fmm(f"mla_in_{i}", xn, w_in_k, 'nn')
            qn, kvn, kper = mla_prep_fwd(f"mla_prep_{i}", z, q_norm[j:j + 1], kv_norm[j:j + 1], tabs, S, EM, R, ts)
            qp, kp, vv = mla_up_fwd(f"mla_up_{i}", qn, kvn, w_uq_k, w_uk_k, w_uv_k, kper, tabs, S, H, R, sm_scale)
            o, y, lse, carried = flash_fwd(f"attn_{i}", qp, kp, vv, z, S, H, ta, ag_rider(f"attn_{i}"))
            if carried is not None:
                ag_done(f"attn_{i}", carried)
            out, h1, a = out_post_fwd(f"mla_out_{i}", y, full['mla_w_out', j], h, post_norm[i:i + 1],
                                      ple_norm[i:i + 1], S, D, ts)
            sv.update(z=z, qn=qn, kvn=kvn, qp=qp, kp=kp, vv=vv, o=o, lse=lse, y=y)
        if i % 2 == 0:
            h1, a = post_fwd(f"post_norm_{i}", h, out, post_norm[i:i + 1], ple_norm[i:i + 1], S, D, ts)
        if i < L - 1:
            gl, pp, h, xn = ple_fwd(f"ple_{i}", a, full['ple_w_gate', i], p[i, 0], full['ple_w_proj', i], h1,
                                    next_gain=pre_norm[i + 1:i + 2])
        else:
            gl = pp = None
            dpp, dgl, dh, loss_acc = ple_fwd(f"ple_{i}", a, full['ple_w_gate', i], p[i, 0], full['ple_w_proj', i], h1,
                                           target=loss_target[0])
        sv.update(out=out, h1=h1, a=a, gl=gl, pp=pp)
        saved.append(sv)

    loss = lax.psum(loss_acc[0, 0] * (0.5 / D), ("x", "y", "c"))

    gw = {n: [None] * wl[n].shape[0] for n in WEIGHTS}
    recv = {}

    def rs_rider(host):
        if host not in RS_PLAN:
            return None
        return scatter_rider(RS_PLAN[host])

    def scatter_rider(keys):
        arrs = [gw[n][l] if n in SLOT_NATIVE else _to_slots(gw[n][l], SHARD_AXIS[n] - 1).astype(BF16)
                for n, l, *_ in keys]
        halves = [key[2] if len(key) == 3 else None for key in keys]
        return Rider(arrs, False, [None if h is None else (h * (a.shape[1] // 2), a.shape[1] // 2)
                                   for a, h in zip(arrs, halves)])

    def rs_done(host, results):
        for key, r in zip(RS_PLAN[host], results):
            recv[key] = r

    def bmm(name, a, b, mode, slots=False):
        out_dtype = BF16 if mode == 'tn' else F32
        if name not in RS_PLAN:
            return mm(name, a, b, mode, out_dtype, slots=slots)
        out, carried = mm(name, a, b, mode, out_dtype, rs_rider(name), slots)
        rs_done(name, carried)
        return out

    for i in reversed(range(L)):
        j = i // 2
        sv = saved[i]
        gw['ple_w_proj'][i] = bmm(f"ple_proj_dw_{i}", p[i, 0], dpp, 'tn')
        gw['ple_w_gate'][i] = bmm(f"ple_gate_dw_{i}", sv['a'], dgl, 'tn')
        dh1, dout, dpost, dple = post_bwd(f"post_norm_bwd_{i}", dgl, full['ple_w_gate', i], dh, sv['h1'], sv['out'],
                                          post_norm[i:i + 1], ple_norm[i:i + 1], S, D, ts)
        gw['post_norm'][i], gw['ple_norm'][i] = dpost[0], dple[0]
        xn = sv['xn']
        if i % 2 == 0:
            gw['pool_w_out'][j] = bmm(f"pool_out_dw_{i}", sv['y'], dout, 'tn')
            (dmixed, dg, dscale), carried = pool_out_bwd(
                f"pool_out_dx_{i}", dout, full['pool_w_out', j], sv['mixed'], sv['z'], pool_scale[j:j + 1], S, E, NG,
                rs_rider(f"pool_out_dx_{i}"))
            if carried is not None:
                rs_done(f"pool_out_dx_{i}", carried)
            gw['pool_scale'][j] = dscale[0]
            gw['pool_w_group'][j] = bmm(f"pool_group_dw_{i}", sv['pooled'], dmixed, 'tn')
            dpooled = bmm(f"pool_group_dx_{i}", dmixed, full['pool_w_group', j], 'nt')
            dz = pool_bwd(f"pool_window_bwd_{i}", dpooled, dg, S, E, NG, tsw)
            gw['pool_w_in'][j] = bmm(f"pool_in_dw_{i}", xn, dz, 'tn', slots=True)
            dxn = bmm(f"pool_in_dx_{i}", dz, full['pool_w_in', j], 'nt', slots=True)
        else:
            w_in_k, w_uq_k, w_uk_k, w_uv_k = mla_w[j]
            gw['mla_w_out'][j] = bmm(f"mla_out_dw_{i}", sv['y'], dout, 'tn')
            dy = bmm(f"mla_out_dx_{i}", dout, full['mla_w_out', j], 'nt')
            do, dg = mla_gate_bwd(f"mla_gate_bwd_{i}", dy, sv['o'], sv['z'], S, EM, ts)
            dqt, dkp, dvv, carried = flash_bwd(f"attn_bwd_{i}", sv['qp'], sv['kp'], sv['vv'], do, sv['o'], sv['lse'],
                                               S, H, ta, rs_rider(f"attn_bwd_{i}"))
            if carried is not None:
                rs_done(f"attn_bwd_{i}", carried)
            dq_raw = mla_unpack_q_bwd(f"mla_pack_q_bwd_{i}", dqt, tabs, S, H, ta, sm_scale)
            dk_raw, dkpe = mla_unpack_k_bwd(f"mla_pack_k_bwd_{i}", dkp, S, H, tsw)
            g_uq = bmm(f"mla_uq_dw_{i}", sv['qn'], dq_raw, 'tn')
            g_uk = bmm(f"mla_uk_dw_{i}", sv['kvn'], dk_raw, 'tn')
            g_uv = bmm(f"mla_uv_dw_{i}", sv['kvn'], dvv, 'tn')
            gw['mla_w_uq'][j] = g_uq.reshape(R, H, HEAD_PAD)[:, :, :NOPE_DIM + ROPE_DIM].reshape(R, -1)
            gw['mla_w_ukv'][j] = jnp.concatenate(
                [g_uk.reshape(R, H, HEAD_PAD)[:, :, :NOPE_DIM], g_uv.reshape(R, H, V_DIM)], axis=2).reshape(R, -1)
            dqn = bmm(f"mla_uq_dx_{i}", dq_raw, w_uq_k, 'nt')
            dkvn_k = bmm(f"mla_uk_dx_{i}", dk_raw, w_uk_k, 'nt')
            dkvn_v = bmm(f"mla_uv_dx_{i}", dvv, w_uv_k, 'nt')
            dz, dqg, dkvg = mla_prep_bwd(f"mla_prep_bwd_{i}", dqn, dkvn_k, dkvn_v, sv['z'], dkpe, dg,
                                         q_norm[j:j + 1], kv_norm[j:j + 1], tabs, S, EM, R, ts)
            g_in = bmm(f"mla_in_dw_{i}", xn, dz, 'tn')
            dxn = bmm(f"mla_in_dx_{i}", dz, w_in_k, 'nt')
            gw['mla_q_norm'][j], gw['mla_kv_norm'][j] = dqg[0], dkvg[0]
            gw['mla_w_in'][j] = jnp.concatenate([g_in[:, EM:EM + 2 * R + ROPE_DIM], g_in[:, :EM]], axis=1)
        if i > 0:
            dh, dpp, dgl, dpre = pre_bwd(f"pre_norm_bwd_{i}", dxn, dh1, sv['h'], pre_norm[i:i + 1], S, D, ts,
                                         below=(saved[i - 1]['pp'], saved[i - 1]['gl']))
        else:
            dh, dpre = pre_bwd(f"pre_norm_bwd_{i}", dxn, dh1, sv['h'], pre_norm[i:i + 1], S, D, ts)
        gw['pre_norm'][i] = dpre[0]
    grad_x = dh[None]

    last = exchange("scatter_last", scatter_rider(RS_LAST))
    for key, r in zip(RS_LAST, last):
        recv[key] = r
    for n, l in {key[:2] for key in recv if len(key) == 3}:
        recv[n, l] = jnp.concatenate([recv[n, l, 0], recv[n, l, 1]], axis=1)
    small_names = SMALL_REPL + SMALL_SHARD
    gw = {n: jnp.stack(gw[n]) for n in small_names}
    small_g = jnp.concatenate([gw[n].reshape(1, -1) for n in small_names], axis=1)
    small_all = exchange("gather_small_grads", Rider([small_g], True))[0]

    outs = {}
    for n in BIG:
        shp = wl[n].shape
        three = lambda a: a.reshape(shp[0], -1, shp[-1])
        res = None
        for l in range(shp[0]):
            res = adamw(f"adamw_{n}_{l}", recv[n, l], three(wl[n]), three(ml[n]), three(vl[n]), l, res)
        outs[n] = [a.reshape(shp) for a in res]

    pieces, off = [], 0
    for n in small_names:
        sz = gw[n].size
        g = small_all[:, :, off:off + sz]
        off += sz
        if n in SMALL_SHARD:
            rows_, cols_ = gw[n].shape
            g = lax.dynamic_slice_in_dim(g.reshape(N_DEV, rows_, cols_), me * (cols_ // N_DEV), cols_ // N_DEV, axis=2)
            g = g.reshape(N_DEV, 1, -1)
        pieces.append(g)
    gs = jnp.concatenate(pieces, axis=2)
    flat = lambda d: jnp.concatenate([d[n].reshape(1, -1) for n in small_names], axis=1)
    res = adamw("adamw_small", gs, flat(wl)[None], flat(ml)[None], flat(vl)[None])
    off = 0
    for n in small_names:
        sz = wl[n].size
        outs[n] = [a[0, :, off:off + sz].reshape(wl[n].shape) for a in res]
        off += sz

    return (loss, grad_x, *[outs[n][0] for n in WEIGHTS], *[outs[n][1] for n in WEIGHTS],
            *[outs[n][2] for n in WEIGHTS], *[outs[n][3] for n in WEIGHTS])
```
